```python
import math
import jax, jax.numpy as jnp
from jax import lax
import numpy as np

D_MODEL = 1024
BATCH = 4
SEQ = 4096
DEPTH = 1

S5_WIDTH = D_MODEL // 2
S5_GROUP = 16
S5_GROUPS = S5_WIDTH // S5_GROUP
S5_STATE = 64
ATTN_HEADS = 8
HEAD_DIM = 64
ATTN_WIDTH = ATTN_HEADS * HEAD_DIM
Q_BLOCK = 128
FFN_HIDDEN = int(math.ceil(8 * D_MODEL / 3 / 256)) * 256
N_ADA = 6
RMS_EPS = 1e-6
IN_SIZES = (S5_WIDTH, ATTN_WIDTH, ATTN_WIDTH, ATTN_WIDTH, D_MODEL, D_MODEL)
IN_SPLITS = tuple(int(s) for s in np.cumsum(IN_SIZES)[:-1])
IN_WIDTH = int(sum(IN_SIZES))

kernel_name = "hybrid_s5_stickbreaking_adaln_block"


def rmsnorm(x, g):
    xf = x.astype(jnp.float32)
    xf = xf * lax.rsqrt(jnp.mean(xf * xf, axis=-1, keepdims=True) + RMS_EPS)
    return (xf * g.astype(jnp.float32)).astype(x.dtype)


def modulate(h, shift, scale):
    return h * (1.0 + scale) + shift


def _linear_recurrence(e1, e2):
    a1, b1 = e1
    a2, b2 = e2
    return a2 * a1, a2 * b1 + b2


def s5_branch(u, lam_re, lam_im, log_dt, b_re, b_im, c_re, c_im, d_skip, w_glu, b_glu):
    bsz, seq, _ = u.shape
    uf = u.astype(jnp.float32).reshape(bsz, seq, S5_GROUPS, S5_GROUP)
    lam = lax.complex(lam_re.astype(jnp.float32), lam_im.astype(jnp.float32))
    dt = jnp.exp(log_dt.astype(jnp.float32))[:, None]
    lam_bar = jnp.exp(lam * dt)
    b = lax.complex(b_re.astype(jnp.float32), b_im.astype(jnp.float32))
    b_bar = ((lam_bar - 1.0) / lam)[..., None] * b
    bu = jnp.einsum('btgi,gpi->btgp', uf.astype(jnp.complex64), b_bar)
    a = jnp.broadcast_to(lam_bar, bu.shape)
    _, states = lax.associative_scan(_linear_recurrence, (a, bu), axis=1)
    cmat = lax.complex(c_re.astype(jnp.float32), c_im.astype(jnp.float32))
    y = jnp.einsum('btgp,gip->btgi', states, cmat).real + d_skip.astype(jnp.float32) * uf
    y = jax.nn.gelu(y.reshape(bsz, seq, S5_WIDTH))
    y = y * jax.nn.sigmoid(y @ w_glu.astype(jnp.float32) + b_glu.astype(jnp.float32))
    return y.astype(u.dtype)


def stick_breaking_attention(q, k, v):
    bsz, seq, _ = q.shape
    n_blocks = seq // Q_BLOCK
    scale = 1.0 / math.sqrt(HEAD_DIM)
    qh = q.astype(jnp.float32).reshape(bsz, n_blocks, Q_BLOCK, ATTN_HEADS, HEAD_DIM)
    qh = qh.transpose(1, 0, 3, 2, 4)
    kh = k.astype(jnp.float32).reshape(bsz, seq, ATTN_HEADS, HEAD_DIM).transpose(0, 2, 1, 3)
    vh = v.astype(jnp.float32).reshape(bsz, seq, ATTN_HEADS, HEAD_DIM).transpose(0, 2, 1, 3)
    starts = jnp.arange(n_blocks, dtype=jnp.int32) * Q_BLOCK
    key_pos = jnp.arange(seq, dtype=jnp.int32)[None, :]

    def one_block(args):
        qb, t0 = args
        z = jnp.einsum('bhqd,bhkd->bhqk', qb, kh) * scale
        q_pos = t0 + jnp.arange(Q_BLOCK, dtype=jnp.int32)[:, None]
        mask = key_pos < q_pos
        log_not = jnp.where(mask, jax.nn.log_sigmoid(-z), 0.0)
        suffix = lax.cumsum(log_not, axis=3, reverse=True) - log_not
        weights = jnp.where(mask, jnp.exp(jax.nn.log_sigmoid(z) + suffix), 0.0)
        return jnp.einsum('bhqk,bhkd->bhqd', weights, vh)

    out = lax.map(one_block, (qh, starts))
    out = out.transpose(1, 0, 3, 2, 4).reshape(bsz, seq, ATTN_WIDTH)
    return out.astype(q.dtype)


def setup_inputs(seed: int = 0) -> dict:
    key = jax.random.key(seed)
    ks = jax.random.split(key, 26)
    f32 = jnp.float32
    D, L, G, P, GS = D_MODEL, DEPTH, S5_GROUPS, S5_STATE, S5_GROUP

    def nrm(k, shape, s):
        return jax.random.normal(k, shape, f32) * s

    return {
        "x": nrm(ks[0], (BATCH, SEQ, D), 1.0),
        "c": nrm(ks[1], (BATCH, D), 1.0),
        "w_ada": nrm(ks[2], (L, D, N_ADA * D), 0.1 * D ** -0.5),
        "b_ada": nrm(ks[3], (L, N_ADA * D), 0.1),
        "norm1_g": 1.0 + nrm(ks[4], (L, D), 0.01),
        "w_in": nrm(ks[5], (L, D, IN_WIDTH), D ** -0.5),
        "lam_re": -0.5 + nrm(ks[6], (L, G, P), 0.01),
        "lam_im": jnp.pi * jnp.arange(P, dtype=f32)[None, None, :] + nrm(ks[7], (L, G, P), 0.01),
        "log_dt": jax.random.uniform(ks[8], (L, G), f32, math.log(1e-3), math.log(1e-1)),
        "b_re": nrm(ks[9], (L, G, P, GS), (2.0 * GS) ** -0.5),
        "b_im": nrm(ks[10], (L, G, P, GS), (2.0 * GS) ** -0.5),
        "c_re": nrm(ks[11], (L, G, GS, P), (2.0 * P) ** -0.5),
        "c_im": nrm(ks[12], (L, G, GS, P), (2.0 * P) ** -0.5),
        "d_skip": nrm(ks[13], (L, G, GS), 1.0),
        "w_glu": nrm(ks[14], (L, S5_WIDTH, S5_WIDTH), S5_WIDTH ** -0.5),
        "b_glu": nrm(ks[15], (L, S5_WIDTH), 0.02),
        "w_a": nrm(ks[16], (L, S5_WIDTH, D), S5_WIDTH ** -0.5),
        "w_b": nrm(ks[17], (L, ATTN_WIDTH, D), ATTN_WIDTH ** -0.5),
        "w_o": nrm(ks[18], (L, D, D), D ** -0.5),
        "norm2_g": 1.0 + nrm(ks[19], (L, D), 0.01),
        "w_ffn_gate": nrm(ks[20], (L, D, FFN_HIDDEN), D ** -0.5),
        "w_ffn_up": nrm(ks[21], (L, D, FFN_HIDDEN), D ** -0.5),
        "w_ffn_down": nrm(ks[22], (L, FFN_HIDDEN, D), FFN_HIDDEN ** -0.5),
        "norm_f_g": 1.0 + nrm(ks[23], (D,), 0.01),
    }


def reference(x, c, w_ada, b_ada, norm1_g, w_in, lam_re, lam_im, log_dt, b_re, b_im, c_re, c_im,
              d_skip, w_glu, b_glu, w_a, w_b, w_o, norm2_g, w_ffn_gate, w_ffn_up, w_ffn_down, norm_f_g):
    cond = jax.nn.silu(c)
    for l in range(DEPTH):
        mod = (cond @ w_ada[l] + b_ada[l])[:, None, :]
        sh1, sc1, g1, sh2, sc2, g2 = jnp.split(mod, N_ADA, axis=-1)

        h = modulate(rmsnorm(x, norm1_g[l]), sh1, sc1)
        proj = h @ w_in[l]
        u, q, k, v, gate_a, gate_b = jnp.split(proj, IN_SPLITS, axis=-1)
        y_a = s5_branch(u, lam_re[l], lam_im[l], log_dt[l], b_re[l], b_im[l], c_re[l], c_im[l],
                        d_skip[l], w_glu[l], b_glu[l]) @ w_a[l]
        y_b = stick_breaking_attention(q, k, v) @ w_b[l]
        merged = (jax.nn.sigmoid(gate_a) * y_a + jax.nn.sigmoid(gate_b) * y_b) @ w_o[l]
        x = x + g1 * merged

        h = modulate(rmsnorm(x, norm2_g[l]), sh2, sc2)
        ffn = (jax.nn.silu(h @ w_ffn_gate[l]) * (h @ w_ffn_up[l])) @ w_ffn_down[l]
        x = x + g2 * ffn
    return rmsnorm(x, norm_f_g)
```

```python
import functools
import math

import numpy as np
import jax
import jax.numpy as jnp
from jax import lax
from jax.experimental import pallas as pl
from jax.experimental.pallas import tpu as pltpu

F32 = jnp.float32
BF16 = jnp.bfloat16

S5_GROUP = 16
S5_STATE = 64
HEAD_DIM = 64
N_ADA = 6
RMS_EPS = 1e-6

LANES = 128
SUBLANES = 8
VMEM_LIMIT_BYTES = 56 * 1024 * 1024

ATTN_BLOCK = 128
S5_TILE = 128
IN_TILE = 256
OUT_TILE = 256


def _dot(a, b):
    return jnp.dot(a, b, preferred_element_type=F32)


def _rms(x):
    return x * lax.rsqrt(jnp.mean(x * x, axis=-1, keepdims=True) + RMS_EPS)


def _ada_kernel(c_ref, w_ref, b_ref, o_ref):
    c = c_ref[...]
    cond = c * jax.nn.sigmoid(c)
    o_ref[...] = jnp.dot(cond, w_ref[...], preferred_element_type=F32,
                         precision=lax.Precision.HIGHEST) + b_ref[...]


def _ada(c, w_ada, b_ada):
    bsz, d = c.shape
    n = w_ada.shape[1]
    rows = -(-bsz // SUBLANES) * SUBLANES
    cp = jnp.zeros((rows, d), F32).at[:bsz].set(c)
    tn = 1536
    out = pl.pallas_call(
        _ada_kernel,
        grid=(n // tn,),
        in_specs=[pl.BlockSpec((rows, d), lambda j: (0, 0)),
                  pl.BlockSpec((d, tn), lambda j: (0, j)),
                  pl.BlockSpec((1, tn), lambda j: (0, j))],
        out_specs=pl.BlockSpec((rows, tn), lambda j: (0, j)),
        out_shape=jax.ShapeDtypeStruct((rows, n), F32),
        name="ada",
    )(cp, w_ada, b_ada.reshape(1, n))
    return out[:bsz]


def _in_proj_kernel(x_ref, mod_ref, g_ref, w_ref, u_ref, q_ref, k_ref, v_ref, ga_ref, gb_ref,
                    *, sw, aw, d):
    x = x_ref[0]
    mod = mod_ref[0]
    h = _rms(x) * g_ref[...]
    h = h * (1.0 + mod[1:2]) + mod[0:1]
    hb = h.astype(BF16)
    o = 0
    u_ref[0] = _dot(hb, w_ref[:, o:o + sw]); o += sw
    q_ref[0] = (_dot(hb, w_ref[:, o:o + aw]) * (1.0 / math.sqrt(HEAD_DIM))).astype(BF16); o += aw
    k_ref[0] = _dot(hb, w_ref[:, o:o + aw]).astype(BF16); o += aw
    v_ref[0] = _dot(hb, w_ref[:, o:o + aw]).astype(BF16); o += aw
    ga_ref[0] = _dot(hb, w_ref[:, o:o + d]); o += d
    gb_ref[0] = _dot(hb, w_ref[:, o:o + d])


def _in_proj(x, mod, norm_g, w_in_b, sw, aw):
    bsz, seq, d = x.shape
    tm = IN_TILE
    n = w_in_b.shape[1]
    tok = lambda w: pl.BlockSpec((1, tm, w), lambda b, i: (b, i, 0))
    return pl.pallas_call(
        functools.partial(_in_proj_kernel, sw=sw, aw=aw, d=d),
        grid=(bsz, seq // tm),
        in_specs=[tok(d),
                  pl.BlockSpec((1, N_ADA, d), lambda b, i: (b, 0, 0)),
                  pl.BlockSpec((1, d), lambda b, i: (0, 0)),
                  pl.BlockSpec((d, n), lambda b, i: (0, 0))],
        out_specs=[tok(sw), tok(aw), tok(aw), tok(aw), tok(d), tok(d)],
        out_shape=[jax.ShapeDtypeStruct((bsz, seq, sw), F32),
                   jax.ShapeDtypeStruct((bsz, seq, aw), BF16),
                   jax.ShapeDtypeStruct((bsz, seq, aw), BF16),
                   jax.ShapeDtypeStruct((bsz, seq, aw), BF16),
                   jax.ShapeDtypeStruct((bsz, seq, d), F32),
                   jax.ShapeDtypeStruct((bsz, seq, d), F32)],
        compiler_params=pltpu.CompilerParams(
            dimension_semantics=("arbitrary", "arbitrary"),
            vmem_limit_bytes=VMEM_LIMIT_BYTES),
        name="in_proj",
    )(x, mod, norm_g, w_in_b)


def _s5_kernel(u_ref, perm_ref, permt_ref, wb_ref, cm_ref, lamr_ref, lami_ref, d_ref,
               wglu_ref, bglu_ref, o_ref, bu_scr, x_scr, ulast_scr, *, tm, sw):
    i = pl.program_id(0)
    n2 = tm // 2
    rows = SUBLANES * n2
    half = sw // 2
    hs = half * S5_STATE // S5_GROUP

    @pl.when(i == 0)
    def _():
        x_scr[...] = jnp.zeros_like(x_scr)
        ulast_scr[...] = jnp.zeros_like(ulast_scr)

    u_nat = u_ref[...].reshape(4 * tm, sw)
    ap = _dot(perm_ref[...], u_nat.astype(BF16))
    a_cur = ap[:rows]
    a_prev = ap[rows:]
    first = a_prev[0:SUBLANES] + ulast_scr[...]
    a_prev = jnp.concatenate([first, a_prev[SUBLANES:]], axis=0)

    sub = lax.broadcasted_iota(jnp.int32, (SUBLANES, sw), 0)
    last = jnp.zeros((SUBLANES, sw), F32)
    for b in range(4):
        row = u_ref[b, tm - 1:tm, :].astype(BF16).astype(F32)
        last = jnp.where(sub == 2 * b, jnp.broadcast_to(row, (SUBLANES, sw)), last)
    ulast_scr[...] = last

    a_cur = a_cur.astype(BF16)
    a_prev = a_prev.astype(BF16)
    for h in range(2):
        lhs = jnp.concatenate([a_cur[:, h * half:(h + 1) * half],
                               a_prev[:, h * half:(h + 1) * half]], axis=1)
        bu_scr[:, 2 * hs * h:2 * hs * (h + 1)] = _dot(lhs, wb_ref[h])

    for h in range(2):
        re0 = 2 * hs * h
        im0 = re0 + hs
        ar = lamr_ref[:, hs * h:hs * (h + 1)]
        ai = lami_ref[:, hs * h:hs * (h + 1)]

        def step(t2, carry, re0=re0, im0=im0, ar=ar, ai=ai):
            xr, xi = carry
            r0 = pl.multiple_of(t2 * SUBLANES, SUBLANES)
            br = bu_scr[pl.ds(r0, SUBLANES), re0:re0 + hs]
            bi = bu_scr[pl.ds(r0, SUBLANES), im0:im0 + hs]
            nr = ar * xr - ai * xi + br
            ni = ar * xi + ai * xr + bi
            bu_scr[pl.ds(r0, SUBLANES), re0:re0 + hs] = nr
            bu_scr[pl.ds(r0, SUBLANES), im0:im0 + hs] = ni
            return nr, ni

        xr, xi = lax.fori_loop(0, n2, step, (x_scr[h, 0], x_scr[h, 1]), unroll=4)
        x_scr[h, 0] = xr
        x_scr[h, 1] = xi

    ys = []
    for h in range(2):
        st = bu_scr[:, 2 * hs * h:2 * hs * (h + 1)].astype(BF16)
        ys.append(_dot(st, cm_ref[h]))
    y_il = jnp.concatenate(ys, axis=1)
    hi = y_il.astype(BF16)
    lo = (y_il - hi.astype(F32)).astype(BF16)
    y = _dot(permt_ref[...], jnp.concatenate([hi, lo], axis=0))
    y = y + d_ref[...] * u_nat
    y = jax.nn.gelu(y)
    z = _dot(y.astype(BF16), wglu_ref[...]) + bglu_ref[...]
    o_ref[...] = (y * jax.nn.sigmoid(z)).reshape(4, tm, sw)


def _s5_perms(tm):
    n2 = tm // 2
    rows = SUBLANES * n2
    perm = np.zeros((2 * rows, 4 * tm), np.float32)
    permt = np.zeros((4 * tm, 2 * rows), np.float32)
    for t2 in range(n2):
        for b in range(4):
            for par in range(2):
                r = SUBLANES * t2 + 2 * b + par
                t = 2 * t2 + par
                perm[r, b * tm + t] = 1.0
                if t >= 1:
                    perm[rows + r, b * tm + t - 1] = 1.0
                permt[b * tm + t, r] = 1.0
                permt[b * tm + t, rows + r] = 1.0
    return jnp.asarray(perm, BF16), jnp.asarray(permt, BF16)


def _block_diag(blocks):
    n, a, b = blocks.shape
    eye = jnp.eye(n, dtype=blocks.dtype)
    return jnp.einsum('nab,nm->namb', blocks, eye).reshape(n * a, n * b)


def _s5_weights(lam_re, lam_im, log_dt, b_re, b_im, c_re, c_im):
    g = lam_re.shape[0]
    dt = jnp.exp(log_dt)[:, None]
    mag = jnp.exp(lam_re * dt)
    lbr = mag * jnp.cos(lam_im * dt)
    lbi = mag * jnp.sin(lam_im * dt)
    nr, ni = lbr - 1.0, lbi
    den = lam_re * lam_re + lam_im * lam_im
    cr = (nr * lam_re + ni * lam_im) / den
    ci = (ni * lam_re - nr * lam_im) / den
    bbr = cr[..., None] * b_re - ci[..., None] * b_im
    bbi = cr[..., None] * b_im + ci[..., None] * b_re
    lr = lbr[..., None] * bbr - lbi[..., None] * bbi
    li = lbr[..., None] * bbi + lbi[..., None] * bbr
    l2r = lbr * lbr - lbi * lbi
    l2i = 2.0 * lbr * lbi
    gh = g // 2
    wbs, cms = [], []
    for h in range(2):
        s = slice(h * gh, (h + 1) * gh)
        t = lambda a: jnp.swapaxes(a[s], 1, 2)
        top = jnp.concatenate([_block_diag(t(bbr)), _block_diag(t(bbi))], axis=1)
        bot = jnp.concatenate([_block_diag(t(lr)), _block_diag(t(li))], axis=1)
        wbs.append(jnp.concatenate([top, bot], axis=0))
        cms.append(jnp.concatenate([_block_diag(jnp.swapaxes(c_re[s], 1, 2)),
                                    -_block_diag(jnp.swapaxes(c_im[s], 1, 2))], axis=0))
    wb = jnp.stack(wbs).astype(BF16)
    cm = jnp.stack(cms).astype(BF16)
    lamr = jnp.broadcast_to(l2r.reshape(1, -1), (SUBLANES, l2r.size))
    lami = jnp.broadcast_to(l2i.reshape(1, -1), (SUBLANES, l2i.size))
    return wb, cm, lamr, lami


def _s5(u, wb, cm, lamr, lami, d_skip, w_glu_b, b_glu):
    bsz, seq, sw = u.shape
    assert bsz == 4, "the scan packs 4 batch rows x 2 token parities into 8 sublanes"
    tm = S5_TILE
    n2 = tm // 2
    rows = SUBLANES * n2
    ns = lamr.shape[1]
    perm, permt = _s5_perms(tm)
    const = lambda a: pl.BlockSpec(a.shape, lambda i: (0,) * a.ndim)
    d_row = d_skip.reshape(1, sw)
    bg = b_glu.reshape(1, sw)
    return pl.pallas_call(
        functools.partial(_s5_kernel, tm=tm, sw=sw),
        grid=(seq // tm,),
        in_specs=[pl.BlockSpec((4, tm, sw), lambda i: (0, i, 0)),
                  const(perm), const(permt), const(wb), const(cm), const(lamr), const(lami),
                  const(d_row), const(w_glu_b), const(bg)],
        out_specs=pl.BlockSpec((4, tm, sw), lambda i: (0, i, 0)),
        out_shape=jax.ShapeDtypeStruct((bsz, seq, sw), F32),
        scratch_shapes=[pltpu.VMEM((rows, 2 * ns), F32),
                        pltpu.VMEM((2, 2, SUBLANES, ns // 2), F32),
                        pltpu.VMEM((SUBLANES, sw), F32)],
        compiler_params=pltpu.CompilerParams(
            dimension_semantics=("arbitrary",),
            vmem_limit_bytes=VMEM_LIMIT_BYTES),
        name="s5",
    )(u, perm, permt, wb, cm, lamr, lami, d_row, w_glu_b, bg)


def _attn_kernel(q_ref, k_ref, v_ref, tri_ref, o_ref, c_scr, acc_scr):
    blk = ATTN_BLOCK
    qi = pl.program_id(2)
    q2 = q_ref[0]
    lane = lax.broadcasted_iota(jnp.int32, (blk, LANES), 1)
    head0 = lane < HEAD_DIM
    zero = jnp.zeros_like(q2)
    qh = (jnp.where(head0, q2, zero), jnp.where(head0, zero, q2))
    row = lax.broadcasted_iota(jnp.int32, (blk, blk), 0)
    col = lax.broadcasted_iota(jnp.int32, (blk, blk), 1)
    causal = col < row

    def block(kb, valid):
        start = pl.multiple_of(kb * blk, blk)
        kblk = k_ref[0, pl.ds(start, blk), :]
        vblk = v_ref[0, pl.ds(start, blk), :]
        for hh in range(2):
            z = lax.dot_general(qh[hh], kblk, (((1,), (1,)), ((), ())),
                                preferred_element_type=F32)
            sp = jnp.maximum(z, 0.0) + jnp.log(1.0 + jnp.exp(-jnp.abs(z)))
            if valid is not None:
                sp = jnp.where(valid, sp, 0.0)
            hi = sp.astype(BF16)
            lo = (sp - hi.astype(F32)).astype(BF16)
            cs = _dot(jnp.concatenate([hi, lo], axis=1), tri_ref[...])
            w = jnp.exp(z - sp - cs[:, :blk] - c_scr[hh])
            if valid is not None:
                w = jnp.where(valid, w, 0.0)
            acc_scr[hh] = acc_scr[hh] + _dot(w.astype(BF16), vblk)
            c_scr[hh] = c_scr[hh] + cs[:, blk:]

    c_scr[...] = jnp.zeros_like(c_scr)
    acc_scr[...] = jnp.zeros_like(acc_scr)
    block(qi, causal)

    def body(j, carry):
        block(qi - 1 - j, None)
        return carry

    lax.fori_loop(0, qi, body, 0)
    o_ref[0] = jnp.where(head0, acc_scr[0], acc_scr[1]).astype(o_ref.dtype)


def _attn_tri():
    blk = ATTN_BLOCK
    m = np.arange(blk)[:, None]
    j = np.arange(blk)[None, :]
    u2 = np.concatenate([(m > j).astype(np.float32), np.ones((blk, blk), np.float32)], axis=1)
    return jnp.asarray(np.concatenate([u2, u2], axis=0), BF16)


def _attn(q, k, v):
    bsz, seq, aw = q.shape
    blk = ATTN_BLOCK
    tri = _attn_tri()
    return pl.pallas_call(
        _attn_kernel,
        grid=(bsz, aw // LANES, seq // blk),
        in_specs=[pl.BlockSpec((1, blk, LANES), lambda b, p, i: (b, i, p)),
                  pl.BlockSpec((1, seq, LANES), lambda b, p, i: (b, 0, p)),
                  pl.BlockSpec((1, seq, LANES), lambda b, p, i: (b, 0, p)),
                  pl.BlockSpec(tri.shape, lambda b, p, i: (0, 0))],
        out_specs=pl.BlockSpec((1, blk, LANES), lambda b, p, i: (b, i, p)),
        out_shape=jax.ShapeDtypeStruct((bsz, seq, aw), BF16),
        scratch_shapes=[pltpu.VMEM((2, blk, blk), F32),
                        pltpu.VMEM((2, blk, LANES), F32)],
        compiler_params=pltpu.CompilerParams(
            dimension_semantics=("arbitrary", "arbitrary", "arbitrary"),
            vmem_limit_bytes=VMEM_LIMIT_BYTES),
        name="attn",
    )(q, k, v, tri)


def _out_ffn_kernel(x_ref, s5_ref, at_ref, ga_ref, gb_ref, mod_ref, n2_ref, nf_ref,
                    wa_ref, wb_ref, wo_ref, wg_ref, wu_ref, wd_ref, o_ref, *, final_norm):
    mod = mod_ref[0]
    ya = _dot(s5_ref[0].astype(BF16), wa_ref[...])
    yb = _dot(at_ref[0], wb_ref[...])
    m = jax.nn.sigmoid(ga_ref[0]) * ya + jax.nn.sigmoid(gb_ref[0]) * yb
    x1 = x_ref[0] + mod[2:3] * _dot(m.astype(BF16), wo_ref[...])
    h = _rms(x1) * n2_ref[...]
    h = (h * (1.0 + mod[4:5]) + mod[3:4]).astype(BF16)
    gate = _dot(h, wg_ref[...])
    up = _dot(h, wu_ref[...])
    act = (gate * jax.nn.sigmoid(gate) * up).astype(BF16)
    x2 = x1 + mod[5:6] * _dot(act, wd_ref[...])
    o_ref[0] = _rms(x2) * nf_ref[...] if final_norm else x2


def _out_ffn(x, s5o, attn, ga, gb, mod, n2g, nfg, wa, wb, wo, wg, wu, wd, final_norm):
    bsz, seq, d = x.shape
    tm = OUT_TILE
    tok = lambda a: pl.BlockSpec((1, tm, a.shape[-1]), lambda b, i: (b, i, 0))
    const = lambda a: pl.BlockSpec(a.shape, lambda b, i: (0,) * a.ndim,
                                   pipeline_mode=pl.Buffered(1))
    return pl.pallas_call(
        functools.partial(_out_ffn_kernel, final_norm=final_norm),
        grid=(bsz, seq // tm),
        in_specs=[tok(x), tok(s5o), tok(attn), tok(ga), tok(gb),
                  pl.BlockSpec((1, N_ADA, d), lambda b, i: (b, 0, 0)),
                  const(n2g), const(nfg),
                  const(wa), const(wb), const(wo), const(wg), const(wu), const(wd)],
        out_specs=pl.BlockSpec((1, tm, d), lambda b, i: (b, i, 0)),
        out_shape=jax.ShapeDtypeStruct((bsz, seq, d), F32),
        compiler_params=pltpu.CompilerParams(
            dimension_semantics=("arbitrary", "arbitrary"),
            vmem_limit_bytes=VMEM_LIMIT_BYTES),
        name="out_ffn",
    )(x, s5o, attn, ga, gb, mod, n2g, nfg, wa, wb, wo, wg, wu, wd)


def kernel(x, c, w_ada, b_ada, norm1_g, w_in, lam_re, lam_im, log_dt, b_re, b_im, c_re, c_im,
           d_skip, w_glu, b_glu, w_a, w_b, w_o, norm2_g, w_ffn_gate, w_ffn_up, w_ffn_down,
           norm_f_g):
    depth = w_ada.shape[0]
    bsz, seq, d = x.shape
    sw = w_glu.shape[1]
    aw = w_b.shape[1]
    for l in range(depth):
        mod = _ada(c, w_ada[l], b_ada[l]).reshape(bsz, N_ADA, d)
        u, q, k, v, ga, gb = _in_proj(x, mod, norm1_g[l].reshape(1, d), w_in[l].astype(BF16),
                                      sw, aw)
        wb, cm, lamr, lami = _s5_weights(lam_re[l], lam_im[l], log_dt[l], b_re[l], b_im[l],
                                         c_re[l], c_im[l])
        s5o = _s5(u, wb, cm, lamr, lami, d_skip[l], w_glu[l].astype(BF16), b_glu[l])
        attn = _attn(q, k, v)
        x = _out_ffn(x, s5o, attn, ga, gb, mod, norm2_g[l].reshape(1, d), norm_f_g.reshape(1, d),
                     w_a[l].astype(BF16), w_b[l].astype(BF16), w_o[l].astype(BF16),
                     w_ffn_gate[l].astype(BF16), w_ffn_up[l].astype(BF16),
                     w_ffn_down[l].astype(BF16), final_norm=(l == depth - 1))
    return x
```

```python
import functools
import math

import numpy as np
import jax
import jax.numpy as jnp
from jax import lax
from jax.experimental import pallas as pl
from jax.experimental.pallas import tpu as pltpu

F32 = jnp.float32
BF16 = jnp.bfloat16

S5_GROUP = 16
S5_STATE = 64
HEAD_DIM = 64
N_ADA = 6
RMS_EPS = 1e-6
Q_SCALE = math.log2(math.e) / math.sqrt(HEAD_DIM)

LANES = 128
SUBLANES = 8
VMEM_LIMIT_BYTES = 56 * 1024 * 1024

ATTN_BLOCK = 128
S5_TILE = 128
IN_TILE = 256
OUT_TILE = 256


def _dot(a, b):
    return jnp.dot(a, b, preferred_element_type=F32)


def _rms(x):
    return x * lax.rsqrt(jnp.mean(x * x, axis=-1, keepdims=True) + RMS_EPS)


def _ada_kernel(c_ref, w_ref, b_ref, o_ref):
    c = c_ref[...]
    cond = c * jax.nn.sigmoid(c)
    o_ref[...] = jnp.dot(cond, w_ref[...], preferred_element_type=F32,
                         precision=lax.Precision.HIGHEST) + b_ref[...]


def _ada(c, w_ada, b_ada):
    bsz, d = c.shape
    n = w_ada.shape[1]
    rows = -(-bsz // SUBLANES) * SUBLANES
    cp = jnp.zeros((rows, d), F32).at[:bsz].set(c)
    tn = 1536
    out = pl.pallas_call(
        _ada_kernel,
        grid=(n // tn,),
        in_specs=[pl.BlockSpec((rows, d), lambda j: (0, 0)),
                  pl.BlockSpec((d, tn), lambda j: (0, j)),
                  pl.BlockSpec((1, tn), lambda j: (0, j))],
        out_specs=pl.BlockSpec((rows, tn), lambda j: (0, j)),
        out_shape=jax.ShapeDtypeStruct((rows, n), F32),
        name="ada",
    )(cp, w_ada, b_ada.reshape(1, n))
    return out[:bsz]


def _in_proj_kernel(x_ref, mod_ref, g_ref, w_ref, u_ref, q_ref, k_ref, v_ref, ga_ref, gb_ref,
                    *, sw, aw, d):
    x = x_ref[0]
    mod = mod_ref[0]
    h = _rms(x) * g_ref[...]
    h = h * (1.0 + mod[1:2]) + mod[0:1]
    hb = h.astype(BF16)
    o = 0
    lane = lax.broadcasted_iota(jnp.int32, (x.shape[0], aw), 1)
    even_head = (lane // HEAD_DIM) % 2 == 0
    u_ref[0] = _dot(hb, w_ref[:, o:o + sw]); o += sw
    q = (_dot(hb, w_ref[:, o:o + aw]) * Q_SCALE).astype(BF16); o += aw
    q_ref[0, 0] = jnp.where(even_head, q, jnp.zeros_like(q))
    q_ref[0, 1] = jnp.where(even_head, jnp.zeros_like(q), q)
    k_ref[0] = _dot(hb, w_ref[:, o:o + aw]).astype(BF16); o += aw
    v = _dot(hb, w_ref[:, o:o + aw]).astype(BF16); o += aw
    v_ref[0, 0] = jnp.where(even_head, v, jnp.zeros_like(v))
    v_ref[0, 1] = jnp.where(even_head, jnp.zeros_like(v), v)
    ga_ref[0] = _dot(hb, w_ref[:, o:o + d]); o += d
    gb_ref[0] = _dot(hb, w_ref[:, o:o + d])


def _in_proj(x, mod, norm_g, w_in_b, sw, aw):
    bsz, seq, d = x.shape
    tm = IN_TILE
    n = w_in_b.shape[1]
    tok = lambda w: pl.BlockSpec((1, tm, w), lambda b, i: (b, i, 0))
    tok2 = lambda w: pl.BlockSpec((1, 2, tm, w), lambda b, i: (b, 0, i, 0))
    return pl.pallas_call(
        functools.partial(_in_proj_kernel, sw=sw, aw=aw, d=d),
        grid=(bsz, seq // tm),
        in_specs=[tok(d),
                  pl.BlockSpec((1, N_ADA, d), lambda b, i: (b, 0, 0)),
                  pl.BlockSpec((1, d), lambda b, i: (0, 0)),
                  pl.BlockSpec((d, n), lambda b, i: (0, 0))],
        out_specs=[tok(sw), tok2(aw), tok(aw), tok2(aw), tok(d), tok(d)],
        out_shape=[jax.ShapeDtypeStruct((bsz, seq, sw), F32),
                   jax.ShapeDtypeStruct((bsz, 2, seq, aw), BF16),
                   jax.ShapeDtypeStruct((bsz, seq, aw), BF16),
                   jax.ShapeDtypeStruct((bsz, 2, seq, aw), BF16),
                   jax.ShapeDtypeStruct((bsz, seq, d), F32),
                   jax.ShapeDtypeStruct((bsz, seq, d), F32)],
        compiler_params=pltpu.CompilerParams(
            dimension_semantics=("arbitrary", "arbitrary"),
            vmem_limit_bytes=VMEM_LIMIT_BYTES),
        name="in_proj",
    )(x, mod, norm_g, w_in_b)


def _s5_kernel(u_ref, perm_ref, permt_ref, wb_ref, cm_ref, lamr_ref, lami_ref, d_ref,
               wglu_ref, bglu_ref, o_ref, bu_scr, x_scr, ulast_scr, *, tm, sw):
    i = pl.program_id(0)
    n2 = tm // 2
    rows = SUBLANES * n2
    half = sw // 2
    hs = half * S5_STATE // S5_GROUP

    @pl.when(i == 0)
    def _():
        x_scr[...] = jnp.zeros_like(x_scr)
        ulast_scr[...] = jnp.zeros_like(ulast_scr)

    u_nat = u_ref[...].reshape(4 * tm, sw)
    ap = _dot(perm_ref[...], u_nat.astype(BF16))
    a_cur = ap[:rows]
    a_prev = ap[rows:]
    first = a_prev[0:SUBLANES] + ulast_scr[...]
    a_prev = jnp.concatenate([first, a_prev[SUBLANES:]], axis=0)

    sub = lax.broadcasted_iota(jnp.int32, (SUBLANES, sw), 0)
    last = jnp.zeros((SUBLANES, sw), F32)
    for b in range(4):
        row = u_ref[b, tm - 1:tm, :].astype(BF16).astype(F32)
        last = jnp.where(sub == 2 * b, jnp.broadcast_to(row, (SUBLANES, sw)), last)
    ulast_scr[...] = last

    a_cur = a_cur.astype(BF16)
    a_prev = a_prev.astype(BF16)
    for h in range(2):
        lhs = jnp.concatenate([a_cur[:, h * half:(h + 1) * half],
                               a_prev[:, h * half:(h + 1) * half]], axis=1)
        bu_scr[:, 2 * hs * h:2 * hs * (h + 1)] = _dot(lhs, wb_ref[h])

    for h in range(2):
        re0 = 2 * hs * h
        im0 = re0 + hs
        ar = lamr_ref[:, hs * h:hs * (h + 1)]
        ai = lami_ref[:, hs * h:hs * (h + 1)]

        def step(t2, carry, re0=re0, im0=im0, ar=ar, ai=ai):
            xr, xi = carry
            r0 = pl.multiple_of(t2 * SUBLANES, SUBLANES)
            br = bu_scr[pl.ds(r0, SUBLANES), re0:re0 + hs]
            bi = bu_scr[pl.ds(r0, SUBLANES), im0:im0 + hs]
            nr = ar * xr - ai * xi + br
            ni = ar * xi + ai * xr + bi
            bu_scr[pl.ds(r0, SUBLANES), re0:re0 + hs] = nr
            bu_scr[pl.ds(r0, SUBLANES), im0:im0 + hs] = ni
            return nr, ni

        xr, xi = lax.fori_loop(0, n2, step, (x_scr[h, 0], x_scr[h, 1]), unroll=4)
        x_scr[h, 0] = xr
        x_scr[h, 1] = xi

    ys = []
    for h in range(2):
        st = bu_scr[:, 2 * hs * h:2 * hs * (h + 1)].astype(BF16)
        ys.append(_dot(st, cm_ref[h]))
    y_il = jnp.concatenate(ys, axis=1)
    hi = y_il.astype(BF16)
    lo = (y_il - hi.astype(F32)).astype(BF16)
    y = _dot(permt_ref[...], jnp.concatenate([hi, lo], axis=0))
    y = y + d_ref[...] * u_nat
    y = jax.nn.gelu(y)
    z = _dot(y.astype(BF16), wglu_ref[...]) + bglu_ref[...]
    o_ref[...] = (y * jax.nn.sigmoid(z)).reshape(4, tm, sw)


def _s5_perms(tm):
    n2 = tm // 2
    rows = SUBLANES * n2
    perm = np.zeros((2 * rows, 4 * tm), np.float32)
    permt = np.zeros((4 * tm, 2 * rows), np.float32)
    for t2 in range(n2):
        for b in range(4):
            for par in range(2):
                r = SUBLANES * t2 + 2 * b + par
                t = 2 * t2 + par
                perm[r, b * tm + t] = 1.0
                if t >= 1:
                    perm[rows + r, b * tm + t - 1] = 1.0
                permt[b * tm + t, r] = 1.0
                permt[b * tm + t, rows + r] = 1.0
    return jnp.asarray(perm, BF16), jnp.asarray(permt, BF16)


def _block_diag(blocks):
    n, a, b = blocks.shape
    eye = jnp.eye(n, dtype=blocks.dtype)
    return jnp.einsum('nab,nm->namb', blocks, eye).reshape(n * a, n * b)


def _s5_weights(lam_re, lam_im, log_dt, b_re, b_im, c_re, c_im):
    g = lam_re.shape[0]
    dt = jnp.exp(log_dt)[:, None]
    mag = jnp.exp(lam_re * dt)
    lbr = mag * jnp.cos(lam_im * dt)
    lbi = mag * jnp.sin(lam_im * dt)
    nr, ni = lbr - 1.0, lbi
    den = lam_re * lam_re + lam_im * lam_im
    cr = (nr * lam_re + ni * lam_im) / den
    ci = (ni * lam_re - nr * lam_im) / den
    bbr = cr[..., None] * b_re - ci[..., None] * b_im
    bbi = cr[..., None] * b_im + ci[..., None] * b_re
    lr = lbr[..., None] * bbr - lbi[..., None] * bbi
    li = lbr[..., None] * bbi + lbi[..., None] * bbr
    l2r = lbr * lbr - lbi * lbi
    l2i = 2.0 * lbr * lbi
    gh = g // 2
    wbs, cms = [], []
    for h in range(2):
        s = slice(h * gh, (h + 1) * gh)
        t = lambda a: jnp.swapaxes(a[s], 1, 2)
        top = jnp.concatenate([_block_diag(t(bbr)), _block_diag(t(bbi))], axis=1)
        bot = jnp.concatenate([_block_diag(t(lr)), _block_diag(t(li))], axis=1)
        wbs.append(jnp.concatenate([top, bot], axis=0))
        cms.append(jnp.concatenate([_block_diag(jnp.swapaxes(c_re[s], 1, 2)),
                                    -_block_diag(jnp.swapaxes(c_im[s], 1, 2))], axis=0))
    wb = jnp.stack(wbs).astype(BF16)
    cm = jnp.stack(cms).astype(BF16)
    lamr = jnp.broadcast_to(l2r.reshape(1, -1), (SUBLANES, l2r.size))
    lami = jnp.broadcast_to(l2i.reshape(1, -1), (SUBLANES, l2i.size))
    return wb, cm, lamr, lami


def _s5(u, wb, cm, lamr, lami, d_skip, w_glu_b, b_glu):
    bsz, seq, sw = u.shape
    assert bsz == 4, "the scan packs 4 batch rows x 2 token parities into 8 sublanes"
    tm = S5_TILE
    n2 = tm // 2
    rows = SUBLANES * n2
    ns = lamr.shape[1]
    perm, permt = _s5_perms(tm)
    const = lambda a: pl.BlockSpec(a.shape, lambda i: (0,) * a.ndim)
    d_row = d_skip.reshape(1, sw)
    bg = b_glu.reshape(1, sw)
    return pl.pallas_call(
        functools.partial(_s5_kernel, tm=tm, sw=sw),
        grid=(seq // tm,),
        in_specs=[pl.BlockSpec((4, tm, sw), lambda i: (0, i, 0)),
                  const(perm), const(permt), const(wb), const(cm), const(lamr), const(lami),
                  const(d_row), const(w_glu_b), const(bg)],
        out_specs=pl.BlockSpec((4, tm, sw), lambda i: (0, i, 0)),
        out_shape=jax.ShapeDtypeStruct((bsz, seq, sw), F32),
        scratch_shapes=[pltpu.VMEM((rows, 2 * ns), F32),
                        pltpu.VMEM((2, 2, SUBLANES, ns // 2), F32),
                        pltpu.VMEM((SUBLANES, sw), F32)],
        compiler_params=pltpu.CompilerParams(
            dimension_semantics=("arbitrary",),
            vmem_limit_bytes=VMEM_LIMIT_BYTES),
        name="s5",
    )(u, perm, permt, wb, cm, lamr, lami, d_row, w_glu_b, bg)


def _attn_kernel(q_ref, k_ref, v_ref, tri_ref, o_ref, c_scr, acc_scr, z_scr, w_scr, *, n_pairs):
    blk = ATTN_BLOCK
    qi = pl.program_id(1)
    row = lax.broadcasted_iota(jnp.int32, (blk, blk), 0)
    col = lax.broadcasted_iota(jnp.int32, (blk, blk), 1)
    causal = col < row
    nt = (((1,), (1,)), ((), ()))

    def scores(kb):
        start = pl.multiple_of(kb * blk, blk)
        for p in range(n_pairs):
            ls = slice(p * LANES, (p + 1) * LANES)
            kblk = k_ref[0, pl.ds(start, blk), ls]
            for hh in range(2):
                z_scr[2 * p + hh] = lax.dot_general(q_ref[0, hh, :, ls], kblk, nt,
                                                    preferred_element_type=F32)

    def weights(valid):
        for h in range(2 * n_pairs):
            z = z_scr[h]
            sp = jnp.maximum(z, 0.0) + jnp.log2(1.0 + jnp.exp2(-jnp.abs(z)))
            if valid is not None:
                sp = jnp.where(valid, sp, 0.0)
            hi = sp.astype(BF16)
            lo = (sp - hi.astype(F32)).astype(BF16)
            incl = _dot(jnp.concatenate([hi, lo], axis=1), tri_ref[...])
            c = c_scr[h]
            w = jnp.exp2(z - incl - c)
            if valid is not None:
                w = jnp.where(valid, w, 0.0)
            w_scr[h] = w.astype(BF16)
            c_scr[h] = c + jnp.broadcast_to(incl[:, 0:1], (blk, blk))

    def values(kb):
        start = pl.multiple_of(kb * blk, blk)
        for p in range(n_pairs):
            ls = slice(p * LANES, (p + 1) * LANES)
            vv = jnp.concatenate([v_ref[0, 0, pl.ds(start, blk), ls],
                                  v_ref[0, 1, pl.ds(start, blk), ls]], axis=0)
            ww = jnp.concatenate([w_scr[2 * p], w_scr[2 * p + 1]], axis=1)
            acc_scr[p] = acc_scr[p] + _dot(ww, vv)

    c_scr[...] = jnp.zeros_like(c_scr)
    acc_scr[...] = jnp.zeros_like(acc_scr)
    scores(qi)
    weights(causal)
    scores(jnp.maximum(qi - 1, 0))

    def body(j, carry):
        kb = qi - j
        values(kb + 1)
        weights(None)
        scores(jnp.maximum(kb - 1, 0))
        return carry

    lax.fori_loop(1, qi + 1, body, 0)
    values(0)
    for p in range(n_pairs):
        o_ref[0, :, p * LANES:(p + 1) * LANES] = acc_scr[p].astype(o_ref.dtype)


def _attn_tri():
    blk = ATTN_BLOCK
    m = np.arange(blk)[:, None]
    j = np.arange(blk)[None, :]
    u = (m >= j).astype(np.float32)
    return jnp.asarray(np.concatenate([u, u], axis=0), BF16)


def _attn(q, k, v):
    bsz, seq, aw = k.shape
    blk = ATTN_BLOCK
    n_pairs = aw // LANES
    tri = _attn_tri()
    return pl.pallas_call(
        functools.partial(_attn_kernel, n_pairs=n_pairs),
        grid=(bsz, seq // blk),
        in_specs=[pl.BlockSpec((1, 2, blk, aw), lambda b, i: (b, 0, i, 0)),
                  pl.BlockSpec((1, seq, aw), lambda b, i: (b, 0, 0)),
                  pl.BlockSpec((1, 2, seq, aw), lambda b, i: (b, 0, 0, 0)),
                  pl.BlockSpec(tri.shape, lambda b, i: (0, 0))],
        out_specs=pl.BlockSpec((1, blk, aw), lambda b, i: (b, i, 0)),
        out_shape=jax.ShapeDtypeStruct((bsz, seq, aw), BF16),
        scratch_shapes=[pltpu.VMEM((2 * n_pairs, blk, blk), F32),
                        pltpu.VMEM((n_pairs, blk, LANES), F32),
                        pltpu.VMEM((2 * n_pairs, blk, blk), F32),
                        pltpu.VMEM((2 * n_pairs, blk, blk), BF16)],
        compiler_params=pltpu.CompilerParams(
            dimension_semantics=("arbitrary", "arbitrary"),
            vmem_limit_bytes=VMEM_LIMIT_BYTES),
        name="attn",
    )(q, k, v, tri)


def _out_ffn_kernel(x_ref, s5_ref, at_ref, ga_ref, gb_ref, mod_ref, n2_ref, nf_ref,
                    wa_ref, wb_ref, wo_ref, wg_ref, wu_ref, wd_ref, o_ref, *, final_norm):
    mod = mod_ref[0]
    ya = _dot(s5_ref[0].astype(BF16), wa_ref[...])
    yb = _dot(at_ref[0], wb_ref[...])
    m = jax.nn.sigmoid(ga_ref[0]) * ya + jax.nn.sigmoid(gb_ref[0]) * yb
    x1 = x_ref[0] + mod[2:3] * _dot(m.astype(BF16), wo_ref[...])
    h = _rms(x1) * n2_ref[...]
    h = (h * (1.0 + mod[4:5]) + mod[3:4]).astype(BF16)
    gate = _dot(h, wg_ref[...])
    up = _dot(h, wu_ref[...])
    act = (gate * jax.nn.sigmoid(gate) * up).astype(BF16)
    x2 = x1 + mod[5:6] * _dot(act, wd_ref[...])
    o_ref[0] = _rms(x2) * nf_ref[...] if final_norm else x2


def _out_ffn(x, s5o, attn, ga, gb, mod, n2g, nfg, wa, wb, wo, wg, wu, wd, final_norm):
    bsz, seq, d = x.shape
    tm = OUT_TILE
    tok = lambda a: pl.BlockSpec((1, tm, a.shape[-1]), lambda b, i: (b, i, 0))
    const = lambda a: pl.BlockSpec(a.shape, lambda b, i: (0,) * a.ndim,
                                   pipeline_mode=pl.Buffered(1))
    return pl.pallas_call(
        functools.partial(_out_ffn_kernel, final_norm=final_norm),
        grid=(bsz, seq // tm),
        in_specs=[tok(x), tok(s5o), tok(attn), tok(ga), tok(gb),
                  pl.BlockSpec((1, N_ADA, d), lambda b, i: (b, 0, 0)),
                  const(n2g), const(nfg),
                  const(wa), const(wb), const(wo), const(wg), const(wu), const(wd)],
        out_specs=pl.BlockSpec((1, tm, d), lambda b, i: (b, i, 0)),
        out_shape=jax.ShapeDtypeStruct((bsz, seq, d), F32),
        compiler_params=pltpu.CompilerParams(
            dimension_semantics=("arbitrary", "arbitrary"),
            vmem_limit_bytes=VMEM_LIMIT_BYTES),
        name="out_ffn",
    )(x, s5o, attn, ga, gb, mod, n2g, nfg, wa, wb, wo, wg, wu, wd)


def kernel(x, c, w_ada, b_ada, norm1_g, w_in, lam_re, lam_im, log_dt, b_re, b_im, c_re, c_im,
           d_skip, w_glu, b_glu, w_a, w_b, w_o, norm2_g, w_ffn_gate, w_ffn_up, w_ffn_down,
           norm_f_g):
    depth = w_ada.shape[0]
    bsz, seq, d = x.shape
    sw = w_glu.shape[1]
    aw = w_b.shape[1]
    for l in range(depth):
        mod = _ada(c, w_ada[l], b_ada[l]).reshape(bsz, N_ADA, d)
        u, q, k, v, ga, gb = _in_proj(x, mod, norm1_g[l].reshape(1, d), w_in[l].astype(BF16),
                                      sw, aw)
        wb, cm, lamr, lami = _s5_weights(lam_re[l], lam_im[l], log_dt[l], b_re[l], b_im[l],
                                         c_re[l], c_im[l])
        s5o = _s5(u, wb, cm, lamr, lami, d_skip[l], w_glu[l].astype(BF16), b_glu[l])
        attn = _attn(q, k, v)
        x = _out_ffn(x, s5o, attn, ga, gb, mod, norm2_g[l].reshape(1, d), norm_f_g.reshape(1, d),
                     w_a[l].astype(BF16), w_b[l].astype(BF16), w_o[l].astype(BF16),
                     w_ffn_gate[l].astype(BF16), w_ffn_up[l].astype(BF16),
                     w_ffn_down[l].astype(BF16), final_norm=(l == depth - 1))
    return x
```

```python
import functools
import math

import numpy as np
import jax
import jax.numpy as jnp
from jax import lax
from jax.experimental import pallas as pl
from jax.experimental.pallas import tpu as pltpu

F32 = jnp.float32
BF16 = jnp.bfloat16

S5_GROUP = 16
S5_STATE = 64
HEAD_DIM = 64
N_ADA = 6
RMS_EPS = 1e-6
Q_SCALE = math.log2(math.e) / math.sqrt(HEAD_DIM)
UNDERFLOW_LOG2 = 151.0

LANES = 128
SUBLANES = 8
VMEM_LIMIT_BYTES = 56 * 1024 * 1024

ATTN_BLOCK = 128
S5_TILE = 128
IN_TILE = 256
OUT_TILE = 256


def _dot(a, b):
    return jnp.dot(a, b, preferred_element_type=F32)


def _rms(x):
    return x * lax.rsqrt(jnp.mean(x * x, axis=-1, keepdims=True) + RMS_EPS)


def _ada_kernel(c_ref, w_ref, b_ref, o_ref):
    c = c_ref[...]
    cond = c * jax.nn.sigmoid(c)
    o_ref[...] = jnp.dot(cond, w_ref[...], preferred_element_type=F32,
                         precision=lax.Precision.HIGHEST) + b_ref[...]


def _ada(c, w_ada, b_ada):
    bsz, d = c.shape
    n = w_ada.shape[1]
    rows = -(-bsz // SUBLANES) * SUBLANES
    cp = jnp.zeros((rows, d), F32).at[:bsz].set(c)
    tn = 1536
    out = pl.pallas_call(
        _ada_kernel,
        grid=(n // tn,),
        in_specs=[pl.BlockSpec((rows, d), lambda j: (0, 0)),
                  pl.BlockSpec((d, tn), lambda j: (0, j)),
                  pl.BlockSpec((1, tn), lambda j: (0, j))],
        out_specs=pl.BlockSpec((rows, tn), lambda j: (0, j)),
        out_shape=jax.ShapeDtypeStruct((rows, n), F32),
        name="ada",
    )(cp, w_ada, b_ada.reshape(1, n))
    return out[:bsz]


def _in_proj_kernel(x_ref, mod_ref, g_ref, w_ref, u_ref, q_ref, k_ref, v_ref, ga_ref, gb_ref,
                    *, sw, aw, d):
    x = x_ref[0]
    mod = mod_ref[0]
    h = _rms(x) * g_ref[...]
    h = h * (1.0 + mod[1:2]) + mod[0:1]
    hb = h.astype(BF16)
    o = 0
    lane = lax.broadcasted_iota(jnp.int32, (x.shape[0], aw), 1)
    even_head = (lane // HEAD_DIM) % 2 == 0
    u_ref[0] = _dot(hb, w_ref[:, o:o + sw]); o += sw
    q = (_dot(hb, w_ref[:, o:o + aw]) * Q_SCALE).astype(BF16); o += aw
    q_ref[0, 0] = jnp.where(even_head, q, jnp.zeros_like(q))
    q_ref[0, 1] = jnp.where(even_head, jnp.zeros_like(q), q)
    k_ref[0] = _dot(hb, w_ref[:, o:o + aw]).astype(BF16); o += aw
    v = _dot(hb, w_ref[:, o:o + aw]).astype(BF16); o += aw
    v_ref[0, 0] = jnp.where(even_head, v, jnp.zeros_like(v))
    v_ref[0, 1] = jnp.where(even_head, jnp.zeros_like(v), v)
    ga_ref[0] = _dot(hb, w_ref[:, o:o + d]); o += d
    gb_ref[0] = _dot(hb, w_ref[:, o:o + d])


def _in_proj(x, mod, norm_g, w_in_b, sw, aw):
    bsz, seq, d = x.shape
    tm = IN_TILE
    n = w_in_b.shape[1]
    tok = lambda w: pl.BlockSpec((1, tm, w), lambda b, i: (b, i, 0))
    tok2 = lambda w: pl.BlockSpec((1, 2, tm, w), lambda b, i: (b, 0, i, 0))
    return pl.pallas_call(
        functools.partial(_in_proj_kernel, sw=sw, aw=aw, d=d),
        grid=(bsz, seq // tm),
        in_specs=[tok(d),
                  pl.BlockSpec((1, N_ADA, d), lambda b, i: (b, 0, 0)),
                  pl.BlockSpec((1, d), lambda b, i: (0, 0)),
                  pl.BlockSpec((d, n), lambda b, i: (0, 0))],
        out_specs=[tok(sw), tok2(aw), tok(aw), tok2(aw), tok(d), tok(d)],
        out_shape=[jax.ShapeDtypeStruct((bsz, seq, sw), F32),
                   jax.ShapeDtypeStruct((bsz, 2, seq, aw), BF16),
                   jax.ShapeDtypeStruct((bsz, seq, aw), BF16),
                   jax.ShapeDtypeStruct((bsz, 2, seq, aw), BF16),
                   jax.ShapeDtypeStruct((bsz, seq, d), F32),
                   jax.ShapeDtypeStruct((bsz, seq, d), F32)],
        compiler_params=pltpu.CompilerParams(
            dimension_semantics=("arbitrary", "arbitrary"),
            vmem_limit_bytes=VMEM_LIMIT_BYTES),
        name="in_proj",
    )(x, mod, norm_g, w_in_b)


def _s5_kernel(u_ref, perm_ref, permt_ref, wb_ref, cm_ref, lamr_ref, lami_ref, d_ref,
               wglu_ref, bglu_ref, o_ref, bu_scr, x_scr, ulast_scr, *, tm, sw):
    i = pl.program_id(0)
    n2 = tm // 2
    rows = SUBLANES * n2
    half = sw // 2
    hs = half * S5_STATE // S5_GROUP

    @pl.when(i == 0)
    def _():
        x_scr[...] = jnp.zeros_like(x_scr)
        ulast_scr[...] = jnp.zeros_like(ulast_scr)

    u_nat = u_ref[...].reshape(4 * tm, sw)
    ap = _dot(perm_ref[...], u_nat.astype(BF16))
    a_cur = ap[:rows]
    a_prev = ap[rows:]
    first = a_prev[0:SUBLANES] + ulast_scr[...]
    a_prev = jnp.concatenate([first, a_prev[SUBLANES:]], axis=0)

    sub = lax.broadcasted_iota(jnp.int32, (SUBLANES, sw), 0)
    last = jnp.zeros((SUBLANES, sw), F32)
    for b in range(4):
        row = u_ref[b, tm - 1:tm, :].astype(BF16).astype(F32)
        last = jnp.where(sub == 2 * b, jnp.broadcast_to(row, (SUBLANES, sw)), last)
    ulast_scr[...] = last

    a_cur = a_cur.astype(BF16)
    a_prev = a_prev.astype(BF16)
    for h in range(2):
        lhs = jnp.concatenate([a_cur[:, h * half:(h + 1) * half],
                               a_prev[:, h * half:(h + 1) * half]], axis=1)
        bu_scr[:, 2 * hs * h:2 * hs * (h + 1)] = _dot(lhs, wb_ref[h])

    for h in range(2):
        re0 = 2 * hs * h
        im0 = re0 + hs
        ar = lamr_ref[:, hs * h:hs * (h + 1)]
        ai = lami_ref[:, hs * h:hs * (h + 1)]

        def step(t2, carry, re0=re0, im0=im0, ar=ar, ai=ai):
            xr, xi = carry
            r0 = pl.multiple_of(t2 * SUBLANES, SUBLANES)
            br = bu_scr[pl.ds(r0, SUBLANES), re0:re0 + hs]
            bi = bu_scr[pl.ds(r0, SUBLANES), im0:im0 + hs]
            nr = ar * xr - ai * xi + br
            ni = ar * xi + ai * xr + bi
            bu_scr[pl.ds(r0, SUBLANES), re0:re0 + hs] = nr
            bu_scr[pl.ds(r0, SUBLANES), im0:im0 + hs] = ni
            return nr, ni

        xr, xi = lax.fori_loop(0, n2, step, (x_scr[h, 0], x_scr[h, 1]), unroll=4)
        x_scr[h, 0] = xr
        x_scr[h, 1] = xi

    ys = []
    for h in range(2):
        st = bu_scr[:, 2 * hs * h:2 * hs * (h + 1)].astype(BF16)
        ys.append(_dot(st, cm_ref[h]))
    y_il = jnp.concatenate(ys, axis=1)
    hi = y_il.astype(BF16)
    lo = (y_il - hi.astype(F32)).astype(BF16)
    y = _dot(permt_ref[...], jnp.concatenate([hi, lo], axis=0))
    y = y + d_ref[...] * u_nat
    y = jax.nn.gelu(y)
    z = _dot(y.astype(BF16), wglu_ref[...]) + bglu_ref[...]
    o_ref[...] = (y * jax.nn.sigmoid(z)).reshape(4, tm, sw)


def _s5_perms(tm):
    n2 = tm // 2
    rows = SUBLANES * n2
    perm = np.zeros((2 * rows, 4 * tm), np.float32)
    permt = np.zeros((4 * tm, 2 * rows), np.float32)
    for t2 in range(n2):
        for b in range(4):
            for par in range(2):
                r = SUBLANES * t2 + 2 * b + par
                t = 2 * t2 + par
                perm[r, b * tm + t] = 1.0
                if t >= 1:
                    perm[rows + r, b * tm + t - 1] = 1.0
                permt[b * tm + t, r] = 1.0
                permt[b * tm + t, rows + r] = 1.0
    return jnp.asarray(perm, BF16), jnp.asarray(permt, BF16)


def _block_diag(blocks):
    n, a, b = blocks.shape
    eye = jnp.eye(n, dtype=blocks.dtype)
    return jnp.einsum('nab,nm->namb', blocks, eye).reshape(n * a, n * b)


def _s5_weights(lam_re, lam_im, log_dt, b_re, b_im, c_re, c_im):
    g = lam_re.shape[0]
    dt = jnp.exp(log_dt)[:, None]
    mag = jnp.exp(lam_re * dt)
    lbr = mag * jnp.cos(lam_im * dt)
    lbi = mag * jnp.sin(lam_im * dt)
    nr, ni = lbr - 1.0, lbi
    den = lam_re * lam_re + lam_im * lam_im
    cr = (nr * lam_re + ni * lam_im) / den
    ci = (ni * lam_re - nr * lam_im) / den
    bbr = cr[..., None] * b_re - ci[..., None] * b_im
    bbi = cr[..., None] * b_im + ci[..., None] * b_re
    lr = lbr[..., None] * bbr - lbi[..., None] * bbi
    li = lbr[..., None] * bbi + lbi[..., None] * bbr
    l2r = lbr * lbr - lbi * lbi
    l2i = 2.0 * lbr * lbi
    gh = g // 2
    wbs, cms = [], []
    for h in range(2):
        s = slice(h * gh, (h + 1) * gh)
        t = lambda a: jnp.swapaxes(a[s], 1, 2)
        top = jnp.concatenate([_block_diag(t(bbr)), _block_diag(t(bbi))], axis=1)
        bot = jnp.concatenate([_block_diag(t(lr)), _block_diag(t(li))], axis=1)
        wbs.append(jnp.concatenate([top, bot], axis=0))
        cms.append(jnp.concatenate([_block_diag(jnp.swapaxes(c_re[s], 1, 2)),
                                    -_block_diag(jnp.swapaxes(c_im[s], 1, 2))], axis=0))
    wb = jnp.stack(wbs).astype(BF16)
    cm = jnp.stack(cms).astype(BF16)
    lamr = jnp.broadcast_to(l2r.reshape(1, -1), (SUBLANES, l2r.size))
    lami = jnp.broadcast_to(l2i.reshape(1, -1), (SUBLANES, l2i.size))
    return wb, cm, lamr, lami


def _s5(u, wb, cm, lamr, lami, d_skip, w_glu_b, b_glu):
    bsz, seq, sw = u.shape
    assert bsz == 4, "the scan packs 4 batch rows x 2 token parities into 8 sublanes"
    tm = S5_TILE
    n2 = tm // 2
    rows = SUBLANES * n2
    ns = lamr.shape[1]
    perm, permt = _s5_perms(tm)
    const = lambda a: pl.BlockSpec(a.shape, lambda i: (0,) * a.ndim)
    d_row = d_skip.reshape(1, sw)
    bg = b_glu.reshape(1, sw)
    return pl.pallas_call(
        functools.partial(_s5_kernel, tm=tm, sw=sw),
        grid=(seq // tm,),
        in_specs=[pl.BlockSpec((4, tm, sw), lambda i: (0, i, 0)),
                  const(perm), const(permt), const(wb), const(cm), const(lamr), const(lami),
                  const(d_row), const(w_glu_b), const(bg)],
        out_specs=pl.BlockSpec((4, tm, sw), lambda i: (0, i, 0)),
        out_shape=jax.ShapeDtypeStruct((bsz, seq, sw), F32),
        scratch_shapes=[pltpu.VMEM((rows, 2 * ns), F32),
                        pltpu.VMEM((2, 2, SUBLANES, ns // 2), F32),
                        pltpu.VMEM((SUBLANES, sw), F32)],
        compiler_params=pltpu.CompilerParams(
            dimension_semantics=("arbitrary",),
            vmem_limit_bytes=VMEM_LIMIT_BYTES),
        name="s5",
    )(u, perm, permt, wb, cm, lamr, lami, d_row, w_glu_b, bg)


def _attn_kernel(q_ref, k_ref, v_ref, tri_ref, o_ref, c_scr, acc_scr, z_scr, w_scr, *, n_pairs):
    blk = ATTN_BLOCK
    qi = pl.program_id(1)
    row = lax.broadcasted_iota(jnp.int32, (blk, blk), 0)
    col = lax.broadcasted_iota(jnp.int32, (blk, blk), 1)
    causal = col < row
    nt = (((1,), (1,)), ((), ()))

    def scores(kb):
        start = pl.multiple_of(kb * blk, blk)
        for p in range(n_pairs):
            ls = slice(p * LANES, (p + 1) * LANES)
            kblk = k_ref[0, pl.ds(start, blk), ls]
            for hh in range(2):
                z_scr[2 * p + hh] = lax.dot_general(q_ref[0, hh, :, ls], kblk, nt,
                                                    preferred_element_type=F32)

    def weights(valid):
        for h in range(2 * n_pairs):
            z = z_scr[h]
            sp = jnp.maximum(z, 0.0) + jnp.log2(1.0 + jnp.exp2(-jnp.abs(z)))
            if valid is not None:
                sp = jnp.where(valid, sp, 0.0)
            hi = sp.astype(BF16)
            lo = (sp - hi.astype(F32)).astype(BF16)
            incl = _dot(jnp.concatenate([hi, lo], axis=1), tri_ref[...])
            c = c_scr[h]
            w = jnp.exp2(z - incl - c)
            if valid is not None:
                w = jnp.where(valid, w, 0.0)
            w_scr[h] = w.astype(BF16)
            c_scr[h] = c + jnp.broadcast_to(incl[:, 0:1], (blk, blk))

    def values(kb):
        start = pl.multiple_of(kb * blk, blk)
        for p in range(n_pairs):
            ls = slice(p * LANES, (p + 1) * LANES)
            vv = jnp.concatenate([v_ref[0, 0, pl.ds(start, blk), ls],
                                  v_ref[0, 1, pl.ds(start, blk), ls]], axis=0)
            ww = jnp.concatenate([w_scr[2 * p], w_scr[2 * p + 1]], axis=1)
            acc_scr[p] = acc_scr[p] + _dot(ww, vv)

    c_scr[...] = jnp.zeros_like(c_scr)
    acc_scr[...] = jnp.zeros_like(acc_scr)
    scores(qi)
    weights(causal)
    scores(jnp.maximum(qi - 1, 0))

    def more(carry):
        j, cmin = carry
        return jnp.logical_and(j <= qi, cmin < UNDERFLOW_LOG2)

    def body(carry):
        j, _ = carry
        kb = qi - j
        values(kb + 1)
        weights(None)
        scores(jnp.maximum(kb - 1, 0))
        m = c_scr[0]
        for h in range(1, 2 * n_pairs):
            m = jnp.minimum(m, c_scr[h])
        return j + 1, jnp.min(m)

    j_end, _ = lax.while_loop(more, body, (jnp.int32(1), jnp.float32(0.0)))
    values(qi - (j_end - 1))
    for p in range(n_pairs):
        o_ref[0, :, p * LANES:(p + 1) * LANES] = acc_scr[p].astype(o_ref.dtype)


def _attn_tri():
    blk = ATTN_BLOCK
    m = np.arange(blk)[:, None]
    j = np.arange(blk)[None, :]
    u = (m >= j).astype(np.float32)
    return jnp.asarray(np.concatenate([u, u], axis=0), BF16)


def _attn(q, k, v):
    bsz, seq, aw = k.shape
    blk = ATTN_BLOCK
    n_pairs = aw // LANES
    tri = _attn_tri()
    return pl.pallas_call(
        functools.partial(_attn_kernel, n_pairs=n_pairs),
        grid=(bsz, seq // blk),
        in_specs=[pl.BlockSpec((1, 2, blk, aw), lambda b, i: (b, 0, i, 0)),
                  pl.BlockSpec((1, seq, aw), lambda b, i: (b, 0, 0)),
                  pl.BlockSpec((1, 2, seq, aw), lambda b, i: (b, 0, 0, 0)),
                  pl.BlockSpec(tri.shape, lambda b, i: (0, 0))],
        out_specs=pl.BlockSpec((1, blk, aw), lambda b, i: (b, i, 0)),
        out_shape=jax.ShapeDtypeStruct((bsz, seq, aw), BF16),
        scratch_shapes=[pltpu.VMEM((2 * n_pairs, blk, blk), F32),
                        pltpu.VMEM((n_pairs, blk, LANES), F32),
                        pltpu.VMEM((2 * n_pairs, blk, blk), F32),
                        pltpu.VMEM((2 * n_pairs, blk, blk), BF16)],
        compiler_params=pltpu.CompilerParams(
            dimension_semantics=("arbitrary", "arbitrary"),
            vmem_limit_bytes=VMEM_LIMIT_BYTES),
        name="attn",
    )(q, k, v, tri)


def _out_ffn_kernel(x_ref, s5_ref, at_ref, ga_ref, gb_ref, mod_ref, n2_ref, nf_ref,
                    wa_ref, wb_ref, wo_ref, wg_ref, wu_ref, wd_ref, o_ref, *, final_norm):
    mod = mod_ref[0]
    ya = _dot(s5_ref[0].astype(BF16), wa_ref[...])
    yb = _dot(at_ref[0], wb_ref[...])
    m = jax.nn.sigmoid(ga_ref[0]) * ya + jax.nn.sigmoid(gb_ref[0]) * yb
    x1 = x_ref[0] + mod[2:3] * _dot(m.astype(BF16), wo_ref[...])
    h = _rms(x1) * n2_ref[...]
    h = (h * (1.0 + mod[4:5]) + mod[3:4]).astype(BF16)
    gate = _dot(h, wg_ref[...])
    up = _dot(h, wu_ref[...])
    act = (gate * jax.nn.sigmoid(gate) * up).astype(BF16)
    x2 = x1 + mod[5:6] * _dot(act, wd_ref[...])
    o_ref[0] = _rms(x2) * nf_ref[...] if final_norm else x2


def _out_ffn(x, s5o, attn, ga, gb, mod, n2g, nfg, wa, wb, wo, wg, wu, wd, final_norm):
    bsz, seq, d = x.shape
    tm = OUT_TILE
    tok = lambda a: pl.BlockSpec((1, tm, a.shape[-1]), lambda b, i: (b, i, 0))
    const = lambda a: pl.BlockSpec(a.shape, lambda b, i: (0,) * a.ndim,
                                   pipeline_mode=pl.Buffered(1))
    return pl.pallas_call(
        functools.partial(_out_ffn_kernel, final_norm=final_norm),
        grid=(bsz, seq // tm),
        in_specs=[tok(x), tok(s5o), tok(attn), tok(ga), tok(gb),
                  pl.BlockSpec((1, N_ADA, d), lambda b, i: (b, 0, 0)),
                  const(n2g), const(nfg),
                  const(wa), const(wb), const(wo), const(wg), const(wu), const(wd)],
        out_specs=pl.BlockSpec((1, tm, d), lambda b, i: (b, i, 0)),
        out_shape=jax.ShapeDtypeStruct((bsz, seq, d), F32),
        compiler_params=pltpu.CompilerParams(
            dimension_semantics=("arbitrary", "arbitrary"),
            vmem_limit_bytes=VMEM_LIMIT_BYTES),
        name="out_ffn",
    )(x, s5o, attn, ga, gb, mod, n2g, nfg, wa, wb, wo, wg, wu, wd)


def kernel(x, c, w_ada, b_ada, norm1_g, w_in, lam_re, lam_im, log_dt, b_re, b_im, c_re, c_im,
           d_skip, w_glu, b_glu, w_a, w_b, w_o, norm2_g, w_ffn_gate, w_ffn_up, w_ffn_down,
           norm_f_g):
    depth = w_ada.shape[0]
    bsz, seq, d = x.shape
    sw = w_glu.shape[1]
    aw = w_b.shape[1]
    for l in range(depth):
        mod = _ada(c, w_ada[l], b_ada[l]).reshape(bsz, N_ADA, d)
        u, q, k, v, ga, gb = _in_proj(x, mod, norm1_g[l].reshape(1, d), w_in[l].astype(BF16),
                                      sw, aw)
        wb, cm, lamr, lami = _s5_weights(lam_re[l], lam_im[l], log_dt[l], b_re[l], b_im[l],
                                         c_re[l], c_im[l])
        s5o = _s5(u, wb, cm, lamr, lami, d_skip[l], w_glu[l].astype(BF16), b_glu[l])
        attn = _attn(q, k, v)
        x = _out_ffn(x, s5o, attn, ga, gb, mod, norm2_g[l].reshape(1, d), norm_f_g.reshape(1, d),
                     w_a[l].astype(BF16), w_b[l].astype(BF16), w_o[l].astype(BF16),
                     w_ffn_gate[l].astype(BF16), w_ffn_up[l].astype(BF16),
                     w_ffn_down[l].astype(BF16), final_norm=(l == depth - 1))
    return x
```

```python
import functools
import math

import numpy as np
import jax
import jax.numpy as jnp
from jax import lax
from jax.experimental import pallas as pl
from jax.experimental.pallas import tpu as pltpu

F32 = jnp.float32
BF16 = jnp.bfloat16

S5_GROUP = 16
S5_STATE = 64
HEAD_DIM = 64
N_ADA = 6
RMS_EPS = 1e-6
Q_SCALE = math.log2(math.e) / math.sqrt(HEAD_DIM)
UNDERFLOW_LOG2 = 151.0

LANES = 128
SUBLANES = 8
VMEM_LIMIT_BYTES = 56 * 1024 * 1024

ATTN_BLOCK = 128
ATTN_REGION = 3
S5_TILE = 128
S5_SLABS = 4
IN_TILE = 256
OUT_TILE = 256


def _dot(a, b):
    return jnp.dot(a, b, preferred_element_type=F32)


def _rms(x):
    return x * lax.rsqrt(jnp.mean(x * x, axis=-1, keepdims=True) + RMS_EPS)


def _ada_kernel(c_ref, w_ref, b_ref, o_ref):
    c = c_ref[...]
    cond = c * jax.nn.sigmoid(c)
    o_ref[...] = jnp.dot(cond, w_ref[...], preferred_element_type=F32,
                         precision=lax.Precision.HIGHEST) + b_ref[...]


def _ada(c, w_ada, b_ada):
    bsz, d = c.shape
    n = w_ada.shape[1]
    rows = -(-bsz // SUBLANES) * SUBLANES
    cp = jnp.zeros((rows, d), F32).at[:bsz].set(c)
    tn = 1536
    out = pl.pallas_call(
        _ada_kernel,
        grid=(n // tn,),
        in_specs=[pl.BlockSpec((rows, d), lambda j: (0, 0)),
                  pl.BlockSpec((d, tn), lambda j: (0, j)),
                  pl.BlockSpec((1, tn), lambda j: (0, j))],
        out_specs=pl.BlockSpec((rows, tn), lambda j: (0, j)),
        out_shape=jax.ShapeDtypeStruct((rows, n), F32),
        name="ada",
    )(cp, w_ada, b_ada.reshape(1, n))
    return out[:bsz]


def _in_proj_kernel(x_ref, mod_ref, g_ref, w_ref, u_ref, q_ref, k_ref, v_ref, ga_ref, gb_ref,
                    *, sw, aw, d):
    x = x_ref[0]
    mod = mod_ref[0]
    h = _rms(x) * g_ref[...]
    h = h * (1.0 + mod[1:2]) + mod[0:1]
    hb = h.astype(BF16)
    o = 0
    lane = lax.broadcasted_iota(jnp.int32, (x.shape[0], aw), 1)
    even_head = (lane // HEAD_DIM) % 2 == 0
    u_ref[0] = _dot(hb, w_ref[:, o:o + sw]); o += sw
    q = (_dot(hb, w_ref[:, o:o + aw]) * Q_SCALE).astype(BF16); o += aw
    q_ref[0, 0] = jnp.where(even_head, q, jnp.zeros_like(q))
    q_ref[0, 1] = jnp.where(even_head, jnp.zeros_like(q), q)
    k_ref[0] = _dot(hb, w_ref[:, o:o + aw]).astype(BF16); o += aw
    v = _dot(hb, w_ref[:, o:o + aw]).astype(BF16); o += aw
    v_ref[0, 0] = jnp.where(even_head, v, jnp.zeros_like(v))
    v_ref[0, 1] = jnp.where(even_head, jnp.zeros_like(v), v)
    ga_ref[0] = _dot(hb, w_ref[:, o:o + d]); o += d
    gb_ref[0] = _dot(hb, w_ref[:, o:o + d])


def _in_proj(x, mod, norm_g, w_in_b, sw, aw):
    bsz, seq, d = x.shape
    tm = IN_TILE
    n = w_in_b.shape[1]
    tok = lambda w: pl.BlockSpec((1, tm, w), lambda b, i: (b, i, 0))
    tok2 = lambda w: pl.BlockSpec((1, 2, tm, w), lambda b, i: (b, 0, i, 0))
    return pl.pallas_call(
        functools.partial(_in_proj_kernel, sw=sw, aw=aw, d=d),
        grid=(bsz, seq // tm),
        in_specs=[tok(d),
                  pl.BlockSpec((1, N_ADA, d), lambda b, i: (b, 0, 0)),
                  pl.BlockSpec((1, d), lambda b, i: (0, 0)),
                  pl.BlockSpec((d, n), lambda b, i: (0, 0))],
        out_specs=[tok(sw), tok2(aw), tok(aw), tok2(aw), tok(d), tok(d)],
        out_shape=[jax.ShapeDtypeStruct((bsz, seq, sw), F32),
                   jax.ShapeDtypeStruct((bsz, 2, seq, aw), BF16),
                   jax.ShapeDtypeStruct((bsz, seq, aw), BF16),
                   jax.ShapeDtypeStruct((bsz, 2, seq, aw), BF16),
                   jax.ShapeDtypeStruct((bsz, seq, d), F32),
                   jax.ShapeDtypeStruct((bsz, seq, d), F32)],
        compiler_params=pltpu.CompilerParams(
            dimension_semantics=("arbitrary", "arbitrary"),
            vmem_limit_bytes=VMEM_LIMIT_BYTES),
        name="in_proj",
    )(x, mod, norm_g, w_in_b)


def _s5_kernel(u_ref, perm_ref, permt_ref, wb_ref, cm_ref, lamr_ref, lami_ref, d_ref,
               wglu_ref, bglu_ref, o_ref, bu_scr, x_scr, ulast_scr, *, tm, sw):
    i = pl.program_id(0)
    n2 = tm // 2
    rows = SUBLANES * n2
    cw = sw // S5_SLABS
    hs = cw * S5_STATE // S5_GROUP

    @pl.when(i == 0)
    def _():
        x_scr[...] = jnp.zeros_like(x_scr)
        ulast_scr[...] = jnp.zeros_like(ulast_scr)

    u_nat = u_ref[...].reshape(4 * tm, sw)
    a_cur = _dot(perm_ref[...], u_nat.astype(BF16))
    odd = (lax.broadcasted_iota(jnp.int32, (rows, sw), 0) & 1) == 1
    a_prev = jnp.where(odd, pltpu.roll(a_cur, 1, 0), pltpu.roll(a_cur, SUBLANES - 1, 0))
    sub = lax.broadcasted_iota(jnp.int32, (SUBLANES, sw), 0)
    first = jnp.where((sub & 1) == 1, a_prev[:SUBLANES], ulast_scr[...])
    a_prev = jnp.concatenate([first, a_prev[SUBLANES:]], axis=0)

    last = jnp.zeros((SUBLANES, sw), F32)
    for b in range(4):
        row = u_ref[b, tm - 1:tm, :].astype(BF16).astype(F32)
        last = jnp.where(sub == 2 * b, jnp.broadcast_to(row, (SUBLANES, sw)), last)
    ulast_scr[...] = last

    a_cur = a_cur.astype(BF16)
    a_prev = a_prev.astype(BF16)
    for s in range(S5_SLABS):
        lhs = jnp.concatenate([a_cur[:, s * cw:(s + 1) * cw],
                               a_prev[:, s * cw:(s + 1) * cw]], axis=1)
        bu_scr[:, 2 * hs * s:2 * hs * (s + 1)] = _dot(lhs, wb_ref[s])

    for s0 in range(0, S5_SLABS, 2):
        slabs = (s0, s0 + 1)
        lam = [(lamr_ref[:, hs * s:hs * (s + 1)], lami_ref[:, hs * s:hs * (s + 1)]) for s in slabs]

        def step(t2, carry, slabs=slabs, lam=lam):
            r0 = pl.multiple_of(t2 * SUBLANES, SUBLANES)
            out = []
            for s, (ar, ai), (xr, xi) in zip(slabs, lam, carry):
                re0 = 2 * hs * s
                im0 = re0 + hs
                br = bu_scr[pl.ds(r0, SUBLANES), re0:re0 + hs]
                bi = bu_scr[pl.ds(r0, SUBLANES), im0:im0 + hs]
                nr = ar * xr - ai * xi + br
                ni = ar * xi + ai * xr + bi
                bu_scr[pl.ds(r0, SUBLANES), re0:re0 + hs] = nr
                bu_scr[pl.ds(r0, SUBLANES), im0:im0 + hs] = ni
                out.append((nr, ni))
            return tuple(out)

        init = tuple((x_scr[s, 0], x_scr[s, 1]) for s in slabs)
        res = lax.fori_loop(0, n2, step, init, unroll=4)
        for s, (xr, xi) in zip(slabs, res):
            x_scr[s, 0] = xr
            x_scr[s, 1] = xi

    ys = []
    for s in range(S5_SLABS):
        st = bu_scr[:, 2 * hs * s:2 * hs * (s + 1)].astype(BF16)
        ys.append(_dot(st, cm_ref[s]))
    y_il = jnp.concatenate(ys, axis=1)
    hi = y_il.astype(BF16)
    lo = (y_il - hi.astype(F32)).astype(BF16)
    y = _dot(permt_ref[...], jnp.concatenate([hi, lo], axis=0))
    y = y + d_ref[...] * u_nat
    y = jax.nn.gelu(y)
    z = _dot(y.astype(BF16), wglu_ref[...]) + bglu_ref[...]
    o_ref[...] = (y * jax.nn.sigmoid(z)).reshape(4, tm, sw)


def _s5_perms(tm):
    n2 = tm // 2
    rows = SUBLANES * n2
    perm = np.zeros((rows, 4 * tm), np.float32)
    permt = np.zeros((4 * tm, 2 * rows), np.float32)
    for t2 in range(n2):
        for b in range(4):
            for par in range(2):
                r = SUBLANES * t2 + 2 * b + par
                t = 2 * t2 + par
                perm[r, b * tm + t] = 1.0
                permt[b * tm + t, r] = 1.0
                permt[b * tm + t, rows + r] = 1.0
    return jnp.asarray(perm, BF16), jnp.asarray(permt, BF16)


def _block_diag(blocks):
    n, a, b = blocks.shape
    eye = jnp.eye(n, dtype=blocks.dtype)
    return jnp.einsum('nab,nm->namb', blocks, eye).reshape(n * a, n * b)


def _s5_weights(lam_re, lam_im, log_dt, b_re, b_im, c_re, c_im):
    g = lam_re.shape[0]
    dt = jnp.exp(log_dt)[:, None]
    mag = jnp.exp(lam_re * dt)
    lbr = mag * jnp.cos(lam_im * dt)
    lbi = mag * jnp.sin(lam_im * dt)
    nr, ni = lbr - 1.0, lbi
    den = lam_re * lam_re + lam_im * lam_im
    cr = (nr * lam_re + ni * lam_im) / den
    ci = (ni * lam_re - nr * lam_im) / den
    bbr = cr[..., None] * b_re - ci[..., None] * b_im
    bbi = cr[..., None] * b_im + ci[..., None] * b_re
    lr = lbr[..., None] * bbr - lbi[..., None] * bbi
    li = lbr[..., None] * bbi + lbi[..., None] * bbr
    l2r = lbr * lbr - lbi * lbi
    l2i = 2.0 * lbr * lbi
    gh = g // S5_SLABS
    wbs, cms = [], []
    for h in range(S5_SLABS):
        s = slice(h * gh, (h + 1) * gh)
        t = lambda a: jnp.swapaxes(a[s], 1, 2)
        top = jnp.concatenate([_block_diag(t(bbr)), _block_diag(t(bbi))], axis=1)
        bot = jnp.concatenate([_block_diag(t(lr)), _block_diag(t(li))], axis=1)
        wbs.append(jnp.concatenate([top, bot], axis=0))
        cms.append(jnp.concatenate([_block_diag(jnp.swapaxes(c_re[s], 1, 2)),
                                    -_block_diag(jnp.swapaxes(c_im[s], 1, 2))], axis=0))
    wb = jnp.stack(wbs).astype(BF16)
    cm = jnp.stack(cms).astype(BF16)
    lamr = jnp.broadcast_to(l2r.reshape(1, -1), (SUBLANES, l2r.size))
    lami = jnp.broadcast_to(l2i.reshape(1, -1), (SUBLANES, l2i.size))
    return wb, cm, lamr, lami


def _s5(u, wb, cm, lamr, lami, d_skip, w_glu_b, b_glu):
    bsz, seq, sw = u.shape
    assert bsz == 4, "the scan packs 4 batch rows x 2 token parities into 8 sublanes"
    tm = S5_TILE
    n2 = tm // 2
    rows = SUBLANES * n2
    ns = lamr.shape[1]
    perm, permt = _s5_perms(tm)
    const = lambda a: pl.BlockSpec(a.shape, lambda i: (0,) * a.ndim)
    d_row = d_skip.reshape(1, sw)
    bg = b_glu.reshape(1, sw)
    return pl.pallas_call(
        functools.partial(_s5_kernel, tm=tm, sw=sw),
        grid=(seq // tm,),
        in_specs=[pl.BlockSpec((4, tm, sw), lambda i: (0, i, 0)),
                  const(perm), const(permt), const(wb), const(cm), const(lamr), const(lami),
                  const(d_row), const(w_glu_b), const(bg)],
        out_specs=pl.BlockSpec((4, tm, sw), lambda i: (0, i, 0)),
        out_shape=jax.ShapeDtypeStruct((bsz, seq, sw), F32),
        scratch_shapes=[pltpu.VMEM((rows, 2 * ns), F32),
                        pltpu.VMEM((S5_SLABS, 2, SUBLANES, ns // S5_SLABS), F32),
                        pltpu.VMEM((SUBLANES, sw), F32)],
        compiler_params=pltpu.CompilerParams(
            dimension_semantics=("arbitrary",),
            vmem_limit_bytes=VMEM_LIMIT_BYTES),
        name="s5",
    )(u, perm, permt, wb, cm, lamr, lami, d_row, w_glu_b, bg)


def _attn_kernel(q_ref, k_ref, v_ref, tri_ref, o_ref, c_scr, acc_scr, z_scr, w_scr, *, n_pairs):
    blk = ATTN_BLOCK
    nh = 2 * n_pairs
    qi = pl.program_id(1)
    row = lax.broadcasted_iota(jnp.int32, (blk, blk), 0)
    col = lax.broadcasted_iota(jnp.int32, (blk, blk), 1)
    causal = col < row
    nt = (((1,), (1,)), ((), ()))

    def region(kbs, diag_first):
        starts = [pl.multiple_of(kb * blk, blk) for kb in kbs]
        for s, start in enumerate(starts):
            for p in range(n_pairs):
                ls = slice(p * LANES, (p + 1) * LANES)
                kblk = k_ref[0, pl.ds(start, blk), ls]
                for hh in range(2):
                    z_scr[s * nh + 2 * p + hh] = lax.dot_general(
                        q_ref[0, hh, :, ls], kblk, nt, preferred_element_type=F32)
        for s in range(len(kbs)):
            valid = causal if (diag_first and s == 0) else None
            for h in range(nh):
                z = z_scr[s * nh + h]
                sp = jnp.maximum(z, 0.0) + jnp.log2(1.0 + jnp.exp2(-jnp.abs(z)))
                if valid is not None:
                    sp = jnp.where(valid, sp, 0.0)
                hi = sp.astype(BF16)
                lo = (sp - hi.astype(F32)).astype(BF16)
                incl = _dot(jnp.concatenate([hi, lo], axis=1), tri_ref[...])
                c = c_scr[h]
                w = jnp.exp2(z - incl - c)
                if valid is not None:
                    w = jnp.where(valid, w, 0.0)
                w_scr[s * nh + h] = w.astype(BF16)
                c_scr[h] = c + jnp.broadcast_to(incl[:, 0:1], (blk, blk))
        for p in range(n_pairs):
            ls = slice(p * LANES, (p + 1) * LANES)
            ww = jnp.concatenate([w_scr[s * nh + 2 * p + hh]
                                  for s in range(len(kbs)) for hh in range(2)], axis=1)
            vv = jnp.concatenate([v_ref[0, hh, pl.ds(start, blk), ls]
                                  for start in starts for hh in range(2)], axis=0)
            acc_scr[p] = acc_scr[p] + _dot(ww, vv)

    def c_min():
        m = c_scr[0]
        for h in range(1, 2 * n_pairs):
            m = jnp.minimum(m, c_scr[h])
        return jnp.min(m)

    c_scr[...] = jnp.zeros_like(c_scr)
    acc_scr[...] = jnp.zeros_like(acc_scr)

    @pl.when(qi >= 2)
    def _():
        region([qi, qi - 1, qi - 2], True)

    @pl.when(qi == 1)
    def _():
        region([1, 0], True)

    @pl.when(qi == 0)
    def _():
        region([0], True)

    def more(carry):
        kb, cmin = carry
        return jnp.logical_and(kb >= 0, cmin < UNDERFLOW_LOG2)

    def body(carry):
        kb, _ = carry
        region([kb], False)
        return kb - 1, c_min()

    lax.while_loop(more, body, (qi - 3, c_min()))
    for p in range(n_pairs):
        o_ref[0, :, p * LANES:(p + 1) * LANES] = acc_scr[p].astype(o_ref.dtype)


def _attn_tri():
    blk = ATTN_BLOCK
    m = np.arange(blk)[:, None]
    j = np.arange(blk)[None, :]
    u = (m >= j).astype(np.float32)
    return jnp.asarray(np.concatenate([u, u], axis=0), BF16)


def _attn(q, k, v):
    bsz, seq, aw = k.shape
    blk = ATTN_BLOCK
    n_pairs = aw // LANES
    tri = _attn_tri()
    return pl.pallas_call(
        functools.partial(_attn_kernel, n_pairs=n_pairs),
        grid=(bsz, seq // blk),
        in_specs=[pl.BlockSpec((1, 2, blk, aw), lambda b, i: (b, 0, i, 0)),
                  pl.BlockSpec((1, seq, aw), lambda b, i: (b, 0, 0)),
                  pl.BlockSpec((1, 2, seq, aw), lambda b, i: (b, 0, 0, 0)),
                  pl.BlockSpec(tri.shape, lambda b, i: (0, 0))],
        out_specs=pl.BlockSpec((1, blk, aw), lambda b, i: (b, i, 0)),
        out_shape=jax.ShapeDtypeStruct((bsz, seq, aw), BF16),
        scratch_shapes=[pltpu.VMEM((2 * n_pairs, blk, blk), F32),
                        pltpu.VMEM((n_pairs, blk, LANES), F32),
                        pltpu.VMEM((ATTN_REGION * 2 * n_pairs, blk, blk), F32),
                        pltpu.VMEM((ATTN_REGION * 2 * n_pairs, blk, blk), BF16)],
        compiler_params=pltpu.CompilerParams(
            dimension_semantics=("arbitrary", "arbitrary"),
            vmem_limit_bytes=VMEM_LIMIT_BYTES),
        name="attn",
    )(q, k, v, tri)


def _out_ffn_kernel(x_ref, s5_ref, at_ref, ga_ref, gb_ref, mod_ref, n2_ref, nf_ref,
                    wa_ref, wb_ref, wo_ref, wg_ref, wu_ref, wd_ref, o_ref, *, final_norm):
    mod = mod_ref[0]
    ya = _dot(s5_ref[0].astype(BF16), wa_ref[...])
    yb = _dot(at_ref[0], wb_ref[...])
    m = jax.nn.sigmoid(ga_ref[0]) * ya + jax.nn.sigmoid(gb_ref[0]) * yb
    x1 = x_ref[0] + mod[2:3] * _dot(m.astype(BF16), wo_ref[...])
    h = _rms(x1) * n2_ref[...]
    h = (h * (1.0 + mod[4:5]) + mod[3:4]).astype(BF16)
    gate = _dot(h, wg_ref[...])
    up = _dot(h, wu_ref[...])
    act = (gate * jax.nn.sigmoid(gate) * up).astype(BF16)
    x2 = x1 + mod[5:6] * _dot(act, wd_ref[...])
    o_ref[0] = _rms(x2) * nf_ref[...] if final_norm else x2


def _out_ffn(x, s5o, attn, ga, gb, mod, n2g, nfg, wa, wb, wo, wg, wu, wd, final_norm):
    bsz, seq, d = x.shape
    tm = OUT_TILE
    tok = lambda a: pl.BlockSpec((1, tm, a.shape[-1]), lambda b, i: (b, i, 0))
    const = lambda a: pl.BlockSpec(a.shape, lambda b, i: (0,) * a.ndim,
                                   pipeline_mode=pl.Buffered(1))
    return pl.pallas_call(
        functools.partial(_out_ffn_kernel, final_norm=final_norm),
        grid=(bsz, seq // tm),
        in_specs=[tok(x), tok(s5o), tok(attn), tok(ga), tok(gb),
                  pl.BlockSpec((1, N_ADA, d), lambda b, i: (b, 0, 0)),
                  const(n2g), const(nfg),
                  const(wa), const(wb), const(wo), const(wg), const(wu), const(wd)],
        out_specs=pl.BlockSpec((1, tm, d), lambda b, i: (b, i, 0)),
        out_shape=jax.ShapeDtypeStruct((bsz, seq, d), F32),
        compiler_params=pltpu.CompilerParams(
            dimension_semantics=("arbitrary", "arbitrary"),
            vmem_limit_bytes=VMEM_LIMIT_BYTES),
        name="out_ffn",
    )(x, s5o, attn, ga, gb, mod, n2g, nfg, wa, wb, wo, wg, wu, wd)


def kernel(x, c, w_ada, b_ada, norm1_g, w_in, lam_re, lam_im, log_dt, b_re, b_im, c_re, c_im,
           d_skip, w_glu, b_glu, w_a, w_b, w_o, norm2_g, w_ffn_gate, w_ffn_up, w_ffn_down,
           norm_f_g):
    depth = w_ada.shape[0]
    bsz, seq, d = x.shape
    sw = w_glu.shape[1]
    aw = w_b.shape[1]
    for l in range(depth):
        mod = _ada(c, w_ada[l], b_ada[l]).reshape(bsz, N_ADA, d)
        u, q, k, v, ga, gb = _in_proj(x, mod, norm1_g[l].reshape(1, d), w_in[l].astype(BF16),
                                      sw, aw)
        wb, cm, lamr, lami = _s5_weights(lam_re[l], lam_im[l], log_dt[l], b_re[l], b_im[l],
                                         c_re[l], c_im[l])
        s5o = _s5(u, wb, cm, lamr, lami, d_skip[l], w_glu[l].astype(BF16), b_glu[l])
        attn = _attn(q, k, v)
        x = _out_ffn(x, s5o, attn, ga, gb, mod, norm2_g[l].reshape(1, d), norm_f_g.reshape(1, d),
                     w_a[l].astype(BF16), w_b[l].astype(BF16), w_o[l].astype(BF16),
                     w_ffn_gate[l].astype(BF16), w_ffn_up[l].astype(BF16),
                     w_ffn_down[l].astype(BF16), final_norm=(l == depth - 1))
    return x
```

```python
import functools
import math

import numpy as np
import jax
import jax.numpy as jnp
from jax import lax
from jax.experimental import pallas as pl
from jax.experimental.pallas import tpu as pltpu

F32 = jnp.float32
BF16 = jnp.bfloat16

S5_GROUP = 16
S5_STATE = 64
HEAD_DIM = 64
N_ADA = 6
RMS_EPS = 1e-6
Q_SCALE = math.log2(math.e) / math.sqrt(HEAD_DIM)
UNDERFLOW_LOG2 = 151.0

LANES = 128
SUBLANES = 8
VMEM_LIMIT_BYTES = 56 * 1024 * 1024

ATTN_BLOCK = 128
ATTN_REGION = 3
S5_TILE = 128
S5_SLABS = 4
IN_TILE = 256
OUT_TILE = 256


def _dot(a, b):
    return jnp.dot(a, b, preferred_element_type=F32)


def _rms(x):
    return x * lax.rsqrt(jnp.mean(x * x, axis=-1, keepdims=True) + RMS_EPS)


def _ada_kernel(c_ref, w_ref, b_ref, o_ref):
    c = c_ref[...]
    cond = c * jax.nn.sigmoid(c)
    o_ref[...] = _dot(cond.astype(BF16), w_ref[...].astype(BF16)) + b_ref[...]


def _ada(c, w_ada, b_ada):
    bsz, d = c.shape
    n = w_ada.shape[1]
    rows = -(-bsz // SUBLANES) * SUBLANES
    cp = jnp.zeros((rows, d), F32).at[:bsz].set(c)
    tn = 1536
    out = pl.pallas_call(
        _ada_kernel,
        grid=(n // tn,),
        in_specs=[pl.BlockSpec((rows, d), lambda j: (0, 0)),
                  pl.BlockSpec((d, tn), lambda j: (0, j)),
                  pl.BlockSpec((1, tn), lambda j: (0, j))],
        out_specs=pl.BlockSpec((rows, tn), lambda j: (0, j)),
        out_shape=jax.ShapeDtypeStruct((rows, n), F32),
        name="ada",
    )(cp, w_ada, b_ada.reshape(1, n))
    return out[:bsz]


def _in_proj_kernel(x_ref, mod_ref, g_ref, w_ref, u_ref, q_ref, k_ref, v_ref, ga_ref, gb_ref,
                    *, sw, aw, d):
    x = x_ref[0]
    mod = mod_ref[0]
    h = _rms(x) * g_ref[...]
    h = h * (1.0 + mod[1:2]) + mod[0:1]
    hb = h.astype(BF16)
    o = 0
    lane = lax.broadcasted_iota(jnp.int32, (x.shape[0], aw), 1)
    even_head = (lane // HEAD_DIM) % 2 == 0
    u_ref[0] = _dot(hb, w_ref[:, o:o + sw]); o += sw
    q = (_dot(hb, w_ref[:, o:o + aw]) * Q_SCALE).astype(BF16); o += aw
    q_ref[0, 0] = jnp.where(even_head, q, jnp.zeros_like(q))
    q_ref[0, 1] = jnp.where(even_head, jnp.zeros_like(q), q)
    k_ref[0] = _dot(hb, w_ref[:, o:o + aw]).astype(BF16); o += aw
    v = _dot(hb, w_ref[:, o:o + aw]).astype(BF16); o += aw
    v_ref[0, 0] = jnp.where(even_head, v, jnp.zeros_like(v))
    v_ref[0, 1] = jnp.where(even_head, jnp.zeros_like(v), v)
    ga_ref[0] = _dot(hb, w_ref[:, o:o + d]); o += d
    gb_ref[0] = _dot(hb, w_ref[:, o:o + d])


def _in_proj(x, mod, norm_g, w_in_b, sw, aw):
    bsz, seq, d = x.shape
    tm = IN_TILE
    n = w_in_b.shape[1]
    tok = lambda w: pl.BlockSpec((1, tm, w), lambda b, i: (b, i, 0))
    tok2 = lambda w: pl.BlockSpec((1, 2, tm, w), lambda b, i: (b, 0, i, 0))
    return pl.pallas_call(
        functools.partial(_in_proj_kernel, sw=sw, aw=aw, d=d),
        grid=(bsz, seq // tm),
        in_specs=[tok(d),
                  pl.BlockSpec((1, N_ADA, d), lambda b, i: (b, 0, 0)),
                  pl.BlockSpec((1, d), lambda b, i: (0, 0)),
                  pl.BlockSpec((d, n), lambda b, i: (0, 0))],
        out_specs=[tok(sw), tok2(aw), tok(aw), tok2(aw), tok(d), tok(d)],
        out_shape=[jax.ShapeDtypeStruct((bsz, seq, sw), F32),
                   jax.ShapeDtypeStruct((bsz, 2, seq, aw), BF16),
                   jax.ShapeDtypeStruct((bsz, seq, aw), BF16),
                   jax.ShapeDtypeStruct((bsz, 2, seq, aw), BF16),
                   jax.ShapeDtypeStruct((bsz, seq, d), F32),
                   jax.ShapeDtypeStruct((bsz, seq, d), F32)],
        compiler_params=pltpu.CompilerParams(
            dimension_semantics=("arbitrary", "arbitrary"),
            vmem_limit_bytes=VMEM_LIMIT_BYTES),
        name="in_proj",
    )(x, mod, norm_g, w_in_b)


def _s5_kernel(u_ref, perm_ref, permt_ref, wb_ref, cm_ref, lamr_ref, lami_ref, d_ref,
               wglu_ref, bglu_ref, o_ref, bu_scr, x_scr, ulast_scr, *, tm, sw):
    i = pl.program_id(0)
    n2 = tm // 2
    rows = SUBLANES * n2
    cw = sw // S5_SLABS
    hs = cw * S5_STATE // S5_GROUP

    @pl.when(i == 0)
    def _():
        x_scr[...] = jnp.zeros_like(x_scr)
        ulast_scr[...] = jnp.zeros_like(ulast_scr)

    u_nat = u_ref[...].reshape(4 * tm, sw)
    a_cur = _dot(perm_ref[...], u_nat.astype(BF16))
    odd = (lax.broadcasted_iota(jnp.int32, (rows, sw), 0) & 1) == 1
    a_prev = jnp.where(odd, pltpu.roll(a_cur, 1, 0), pltpu.roll(a_cur, SUBLANES - 1, 0))
    sub = lax.broadcasted_iota(jnp.int32, (SUBLANES, sw), 0)
    first = jnp.where((sub & 1) == 1, a_prev[:SUBLANES], ulast_scr[...])
    a_prev = jnp.concatenate([first, a_prev[SUBLANES:]], axis=0)

    last = jnp.zeros((SUBLANES, sw), F32)
    for b in range(4):
        row = u_ref[b, tm - 1:tm, :].astype(BF16).astype(F32)
        last = jnp.where(sub == 2 * b, jnp.broadcast_to(row, (SUBLANES, sw)), last)
    ulast_scr[...] = last

    a_cur = a_cur.astype(BF16)
    a_prev = a_prev.astype(BF16)
    for s in range(S5_SLABS):
        lhs = jnp.concatenate([a_cur[:, s * cw:(s + 1) * cw],
                               a_prev[:, s * cw:(s + 1) * cw]], axis=1)
        bu_scr[:, 2 * hs * s:2 * hs * (s + 1)] = _dot(lhs, wb_ref[s])
    for s in range(S5_SLABS):
        re0 = 2 * hs * s
        im0 = re0 + hs
        ar = lamr_ref[:, hs * s:hs * (s + 1)]
        ai = lami_ref[:, hs * s:hs * (s + 1)]
        xr = x_scr[s, 0]
        xi = x_scr[s, 1]
        for t2 in range(n2):
            rs = slice(SUBLANES * t2, SUBLANES * (t2 + 1))
            br = bu_scr[rs, re0:re0 + hs]
            bi = bu_scr[rs, im0:im0 + hs]
            xr, xi = ar * xr - ai * xi + br, ar * xi + ai * xr + bi
            bu_scr[rs, re0:re0 + hs] = xr
            bu_scr[rs, im0:im0 + hs] = xi
        x_scr[s, 0] = xr
        x_scr[s, 1] = xi
    ys = [_dot(bu_scr[:, 2 * hs * s:2 * hs * (s + 1)].astype(BF16), cm_ref[s])
          for s in range(S5_SLABS)]
    y_il = jnp.concatenate(ys, axis=1)
    hi = y_il.astype(BF16)
    lo = (y_il - hi.astype(F32)).astype(BF16)
    y = _dot(permt_ref[...], jnp.concatenate([hi, lo], axis=0))
    y = y + d_ref[...] * u_nat
    y = jax.nn.gelu(y)
    z = _dot(y.astype(BF16), wglu_ref[...]) + bglu_ref[...]
    o_ref[...] = (y * jax.nn.sigmoid(z)).reshape(4, tm, sw)


def _s5_perms(tm):
    n2 = tm // 2
    rows = SUBLANES * n2
    perm = np.zeros((rows, 4 * tm), np.float32)
    permt = np.zeros((4 * tm, 2 * rows), np.float32)
    for t2 in range(n2):
        for b in range(4):
            for par in range(2):
                r = SUBLANES * t2 + 2 * b + par
                t = 2 * t2 + par
                perm[r, b * tm + t] = 1.0
                permt[b * tm + t, r] = 1.0
                permt[b * tm + t, rows + r] = 1.0
    return jnp.asarray(perm, BF16), jnp.asarray(permt, BF16)


def _block_diag(blocks):
    n, a, b = blocks.shape
    eye = jnp.eye(n, dtype=blocks.dtype)
    return jnp.einsum('nab,nm->namb', blocks, eye).reshape(n * a, n * b)


def _s5_weights(lam_re, lam_im, log_dt, b_re, b_im, c_re, c_im):
    g = lam_re.shape[0]
    dt = jnp.exp(log_dt)[:, None]
    mag = jnp.exp(lam_re * dt)
    lbr = mag * jnp.cos(lam_im * dt)
    lbi = mag * jnp.sin(lam_im * dt)
    nr, ni = lbr - 1.0, lbi
    den = lam_re * lam_re + lam_im * lam_im
    cr = (nr * lam_re + ni * lam_im) / den
    ci = (ni * lam_re - nr * lam_im) / den
    bbr = cr[..., None] * b_re - ci[..., None] * b_im
    bbi = cr[..., None] * b_im + ci[..., None] * b_re
    lr = lbr[..., None] * bbr - lbi[..., None] * bbi
    li = lbr[..., None] * bbi + lbi[..., None] * bbr
    l2r = lbr * lbr - lbi * lbi
    l2i = 2.0 * lbr * lbi
    gh = g // S5_SLABS
    wbs, cms = [], []
    for h in range(S5_SLABS):
        s = slice(h * gh, (h + 1) * gh)
        t = lambda a: jnp.swapaxes(a[s], 1, 2)
        top = jnp.concatenate([_block_diag(t(bbr)), _block_diag(t(bbi))], axis=1)
        bot = jnp.concatenate([_block_diag(t(lr)), _block_diag(t(li))], axis=1)
        wbs.append(jnp.concatenate([top, bot], axis=0))
        cms.append(jnp.concatenate([_block_diag(jnp.swapaxes(c_re[s], 1, 2)),
                                    -_block_diag(jnp.swapaxes(c_im[s], 1, 2))], axis=0))
    wb = jnp.stack(wbs).astype(BF16)
    cm = jnp.stack(cms).astype(BF16)
    lamr = jnp.broadcast_to(l2r.reshape(1, -1), (SUBLANES, l2r.size))
    lami = jnp.broadcast_to(l2i.reshape(1, -1), (SUBLANES, l2i.size))
    return wb, cm, lamr, lami


def _s5(u, wb, cm, lamr, lami, d_skip, w_glu_b, b_glu):
    bsz, seq, sw = u.shape
    assert bsz == 4, "the scan packs 4 batch rows x 2 token parities into 8 sublanes"
    tm = S5_TILE
    n2 = tm // 2
    rows = SUBLANES * n2
    ns = lamr.shape[1]
    perm, permt = _s5_perms(tm)
    const = lambda a: pl.BlockSpec(a.shape, lambda i: (0,) * a.ndim)
    d_row = d_skip.reshape(1, sw)
    bg = b_glu.reshape(1, sw)
    return pl.pallas_call(
        functools.partial(_s5_kernel, tm=tm, sw=sw),
        grid=(seq // tm,),
        in_specs=[pl.BlockSpec((4, tm, sw), lambda i: (0, i, 0)),
                  const(perm), const(permt), const(wb), const(cm), const(lamr), const(lami),
                  const(d_row), const(w_glu_b), const(bg)],
        out_specs=pl.BlockSpec((4, tm, sw), lambda i: (0, i, 0)),
        out_shape=jax.ShapeDtypeStruct((bsz, seq, sw), F32),
        scratch_shapes=[pltpu.VMEM((rows, 2 * ns), F32),
                        pltpu.VMEM((S5_SLABS, 2, SUBLANES, ns // S5_SLABS), F32),
                        pltpu.VMEM((SUBLANES, sw), F32)],
        compiler_params=pltpu.CompilerParams(
            dimension_semantics=("arbitrary",),
            vmem_limit_bytes=VMEM_LIMIT_BYTES),
        name="s5",
    )(u, perm, permt, wb, cm, lamr, lami, d_row, w_glu_b, bg)


def _attn_kernel(q_ref, k_ref, v_ref, tri_ref, o_ref, c_scr, acc_scr, z_scr, w_scr, *, n_pairs):
    blk = ATTN_BLOCK
    nh = 2 * n_pairs
    qi = pl.program_id(1)
    row = lax.broadcasted_iota(jnp.int32, (blk, blk), 0)
    col = lax.broadcasted_iota(jnp.int32, (blk, blk), 1)
    causal = col < row
    nt = (((1,), (1,)), ((), ()))

    def region(kbs, diag_first):
        starts = [pl.multiple_of(kb * blk, blk) for kb in kbs]
        for s, start in enumerate(starts):
            for p in range(n_pairs):
                ls = slice(p * LANES, (p + 1) * LANES)
                kblk = k_ref[0, pl.ds(start, blk), ls]
                for hh in range(2):
                    z_scr[s * nh + 2 * p + hh] = lax.dot_general(
                        q_ref[0, hh, :, ls], kblk, nt, preferred_element_type=F32)
        for s in range(len(kbs)):
            valid = causal if (diag_first and s == 0) else None
            for h in range(nh):
                z = z_scr[s * nh + h]
                sp = jnp.maximum(z, 0.0) + jnp.log2(1.0 + jnp.exp2(-jnp.abs(z)))
                if valid is not None:
                    sp = jnp.where(valid, sp, 0.0)
                incl = _dot(sp.astype(BF16), tri_ref[...])
                c = c_scr[h]
                w = jnp.exp2(z - incl - c)
                if valid is not None:
                    w = jnp.where(valid, w, 0.0)
                w_scr[s * nh + h] = w.astype(BF16)
                c_scr[h] = c + jnp.broadcast_to(incl[:, 0:1], (blk, blk))
        for p in range(n_pairs):
            ls = slice(p * LANES, (p + 1) * LANES)
            ww = jnp.concatenate([w_scr[s * nh + 2 * p + hh]
                                  for s in range(len(kbs)) for hh in range(2)], axis=1)
            vv = jnp.concatenate([v_ref[0, hh, pl.ds(start, blk), ls]
                                  for start in starts for hh in range(2)], axis=0)
            acc_scr[p] = acc_scr[p] + _dot(ww, vv)

    def c_min():
        m = c_scr[0]
        for h in range(1, 2 * n_pairs):
            m = jnp.minimum(m, c_scr[h])
        return jnp.min(m)

    c_scr[...] = jnp.zeros_like(c_scr)
    acc_scr[...] = jnp.zeros_like(acc_scr)

    @pl.when(qi >= 2)
    def _():
        region([qi, qi - 1, qi - 2], True)

    @pl.when(qi == 1)
    def _():
        region([1, 0], True)

    @pl.when(qi == 0)
    def _():
        region([0], True)

    def more(carry):
        kb, cmin = carry
        return jnp.logical_and(kb >= 0, cmin < UNDERFLOW_LOG2)

    def body(carry):
        kb, _ = carry
        region([kb], False)
        return kb - 1, c_min()

    lax.while_loop(more, body, (qi - 3, c_min()))
    for p in range(n_pairs):
        o_ref[0, :, p * LANES:(p + 1) * LANES] = acc_scr[p].astype(o_ref.dtype)


def _attn_tri():
    blk = ATTN_BLOCK
    m = np.arange(blk)[:, None]
    j = np.arange(blk)[None, :]
    u = (m >= j).astype(np.float32)
    return jnp.asarray(u, BF16)


def _attn(q, k, v):
    bsz, seq, aw = k.shape
    blk = ATTN_BLOCK
    n_pairs = aw // LANES
    tri = _attn_tri()
    return pl.pallas_call(
        functools.partial(_attn_kernel, n_pairs=n_pairs),
        grid=(bsz, seq // blk),
        in_specs=[pl.BlockSpec((1, 2, blk, aw), lambda b, i: (b, 0, i, 0)),
                  pl.BlockSpec((1, seq, aw), lambda b, i: (b, 0, 0)),
                  pl.BlockSpec((1, 2, seq, aw), lambda b, i: (b, 0, 0, 0)),
                  pl.BlockSpec(tri.shape, lambda b, i: (0, 0))],
        out_specs=pl.BlockSpec((1, blk, aw), lambda b, i: (b, i, 0)),
        out_shape=jax.ShapeDtypeStruct((bsz, seq, aw), BF16),
        scratch_shapes=[pltpu.VMEM((2 * n_pairs, blk, blk), F32),
                        pltpu.VMEM((n_pairs, blk, LANES), F32),
                        pltpu.VMEM((ATTN_REGION * 2 * n_pairs, blk, blk), F32),
                        pltpu.VMEM((ATTN_REGION * 2 * n_pairs, blk, blk), BF16)],
        compiler_params=pltpu.CompilerParams(
            dimension_semantics=("arbitrary", "arbitrary"),
            vmem_limit_bytes=VMEM_LIMIT_BYTES),
        name="attn",
    )(q, k, v, tri)


def _out_ffn_kernel(x_ref, s5_ref, at_ref, ga_ref, gb_ref, mod_ref, n2_ref, nf_ref,
                    wa_ref, wb_ref, wo_ref, wg_ref, wu_ref, wd_ref, o_ref, *, final_norm):
    mod = mod_ref[0]
    ya = _dot(s5_ref[0].astype(BF16), wa_ref[...])
    yb = _dot(at_ref[0], wb_ref[...])
    m = jax.nn.sigmoid(ga_ref[0]) * ya + jax.nn.sigmoid(gb_ref[0]) * yb
    x1 = x_ref[0] + mod[2:3] * _dot(m.astype(BF16), wo_ref[...])
    h = _rms(x1) * n2_ref[...]
    h = (h * (1.0 + mod[4:5]) + mod[3:4]).astype(BF16)
    gate = _dot(h, wg_ref[...])
    up = _dot(h, wu_ref[...])
    act = (gate * jax.nn.sigmoid(gate) * up).astype(BF16)
    x2 = x1 + mod[5:6] * _dot(act, wd_ref[...])
    o_ref[0] = _rms(x2) * nf_ref[...] if final_norm else x2


def _out_ffn(x, s5o, attn, ga, gb, mod, n2g, nfg, wa, wb, wo, wg, wu, wd, final_norm):
    bsz, seq, d = x.shape
    tm = OUT_TILE
    tok = lambda a: pl.BlockSpec((1, tm, a.shape[-1]), lambda b, i: (b, i, 0))
    const = lambda a: pl.BlockSpec(a.shape, lambda b, i: (0,) * a.ndim,
                                   pipeline_mode=pl.Buffered(1))
    return pl.pallas_call(
        functools.partial(_out_ffn_kernel, final_norm=final_norm),
        grid=(bsz, seq // tm),
        in_specs=[tok(x), tok(s5o), tok(attn), tok(ga), tok(gb),
                  pl.BlockSpec((1, N_ADA, d), lambda b, i: (b, 0, 0)),
                  const(n2g), const(nfg),
                  const(wa), const(wb), const(wo), const(wg), const(wu), const(wd)],
        out_specs=pl.BlockSpec((1, tm, d), lambda b, i: (b, i, 0)),
        out_shape=jax.ShapeDtypeStruct((bsz, seq, d), F32),
        compiler_params=pltpu.CompilerParams(
            dimension_semantics=("arbitrary", "arbitrary"),
            vmem_limit_bytes=VMEM_LIMIT_BYTES),
        name="out_ffn",
    )(x, s5o, attn, ga, gb, mod, n2g, nfg, wa, wb, wo, wg, wu, wd)


def kernel(x, c, w_ada, b_ada, norm1_g, w_in, lam_re, lam_im, log_dt, b_re, b_im, c_re, c_im,
           d_skip, w_glu, b_glu, w_a, w_b, w_o, norm2_g, w_ffn_gate, w_ffn_up, w_ffn_down,
           norm_f_g):
    depth = w_ada.shape[0]
    bsz, seq, d = x.shape
    sw = w_glu.shape[1]
    aw = w_b.shape[1]
    for l in range(depth):
        mod = _ada(c, w_ada[l], b_ada[l]).reshape(bsz, N_ADA, d)
        u, q, k, v, ga, gb = _in_proj(x, mod, norm1_g[l].reshape(1, d), w_in[l].astype(BF16),
                                      sw, aw)
        wb, cm, lamr, lami = _s5_weights(lam_re[l], lam_im[l], log_dt[l], b_re[l], b_im[l],
                                         c_re[l], c_im[l])
        s5o = _s5(u, wb, cm, lamr, lami, d_skip[l], w_glu[l].astype(BF16), b_glu[l])
        attn = _attn(q, k, v)
        x = _out_ffn(x, s5o, attn, ga, gb, mod, norm2_g[l].reshape(1, d), norm_f_g.reshape(1, d),
                     w_a[l].astype(BF16), w_b[l].astype(BF16), w_o[l].astype(BF16),
                     w_ffn_gate[l].astype(BF16), w_ffn_up[l].astype(BF16),
                     w_ffn_down[l].astype(BF16), final_norm=(l == depth - 1))
    return x
```

```python
import functools
import math

import numpy as np
import jax
import jax.numpy as jnp
from jax import lax
from jax.experimental import pallas as pl
from jax.experimental.pallas import tpu as pltpu

F32 = jnp.float32
BF16 = jnp.bfloat16

S5_GROUP = 16
S5_STATE = 64
HEAD_DIM = 64
N_ADA = 6
RMS_EPS = 1e-6
Q_SCALE = math.log2(math.e) / math.sqrt(HEAD_DIM)
UNDERFLOW_LOG2 = 151.0

LANES = 128
SUBLANES = 8
VMEM_LIMIT_BYTES = 56 * 1024 * 1024

ATTN_BLOCK = 128
ATTN_REGION = 3
ATTN_TOP_ROWS = 32
S5_TILE = 128
S5_SLABS = 4
IN_TILE = 512
OUT_TILE = 512
OUT_GROUPS = 2


def _dot(a, b):
    return jnp.dot(a, b, preferred_element_type=F32)


def _rms(x):
    return x * lax.rsqrt(jnp.mean(x * x, axis=-1, keepdims=True) + RMS_EPS)


def _ada_kernel(c_ref, w_ref, b_ref, o_ref):
    c = c_ref[...]
    cond = c * jax.nn.sigmoid(c)
    o_ref[...] = _dot(cond.astype(BF16), w_ref[...].astype(BF16)) + b_ref[...]


def _ada(c, w_ada, b_ada):
    bsz, d = c.shape
    n = w_ada.shape[1]
    rows = -(-bsz // SUBLANES) * SUBLANES
    cp = jnp.zeros((rows, d), F32).at[:bsz].set(c)
    tn = 1536
    out = pl.pallas_call(
        _ada_kernel,
        grid=(n // tn,),
        in_specs=[pl.BlockSpec((rows, d), lambda j: (0, 0)),
                  pl.BlockSpec((d, tn), lambda j: (0, j)),
                  pl.BlockSpec((1, tn), lambda j: (0, j))],
        out_specs=pl.BlockSpec((rows, tn), lambda j: (0, j)),
        out_shape=jax.ShapeDtypeStruct((rows, n), F32),
        name="ada",
    )(cp, w_ada, b_ada.reshape(1, n))
    return out[:bsz]


def _in_proj_kernel(x_ref, mod_ref, g_ref, w_ref, u_ref, q_ref, k_ref, v_ref, ga_ref, gb_ref,
                    *, sw, aw, d):
    x = x_ref[0]
    mod = mod_ref[0]
    h = _rms(x) * g_ref[...]
    h = h * (1.0 + mod[1:2]) + mod[0:1]
    hb = h.astype(BF16)
    o = 0
    lane = lax.broadcasted_iota(jnp.int32, (x.shape[0], aw), 1)
    even_head = (lane // HEAD_DIM) % 2 == 0
    u_ref[0] = _dot(hb, w_ref[:, o:o + sw]); o += sw
    q = (_dot(hb, w_ref[:, o:o + aw]) * Q_SCALE).astype(BF16); o += aw
    q_ref[0, 0] = jnp.where(even_head, q, jnp.zeros_like(q))
    q_ref[0, 1] = jnp.where(even_head, jnp.zeros_like(q), q)
    k_ref[0] = _dot(hb, w_ref[:, o:o + aw]).astype(BF16); o += aw
    v = _dot(hb, w_ref[:, o:o + aw]).astype(BF16); o += aw
    v_ref[0, 0] = jnp.where(even_head, v, jnp.zeros_like(v))
    v_ref[0, 1] = jnp.where(even_head, jnp.zeros_like(v), v)
    ga_ref[0] = _dot(hb, w_ref[:, o:o + d]); o += d
    gb_ref[0] = _dot(hb, w_ref[:, o:o + d])


def _in_proj(x, mod, norm_g, w_in_b, sw, aw):
    bsz, seq, d = x.shape
    tm = IN_TILE
    n = w_in_b.shape[1]
    tok = lambda w: pl.BlockSpec((1, tm, w), lambda b, i: (b, i, 0))
    tok2 = lambda w: pl.BlockSpec((1, 2, tm, w), lambda b, i: (b, 0, i, 0))
    return pl.pallas_call(
        functools.partial(_in_proj_kernel, sw=sw, aw=aw, d=d),
        grid=(bsz, seq // tm),
        in_specs=[tok(d),
                  pl.BlockSpec((1, N_ADA, d), lambda b, i: (b, 0, 0)),
                  pl.BlockSpec((1, d), lambda b, i: (0, 0)),
                  pl.BlockSpec((d, n), lambda b, i: (0, 0))],
        out_specs=[tok(sw), tok2(aw), tok(aw), tok2(aw), tok(d), tok(d)],
        out_shape=[jax.ShapeDtypeStruct((bsz, seq, sw), F32),
                   jax.ShapeDtypeStruct((bsz, 2, seq, aw), BF16),
                   jax.ShapeDtypeStruct((bsz, seq, aw), BF16),
                   jax.ShapeDtypeStruct((bsz, 2, seq, aw), BF16),
                   jax.ShapeDtypeStruct((bsz, seq, d), F32),
                   jax.ShapeDtypeStruct((bsz, seq, d), F32)],
        compiler_params=pltpu.CompilerParams(
            dimension_semantics=("arbitrary", "arbitrary"),
            vmem_limit_bytes=VMEM_LIMIT_BYTES),
        name="in_proj",
    )(x, mod, norm_g, w_in_b)


def _s5_kernel(u_ref, perm_ref, permt_ref, wb_ref, cm_ref, lamr_ref, lami_ref, d_ref,
               wglu_ref, bglu_ref, o_ref, bu_scr, x_scr, ulast_scr, *, tm, sw):
    i = pl.program_id(0)
    n2 = tm // 2
    rows = SUBLANES * n2
    cw = sw // S5_SLABS
    hs = cw * S5_STATE // S5_GROUP

    @pl.when(i == 0)
    def _():
        x_scr[...] = jnp.zeros_like(x_scr)
        ulast_scr[...] = jnp.zeros_like(ulast_scr)

    u_nat = u_ref[...].reshape(4 * tm, sw)
    a_cur = _dot(perm_ref[...], u_nat.astype(BF16))
    odd = (lax.broadcasted_iota(jnp.int32, (rows, sw), 0) & 1) == 1
    a_prev = jnp.where(odd, pltpu.roll(a_cur, 1, 0), pltpu.roll(a_cur, SUBLANES - 1, 0))
    sub = lax.broadcasted_iota(jnp.int32, (SUBLANES, sw), 0)
    first = jnp.where((sub & 1) == 1, a_prev[:SUBLANES], ulast_scr[...])
    a_prev = jnp.concatenate([first, a_prev[SUBLANES:]], axis=0)

    last = jnp.zeros((SUBLANES, sw), F32)
    for b in range(4):
        row = u_ref[b, tm - 1:tm, :].astype(BF16).astype(F32)
        last = jnp.where(sub == 2 * b, jnp.broadcast_to(row, (SUBLANES, sw)), last)
    ulast_scr[...] = last

    a_cur = a_cur.astype(BF16)
    a_prev = a_prev.astype(BF16)
    for s in range(S5_SLABS):
        lhs = jnp.concatenate([a_cur[:, s * cw:(s + 1) * cw],
                               a_prev[:, s * cw:(s + 1) * cw]], axis=1)
        bu_scr[:, 2 * hs * s:2 * hs * (s + 1)] = _dot(lhs, wb_ref[s])
    for s in range(S5_SLABS):
        re0 = 2 * hs * s
        im0 = re0 + hs
        ar = lamr_ref[:, hs * s:hs * (s + 1)]
        ai = lami_ref[:, hs * s:hs * (s + 1)]
        xr = x_scr[s, 0]
        xi = x_scr[s, 1]
        for t2 in range(n2):
            rs = slice(SUBLANES * t2, SUBLANES * (t2 + 1))
            br = bu_scr[rs, re0:re0 + hs]
            bi = bu_scr[rs, im0:im0 + hs]
            xr, xi = ar * xr - ai * xi + br, ar * xi + ai * xr + bi
            bu_scr[rs, re0:re0 + hs] = xr
            bu_scr[rs, im0:im0 + hs] = xi
        x_scr[s, 0] = xr
        x_scr[s, 1] = xi
    ys = [_dot(bu_scr[:, 2 * hs * s:2 * hs * (s + 1)].astype(BF16), cm_ref[s])
          for s in range(S5_SLABS)]
    y_il = jnp.concatenate(ys, axis=1)
    hi = y_il.astype(BF16)
    lo = (y_il - hi.astype(F32)).astype(BF16)
    y = _dot(permt_ref[...], jnp.concatenate([hi, lo], axis=0))
    y = y + d_ref[...] * u_nat
    y = jax.nn.gelu(y)
    z = _dot(y.astype(BF16), wglu_ref[...]) + bglu_ref[...]
    o_ref[...] = (y * jax.nn.sigmoid(z)).reshape(4, tm, sw)


def _s5_perms(tm):
    n2 = tm // 2
    rows = SUBLANES * n2
    perm = np.zeros((rows, 4 * tm), np.float32)
    permt = np.zeros((4 * tm, 2 * rows), np.float32)
    for t2 in range(n2):
        for b in range(4):
            for par in range(2):
                r = SUBLANES * t2 + 2 * b + par
                t = 2 * t2 + par
                perm[r, b * tm + t] = 1.0
                permt[b * tm + t, r] = 1.0
                permt[b * tm + t, rows + r] = 1.0
    return jnp.asarray(perm, BF16), jnp.asarray(permt, BF16)


def _block_diag(blocks):
    n, a, b = blocks.shape
    eye = jnp.eye(n, dtype=blocks.dtype)
    return jnp.einsum('nab,nm->namb', blocks, eye).reshape(n * a, n * b)


def _s5_weights(lam_re, lam_im, log_dt, b_re, b_im, c_re, c_im):
    g = lam_re.shape[0]
    dt = jnp.exp(log_dt)[:, None]
    mag = jnp.exp(lam_re * dt)
    lbr = mag * jnp.cos(lam_im * dt)
    lbi = mag * jnp.sin(lam_im * dt)
    nr, ni = lbr - 1.0, lbi
    den = lam_re * lam_re + lam_im * lam_im
    cr = (nr * lam_re + ni * lam_im) / den
    ci = (ni * lam_re - nr * lam_im) / den
    bbr = cr[..., None] * b_re - ci[..., None] * b_im
    bbi = cr[..., None] * b_im + ci[..., None] * b_re
    lr = lbr[..., None] * bbr - lbi[..., None] * bbi
    li = lbr[..., None] * bbi + lbi[..., None] * bbr
    l2r = lbr * lbr - lbi * lbi
    l2i = 2.0 * lbr * lbi
    gh = g // S5_SLABS
    wbs, cms = [], []
    for h in range(S5_SLABS):
        s = slice(h * gh, (h + 1) * gh)
        t = lambda a: jnp.swapaxes(a[s], 1, 2)
        top = jnp.concatenate([_block_diag(t(bbr)), _block_diag(t(bbi))], axis=1)
        bot = jnp.concatenate([_block_diag(t(lr)), _block_diag(t(li))], axis=1)
        wbs.append(jnp.concatenate([top, bot], axis=0))
        cms.append(jnp.concatenate([_block_diag(jnp.swapaxes(c_re[s], 1, 2)),
                                    -_block_diag(jnp.swapaxes(c_im[s], 1, 2))], axis=0))
    wb = jnp.stack(wbs).astype(BF16)
    cm = jnp.stack(cms).astype(BF16)
    lamr = jnp.broadcast_to(l2r.reshape(1, -1), (SUBLANES, l2r.size))
    lami = jnp.broadcast_to(l2i.reshape(1, -1), (SUBLANES, l2i.size))
    return wb, cm, lamr, lami


def _s5(u, wb, cm, lamr, lami, d_skip, w_glu_b, b_glu):
    bsz, seq, sw = u.shape
    assert bsz == 4, "the scan packs 4 batch rows x 2 token parities into 8 sublanes"
    tm = S5_TILE
    n2 = tm // 2
    rows = SUBLANES * n2
    ns = lamr.shape[1]
    perm, permt = _s5_perms(tm)
    const = lambda a: pl.BlockSpec(a.shape, lambda i: (0,) * a.ndim)
    d_row = d_skip.reshape(1, sw)
    bg = b_glu.reshape(1, sw)
    return pl.pallas_call(
        functools.partial(_s5_kernel, tm=tm, sw=sw),
        grid=(seq // tm,),
        in_specs=[pl.BlockSpec((4, tm, sw), lambda i: (0, i, 0)),
                  const(perm), const(permt), const(wb), const(cm), const(lamr), const(lami),
                  const(d_row), const(w_glu_b), const(bg)],
        out_specs=pl.BlockSpec((4, tm, sw), lambda i: (0, i, 0)),
        out_shape=jax.ShapeDtypeStruct((bsz, seq, sw), F32),
        scratch_shapes=[pltpu.VMEM((rows, 2 * ns), F32),
                        pltpu.VMEM((S5_SLABS, 2, SUBLANES, ns // S5_SLABS), F32),
                        pltpu.VMEM((SUBLANES, sw), F32)],
        compiler_params=pltpu.CompilerParams(
            dimension_semantics=("arbitrary",),
            vmem_limit_bytes=VMEM_LIMIT_BYTES),
        name="s5",
    )(u, perm, permt, wb, cm, lamr, lami, d_row, w_glu_b, bg)


def _attn_kernel(q_ref, k_ref, v_ref, tri_ref, o_ref, c_scr, acc_scr, z_scr, w_scr, *, n_pairs):
    blk = ATTN_BLOCK
    top = ATTN_TOP_ROWS
    nh = 2 * n_pairs
    qi = pl.program_id(1)
    nt = (((1,), (1,)), ((), ()))

    def region(units, fresh):
        starts = [pl.multiple_of(kb * blk, blk) for kb, _, _ in units]
        for u, (_, r0, r1) in enumerate(units):
            for p in range(n_pairs):
                ls = slice(p * LANES, (p + 1) * LANES)
                kblk = k_ref[0, pl.ds(starts[u], blk), ls]
                for hh in range(2):
                    z_scr[u * nh + 2 * p + hh, r0:r1] = lax.dot_general(
                        q_ref[0, hh, r0:r1, ls], kblk, nt, preferred_element_type=F32)
        for u, (_, r0, r1) in enumerate(units):
            first = fresh and u == 0
            if first:
                row = lax.broadcasted_iota(jnp.int32, (r1 - r0, blk), 0) + r0
                valid = lax.broadcasted_iota(jnp.int32, (r1 - r0, blk), 1) < row
            for h in range(nh):
                z = z_scr[u * nh + h, r0:r1]
                sp = jnp.maximum(z, 0.0) + jnp.log2(1.0 + jnp.exp2(-jnp.abs(z)))
                if first:
                    sp = jnp.where(valid, sp, 0.0)
                incl = _dot(sp.astype(BF16), tri_ref[...])
                total = jnp.broadcast_to(incl[:, 0:1], (r1 - r0, blk))
                if first:
                    w = jnp.where(valid, jnp.exp2(z - incl), 0.0)
                    c_scr[h, r0:r1] = total
                else:
                    c = c_scr[h, r0:r1]
                    w = jnp.exp2(z - incl - c)
                    c_scr[h, r0:r1] = c + total
                w_scr[u * nh + h, r0:r1] = w.astype(BF16)
        ranges = list(dict.fromkeys((r0, r1) for _, r0, r1 in units))
        for g, (r0, r1) in enumerate(ranges):
            us = [u for u, (_, a, b) in enumerate(units) if (a, b) == (r0, r1)]
            for p in range(n_pairs):
                ls = slice(p * LANES, (p + 1) * LANES)
                ww = jnp.concatenate([w_scr[u * nh + 2 * p + hh, r0:r1]
                                      for u in us for hh in range(2)], axis=1)
                vv = jnp.concatenate([v_ref[0, hh, pl.ds(starts[u], blk), ls]
                                      for u in us for hh in range(2)], axis=0)
                if fresh and g == 0:
                    acc_scr[p, r0:r1] = _dot(ww, vv)
                else:
                    acc_scr[p, r0:r1] = acc_scr[p, r0:r1] + _dot(ww, vv)

    def c_min(r0, r1):
        m = c_scr[0, r0:r1]
        for h in range(1, nh):
            m = jnp.minimum(m, c_scr[h, r0:r1])
        return jnp.min(m)

    @pl.when(qi >= 2)
    def _():
        region([(qi, 0, blk), (qi - 1, 0, blk), (qi - 2, 0, top)], True)

    @pl.when(qi == 1)
    def _():
        region([(1, 0, blk), (0, 0, blk)], True)

    @pl.when(qi == 0)
    def _():
        region([(0, 0, blk)], True)

    def sweep(first_kb, cmin, r0, r1):
        def more(carry):
            kb, cmin = carry
            return jnp.logical_and(kb >= 0, cmin < UNDERFLOW_LOG2)

        def body(carry):
            kb, _ = carry
            region([(kb, r0, r1)], False)
            return kb - 1, c_min(r0, r1)

        lax.while_loop(more, body, (first_kb, cmin))

    cmin_top, cmin_rest = c_min(0, top), c_min(top, blk)
    sweep(jnp.where(qi >= 2, qi - 3, -1), cmin_top, 0, top)
    sweep(jnp.where(qi >= 2, qi - 2, -1), cmin_rest, top, blk)
    for p in range(n_pairs):
        o_ref[0, :, p * LANES:(p + 1) * LANES] = acc_scr[p].astype(o_ref.dtype)


def _attn_tri():
    blk = ATTN_BLOCK
    m = np.arange(blk)[:, None]
    j = np.arange(blk)[None, :]
    u = (m >= j).astype(np.float32)
    return jnp.asarray(u, BF16)


def _attn(q, k, v):
    bsz, seq, aw = k.shape
    blk = ATTN_BLOCK
    n_pairs = aw // LANES
    tri = _attn_tri()
    return pl.pallas_call(
        functools.partial(_attn_kernel, n_pairs=n_pairs),
        grid=(bsz, seq // blk),
        in_specs=[pl.BlockSpec((1, 2, blk, aw), lambda b, i: (b, 0, i, 0)),
                  pl.BlockSpec((1, seq, aw), lambda b, i: (b, 0, 0)),
                  pl.BlockSpec((1, 2, seq, aw), lambda b, i: (b, 0, 0, 0)),
                  pl.BlockSpec(tri.shape, lambda b, i: (0, 0))],
        out_specs=pl.BlockSpec((1, blk, aw), lambda b, i: (b, i, 0)),
        out_shape=jax.ShapeDtypeStruct((bsz, seq, aw), BF16),
        scratch_shapes=[pltpu.VMEM((2 * n_pairs, blk, blk), F32),
                        pltpu.VMEM((n_pairs, blk, LANES), F32),
                        pltpu.VMEM((ATTN_REGION * 2 * n_pairs, blk, blk), F32),
                        pltpu.VMEM((ATTN_REGION * 2 * n_pairs, blk, blk), BF16)],
        compiler_params=pltpu.CompilerParams(
            dimension_semantics=("arbitrary", "arbitrary"),
            vmem_limit_bytes=VMEM_LIMIT_BYTES),
        name="attn",
    )(q, k, v, tri)


def _out_ffn_kernel(x_ref, s5_ref, at_ref, ga_ref, gb_ref, mod_ref, n2_ref, nf_ref,
                    wa_ref, wb_ref, wo_ref, wg_ref, wu_ref, wd_ref, o_ref, *, final_norm):
    mod = mod_ref[0]
    tm = x_ref.shape[1]
    groups = [slice(r, r + tm // OUT_GROUPS) for r in range(0, tm, tm // OUT_GROUPS)]
    ms = []
    for g in groups:
        ya = _dot(s5_ref[0, g].astype(BF16), wa_ref[...])
        yb = _dot(at_ref[0, g], wb_ref[...])
        m = jax.nn.sigmoid(ga_ref[0, g]) * ya + jax.nn.sigmoid(gb_ref[0, g]) * yb
        ms.append(m.astype(BF16))
    x1s, hs = [], []
    for g, m in zip(groups, ms):
        x1 = x_ref[0, g] + mod[2:3] * _dot(m, wo_ref[...])
        h = _rms(x1) * n2_ref[...]
        x1s.append(x1)
        hs.append((h * (1.0 + mod[4:5]) + mod[3:4]).astype(BF16))
    acts = []
    for h in hs:
        gate = _dot(h, wg_ref[...])
        up = _dot(h, wu_ref[...])
        acts.append((gate * jax.nn.sigmoid(gate) * up).astype(BF16))
    for g, x1, act in zip(groups, x1s, acts):
        x2 = x1 + mod[5:6] * _dot(act, wd_ref[...])
        o_ref[0, g] = _rms(x2) * nf_ref[...] if final_norm else x2


def _out_ffn(x, s5o, attn, ga, gb, mod, n2g, nfg, wa, wb, wo, wg, wu, wd, final_norm):
    bsz, seq, d = x.shape
    tm = OUT_TILE
    tok = lambda a: pl.BlockSpec((1, tm, a.shape[-1]), lambda b, i: (b, i, 0))
    const = lambda a: pl.BlockSpec(a.shape, lambda b, i: (0,) * a.ndim,
                                   pipeline_mode=pl.Buffered(1))
    return pl.pallas_call(
        functools.partial(_out_ffn_kernel, final_norm=final_norm),
        grid=(bsz, seq // tm),
        in_specs=[tok(x), tok(s5o), tok(attn), tok(ga), tok(gb),
                  pl.BlockSpec((1, N_ADA, d), lambda b, i: (b, 0, 0)),
                  const(n2g), const(nfg),
                  const(wa), const(wb), const(wo), const(wg), const(wu), const(wd)],
        out_specs=pl.BlockSpec((1, tm, d), lambda b, i: (b, i, 0)),
        out_shape=jax.ShapeDtypeStruct((bsz, seq, d), F32),
        compiler_params=pltpu.CompilerParams(
            dimension_semantics=("arbitrary", "arbitrary"),
            vmem_limit_bytes=VMEM_LIMIT_BYTES),
        name="out_ffn",
    )(x, s5o, attn, ga, gb, mod, n2g, nfg, wa, wb, wo, wg, wu, wd)


def kernel(x, c, w_ada, b_ada, norm1_g, w_in, lam_re, lam_im, log_dt, b_re, b_im, c_re, c_im,
           d_skip, w_glu, b_glu, w_a, w_b, w_o, norm2_g, w_ffn_gate, w_ffn_up, w_ffn_down,
           norm_f_g):
    depth = w_ada.shape[0]
    bsz, seq, d = x.shape
    sw = w_glu.shape[1]
    aw = w_b.shape[1]
    for l in range(depth):
        mod = _ada(c, w_ada[l], b_ada[l]).reshape(bsz, N_ADA, d)
        u, q, k, v, ga, gb = _in_proj(x, mod, norm1_g[l].reshape(1, d), w_in[l].astype(BF16),
                                      sw, aw)
        wb, cm, lamr, lami = _s5_weights(lam_re[l], lam_im[l], log_dt[l], b_re[l], b_im[l],
                                         c_re[l], c_im[l])
        s5o = _s5(u, wb, cm, lamr, lami, d_skip[l], w_glu[l].astype(BF16), b_glu[l])
        attn = _attn(q, k, v)
        x = _out_ffn(x, s5o, attn, ga, gb, mod, norm2_g[l].reshape(1, d), norm_f_g.reshape(1, d),
                     w_a[l].astype(BF16), w_b[l].astype(BF16), w_o[l].astype(BF16),
                     w_ffn_gate[l].astype(BF16), w_ffn_up[l].astype(BF16),
                     w_ffn_down[l].astype(BF16), final_norm=(l == depth - 1))
    return x
```

```python
import functools
import math

import numpy as np
import jax
import jax.numpy as jnp
from jax import lax
from jax.experimental import pallas as pl
from jax.experimental.pallas import tpu as pltpu

F32 = jnp.float32
BF16 = jnp.bfloat16

S5_GROUP = 16
S5_STATE = 64
HEAD_DIM = 64
N_ADA = 6
RMS_EPS = 1e-6
Q_SCALE = math.log2(math.e) / math.sqrt(HEAD_DIM)
UNDERFLOW_LOG2 = 151.0

LANES = 128
SUBLANES = 8
VMEM_LIMIT_BYTES = 56 * 1024 * 1024

ATTN_BLOCK = 128
ATTN_REGION = 3
ATTN_SUBS = 4
ATTN_TOP_ROWS = 32
S5_TILE = 128
S5_SLABS = 4
IN_TILE = 512
IN_GROUPS = 2
OUT_TILE = 512
OUT_GROUPS = 2


def _dot(a, b):
    return jnp.dot(a, b, preferred_element_type=F32)


def _rms(x):
    return x * lax.rsqrt(jnp.mean(x * x, axis=-1, keepdims=True) + RMS_EPS)


def _ada_kernel(c_ref, w_ref, b_ref, o_ref):
    c = c_ref[...]
    cond = c * jax.nn.sigmoid(c)
    o_ref[...] = _dot(cond.astype(BF16), w_ref[...].astype(BF16)) + b_ref[...]


def _ada(c, w_ada, b_ada):
    bsz, d = c.shape
    n = w_ada.shape[1]
    rows = -(-bsz // SUBLANES) * SUBLANES
    cp = jnp.zeros((rows, d), F32).at[:bsz].set(c)
    tn = 1536
    out = pl.pallas_call(
        _ada_kernel,
        grid=(n // tn,),
        in_specs=[pl.BlockSpec((rows, d), lambda j: (0, 0)),
                  pl.BlockSpec((d, tn), lambda j: (0, j)),
                  pl.BlockSpec((1, tn), lambda j: (0, j))],
        out_specs=pl.BlockSpec((rows, tn), lambda j: (0, j)),
        out_shape=jax.ShapeDtypeStruct((rows, n), F32),
        name="ada",
    )(cp, w_ada, b_ada.reshape(1, n))
    return out[:bsz]


def _in_proj_kernel(x_ref, mod_ref, g_ref, w_ref, u_ref, q_ref, k_ref, v_ref, ga_ref, gb_ref,
                    *, sw, aw, d):
    mod = mod_ref[0]
    tm = x_ref.shape[1]
    groups = [slice(r, r + tm // IN_GROUPS) for r in range(0, tm, tm // IN_GROUPS)]
    lane = lax.broadcasted_iota(jnp.int32, (tm // IN_GROUPS, aw), 1)
    even_head = (lane // HEAD_DIM) % 2 == 0
    for g in groups:
        h = _rms(x_ref[0, g]) * g_ref[...]
        hb = (h * (1.0 + mod[1:2]) + mod[0:1]).astype(BF16)
        o = 0
        u_ref[0, g] = _dot(hb, w_ref[:, o:o + sw]); o += sw
        q = (_dot(hb, w_ref[:, o:o + aw]) * Q_SCALE).astype(BF16); o += aw
        q_ref[0, 0, g] = jnp.where(even_head, q, jnp.zeros_like(q))
        q_ref[0, 1, g] = jnp.where(even_head, jnp.zeros_like(q), q)
        k_ref[0, g] = _dot(hb, w_ref[:, o:o + aw]).astype(BF16); o += aw
        v = _dot(hb, w_ref[:, o:o + aw]).astype(BF16); o += aw
        v_ref[0, 0, g] = jnp.where(even_head, v, jnp.zeros_like(v))
        v_ref[0, 1, g] = jnp.where(even_head, jnp.zeros_like(v), v)
        ga_ref[0, g] = _dot(hb, w_ref[:, o:o + d]); o += d
        gb_ref[0, g] = _dot(hb, w_ref[:, o:o + d])


def _in_proj(x, mod, norm_g, w_in_b, sw, aw):
    bsz, seq, d = x.shape
    tm = IN_TILE
    n = w_in_b.shape[1]
    tok = lambda w: pl.BlockSpec((1, tm, w), lambda b, i: (b, i, 0))
    tok2 = lambda w: pl.BlockSpec((1, 2, tm, w), lambda b, i: (b, 0, i, 0))
    return pl.pallas_call(
        functools.partial(_in_proj_kernel, sw=sw, aw=aw, d=d),
        grid=(bsz, seq // tm),
        in_specs=[tok(d),
                  pl.BlockSpec((1, N_ADA, d), lambda b, i: (b, 0, 0)),
                  pl.BlockSpec((1, d), lambda b, i: (0, 0)),
                  pl.BlockSpec((d, n), lambda b, i: (0, 0))],
        out_specs=[tok(sw), tok2(aw), tok(aw), tok2(aw), tok(d), tok(d)],
        out_shape=[jax.ShapeDtypeStruct((bsz, seq, sw), F32),
                   jax.ShapeDtypeStruct((bsz, 2, seq, aw), BF16),
                   jax.ShapeDtypeStruct((bsz, seq, aw), BF16),
                   jax.ShapeDtypeStruct((bsz, 2, seq, aw), BF16),
                   jax.ShapeDtypeStruct((bsz, seq, d), F32),
                   jax.ShapeDtypeStruct((bsz, seq, d), F32)],
        compiler_params=pltpu.CompilerParams(
            dimension_semantics=("arbitrary", "arbitrary"),
            vmem_limit_bytes=VMEM_LIMIT_BYTES),
        name="in_proj",
    )(x, mod, norm_g, w_in_b)


def _s5_kernel(u_ref, perm_ref, permt_ref, wb_ref, cm_ref, lamr_ref, lami_ref, d_ref,
               wglu_ref, bglu_ref, o_ref, bu_scr, x_scr, ulast_scr, *, tm, sw):
    i = pl.program_id(0)
    n2 = tm // 2
    rows = SUBLANES * n2
    cw = sw // S5_SLABS
    hs = cw * S5_STATE // S5_GROUP

    @pl.when(i == 0)
    def _():
        x_scr[...] = jnp.zeros_like(x_scr)
        ulast_scr[...] = jnp.zeros_like(ulast_scr)

    u_nat = u_ref[...].reshape(4 * tm, sw)
    a_cur = _dot(perm_ref[...], u_nat.astype(BF16))
    odd = (lax.broadcasted_iota(jnp.int32, (rows, sw), 0) & 1) == 1
    a_prev = jnp.where(odd, pltpu.roll(a_cur, 1, 0), pltpu.roll(a_cur, SUBLANES - 1, 0))
    sub = lax.broadcasted_iota(jnp.int32, (SUBLANES, sw), 0)
    first = jnp.where((sub & 1) == 1, a_prev[:SUBLANES], ulast_scr[...])
    a_prev = jnp.concatenate([first, a_prev[SUBLANES:]], axis=0)

    last = jnp.zeros((SUBLANES, sw), F32)
    for b in range(4):
        row = u_ref[b, tm - 1:tm, :].astype(BF16).astype(F32)
        last = jnp.where(sub == 2 * b, jnp.broadcast_to(row, (SUBLANES, sw)), last)
    ulast_scr[...] = last

    a_cur = a_cur.astype(BF16)
    a_prev = a_prev.astype(BF16)
    for s in range(S5_SLABS):
        lhs = jnp.concatenate([a_cur[:, s * cw:(s + 1) * cw],
                               a_prev[:, s * cw:(s + 1) * cw]], axis=1)
        bu_scr[:, 2 * hs * s:2 * hs * (s + 1)] = _dot(lhs, wb_ref[s])
    for s in range(S5_SLABS):
        re0 = 2 * hs * s
        im0 = re0 + hs
        ar = lamr_ref[:, hs * s:hs * (s + 1)]
        ai = lami_ref[:, hs * s:hs * (s + 1)]
        xr = x_scr[s, 0]
        xi = x_scr[s, 1]
        for t2 in range(n2):
            rs = slice(SUBLANES * t2, SUBLANES * (t2 + 1))
            br = bu_scr[rs, re0:re0 + hs]
            bi = bu_scr[rs, im0:im0 + hs]
            xr, xi = ar * xr - ai * xi + br, ar * xi + ai * xr + bi
            bu_scr[rs, re0:re0 + hs] = xr
            bu_scr[rs, im0:im0 + hs] = xi
        x_scr[s, 0] = xr
        x_scr[s, 1] = xi
    ys = [_dot(bu_scr[:, 2 * hs * s:2 * hs * (s + 1)].astype(BF16), cm_ref[s])
          for s in range(S5_SLABS)]
    y_il = jnp.concatenate(ys, axis=1)
    hi = y_il.astype(BF16)
    lo = (y_il - hi.astype(F32)).astype(BF16)
    y = _dot(permt_ref[...], jnp.concatenate([hi, lo], axis=0))
    y = y + d_ref[...] * u_nat
    y = jax.nn.gelu(y)
    z = _dot(y.astype(BF16), wglu_ref[...]) + bglu_ref[...]
    o_ref[...] = (y * jax.nn.sigmoid(z)).reshape(4, tm, sw)


def _s5_perms(tm):
    n2 = tm // 2
    rows = SUBLANES * n2
    perm = np.zeros((rows, 4 * tm), np.float32)
    permt = np.zeros((4 * tm, 2 * rows), np.float32)
    for t2 in range(n2):
        for b in range(4):
            for par in range(2):
                r = SUBLANES * t2 + 2 * b + par
                t = 2 * t2 + par
                perm[r, b * tm + t] = 1.0
                permt[b * tm + t, r] = 1.0
                permt[b * tm + t, rows + r] = 1.0
    return jnp.asarray(perm, BF16), jnp.asarray(permt, BF16)


def _block_diag(blocks):
    n, a, b = blocks.shape
    eye = jnp.eye(n, dtype=blocks.dtype)
    return jnp.einsum('nab,nm->namb', blocks, eye).reshape(n * a, n * b)


def _s5_weights(lam_re, lam_im, log_dt, b_re, b_im, c_re, c_im):
    g = lam_re.shape[0]
    dt = jnp.exp(log_dt)[:, None]
    mag = jnp.exp(lam_re * dt)
    lbr = mag * jnp.cos(lam_im * dt)
    lbi = mag * jnp.sin(lam_im * dt)
    nr, ni = lbr - 1.0, lbi
    den = lam_re * lam_re + lam_im * lam_im
    cr = (nr * lam_re + ni * lam_im) / den
    ci = (ni * lam_re - nr * lam_im) / den
    bbr = cr[..., None] * b_re - ci[..., None] * b_im
    bbi = cr[..., None] * b_im + ci[..., None] * b_re
    lr = lbr[..., None] * bbr - lbi[..., None] * bbi
    li = lbr[..., None] * bbi + lbi[..., None] * bbr
    l2r = lbr * lbr - lbi * lbi
    l2i = 2.0 * lbr * lbi
    gh = g // S5_SLABS
    wbs, cms = [], []
    for h in range(S5_SLABS):
        s = slice(h * gh, (h + 1) * gh)
        t = lambda a: jnp.swapaxes(a[s], 1, 2)
        top = jnp.concatenate([_block_diag(t(bbr)), _block_diag(t(bbi))], axis=1)
        bot = jnp.concatenate([_block_diag(t(lr)), _block_diag(t(li))], axis=1)
        wbs.append(jnp.concatenate([top, bot], axis=0))
        cms.append(jnp.concatenate([_block_diag(jnp.swapaxes(c_re[s], 1, 2)),
                                    -_block_diag(jnp.swapaxes(c_im[s], 1, 2))], axis=0))
    wb = jnp.stack(wbs).astype(BF16)
    cm = jnp.stack(cms).astype(BF16)
    lamr = jnp.broadcast_to(l2r.reshape(1, -1), (SUBLANES, l2r.size))
    lami = jnp.broadcast_to(l2i.reshape(1, -1), (SUBLANES, l2i.size))
    return wb, cm, lamr, lami


def _s5(u, wb, cm, lamr, lami, d_skip, w_glu_b, b_glu):
    bsz, seq, sw = u.shape
    assert bsz == 4, "the scan packs 4 batch rows x 2 token parities into 8 sublanes"
    tm = S5_TILE
    n2 = tm // 2
    rows = SUBLANES * n2
    ns = lamr.shape[1]
    perm, permt = _s5_perms(tm)
    const = lambda a: pl.BlockSpec(a.shape, lambda i: (0,) * a.ndim)
    d_row = d_skip.reshape(1, sw)
    bg = b_glu.reshape(1, sw)
    return pl.pallas_call(
        functools.partial(_s5_kernel, tm=tm, sw=sw),
        grid=(seq // tm,),
        in_specs=[pl.BlockSpec((4, tm, sw), lambda i: (0, i, 0)),
                  const(perm), const(permt), const(wb), const(cm), const(lamr), const(lami),
                  const(d_row), const(w_glu_b), const(bg)],
        out_specs=pl.BlockSpec((4, tm, sw), lambda i: (0, i, 0)),
        out_shape=jax.ShapeDtypeStruct((bsz, seq, sw), F32),
        scratch_shapes=[pltpu.VMEM((rows, 2 * ns), F32),
                        pltpu.VMEM((S5_SLABS, 2, SUBLANES, ns // S5_SLABS), F32),
                        pltpu.VMEM((SUBLANES, sw), F32)],
        compiler_params=pltpu.CompilerParams(
            dimension_semantics=("arbitrary",),
            vmem_limit_bytes=VMEM_LIMIT_BYTES),
        name="s5",
    )(u, perm, permt, wb, cm, lamr, lami, d_row, w_glu_b, bg)


def _attn_kernel(q_ref, k_ref, v_ref, tri_ref, o_ref, c_scr, acc_scr, z_scr, w_scr, *, n_pairs):
    blk = ATTN_BLOCK
    top = ATTN_TOP_ROWS
    nh = 2 * n_pairs
    step = pl.program_id(1)
    nt = (((1,), (1,)), ((), ()))

    def region(units):
        starts = [pl.multiple_of(kb * blk, blk) for kb, _, _, _ in units]
        for u, (_, r0, r1, _) in enumerate(units):
            for p in range(n_pairs):
                ls = slice(p * LANES, (p + 1) * LANES)
                kblk = k_ref[0, pl.ds(starts[u], blk), ls]
                for hh in range(2):
                    z_scr[u * nh + 2 * p + hh, 0:r1 - r0] = lax.dot_general(
                        q_ref[0, hh, r0:r1, ls], kblk, nt, preferred_element_type=F32)
        for u, (_, r0, r1, diag) in enumerate(units):
            if diag:
                row = lax.broadcasted_iota(jnp.int32, (r1 - r0, blk), 0) + r0 % blk
                valid = lax.broadcasted_iota(jnp.int32, (r1 - r0, blk), 1) < row
            for h in range(nh):
                z = z_scr[u * nh + h, 0:r1 - r0]
                sp = jnp.maximum(z, 0.0) + jnp.log2(1.0 + jnp.exp2(-jnp.abs(z)))
                if diag:
                    sp = jnp.where(valid, sp, 0.0)
                incl = _dot(sp.astype(BF16), tri_ref[...])
                total = jnp.broadcast_to(incl[:, 0:1], (r1 - r0, blk))
                if diag:
                    w = jnp.where(valid, jnp.exp2(z - incl), 0.0)
                    c_scr[h, r0:r1] = total
                else:
                    c = c_scr[h, r0:r1]
                    w = jnp.exp2(z - incl - c)
                    c_scr[h, r0:r1] = c + total
                w_scr[u * nh + h, 0:r1 - r0] = w.astype(BF16)
        for r0, r1 in dict.fromkeys((r0, r1) for _, r0, r1, _ in units):
            us = [u for u, (_, a, b, _) in enumerate(units) if (a, b) == (r0, r1)]
            for p in range(n_pairs):
                ls = slice(p * LANES, (p + 1) * LANES)
                ww = jnp.concatenate([w_scr[u * nh + 2 * p + hh, 0:r1 - r0]
                                      for u in us for hh in range(2)], axis=1)
                vv = jnp.concatenate([v_ref[0, hh, pl.ds(starts[u], blk), ls]
                                      for u in us for hh in range(2)], axis=0)
                if any(units[u][3] for u in us):
                    acc_scr[p, r0:r1] = _dot(ww, vv)
                else:
                    acc_scr[p, r0:r1] = acc_scr[p, r0:r1] + _dot(ww, vv)

    def c_min(r0, r1):
        m = c_scr[0, r0:r1]
        for h in range(1, nh):
            m = jnp.minimum(m, c_scr[h, r0:r1])
        return jnp.min(m)

    def head_units(sub, qi, n_prev):
        base = sub * blk
        units = [(qi, base, base + blk, True)]
        if n_prev >= 1:
            units.append((qi - 1, base, base + blk, False))
        if n_prev >= 2:
            units.append((qi - 2, base, base + top, False))
        return units

    @pl.when(step >= 1)
    def _():
        region([u for sub in range(ATTN_SUBS)
                for u in head_units(sub, ATTN_SUBS * step + sub, 2)])

    @pl.when(step == 0)
    def _():
        region([u for sub in range(ATTN_SUBS) for u in head_units(sub, sub, min(sub, 2))])

    def sweep(first_kb, cmin, r0, r1):
        def more(carry):
            kb, cmin = carry
            return jnp.logical_and(kb >= 0, cmin < UNDERFLOW_LOG2)

        def body(carry):
            kb, _ = carry
            region([(kb, r0, r1, False)])
            return kb - 1, c_min(r0, r1)

        lax.while_loop(more, body, (first_kb, cmin))

    tails = []
    for sub in range(ATTN_SUBS):
        qi = ATTN_SUBS * step + sub
        base = sub * blk
        tails.append((jnp.where(qi >= 2, qi - 3, -1), base, base + top))
        tails.append((jnp.where(qi >= 2, qi - 2, -1), base + top, base + blk))
    cmins = [c_min(r0, r1) for _, r0, r1 in tails]
    for (first_kb, r0, r1), cmin in zip(tails, cmins):
        sweep(first_kb, cmin, r0, r1)
    for p in range(n_pairs):
        o_ref[0, :, p * LANES:(p + 1) * LANES] = acc_scr[p].astype(o_ref.dtype)


def _attn_tri():
    blk = ATTN_BLOCK
    m = np.arange(blk)[:, None]
    j = np.arange(blk)[None, :]
    u = (m >= j).astype(np.float32)
    return jnp.asarray(u, BF16)


def _attn(q, k, v):
    bsz, seq, aw = k.shape
    blk = ATTN_BLOCK
    tq = blk * ATTN_SUBS
    n_pairs = aw // LANES
    nz = ATTN_REGION * ATTN_SUBS * 2 * n_pairs
    tri = _attn_tri()
    return pl.pallas_call(
        functools.partial(_attn_kernel, n_pairs=n_pairs),
        grid=(bsz, seq // tq),
        in_specs=[pl.BlockSpec((1, 2, tq, aw), lambda b, i: (b, 0, i, 0)),
                  pl.BlockSpec((1, seq, aw), lambda b, i: (b, 0, 0)),
                  pl.BlockSpec((1, 2, seq, aw), lambda b, i: (b, 0, 0, 0)),
                  pl.BlockSpec(tri.shape, lambda b, i: (0, 0))],
        out_specs=pl.BlockSpec((1, tq, aw), lambda b, i: (b, i, 0)),
        out_shape=jax.ShapeDtypeStruct((bsz, seq, aw), BF16),
        scratch_shapes=[pltpu.VMEM((2 * n_pairs, tq, blk), F32),
                        pltpu.VMEM((n_pairs, tq, LANES), F32),
                        pltpu.VMEM((nz, blk, blk), F32),
                        pltpu.VMEM((nz, blk, blk), BF16)],
        compiler_params=pltpu.CompilerParams(
            dimension_semantics=("arbitrary", "arbitrary"),
            vmem_limit_bytes=VMEM_LIMIT_BYTES),
        name="attn",
    )(q, k, v, tri)


def _out_ffn_kernel(x_ref, s5_ref, at_ref, ga_ref, gb_ref, mod_ref, n2_ref, nf_ref,
                    wa_ref, wb_ref, wo_ref, wg_ref, wu_ref, wd_ref, o_ref, *, final_norm):
    mod = mod_ref[0]
    tm = x_ref.shape[1]
    groups = [slice(r, r + tm // OUT_GROUPS) for r in range(0, tm, tm // OUT_GROUPS)]
    ms = []
    for g in groups:
        ya = _dot(s5_ref[0, g].astype(BF16), wa_ref[...])
        yb = _dot(at_ref[0, g], wb_ref[...])
        m = jax.nn.sigmoid(ga_ref[0, g]) * ya + jax.nn.sigmoid(gb_ref[0, g]) * yb
        ms.append(m.astype(BF16))
    x1s, hs = [], []
    for g, m in zip(groups, ms):
        x1 = x_ref[0, g] + mod[2:3] * _dot(m, wo_ref[...])
        h = _rms(x1) * n2_ref[...]
        x1s.append(x1)
        hs.append((h * (1.0 + mod[4:5]) + mod[3:4]).astype(BF16))
    acts = []
    for h in hs:
        gate = _dot(h, wg_ref[...])
        up = _dot(h, wu_ref[...])
        acts.append((gate * jax.nn.sigmoid(gate) * up).astype(BF16))
    for g, x1, act in zip(groups, x1s, acts):
        x2 = x1 + mod[5:6] * _dot(act, wd_ref[...])
        o_ref[0, g] = _rms(x2) * nf_ref[...] if final_norm else x2


def _out_ffn(x, s5o, attn, ga, gb, mod, n2g, nfg, wa, wb, wo, wg, wu, wd, final_norm):
    bsz, seq, d = x.shape
    tm = OUT_TILE
    tok = lambda a: pl.BlockSpec((1, tm, a.shape[-1]), lambda b, i: (b, i, 0))
    const = lambda a: pl.BlockSpec(a.shape, lambda b, i: (0,) * a.ndim,
                                   pipeline_mode=pl.Buffered(1))
    return pl.pallas_call(
        functools.partial(_out_ffn_kernel, final_norm=final_norm),
        grid=(bsz, seq // tm),
        in_specs=[tok(x), tok(s5o), tok(attn), tok(ga), tok(gb),
                  pl.BlockSpec((1, N_ADA, d), lambda b, i: (b, 0, 0)),
                  const(n2g), const(nfg),
                  const(wa), const(wb), const(wo), const(wg), const(wu), const(wd)],
        out_specs=pl.BlockSpec((1, tm, d), lambda b, i: (b, i, 0)),
        out_shape=jax.ShapeDtypeStruct((bsz, seq, d), F32),
        compiler_params=pltpu.CompilerParams(
            dimension_semantics=("arbitrary", "arbitrary"),
            vmem_limit_bytes=VMEM_LIMIT_BYTES),
        name="out_ffn",
    )(x, s5o, attn, ga, gb, mod, n2g, nfg, wa, wb, wo, wg, wu, wd)


def kernel(x, c, w_ada, b_ada, norm1_g, w_in, lam_re, lam_im, log_dt, b_re, b_im, c_re, c_im,
           d_skip, w_glu, b_glu, w_a, w_b, w_o, norm2_g, w_ffn_gate, w_ffn_up, w_ffn_down,
           norm_f_g):
    depth = w_ada.shape[0]
    bsz, seq, d = x.shape
    sw = w_glu.shape[1]
    aw = w_b.shape[1]
    for l in range(depth):
        mod = _ada(c, w_ada[l], b_ada[l]).reshape(bsz, N_ADA, d)
        u, q, k, v, ga, gb = _in_proj(x, mod, norm1_g[l].reshape(1, d), w_in[l].astype(BF16),
                                      sw, aw)
        wb, cm, lamr, lami = _s5_weights(lam_re[l], lam_im[l], log_dt[l], b_re[l], b_im[l],
                                         c_re[l], c_im[l])
        s5o = _s5(u, wb, cm, lamr, lami, d_skip[l], w_glu[l].astype(BF16), b_glu[l])
        attn = _attn(q, k, v)
        x = _out_ffn(x, s5o, attn, ga, gb, mod, norm2_g[l].reshape(1, d), norm_f_g.reshape(1, d),
                     w_a[l].astype(BF16), w_b[l].astype(BF16), w_o[l].astype(BF16),
                     w_ffn_gate[l].astype(BF16), w_ffn_up[l].astype(BF16),
                     w_ffn_down[l].astype(BF16), final_norm=(l == depth - 1))
    return x
```

```python
import functools
import math

import numpy as np
import jax
import jax.numpy as jnp
from jax import lax
from jax.experimental import pallas as pl
from jax.experimental.pallas import tpu as pltpu

F32 = jnp.float32
BF16 = jnp.bfloat16

S5_GROUP = 16
S5_STATE = 64
HEAD_DIM = 64
N_ADA = 6
RMS_EPS = 1e-6
Q_SCALE = math.log2(math.e) / math.sqrt(HEAD_DIM)
UNDERFLOW_LOG2 = 151.0

LANES = 128
SUBLANES = 8
VMEM_LIMIT_BYTES = 56 * 1024 * 1024

ATTN_BLOCK = 128
ATTN_REGION = 3
ATTN_SUBS = 4
ATTN_TOP_ROWS = 32
S5_TILE = 512
S5_SUBTILE = 128
S5_SLABS = 4
IN_TILE = 512
IN_GROUPS = 2
OUT_TILE = 512
OUT_GROUPS = 2


def _dot(a, b):
    return jnp.dot(a, b, preferred_element_type=F32)


def _rms(x):
    return x * lax.rsqrt(jnp.mean(x * x, axis=-1, keepdims=True) + RMS_EPS)


def _ada_kernel(c_ref, w_ref, b_ref, o_ref):
    c = c_ref[...]
    cond = c * jax.nn.sigmoid(c)
    o_ref[...] = _dot(cond.astype(BF16), w_ref[...].astype(BF16)) + b_ref[...]


def _ada(c, w_ada, b_ada):
    bsz, d = c.shape
    n = w_ada.shape[1]
    rows = -(-bsz // SUBLANES) * SUBLANES
    cp = jnp.zeros((rows, d), F32).at[:bsz].set(c)
    tn = 1536
    out = pl.pallas_call(
        _ada_kernel,
        grid=(n // tn,),
        in_specs=[pl.BlockSpec((rows, d), lambda j: (0, 0)),
                  pl.BlockSpec((d, tn), lambda j: (0, j)),
                  pl.BlockSpec((1, tn), lambda j: (0, j))],
        out_specs=pl.BlockSpec((rows, tn), lambda j: (0, j)),
        out_shape=jax.ShapeDtypeStruct((rows, n), F32),
        name="ada",
    )(cp, w_ada, b_ada.reshape(1, n))
    return out[:bsz]


def _in_proj_kernel(x_ref, mod_ref, g_ref, w_ref, u_ref, q_ref, k_ref, v_ref, ga_ref, gb_ref,
                    *, sw, aw, d):
    mod = mod_ref[0]
    tm = x_ref.shape[1]
    groups = [slice(r, r + tm // IN_GROUPS) for r in range(0, tm, tm // IN_GROUPS)]
    lane = lax.broadcasted_iota(jnp.int32, (tm // IN_GROUPS, aw), 1)
    even_head = (lane // HEAD_DIM) % 2 == 0
    for g in groups:
        h = _rms(x_ref[0, g]) * g_ref[...]
        hb = (h * (1.0 + mod[1:2]) + mod[0:1]).astype(BF16)
        o = 0
        u_ref[0, g] = _dot(hb, w_ref[:, o:o + sw]); o += sw
        q = (_dot(hb, w_ref[:, o:o + aw]) * Q_SCALE).astype(BF16); o += aw
        q_ref[0, 0, g] = jnp.where(even_head, q, jnp.zeros_like(q))
        q_ref[0, 1, g] = jnp.where(even_head, jnp.zeros_like(q), q)
        k_ref[0, g] = _dot(hb, w_ref[:, o:o + aw]).astype(BF16); o += aw
        v = _dot(hb, w_ref[:, o:o + aw]).astype(BF16); o += aw
        v_ref[0, 0, g] = jnp.where(even_head, v, jnp.zeros_like(v))
        v_ref[0, 1, g] = jnp.where(even_head, jnp.zeros_like(v), v)
        ga_ref[0, g] = _dot(hb, w_ref[:, o:o + d]); o += d
        gb_ref[0, g] = _dot(hb, w_ref[:, o:o + d])


def _in_proj(x, mod, norm_g, w_in_b, sw, aw):
    bsz, seq, d = x.shape
    tm = IN_TILE
    n = w_in_b.shape[1]
    tok = lambda w: pl.BlockSpec((1, tm, w), lambda b, i: (b, i, 0))
    tok2 = lambda w: pl.BlockSpec((1, 2, tm, w), lambda b, i: (b, 0, i, 0))
    return pl.pallas_call(
        functools.partial(_in_proj_kernel, sw=sw, aw=aw, d=d),
        grid=(bsz, seq // tm),
        in_specs=[tok(d),
                  pl.BlockSpec((1, N_ADA, d), lambda b, i: (b, 0, 0)),
                  pl.BlockSpec((1, d), lambda b, i: (0, 0)),
                  pl.BlockSpec((d, n), lambda b, i: (0, 0))],
        out_specs=[tok(sw), tok2(aw), tok(aw), tok2(aw), tok(d), tok(d)],
        out_shape=[jax.ShapeDtypeStruct((bsz, seq, sw), F32),
                   jax.ShapeDtypeStruct((bsz, 2, seq, aw), BF16),
                   jax.ShapeDtypeStruct((bsz, seq, aw), BF16),
                   jax.ShapeDtypeStruct((bsz, 2, seq, aw), BF16),
                   jax.ShapeDtypeStruct((bsz, seq, d), F32),
                   jax.ShapeDtypeStruct((bsz, seq, d), F32)],
        compiler_params=pltpu.CompilerParams(
            dimension_semantics=("arbitrary", "arbitrary"),
            vmem_limit_bytes=VMEM_LIMIT_BYTES),
        name="in_proj",
    )(x, mod, norm_g, w_in_b)


def _s5_kernel(u_ref, perm_ref, permt_ref, wb_ref, cm_ref, lamr_ref, lami_ref, d_ref,
               wglu_ref, bglu_ref, o_ref, x_scr, ulast_scr, *, tm, sw):
    i = pl.program_id(0)
    ts = S5_SUBTILE
    n2 = ts // 2
    rows = SUBLANES * n2
    cw = sw // S5_SLABS
    hs = cw * S5_STATE // S5_GROUP

    @pl.when(i == 0)
    def _():
        x_scr[...] = jnp.zeros_like(x_scr)
        ulast_scr[...] = jnp.zeros_like(ulast_scr)

    sub8 = lax.broadcasted_iota(jnp.int32, (SUBLANES, sw), 0)
    odd = (lax.broadcasted_iota(jnp.int32, (rows, sw), 0) & 1) == 1

    def natural(j):
        return jnp.concatenate([u_ref[b, j * ts:(j + 1) * ts, :] for b in range(4)], axis=0)

    def last_rows(j):
        last = jnp.zeros((SUBLANES, sw), F32)
        for b in range(4):
            row = u_ref[b, (j + 1) * ts - 1:(j + 1) * ts, :].astype(BF16).astype(F32)
            last = jnp.where(sub8 == 2 * b, jnp.broadcast_to(row, (SUBLANES, sw)), last)
        return last

    def input_stage(j, before):
        a_cur = _dot(perm_ref[...], natural(j).astype(BF16))
        a_prev = jnp.where(odd, pltpu.roll(a_cur, 1, 0), pltpu.roll(a_cur, SUBLANES - 1, 0))
        first = jnp.where((sub8 & 1) == 1, a_prev[:SUBLANES], before)
        a_prev = jnp.concatenate([first, a_prev[SUBLANES:]], axis=0).astype(BF16)
        a_cur = a_cur.astype(BF16)
        return [_dot(jnp.concatenate([a_cur[:, s * cw:(s + 1) * cw],
                                      a_prev[:, s * cw:(s + 1) * cw]], axis=1), wb_ref[s])
                for s in range(S5_SLABS)]

    def scan_stage(bus, x):
        states, x_out = [], []
        for s in range(S5_SLABS):
            ar = lamr_ref[:, hs * s:hs * (s + 1)]
            ai = lami_ref[:, hs * s:hs * (s + 1)]
            xr, xi = x[s]
            st = []
            for t2 in range(n2):
                rs = slice(SUBLANES * t2, SUBLANES * (t2 + 1))
                xr, xi = (ar * xr - ai * xi + bus[s][rs, :hs],
                          ar * xi + ai * xr + bus[s][rs, hs:])
                st.append(jnp.concatenate([xr, xi], axis=1))
            x_out.append((xr, xi))
            states.append(jnp.concatenate(st, axis=0).astype(BF16))
        return states, x_out

    def output_stage(j, states):
        y_il = jnp.concatenate([_dot(states[s], cm_ref[s]) for s in range(S5_SLABS)], axis=1)
        hi = y_il.astype(BF16)
        lo = (y_il - hi.astype(F32)).astype(BF16)
        y = _dot(permt_ref[...], jnp.concatenate([hi, lo], axis=0))
        y = y + d_ref[...] * natural(j)
        y = jax.nn.gelu(y)
        z = _dot(y.astype(BF16), wglu_ref[...]) + bglu_ref[...]
        out = y * jax.nn.sigmoid(z)
        for b in range(4):
            o_ref[b, j * ts:(j + 1) * ts, :] = out[b * ts:(b + 1) * ts]

    nsub = tm // ts
    befores = [ulast_scr[...]] + [last_rows(j) for j in range(nsub - 1)]
    bus = [input_stage(j, befores[j]) for j in range(nsub)]
    ulast_scr[...] = last_rows(nsub - 1)
    x = [(x_scr[s, 0], x_scr[s, 1]) for s in range(S5_SLABS)]
    for j in range(nsub):
        states, x = scan_stage(bus[j], x)
        output_stage(j, states)
    for s in range(S5_SLABS):
        x_scr[s, 0], x_scr[s, 1] = x[s]


def _s5_perms(tm):
    n2 = tm // 2
    rows = SUBLANES * n2
    perm = np.zeros((rows, 4 * tm), np.float32)
    permt = np.zeros((4 * tm, 2 * rows), np.float32)
    for t2 in range(n2):
        for b in range(4):
            for par in range(2):
                r = SUBLANES * t2 + 2 * b + par
                t = 2 * t2 + par
                perm[r, b * tm + t] = 1.0
                permt[b * tm + t, r] = 1.0
                permt[b * tm + t, rows + r] = 1.0
    return jnp.asarray(perm, BF16), jnp.asarray(permt, BF16)


def _block_diag(blocks):
    n, a, b = blocks.shape
    eye = jnp.eye(n, dtype=blocks.dtype)
    return jnp.einsum('nab,nm->namb', blocks, eye).reshape(n * a, n * b)


def _s5_weights(lam_re, lam_im, log_dt, b_re, b_im, c_re, c_im):
    g = lam_re.shape[0]
    dt = jnp.exp(log_dt)[:, None]
    mag = jnp.exp(lam_re * dt)
    lbr = mag * jnp.cos(lam_im * dt)
    lbi = mag * jnp.sin(lam_im * dt)
    nr, ni = lbr - 1.0, lbi
    den = lam_re * lam_re + lam_im * lam_im
    cr = (nr * lam_re + ni * lam_im) / den
    ci = (ni * lam_re - nr * lam_im) / den
    bbr = cr[..., None] * b_re - ci[..., None] * b_im
    bbi = cr[..., None] * b_im + ci[..., None] * b_re
    lr = lbr[..., None] * bbr - lbi[..., None] * bbi
    li = lbr[..., None] * bbi + lbi[..., None] * bbr
    l2r = lbr * lbr - lbi * lbi
    l2i = 2.0 * lbr * lbi
    gh = g // S5_SLABS
    wbs, cms = [], []
    for h in range(S5_SLABS):
        s = slice(h * gh, (h + 1) * gh)
        t = lambda a: jnp.swapaxes(a[s], 1, 2)
        top = jnp.concatenate([_block_diag(t(bbr)), _block_diag(t(bbi))], axis=1)
        bot = jnp.concatenate([_block_diag(t(lr)), _block_diag(t(li))], axis=1)
        wbs.append(jnp.concatenate([top, bot], axis=0))
        cms.append(jnp.concatenate([_block_diag(jnp.swapaxes(c_re[s], 1, 2)),
                                    -_block_diag(jnp.swapaxes(c_im[s], 1, 2))], axis=0))
    wb = jnp.stack(wbs).astype(BF16)
    cm = jnp.stack(cms).astype(BF16)
    lamr = jnp.broadcast_to(l2r.reshape(1, -1), (SUBLANES, l2r.size))
    lami = jnp.broadcast_to(l2i.reshape(1, -1), (SUBLANES, l2i.size))
    return wb, cm, lamr, lami


def _s5(u, wb, cm, lamr, lami, d_skip, w_glu_b, b_glu):
    bsz, seq, sw = u.shape
    assert bsz == 4, "the scan packs 4 batch rows x 2 token parities into 8 sublanes"
    tm = S5_TILE
    ns = lamr.shape[1]
    perm, permt = _s5_perms(S5_SUBTILE)
    const = lambda a: pl.BlockSpec(a.shape, lambda i: (0,) * a.ndim)
    d_row = d_skip.reshape(1, sw)
    bg = b_glu.reshape(1, sw)
    return pl.pallas_call(
        functools.partial(_s5_kernel, tm=tm, sw=sw),
        grid=(seq // tm,),
        in_specs=[pl.BlockSpec((4, tm, sw), lambda i: (0, i, 0)),
                  const(perm), const(permt), const(wb), const(cm), const(lamr), const(lami),
                  const(d_row), const(w_glu_b), const(bg)],
        out_specs=pl.BlockSpec((4, tm, sw), lambda i: (0, i, 0)),
        out_shape=jax.ShapeDtypeStruct((bsz, seq, sw), F32),
        scratch_shapes=[pltpu.VMEM((S5_SLABS, 2, SUBLANES, ns // S5_SLABS), F32),
                        pltpu.VMEM((SUBLANES, sw), F32)],
        compiler_params=pltpu.CompilerParams(
            dimension_semantics=("arbitrary",),
            vmem_limit_bytes=VMEM_LIMIT_BYTES),
        name="s5",
    )(u, perm, permt, wb, cm, lamr, lami, d_row, w_glu_b, bg)


def _attn_kernel(q_ref, k_ref, v_ref, tri_ref, o_ref, c_scr, acc_scr, z_scr, w_scr, *, n_pairs):
    blk = ATTN_BLOCK
    top = ATTN_TOP_ROWS
    nh = 2 * n_pairs
    step = pl.program_id(1)
    nt = (((1,), (1,)), ((), ()))

    def region(units):
        starts = [pl.multiple_of(kb * blk, blk) for kb, _, _, _ in units]
        for u, (_, r0, r1, _) in enumerate(units):
            for p in range(n_pairs):
                ls = slice(p * LANES, (p + 1) * LANES)
                kblk = k_ref[0, pl.ds(starts[u], blk), ls]
                for hh in range(2):
                    z_scr[u * nh + 2 * p + hh, 0:r1 - r0] = lax.dot_general(
                        q_ref[0, hh, r0:r1, ls], kblk, nt, preferred_element_type=F32)
        for u, (_, r0, r1, diag) in enumerate(units):
            if diag:
                row = lax.broadcasted_iota(jnp.int32, (r1 - r0, blk), 0) + r0 % blk
                valid = lax.broadcasted_iota(jnp.int32, (r1 - r0, blk), 1) < row
            for h in range(nh):
                z = z_scr[u * nh + h, 0:r1 - r0]
                sp = jnp.maximum(z, 0.0) + jnp.log2(1.0 + jnp.exp2(-jnp.abs(z)))
                if diag:
                    sp = jnp.where(valid, sp, 0.0)
                incl = _dot(sp.astype(BF16), tri_ref[...])
                total = jnp.broadcast_to(incl[:, 0:1], (r1 - r0, blk))
                if diag:
                    w = jnp.where(valid, jnp.exp2(z - incl), 0.0)
                    c_scr[h, r0:r1] = total
                else:
                    c = c_scr[h, r0:r1]
                    w = jnp.exp2(z - incl - c)
                    c_scr[h, r0:r1] = c + total
                w_scr[u * nh + h, 0:r1 - r0] = w.astype(BF16)
        for r0, r1 in dict.fromkeys((r0, r1) for _, r0, r1, _ in units):
            us = [u for u, (_, a, b, _) in enumerate(units) if (a, b) == (r0, r1)]
            for p in range(n_pairs):
                ls = slice(p * LANES, (p + 1) * LANES)
                ww = jnp.concatenate([w_scr[u * nh + 2 * p + hh, 0:r1 - r0]
                                      for u in us for hh in range(2)], axis=1)
                vv = jnp.concatenate([v_ref[0, hh, pl.ds(starts[u], blk), ls]
                                      for u in us for hh in range(2)], axis=0)
                if any(units[u][3] for u in us):
                    acc_scr[p, r0:r1] = _dot(ww, vv)
                else:
                    acc_scr[p, r0:r1] = acc_scr[p, r0:r1] + _dot(ww, vv)

    def c_min(r0, r1):
        m = c_scr[0, r0:r1]
        for h in range(1, nh):
            m = jnp.minimum(m, c_scr[h, r0:r1])
        return jnp.min(m)

    def head_units(sub, qi, n_prev):
        base = sub * blk
        units = [(qi, base, base + blk, True)]
        if n_prev >= 1:
            units.append((qi - 1, base, base + blk, False))
        if n_prev >= 2:
            units.append((qi - 2, base, base + top, False))
        return units

    @pl.when(step >= 1)
    def _():
        region([u for sub in range(ATTN_SUBS)
                for u in head_units(sub, ATTN_SUBS * step + sub, 2)])

    @pl.when(step == 0)
    def _():
        region([u for sub in range(ATTN_SUBS) for u in head_units(sub, sub, min(sub, 2))])

    def sweep(first_kb, cmin, r0, r1):
        def more(carry):
            kb, cmin = carry
            return jnp.logical_and(kb >= 0, cmin < UNDERFLOW_LOG2)

        def body(carry):
            kb, _ = carry
            region([(kb, r0, r1, False)])
            return kb - 1, c_min(r0, r1)

        lax.while_loop(more, body, (first_kb, cmin))

    tails = []
    for sub in range(ATTN_SUBS):
        qi = ATTN_SUBS * step + sub
        base = sub * blk
        tails.append((jnp.where(qi >= 2, qi - 3, -1), base, base + top))
        tails.append((jnp.where(qi >= 2, qi - 2, -1), base + top, base + blk))
    cmins = [c_min(r0, r1) for _, r0, r1 in tails]
    for (first_kb, r0, r1), cmin in zip(tails, cmins):
        sweep(first_kb, cmin, r0, r1)
    for p in range(n_pairs):
        o_ref[0, :, p * LANES:(p + 1) * LANES] = acc_scr[p].astype(o_ref.dtype)


def _attn_tri():
    blk = ATTN_BLOCK
    m = np.arange(blk)[:, None]
    j = np.arange(blk)[None, :]
    u = (m >= j).astype(np.float32)
    return jnp.asarray(u, BF16)


def _attn(q, k, v):
    bsz, seq, aw = k.shape
    blk = ATTN_BLOCK
    tq = blk * ATTN_SUBS
    n_pairs = aw // LANES
    nz = ATTN_REGION * ATTN_SUBS * 2 * n_pairs
    tri = _attn_tri()
    return pl.pallas_call(
        functools.partial(_attn_kernel, n_pairs=n_pairs),
        grid=(bsz, seq // tq),
        in_specs=[pl.BlockSpec((1, 2, tq, aw), lambda b, i: (b, 0, i, 0)),
                  pl.BlockSpec((1, seq, aw), lambda b, i: (b, 0, 0)),
                  pl.BlockSpec((1, 2, seq, aw), lambda b, i: (b, 0, 0, 0)),
                  pl.BlockSpec(tri.shape, lambda b, i: (0, 0))],
        out_specs=pl.BlockSpec((1, tq, aw), lambda b, i: (b, i, 0)),
        out_shape=jax.ShapeDtypeStruct((bsz, seq, aw), BF16),
        scratch_shapes=[pltpu.VMEM((2 * n_pairs, tq, blk), F32),
                        pltpu.VMEM((n_pairs, tq, LANES), F32),
                        pltpu.VMEM((nz, blk, blk), F32),
                        pltpu.VMEM((nz, blk, blk), BF16)],
        compiler_params=pltpu.CompilerParams(
            dimension_semantics=("arbitrary", "arbitrary"),
            vmem_limit_bytes=VMEM_LIMIT_BYTES),
        name="attn",
    )(q, k, v, tri)


def _out_ffn_kernel(x_ref, s5_ref, at_ref, ga_ref, gb_ref, mod_ref, n2_ref, nf_ref,
                    wa_ref, wb_ref, wo_ref, wg_ref, wu_ref, wd_ref, o_ref, *, final_norm):
    mod = mod_ref[0]
    tm = x_ref.shape[1]
    groups = [slice(r, r + tm // OUT_GROUPS) for r in range(0, tm, tm // OUT_GROUPS)]
    ms = []
    for g in groups:
        ya = _dot(s5_ref[0, g].astype(BF16), wa_ref[...])
        yb = _dot(at_ref[0, g], wb_ref[...])
        m = jax.nn.sigmoid(ga_ref[0, g]) * ya + jax.nn.sigmoid(gb_ref[0, g]) * yb
        ms.append(m.astype(BF16))
    x1s, hs = [], []
    for g, m in zip(groups, ms):
        x1 = x_ref[0, g] + mod[2:3] * _dot(m, wo_ref[...])
        h = _rms(x1) * n2_ref[...]
        x1s.append(x1)
        hs.append((h * (1.0 + mod[4:5]) + mod[3:4]).astype(BF16))
    acts = []
    for h in hs:
        gate = _dot(h, wg_ref[...])
        up = _dot(h, wu_ref[...])
        acts.append((gate * jax.nn.sigmoid(gate) * up).astype(BF16))
    for g, x1, act in zip(groups, x1s, acts):
        x2 = x1 + mod[5:6] * _dot(act, wd_ref[...])
        o_ref[0, g] = _rms(x2) * nf_ref[...] if final_norm else x2


def _out_ffn(x, s5o, attn, ga, gb, mod, n2g, nfg, wa, wb, wo, wg, wu, wd, final_norm):
    bsz, seq, d = x.shape
    tm = OUT_TILE
    tok = lambda a: pl.BlockSpec((1, tm, a.shape[-1]), lambda b, i: (b, i, 0))
    const = lambda a: pl.BlockSpec(a.shape, lambda b, i: (0,) * a.ndim,
                                   pipeline_mode=pl.Buffered(1))
    return pl.pallas_call(
        functools.partial(_out_ffn_kernel, final_norm=final_norm),
        grid=(bsz, seq // tm),
        in_specs=[tok(x), tok(s5o), tok(attn), tok(ga), tok(gb),
                  pl.BlockSpec((1, N_ADA, d), lambda b, i: (b, 0, 0)),
                  const(n2g), const(nfg),
                  const(wa), const(wb), const(wo), const(wg), const(wu), const(wd)],
        out_specs=pl.BlockSpec((1, tm, d), lambda b, i: (b, i, 0)),
        out_shape=jax.ShapeDtypeStruct((bsz, seq, d), F32),
        compiler_params=pltpu.CompilerParams(
            dimension_semantics=("arbitrary", "arbitrary"),
            vmem_limit_bytes=VMEM_LIMIT_BYTES),
        name="out_ffn",
    )(x, s5o, attn, ga, gb, mod, n2g, nfg, wa, wb, wo, wg, wu, wd)


def kernel(x, c, w_ada, b_ada, norm1_g, w_in, lam_re, lam_im, log_dt, b_re, b_im, c_re, c_im,
           d_skip, w_glu, b_glu, w_a, w_b, w_o, norm2_g, w_ffn_gate, w_ffn_up, w_ffn_down,
           norm_f_g):
    depth = w_ada.shape[0]
    bsz, seq, d = x.shape
    sw = w_glu.shape[1]
    aw = w_b.shape[1]
    for l in range(depth):
        mod = _ada(c, w_ada[l], b_ada[l]).reshape(bsz, N_ADA, d)
        u, q, k, v, ga, gb = _in_proj(x, mod, norm1_g[l].reshape(1, d), w_in[l].astype(BF16),
                                      sw, aw)
        wb, cm, lamr, lami = _s5_weights(lam_re[l], lam_im[l], log_dt[l], b_re[l], b_im[l],
                                         c_re[l], c_im[l])
        s5o = _s5(u, wb, cm, lamr, lami, d_skip[l], w_glu[l].astype(BF16), b_glu[l])
        attn = _attn(q, k, v)
        x = _out_ffn(x, s5o, attn, ga, gb, mod, norm2_g[l].reshape(1, d), norm_f_g.reshape(1, d),
                     w_a[l].astype(BF16), w_b[l].astype(BF16), w_o[l].astype(BF16),
                     w_ffn_gate[l].astype(BF16), w_ffn_up[l].astype(BF16),
                     w_ffn_down[l].astype(BF16), final_norm=(l == depth - 1))
    return x
```

```python
import functools
import math

import numpy as np
import jax
import jax.numpy as jnp
from jax import lax
from jax.experimental import pallas as pl
from jax.experimental.pallas import tpu as pltpu

F32 = jnp.float32
BF16 = jnp.bfloat16

S5_GROUP = 16
S5_STATE = 64
HEAD_DIM = 64
N_ADA = 6
RMS_EPS = 1e-6
Q_SCALE = math.log2(math.e) / math.sqrt(HEAD_DIM)
UNDERFLOW_LOG2 = 151.0

LANES = 128
SUBLANES = 8
VMEM_LIMIT_BYTES = 56 * 1024 * 1024

ATTN_BLOCK = 128
ATTN_REGION = 3
ATTN_TOP_ROWS = 32
S5_TILE = 512
S5_SUBTILE = 128
S5_SLABS = 4
FRONT_TILE = 512
FRONT_CHUNK = 256
OUT_TILE = 512
OUT_GROUPS = 2


def _dot(a, b):
    return jnp.dot(a, b, preferred_element_type=F32)


def _rms(x):
    return x * lax.rsqrt(jnp.mean(x * x, axis=-1, keepdims=True) + RMS_EPS)


def _ada_kernel(c_ref, w_ref, b_ref, o_ref):
    c = c_ref[...]
    cond = c * jax.nn.sigmoid(c)
    o_ref[...] = _dot(cond.astype(BF16), w_ref[...].astype(BF16)) + b_ref[...]


def _ada(c, w_ada, b_ada):
    bsz, d = c.shape
    n = w_ada.shape[1]
    rows = -(-bsz // SUBLANES) * SUBLANES
    cp = jnp.zeros((rows, d), F32).at[:bsz].set(c)
    tn = 1536
    out = pl.pallas_call(
        _ada_kernel,
        grid=(n // tn,),
        in_specs=[pl.BlockSpec((rows, d), lambda j: (0, 0)),
                  pl.BlockSpec((d, tn), lambda j: (0, j)),
                  pl.BlockSpec((1, tn), lambda j: (0, j))],
        out_specs=pl.BlockSpec((rows, tn), lambda j: (0, j)),
        out_shape=jax.ShapeDtypeStruct((rows, n), F32),
        name="ada",
    )(cp, w_ada, b_ada.reshape(1, n))
    return out[:bsz]


def _s5_kernel(u_ref, perm_ref, permt_ref, wb_ref, cm_ref, lamr_ref, lami_ref, d_ref,
               wglu_ref, bglu_ref, o_ref, x_scr, ulast_scr, *, tm, sw):
    i = pl.program_id(0)
    ts = S5_SUBTILE
    n2 = ts // 2
    rows = SUBLANES * n2
    cw = sw // S5_SLABS
    hs = cw * S5_STATE // S5_GROUP

    @pl.when(i == 0)
    def _():
        x_scr[...] = jnp.zeros_like(x_scr)
        ulast_scr[...] = jnp.zeros_like(ulast_scr)

    sub8 = lax.broadcasted_iota(jnp.int32, (SUBLANES, sw), 0)
    odd = (lax.broadcasted_iota(jnp.int32, (rows, sw), 0) & 1) == 1

    def natural(j):
        return jnp.concatenate([u_ref[b, j * ts:(j + 1) * ts, :] for b in range(4)], axis=0)

    def last_rows(j):
        last = jnp.zeros((SUBLANES, sw), F32)
        for b in range(4):
            row = u_ref[b, (j + 1) * ts - 1:(j + 1) * ts, :].astype(BF16).astype(F32)
            last = jnp.where(sub8 == 2 * b, jnp.broadcast_to(row, (SUBLANES, sw)), last)
        return last

    def input_stage(j, before):
        a_cur = _dot(perm_ref[...], natural(j).astype(BF16))
        a_prev = jnp.where(odd, pltpu.roll(a_cur, 1, 0), pltpu.roll(a_cur, SUBLANES - 1, 0))
        first = jnp.where((sub8 & 1) == 1, a_prev[:SUBLANES], before)
        a_prev = jnp.concatenate([first, a_prev[SUBLANES:]], axis=0).astype(BF16)
        a_cur = a_cur.astype(BF16)
        return [_dot(jnp.concatenate([a_cur[:, s * cw:(s + 1) * cw],
                                      a_prev[:, s * cw:(s + 1) * cw]], axis=1), wb_ref[s])
                for s in range(S5_SLABS)]

    def scan_stage(bus, x):
        states, x_out = [], []
        for s in range(S5_SLABS):
            ar = lamr_ref[:, hs * s:hs * (s + 1)]
            ai = lami_ref[:, hs * s:hs * (s + 1)]
            xr, xi = x[s]
            st = []
            for t2 in range(n2):
                rs = slice(SUBLANES * t2, SUBLANES * (t2 + 1))
                xr, xi = (ar * xr - ai * xi + bus[s][rs, :hs],
                          ar * xi + ai * xr + bus[s][rs, hs:])
                st.append(jnp.concatenate([xr, xi], axis=1))
            x_out.append((xr, xi))
            states.append(jnp.concatenate(st, axis=0).astype(BF16))
        return states, x_out

    def output_stage(j, states):
        y_il = jnp.concatenate([_dot(states[s], cm_ref[s]) for s in range(S5_SLABS)], axis=1)
        hi = y_il.astype(BF16)
        lo = (y_il - hi.astype(F32)).astype(BF16)
        y = _dot(permt_ref[...], jnp.concatenate([hi, lo], axis=0))
        y = y + d_ref[...] * natural(j)
        y = jax.nn.gelu(y)
        z = _dot(y.astype(BF16), wglu_ref[...]) + bglu_ref[...]
        out = y * jax.nn.sigmoid(z)
        for b in range(4):
            o_ref[b, j * ts:(j + 1) * ts, :] = out[b * ts:(b + 1) * ts]

    nsub = tm // ts
    befores = [ulast_scr[...]] + [last_rows(j) for j in range(nsub - 1)]
    bus = [input_stage(j, befores[j]) for j in range(nsub)]
    ulast_scr[...] = last_rows(nsub - 1)
    x = [(x_scr[s, 0], x_scr[s, 1]) for s in range(S5_SLABS)]
    for j in range(nsub):
        states, x = scan_stage(bus[j], x)
        output_stage(j, states)
    for s in range(S5_SLABS):
        x_scr[s, 0], x_scr[s, 1] = x[s]


def _s5_perms(tm):
    n2 = tm // 2
    rows = SUBLANES * n2
    perm = np.zeros((rows, 4 * tm), np.float32)
    permt = np.zeros((4 * tm, 2 * rows), np.float32)
    for t2 in range(n2):
        for b in range(4):
            for par in range(2):
                r = SUBLANES * t2 + 2 * b + par
                t = 2 * t2 + par
                perm[r, b * tm + t] = 1.0
                permt[b * tm + t, r] = 1.0
                permt[b * tm + t, rows + r] = 1.0
    return jnp.asarray(perm, BF16), jnp.asarray(permt, BF16)


def _block_diag(blocks):
    n, a, b = blocks.shape
    eye = jnp.eye(n, dtype=blocks.dtype)
    return jnp.einsum('nab,nm->namb', blocks, eye).reshape(n * a, n * b)


def _s5_weights(lam_re, lam_im, log_dt, b_re, b_im, c_re, c_im):
    g = lam_re.shape[0]
    dt = jnp.exp(log_dt)[:, None]
    mag = jnp.exp(lam_re * dt)
    lbr = mag * jnp.cos(lam_im * dt)
    lbi = mag * jnp.sin(lam_im * dt)
    nr, ni = lbr - 1.0, lbi
    den = lam_re * lam_re + lam_im * lam_im
    cr = (nr * lam_re + ni * lam_im) / den
    ci = (ni * lam_re - nr * lam_im) / den
    bbr = cr[..., None] * b_re - ci[..., None] * b_im
    bbi = cr[..., None] * b_im + ci[..., None] * b_re
    lr = lbr[..., None] * bbr - lbi[..., None] * bbi
    li = lbr[..., None] * bbi + lbi[..., None] * bbr
    l2r = lbr * lbr - lbi * lbi
    l2i = 2.0 * lbr * lbi
    gh = g // S5_SLABS
    wbs, cms = [], []
    for h in range(S5_SLABS):
        s = slice(h * gh, (h + 1) * gh)
        t = lambda a: jnp.swapaxes(a[s], 1, 2)
        top = jnp.concatenate([_block_diag(t(bbr)), _block_diag(t(bbi))], axis=1)
        bot = jnp.concatenate([_block_diag(t(lr)), _block_diag(t(li))], axis=1)
        wbs.append(jnp.concatenate([top, bot], axis=0))
        cms.append(jnp.concatenate([_block_diag(jnp.swapaxes(c_re[s], 1, 2)),
                                    -_block_diag(jnp.swapaxes(c_im[s], 1, 2))], axis=0))
    wb = jnp.stack(wbs).astype(BF16)
    cm = jnp.stack(cms).astype(BF16)
    lamr = jnp.broadcast_to(l2r.reshape(1, -1), (SUBLANES, l2r.size))
    lami = jnp.broadcast_to(l2i.reshape(1, -1), (SUBLANES, l2i.size))
    return wb, cm, lamr, lami


def _s5(u, wb, cm, lamr, lami, d_skip, w_glu_b, b_glu):
    bsz, seq, sw = u.shape
    assert bsz == 4, "the scan packs 4 batch rows x 2 token parities into 8 sublanes"
    tm = S5_TILE
    ns = lamr.shape[1]
    perm, permt = _s5_perms(S5_SUBTILE)
    const = lambda a: pl.BlockSpec(a.shape, lambda i: (0,) * a.ndim)
    d_row = d_skip.reshape(1, sw)
    bg = b_glu.reshape(1, sw)
    return pl.pallas_call(
        functools.partial(_s5_kernel, tm=tm, sw=sw),
        grid=(seq // tm,),
        in_specs=[pl.BlockSpec((4, tm, sw), lambda i: (0, i, 0)),
                  const(perm), const(permt), const(wb), const(cm), const(lamr), const(lami),
                  const(d_row), const(w_glu_b), const(bg)],
        out_specs=pl.BlockSpec((4, tm, sw), lambda i: (0, i, 0)),
        out_shape=jax.ShapeDtypeStruct((bsz, seq, sw), F32),
        scratch_shapes=[pltpu.VMEM((S5_SLABS, 2, SUBLANES, ns // S5_SLABS), F32),
                        pltpu.VMEM((SUBLANES, sw), F32)],
        compiler_params=pltpu.CompilerParams(
            dimension_semantics=("arbitrary",),
            vmem_limit_bytes=VMEM_LIMIT_BYTES),
        name="s5",
    )(u, perm, permt, wb, cm, lamr, lami, d_row, w_glu_b, bg)


def _front_kernel(x_ref, mod_ref, g_ref, w_ref, tri_ref, u_ref, ga_ref, gb_ref, at_ref,
                  q_scr, k_scr, v_scr, c_scr, acc_scr, z_scr, w_scr, *, sw, aw, d):
    blk = ATTN_BLOCK
    top = ATTN_TOP_ROWS
    n_pairs = aw // LANES
    nh = 2 * n_pairs
    tm = x_ref.shape[1]
    nsub = tm // blk
    step = pl.program_id(1)
    tile0 = pl.multiple_of(step * tm, tm)
    mod = mod_ref[0]
    nt = (((1,), (1,)), ((), ()))
    even_head = (lax.broadcasted_iota(jnp.int32, (tm, aw), 1) // HEAD_DIM) % 2 == 0

    def qkv(hb):
        o = sw
        q = (_dot(hb, w_ref[:, o:o + aw]) * Q_SCALE).astype(BF16); o += aw
        q_scr[0] = jnp.where(even_head, q, jnp.zeros_like(q))
        q_scr[1] = jnp.where(even_head, jnp.zeros_like(q), q)
        k_scr[pl.ds(tile0, tm), :] = _dot(hb, w_ref[:, o:o + aw]).astype(BF16); o += aw
        v = _dot(hb, w_ref[:, o:o + aw]).astype(BF16)
        v_scr[0, pl.ds(tile0, tm), :] = jnp.where(even_head, v, jnp.zeros_like(v))
        v_scr[1, pl.ds(tile0, tm), :] = jnp.where(even_head, jnp.zeros_like(v), v)

    def region(units, fillers=()):
        starts = [pl.multiple_of(kb * blk, blk) for kb, _, _, _ in units]
        for u, (_, r0, r1, _) in enumerate(units):
            n = r1 - r0
            for p in range(n_pairs):
                ls = slice(p * LANES, (p + 1) * LANES)
                kblk = k_scr[pl.ds(starts[u], blk), ls]
                zz = lax.dot_general(
                    jnp.concatenate([q_scr[0, r0:r1, ls], q_scr[1, r0:r1, ls]], axis=0),
                    kblk, nt, preferred_element_type=F32)
                z_scr[u * nh + 2 * p, 0:n] = zz[:n]
                z_scr[u * nh + 2 * p + 1, 0:n] = zz[n:]
        fillers = list(fillers)
        every = -(-len(units) // (len(fillers) + 1))
        for u, (_, r0, r1, diag) in enumerate(units):
            n = r1 - r0
            if diag:
                row = lax.broadcasted_iota(jnp.int32, (n, blk), 0) + r0 % blk
                valid = lax.broadcasted_iota(jnp.int32, (n, blk), 1) < row
            zs, sps = [], []
            for h in range(nh):
                z = z_scr[u * nh + h, 0:n]
                sp = jnp.maximum(z, 0.0) + jnp.log2(1.0 + jnp.exp2(-jnp.abs(z)))
                if diag:
                    sp = jnp.where(valid, sp, 0.0)
                zs.append(z)
                sps.append(sp.astype(BF16))
            incl_all = _dot(jnp.concatenate(sps, axis=0), tri_ref[...])
            for h in range(nh):
                z = zs[h]
                incl = incl_all[h * n:(h + 1) * n]
                total = jnp.broadcast_to(incl[:, 0:1], (n, blk))
                if diag:
                    w = jnp.where(valid, jnp.exp2(z - incl), 0.0)
                    c_scr[h, r0:r1] = total
                else:
                    c = c_scr[h, r0:r1]
                    w = jnp.exp2(z - incl - c)
                    c_scr[h, r0:r1] = c + total
                w_scr[u * nh + h, 0:n] = w.astype(BF16)
            if fillers and (u + 1) % every == 0:
                fillers.pop(0)()
        for filler in fillers:
            filler()
        for r0, r1 in dict.fromkeys((r0, r1) for _, r0, r1, _ in units):
            us = [u for u, (_, a, b, _) in enumerate(units) if (a, b) == (r0, r1)]
            for p in range(n_pairs):
                ls = slice(p * LANES, (p + 1) * LANES)
                ww = jnp.concatenate([w_scr[u * nh + 2 * p + hh, 0:r1 - r0]
                                      for u in us for hh in range(2)], axis=1)
                vv = jnp.concatenate([v_scr[hh, pl.ds(starts[u], blk), ls]
                                      for u in us for hh in range(2)], axis=0)
                if any(units[u][3] for u in us):
                    acc_scr[p, r0:r1] = _dot(ww, vv)
                else:
                    acc_scr[p, r0:r1] = acc_scr[p, r0:r1] + _dot(ww, vv)

    def c_min(r0, r1):
        m = c_scr[0, r0:r1]
        for h in range(1, nh):
            m = jnp.minimum(m, c_scr[h, r0:r1])
        return jnp.min(m)

    def head_units(sub, qi, n_prev):
        base = sub * blk
        units = [(qi, base, base + blk, True)]
        if n_prev >= 1:
            units.append((qi - 1, base, base + blk, False))
        if n_prev >= 2:
            units.append((qi - 2, base, base + top, False))
        return units

    def tile(units):
        h = _rms(x_ref[0]) * g_ref[...]
        hb = (h * (1.0 + mod[1:2]) + mod[0:1]).astype(BF16)
        qkv(hb)
        o = sw + 3 * aw

        def chunk(ref, col, w0):
            def run():
                ref[0, :, col:col + FRONT_CHUNK] = _dot(hb, w_ref[:, w0 + col:w0 + col + FRONT_CHUNK])
            return run

        region(units, [chunk(ref, col, w0)
                       for ref, w0, width in ((ga_ref, o, d), (gb_ref, o + d, d), (u_ref, 0, sw))
                       for col in range(0, width, FRONT_CHUNK)])

    @pl.when(step >= 1)
    def _():
        tile([u for sub in range(nsub) for u in head_units(sub, nsub * step + sub, 2)])

    @pl.when(step == 0)
    def _():
        tile([u for sub in range(nsub) for u in head_units(sub, sub, min(sub, 2))])

    def sweep(first_kb, cmin, r0, r1):
        def more(carry):
            kb, cmin = carry
            return jnp.logical_and(kb >= 0, cmin < UNDERFLOW_LOG2)

        def body(carry):
            kb, _ = carry
            region([(kb, r0, r1, False)])
            return kb - 1, c_min(r0, r1)

        lax.while_loop(more, body, (first_kb, cmin))

    tails = []
    for sub in range(nsub):
        qi = nsub * step + sub
        base = sub * blk
        tails.append((jnp.where(qi >= 2, qi - 3, -1), base, base + top))
        tails.append((jnp.where(qi >= 2, qi - 2, -1), base + top, base + blk))
    cmins = [c_min(r0, r1) for _, r0, r1 in tails]
    for (first_kb, r0, r1), cmin in zip(tails, cmins):
        sweep(first_kb, cmin, r0, r1)
    for p in range(n_pairs):
        at_ref[0, :, p * LANES:(p + 1) * LANES] = acc_scr[p].astype(at_ref.dtype)


def _attn_tri():
    blk = ATTN_BLOCK
    m = np.arange(blk)[:, None]
    j = np.arange(blk)[None, :]
    return jnp.asarray((m >= j).astype(np.float32), BF16)


def _front(x, mod, norm_g, w_in_b, sw, aw):
    bsz, seq, d = x.shape
    tm = FRONT_TILE
    blk = ATTN_BLOCK
    n = w_in_b.shape[1]
    n_pairs = aw // LANES
    nz = ATTN_REGION * (tm // blk) * 2 * n_pairs
    tri = _attn_tri()
    tok = lambda w: pl.BlockSpec((1, tm, w), lambda b, i: (b, i, 0))
    const = lambda a: pl.BlockSpec(a.shape, lambda b, i: (0,) * a.ndim,
                                   pipeline_mode=pl.Buffered(1))
    return pl.pallas_call(
        functools.partial(_front_kernel, sw=sw, aw=aw, d=d),
        grid=(bsz, seq // tm),
        in_specs=[tok(d),
                  pl.BlockSpec((1, N_ADA, d), lambda b, i: (b, 0, 0)),
                  const(norm_g), const(w_in_b), const(tri)],
        out_specs=[tok(sw), tok(d), tok(d), tok(aw)],
        out_shape=[jax.ShapeDtypeStruct((bsz, seq, sw), F32),
                   jax.ShapeDtypeStruct((bsz, seq, d), F32),
                   jax.ShapeDtypeStruct((bsz, seq, d), F32),
                   jax.ShapeDtypeStruct((bsz, seq, aw), BF16)],
        scratch_shapes=[pltpu.VMEM((2, tm, aw), BF16),
                        pltpu.VMEM((seq, aw), BF16),
                        pltpu.VMEM((2, seq, aw), BF16),
                        pltpu.VMEM((2 * n_pairs, tm, blk), F32),
                        pltpu.VMEM((n_pairs, tm, LANES), F32),
                        pltpu.VMEM((nz, blk, blk), F32),
                        pltpu.VMEM((nz, blk, blk), BF16)],
        compiler_params=pltpu.CompilerParams(
            dimension_semantics=("arbitrary", "arbitrary"),
            vmem_limit_bytes=VMEM_LIMIT_BYTES),
        name="front",
    )(x, mod, norm_g, w_in_b, tri)


def _out_ffn_kernel(x_ref, s5_ref, at_ref, ga_ref, gb_ref, mod_ref, n2_ref, nf_ref,
                    wa_ref, wb_ref, wo_ref, wg_ref, wu_ref, wd_ref, o_ref, *, final_norm):
    mod = mod_ref[0]
    tm = x_ref.shape[1]
    groups = [slice(r, r + tm // OUT_GROUPS) for r in range(0, tm, tm // OUT_GROUPS)]
    ms = []
    for g in groups:
        ya = _dot(s5_ref[0, g].astype(BF16), wa_ref[...])
        yb = _dot(at_ref[0, g], wb_ref[...])
        m = jax.nn.sigmoid(ga_ref[0, g]) * ya + jax.nn.sigmoid(gb_ref[0, g]) * yb
        ms.append(m.astype(BF16))
    x1s, hs = [], []
    for g, m in zip(groups, ms):
        x1 = x_ref[0, g] + mod[2:3] * _dot(m, wo_ref[...])
        h = _rms(x1) * n2_ref[...]
        x1s.append(x1)
        hs.append((h * (1.0 + mod[4:5]) + mod[3:4]).astype(BF16))
    acts = []
    for h in hs:
        gate = _dot(h, wg_ref[...])
        up = _dot(h, wu_ref[...])
        acts.append((gate * jax.nn.sigmoid(gate) * up).astype(BF16))
    for g, x1, act in zip(groups, x1s, acts):
        x2 = x1 + mod[5:6] * _dot(act, wd_ref[...])
        o_ref[0, g] = _rms(x2) * nf_ref[...] if final_norm else x2


def _out_ffn(x, s5o, attn, ga, gb, mod, n2g, nfg, wa, wb, wo, wg, wu, wd, final_norm):
    bsz, seq, d = x.shape
    tm = OUT_TILE
    tok = lambda a: pl.BlockSpec((1, tm, a.shape[-1]), lambda b, i: (b, i, 0))
    const = lambda a: pl.BlockSpec(a.shape, lambda b, i: (0,) * a.ndim,
                                   pipeline_mode=pl.Buffered(1))
    return pl.pallas_call(
        functools.partial(_out_ffn_kernel, final_norm=final_norm),
        grid=(bsz, seq // tm),
        in_specs=[tok(x), tok(s5o), tok(attn), tok(ga), tok(gb),
                  pl.BlockSpec((1, N_ADA, d), lambda b, i: (b, 0, 0)),
                  const(n2g), const(nfg),
                  const(wa), const(wb), const(wo), const(wg), const(wu), const(wd)],
        out_specs=pl.BlockSpec((1, tm, d), lambda b, i: (b, i, 0)),
        out_shape=jax.ShapeDtypeStruct((bsz, seq, d), F32),
        compiler_params=pltpu.CompilerParams(
            dimension_semantics=("arbitrary", "arbitrary"),
            vmem_limit_bytes=VMEM_LIMIT_BYTES),
        name="out_ffn",
    )(x, s5o, attn, ga, gb, mod, n2g, nfg, wa, wb, wo, wg, wu, wd)


def kernel(x, c, w_ada, b_ada, norm1_g, w_in, lam_re, lam_im, log_dt, b_re, b_im, c_re, c_im,
           d_skip, w_glu, b_glu, w_a, w_b, w_o, norm2_g, w_ffn_gate, w_ffn_up, w_ffn_down,
           norm_f_g):
    depth = w_ada.shape[0]
    bsz, seq, d = x.shape
    sw = w_glu.shape[1]
    aw = w_b.shape[1]
    for l in range(depth):
        mod = _ada(c, w_ada[l], b_ada[l]).reshape(bsz, N_ADA, d)
        u, ga, gb, attn = _front(x, mod, norm1_g[l].reshape(1, d), w_in[l].astype(BF16), sw, aw)
        wb, cm, lamr, lami = _s5_weights(lam_re[l], lam_im[l], log_dt[l], b_re[l], b_im[l],
                                         c_re[l], c_im[l])
        s5o = _s5(u, wb, cm, lamr, lami, d_skip[l], w_glu[l].astype(BF16), b_glu[l])
        x = _out_ffn(x, s5o, attn, ga, gb, mod, norm2_g[l].reshape(1, d), norm_f_g.reshape(1, d),
                     w_a[l].astype(BF16), w_b[l].astype(BF16), w_o[l].astype(BF16),
                     w_ffn_gate[l].astype(BF16), w_ffn_up[l].astype(BF16),
                     w_ffn_down[l].astype(BF16), final_norm=(l == depth - 1))
    return x
```

```python
import functools
import math

import numpy as np
import jax
import jax.numpy as jnp
from jax import lax
from jax.experimental import pallas as pl
from jax.experimental.pallas import tpu as pltpu

F32 = jnp.float32
BF16 = jnp.bfloat16

S5_GROUP = 16
S5_STATE = 64
HEAD_DIM = 64
N_ADA = 6
RMS_EPS = 1e-6
Q_SCALE = math.log2(math.e) / math.sqrt(HEAD_DIM)
UNDERFLOW_LOG2 = 151.0

LANES = 128
SUBLANES = 8
VMEM_LIMIT_BYTES = 56 * 1024 * 1024

ATTN_BLOCK = 128
ATTN_REGION = 3
ATTN_TOP_ROWS = 32
S5_TILE = 512
S5_SUBTILE = 128
S5_SLABS = 4
FRONT_TILE = 512
FRONT_CHUNK = 256
OUT_TILE = 512
OUT_GROUPS = 2


def _dot(a, b):
    return jnp.dot(a, b, preferred_element_type=F32)


def _rms(x):
    return x * lax.rsqrt(jnp.mean(x * x, axis=-1, keepdims=True) + RMS_EPS)


def _ada_kernel(c_ref, w_ref, b_ref, o_ref):
    c = c_ref[...]
    bsz = c.shape[0]
    cond = c * jax.nn.sigmoid(c)
    pad = -bsz % SUBLANES
    if pad:
        cond = jnp.concatenate([cond, jnp.zeros((pad, c.shape[1]), F32)], axis=0)
    o_ref[...] = _dot(cond.astype(BF16), w_ref[...].astype(BF16))[:bsz] + b_ref[...]


def _ada(c, w_ada, b_ada):
    bsz, d = c.shape
    n = w_ada.shape[1]
    tn = 1536
    return pl.pallas_call(
        _ada_kernel,
        grid=(n // tn,),
        in_specs=[pl.BlockSpec((bsz, d), lambda j: (0, 0)),
                  pl.BlockSpec((d, tn), lambda j: (0, j)),
                  pl.BlockSpec((1, tn), lambda j: (0, j))],
        out_specs=pl.BlockSpec((bsz, tn), lambda j: (0, j)),
        out_shape=jax.ShapeDtypeStruct((bsz, n), F32),
        name="ada",
    )(c, w_ada, b_ada.reshape(1, n))


def _s5_kernel(u_ref, perm_ref, permt_ref, wb_ref, cm_ref, lamr_ref, lami_ref, d_ref,
               wglu_ref, bglu_ref, o_ref, x_scr, ulast_scr, *, tm, sw):
    i = pl.program_id(0)
    ts = S5_SUBTILE
    n2 = ts // 2
    rows = SUBLANES * n2
    cw = sw // S5_SLABS
    hs = cw * S5_STATE // S5_GROUP

    @pl.when(i == 0)
    def _():
        x_scr[...] = jnp.zeros_like(x_scr)
        ulast_scr[...] = jnp.zeros_like(ulast_scr)

    sub8 = lax.broadcasted_iota(jnp.int32, (SUBLANES, sw), 0)
    odd = (lax.broadcasted_iota(jnp.int32, (rows, sw), 0) & 1) == 1

    def natural(j):
        return jnp.concatenate([u_ref[b, j * ts:(j + 1) * ts, :] for b in range(4)], axis=0)

    def last_rows(j):
        last = jnp.zeros((SUBLANES, sw), F32)
        for b in range(4):
            row = u_ref[b, (j + 1) * ts - 1:(j + 1) * ts, :].astype(BF16).astype(F32)
            last = jnp.where(sub8 == 2 * b, jnp.broadcast_to(row, (SUBLANES, sw)), last)
        return last

    def input_stage(j, before):
        a_cur = _dot(perm_ref[...], natural(j).astype(BF16))
        a_prev = jnp.where(odd, pltpu.roll(a_cur, 1, 0), pltpu.roll(a_cur, SUBLANES - 1, 0))
        first = jnp.where((sub8 & 1) == 1, a_prev[:SUBLANES], before)
        a_prev = jnp.concatenate([first, a_prev[SUBLANES:]], axis=0).astype(BF16)
        a_cur = a_cur.astype(BF16)
        return [_dot(jnp.concatenate([a_cur[:, s * cw:(s + 1) * cw],
                                      a_prev[:, s * cw:(s + 1) * cw]], axis=1), wb_ref[s])
                for s in range(S5_SLABS)]

    def scan_stage(bus, x):
        states, x_out = [], []
        for s in range(S5_SLABS):
            ar = lamr_ref[:, hs * s:hs * (s + 1)]
            ai = lami_ref[:, hs * s:hs * (s + 1)]
            xr, xi = x[s]
            st = []
            for t2 in range(n2):
                rs = slice(SUBLANES * t2, SUBLANES * (t2 + 1))
                xr, xi = (ar * xr - ai * xi + bus[s][rs, :hs],
                          ar * xi + ai * xr + bus[s][rs, hs:])
                st.append(jnp.concatenate([xr, xi], axis=1))
            x_out.append((xr, xi))
            states.append(jnp.concatenate(st, axis=0).astype(BF16))
        return states, x_out

    def output_stage(j, states):
        y_il = jnp.concatenate([_dot(states[s], cm_ref[s]) for s in range(S5_SLABS)], axis=1)
        hi = y_il.astype(BF16)
        lo = (y_il - hi.astype(F32)).astype(BF16)
        y = _dot(permt_ref[...], jnp.concatenate([hi, lo], axis=0))
        y = y + d_ref[...] * natural(j)
        y = jax.nn.gelu(y)
        z = _dot(y.astype(BF16), wglu_ref[...]) + bglu_ref[...]
        out = y * jax.nn.sigmoid(z)
        for b in range(4):
            o_ref[b, j * ts:(j + 1) * ts, :] = out[b * ts:(b + 1) * ts]

    nsub = tm // ts
    befores = [ulast_scr[...]] + [last_rows(j) for j in range(nsub - 1)]
    bus = [input_stage(j, befores[j]) for j in range(nsub)]
    ulast_scr[...] = last_rows(nsub - 1)
    x = [(x_scr[s, 0], x_scr[s, 1]) for s in range(S5_SLABS)]
    for j in range(nsub):
        states, x = scan_stage(bus[j], x)
        output_stage(j, states)
    for s in range(S5_SLABS):
        x_scr[s, 0], x_scr[s, 1] = x[s]


def _s5_perms(tm):
    n2 = tm // 2
    rows = SUBLANES * n2
    perm = np.zeros((rows, 4 * tm), np.float32)
    permt = np.zeros((4 * tm, 2 * rows), np.float32)
    for t2 in range(n2):
        for b in range(4):
            for par in range(2):
                r = SUBLANES * t2 + 2 * b + par
                t = 2 * t2 + par
                perm[r, b * tm + t] = 1.0
                permt[b * tm + t, r] = 1.0
                permt[b * tm + t, rows + r] = 1.0
    return jnp.asarray(perm, BF16), jnp.asarray(permt, BF16)


def _block_diag(blocks):
    n, a, b = blocks.shape
    eye = jnp.eye(n, dtype=blocks.dtype)
    return (blocks[:, :, None, :] * eye[:, None, :, None]).reshape(n * a, n * b)


def _s5_weights(lam_re, lam_im, log_dt, b_re, b_im, c_re, c_im):
    g = lam_re.shape[0]
    dt = jnp.exp(log_dt)[:, None]
    mag = jnp.exp(lam_re * dt)
    lbr = mag * jnp.cos(lam_im * dt)
    lbi = mag * jnp.sin(lam_im * dt)
    nr, ni = lbr - 1.0, lbi
    den = lam_re * lam_re + lam_im * lam_im
    cr = (nr * lam_re + ni * lam_im) / den
    ci = (ni * lam_re - nr * lam_im) / den
    bbr = cr[..., None] * b_re - ci[..., None] * b_im
    bbi = cr[..., None] * b_im + ci[..., None] * b_re
    lr = lbr[..., None] * bbr - lbi[..., None] * bbi
    li = lbr[..., None] * bbi + lbi[..., None] * bbr
    l2r = lbr * lbr - lbi * lbi
    l2i = 2.0 * lbr * lbi
    gh = g // S5_SLABS
    wbs, cms = [], []
    for h in range(S5_SLABS):
        s = slice(h * gh, (h + 1) * gh)
        t = lambda a: jnp.swapaxes(a[s], 1, 2)
        top = jnp.concatenate([_block_diag(t(bbr)), _block_diag(t(bbi))], axis=1)
        bot = jnp.concatenate([_block_diag(t(lr)), _block_diag(t(li))], axis=1)
        wbs.append(jnp.concatenate([top, bot], axis=0))
        cms.append(jnp.concatenate([_block_diag(jnp.swapaxes(c_re[s], 1, 2)),
                                    -_block_diag(jnp.swapaxes(c_im[s], 1, 2))], axis=0))
    wb = jnp.stack(wbs).astype(BF16)
    cm = jnp.stack(cms).astype(BF16)
    lamr = jnp.broadcast_to(l2r.reshape(1, -1), (SUBLANES, l2r.size))
    lami = jnp.broadcast_to(l2i.reshape(1, -1), (SUBLANES, l2i.size))
    return wb, cm, lamr, lami


def _s5(u, wb, cm, lamr, lami, d_skip, w_glu_b, b_glu):
    bsz, seq, sw = u.shape
    assert bsz == 4, "the scan packs 4 batch rows x 2 token parities into 8 sublanes"
    tm = S5_TILE
    ns = lamr.shape[1]
    perm, permt = _s5_perms(S5_SUBTILE)
    const = lambda a: pl.BlockSpec(a.shape, lambda i: (0,) * a.ndim)
    d_row = d_skip.reshape(1, sw)
    bg = b_glu.reshape(1, sw)
    return pl.pallas_call(
        functools.partial(_s5_kernel, tm=tm, sw=sw),
        grid=(seq // tm,),
        in_specs=[pl.BlockSpec((4, tm, sw), lambda i: (0, i, 0)),
                  const(perm), const(permt), const(wb), const(cm), const(lamr), const(lami),
                  const(d_row), const(w_glu_b), const(bg)],
        out_specs=pl.BlockSpec((4, tm, sw), lambda i: (0, i, 0)),
        out_shape=jax.ShapeDtypeStruct((bsz, seq, sw), F32),
        scratch_shapes=[pltpu.VMEM((S5_SLABS, 2, SUBLANES, ns // S5_SLABS), F32),
                        pltpu.VMEM((SUBLANES, sw), F32)],
        compiler_params=pltpu.CompilerParams(
            dimension_semantics=("arbitrary",),
            vmem_limit_bytes=VMEM_LIMIT_BYTES),
        name="s5",
    )(u, perm, permt, wb, cm, lamr, lami, d_row, w_glu_b, bg)


def _front_kernel(x_ref, mod_ref, g_ref, w_ref, tri_ref, u_ref, ga_ref, gb_ref, at_ref,
                  q_scr, k_scr, v_scr, c_scr, acc_scr, z_scr, w_scr, *, sw, aw, d):
    blk = ATTN_BLOCK
    top = ATTN_TOP_ROWS
    n_pairs = aw // LANES
    nh = 2 * n_pairs
    tm = x_ref.shape[1]
    nsub = tm // blk
    step = pl.program_id(1)
    tile0 = pl.multiple_of(step * tm, tm)
    mod = mod_ref[0]
    nt = (((1,), (1,)), ((), ()))
    even_head = (lax.broadcasted_iota(jnp.int32, (tm, aw), 1) // HEAD_DIM) % 2 == 0

    def qkv(hb):
        o = sw
        q = (_dot(hb, w_ref[:, o:o + aw]) * Q_SCALE).astype(BF16); o += aw
        q_scr[0] = jnp.where(even_head, q, jnp.zeros_like(q))
        q_scr[1] = jnp.where(even_head, jnp.zeros_like(q), q)
        k_scr[pl.ds(tile0, tm), :] = _dot(hb, w_ref[:, o:o + aw]).astype(BF16); o += aw
        v = _dot(hb, w_ref[:, o:o + aw]).astype(BF16)
        v_scr[0, pl.ds(tile0, tm), :] = jnp.where(even_head, v, jnp.zeros_like(v))
        v_scr[1, pl.ds(tile0, tm), :] = jnp.where(even_head, jnp.zeros_like(v), v)

    def region(units, fillers=()):
        starts = [pl.multiple_of(kb * blk, blk) for kb, _, _, _ in units]
        for u, (_, r0, r1, _) in enumerate(units):
            n = r1 - r0
            for p in range(n_pairs):
                ls = slice(p * LANES, (p + 1) * LANES)
                kblk = k_scr[pl.ds(starts[u], blk), ls]
                zz = lax.dot_general(
                    jnp.concatenate([q_scr[0, r0:r1, ls], q_scr[1, r0:r1, ls]], axis=0),
                    kblk, nt, preferred_element_type=F32)
                z_scr[u * nh + 2 * p, 0:n] = zz[:n]
                z_scr[u * nh + 2 * p + 1, 0:n] = zz[n:]
        fillers = list(fillers)
        every = -(-len(units) // (len(fillers) + 1))
        for u, (_, r0, r1, diag) in enumerate(units):
            n = r1 - r0
            if diag:
                row = lax.broadcasted_iota(jnp.int32, (n, blk), 0) + r0 % blk
                valid = lax.broadcasted_iota(jnp.int32, (n, blk), 1) < row
            zs, sps = [], []
            for h in range(nh):
                z = z_scr[u * nh + h, 0:n]
                sp = jnp.maximum(z, 0.0) + jnp.log2(1.0 + jnp.exp2(-jnp.abs(z)))
                if diag:
                    sp = jnp.where(valid, sp, 0.0)
                zs.append(z)
                sps.append(sp.astype(BF16))
            incl_all = _dot(jnp.concatenate(sps, axis=0), tri_ref[...])
            for h in range(nh):
                z = zs[h]
                incl = incl_all[h * n:(h + 1) * n]
                total = jnp.broadcast_to(incl[:, 0:1], (n, blk))
                if diag:
                    w = jnp.where(valid, jnp.exp2(z - incl), 0.0)
                    c_scr[h, r0:r1] = total
                else:
                    c = c_scr[h, r0:r1]
                    w = jnp.exp2(z - incl - c)
                    c_scr[h, r0:r1] = c + total
                w_scr[u * nh + h, 0:n] = w.astype(BF16)
            if fillers and (u + 1) % every == 0:
                fillers.pop(0)()
        for filler in fillers:
            filler()
        for r0, r1 in dict.fromkeys((r0, r1) for _, r0, r1, _ in units):
            us = [u for u, (_, a, b, _) in enumerate(units) if (a, b) == (r0, r1)]
            for p in range(n_pairs):
                ls = slice(p * LANES, (p + 1) * LANES)
                ww = jnp.concatenate([w_scr[u * nh + 2 * p + hh, 0:r1 - r0]
                                      for u in us for hh in range(2)], axis=1)
                vv = jnp.concatenate([v_scr[hh, pl.ds(starts[u], blk), ls]
                                      for u in us for hh in range(2)], axis=0)
                if any(units[u][3] for u in us):
                    acc_scr[p, r0:r1] = _dot(ww, vv)
                else:
                    acc_scr[p, r0:r1] = acc_scr[p, r0:r1] + _dot(ww, vv)

    def c_min(r0, r1):
        m = c_scr[0, r0:r1]
        for h in range(1, nh):
            m = jnp.minimum(m, c_scr[h, r0:r1])
        return jnp.min(m)

    def head_units(sub, qi, n_prev):
        base = sub * blk
        units = [(qi, base, base + blk, True)]
        if n_prev >= 1:
            units.append((qi - 1, base, base + blk, False))
        if n_prev >= 2:
            units.append((qi - 2, base, base + top, False))
        return units

    def tile(units):
        h = _rms(x_ref[0]) * g_ref[...]
        hb = (h * (1.0 + mod[1:2]) + mod[0:1]).astype(BF16)
        qkv(hb)
        o = sw + 3 * aw

        def chunk(ref, col, w0):
            def run():
                ref[0, :, col:col + FRONT_CHUNK] = _dot(hb, w_ref[:, w0 + col:w0 + col + FRONT_CHUNK])
            return run

        region(units, [chunk(ref, col, w0)
                       for ref, w0, width in ((ga_ref, o, d), (gb_ref, o + d, d), (u_ref, 0, sw))
                       for col in range(0, width, FRONT_CHUNK)])

    @pl.when(step >= 1)
    def _():
        tile([u for sub in range(nsub) for u in head_units(sub, nsub * step + sub, 2)])

    @pl.when(step == 0)
    def _():
        tile([u for sub in range(nsub) for u in head_units(sub, sub, min(sub, 2))])

    def sweep(first_kb, cmin, r0, r1):
        def more(carry):
            kb, cmin = carry
            return jnp.logical_and(kb >= 0, cmin < UNDERFLOW_LOG2)

        def body(carry):
            kb, _ = carry
            region([(kb, r0, r1, False)])
            return kb - 1, c_min(r0, r1)

        lax.while_loop(more, body, (first_kb, cmin))

    tails = []
    for sub in range(nsub):
        qi = nsub * step + sub
        base = sub * blk
        tails.append((jnp.where(qi >= 2, qi - 3, -1), base, base + top))
        tails.append((jnp.where(qi >= 2, qi - 2, -1), base + top, base + blk))

    @pl.when(c_min(0, tm) < UNDERFLOW_LOG2)
    def _():
        cmins = [c_min(r0, r1) for _, r0, r1 in tails]
        for (first_kb, r0, r1), cmin in zip(tails, cmins):
            sweep(first_kb, cmin, r0, r1)

    for p in range(n_pairs):
        at_ref[0, :, p * LANES:(p + 1) * LANES] = acc_scr[p].astype(at_ref.dtype)


def _attn_tri():
    blk = ATTN_BLOCK
    m = np.arange(blk)[:, None]
    j = np.arange(blk)[None, :]
    return jnp.asarray((m >= j).astype(np.float32), BF16)


def _front(x, mod, norm_g, w_in_b, sw, aw):
    bsz, seq, d = x.shape
    tm = FRONT_TILE
    blk = ATTN_BLOCK
    n = w_in_b.shape[1]
    n_pairs = aw // LANES
    nz = ATTN_REGION * (tm // blk) * 2 * n_pairs
    tri = _attn_tri()
    tok = lambda w: pl.BlockSpec((1, tm, w), lambda b, i: (b, i, 0))
    const = lambda a: pl.BlockSpec(a.shape, lambda b, i: (0,) * a.ndim,
                                   pipeline_mode=pl.Buffered(1))
    return pl.pallas_call(
        functools.partial(_front_kernel, sw=sw, aw=aw, d=d),
        grid=(bsz, seq // tm),
        in_specs=[tok(d),
                  pl.BlockSpec((1, N_ADA, d), lambda b, i: (b, 0, 0)),
                  const(norm_g), const(w_in_b), const(tri)],
        out_specs=[tok(sw), tok(d), tok(d), tok(aw)],
        out_shape=[jax.ShapeDtypeStruct((bsz, seq, sw), F32),
                   jax.ShapeDtypeStruct((bsz, seq, d), F32),
                   jax.ShapeDtypeStruct((bsz, seq, d), F32),
                   jax.ShapeDtypeStruct((bsz, seq, aw), BF16)],
        scratch_shapes=[pltpu.VMEM((2, tm, aw), BF16),
                        pltpu.VMEM((seq, aw), BF16),
                        pltpu.VMEM((2, seq, aw), BF16),
                        pltpu.VMEM((2 * n_pairs, tm, blk), F32),
                        pltpu.VMEM((n_pairs, tm, LANES), F32),
                        pltpu.VMEM((nz, blk, blk), F32),
                        pltpu.VMEM((nz, blk, blk), BF16)],
        compiler_params=pltpu.CompilerParams(
            dimension_semantics=("arbitrary", "arbitrary"),
            vmem_limit_bytes=VMEM_LIMIT_BYTES),
        name="front",
    )(x, mod, norm_g, w_in_b, tri)


def _out_ffn_kernel(x_ref, s5_ref, at_ref, ga_ref, gb_ref, mod_ref, n2_ref, nf_ref,
                    wa_ref, wb_ref, wo_ref, wg_ref, wu_ref, wd_ref, o_ref, *, final_norm):
    mod = mod_ref[0]
    tm = x_ref.shape[1]
    groups = [slice(r, r + tm // OUT_GROUPS) for r in range(0, tm, tm // OUT_GROUPS)]
    ms = []
    for g in groups:
        ya = _dot(s5_ref[0, g].astype(BF16), wa_ref[...])
        yb = _dot(at_ref[0, g], wb_ref[...])
        m = jax.nn.sigmoid(ga_ref[0, g]) * ya + jax.nn.sigmoid(gb_ref[0, g]) * yb
        ms.append(m.astype(BF16))
    x1s, hs = [], []
    for g, m in zip(groups, ms):
        x1 = x_ref[0, g] + mod[2:3] * _dot(m, wo_ref[...])
        h = _rms(x1) * n2_ref[...]
        x1s.append(x1)
        hs.append((h * (1.0 + mod[4:5]) + mod[3:4]).astype(BF16))
    acts = []
    for h in hs:
        gate = _dot(h, wg_ref[...])
        up = _dot(h, wu_ref[...])
        acts.append((gate * jax.nn.sigmoid(gate) * up).astype(BF16))
    for g, x1, act in zip(groups, x1s, acts):
        x2 = x1 + mod[5:6] * _dot(act, wd_ref[...])
        o_ref[0, g] = _rms(x2) * nf_ref[...] if final_norm else x2


def _out_ffn(x, s5o, attn, ga, gb, mod, n2g, nfg, wa, wb, wo, wg, wu, wd, final_norm):
    bsz, seq, d = x.shape
    tm = OUT_TILE
    tok = lambda a: pl.BlockSpec((1, tm, a.shape[-1]), lambda b, i: (b, i, 0))
    const = lambda a: pl.BlockSpec(a.shape, lambda b, i: (0,) * a.ndim,
                                   pipeline_mode=pl.Buffered(1))
    return pl.pallas_call(
        functools.partial(_out_ffn_kernel, final_norm=final_norm),
        grid=(bsz, seq // tm),
        in_specs=[tok(x), tok(s5o), tok(attn), tok(ga), tok(gb),
                  pl.BlockSpec((1, N_ADA, d), lambda b, i: (b, 0, 0)),
                  const(n2g), const(nfg),
                  const(wa), const(wb), const(wo), const(wg), const(wu), const(wd)],
        out_specs=pl.BlockSpec((1, tm, d), lambda b, i: (b, i, 0)),
        out_shape=jax.ShapeDtypeStruct((bsz, seq, d), F32),
        compiler_params=pltpu.CompilerParams(
            dimension_semantics=("arbitrary", "arbitrary"),
            vmem_limit_bytes=VMEM_LIMIT_BYTES),
        name="out_ffn",
    )(x, s5o, attn, ga, gb, mod, n2g, nfg, wa, wb, wo, wg, wu, wd)


def kernel(x, c, w_ada, b_ada, norm1_g, w_in, lam_re, lam_im, log_dt, b_re, b_im, c_re, c_im,
           d_skip, w_glu, b_glu, w_a, w_b, w_o, norm2_g, w_ffn_gate, w_ffn_up, w_ffn_down,
           norm_f_g):
    depth = w_ada.shape[0]
    bsz, seq, d = x.shape
    sw = w_glu.shape[1]
    aw = w_b.shape[1]
    for l in range(depth):
        mod = _ada(c, w_ada[l], b_ada[l]).reshape(bsz, N_ADA, d)
        u, ga, gb, attn = _front(x, mod, norm1_g[l].reshape(1, d), w_in[l].astype(BF16), sw, aw)
        wb, cm, lamr, lami = _s5_weights(lam_re[l], lam_im[l], log_dt[l], b_re[l], b_im[l],
                                         c_re[l], c_im[l])
        s5o = _s5(u, wb, cm, lamr, lami, d_skip[l], w_glu[l].astype(BF16), b_glu[l])
        x = _out_ffn(x, s5o, attn, ga, gb, mod, norm2_g[l].reshape(1, d), norm_f_g.reshape(1, d),
                     w_a[l].astype(BF16), w_b[l].astype(BF16), w_o[l].astype(BF16),
                     w_ffn_gate[l].astype(BF16), w_ffn_up[l].astype(BF16),
                     w_ffn_down[l].astype(BF16), final_norm=(l == depth - 1))
    return x
```

```python
import functools
import math

import numpy as np
import jax
import jax.numpy as jnp
from jax import lax
from jax.experimental import pallas as pl
from jax.experimental.pallas import tpu as pltpu

F32 = jnp.float32
BF16 = jnp.bfloat16

S5_GROUP = 16
S5_STATE = 64
HEAD_DIM = 64
N_ADA = 6
RMS_EPS = 1e-6
Q_SCALE = math.log2(math.e) / math.sqrt(HEAD_DIM)
UNDERFLOW_LOG2 = 151.0

LANES = 128
SUBLANES = 8
VMEM_LIMIT_BYTES = 56 * 1024 * 1024

ATTN_BLOCK = 128
ATTN_REGION = 3
ATTN_TOP_ROWS = 32
S5_TILE = 512
S5_SUBTILE = 128
S5_SLABS = 4
FRONT_TILE = 512
FRONT_CHUNK = 256
OUT_TILE = 512
OUT_GROUPS = 2


def _dot(a, b):
    return jnp.dot(a, b, preferred_element_type=F32)


def _rms(x):
    return x * lax.rsqrt(jnp.mean(x * x, axis=-1, keepdims=True) + RMS_EPS)


def _ada_kernel(c_ref, w_ref, b_ref, o_ref):
    c = c_ref[...]
    bsz = c.shape[0]
    cond = c * jax.nn.sigmoid(c)
    pad = -bsz % SUBLANES
    if pad:
        cond = jnp.concatenate([cond, jnp.zeros((pad, c.shape[1]), F32)], axis=0)
    o_ref[...] = _dot(cond.astype(BF16), w_ref[...].astype(BF16))[:bsz] + b_ref[...]


def _ada(c, w_ada, b_ada):
    bsz, d = c.shape
    n = w_ada.shape[1]
    tn = 1536
    return pl.pallas_call(
        _ada_kernel,
        grid=(n // tn,),
        in_specs=[pl.BlockSpec((bsz, d), lambda j: (0, 0)),
                  pl.BlockSpec((d, tn), lambda j: (0, j)),
                  pl.BlockSpec((1, tn), lambda j: (0, j))],
        out_specs=pl.BlockSpec((bsz, tn), lambda j: (0, j)),
        out_shape=jax.ShapeDtypeStruct((bsz, n), F32),
        name="ada",
    )(c, w_ada, b_ada.reshape(1, n))


def _s5_kernel(u_ref, perm_ref, permt_ref, wb_ref, cm_ref, lamr_ref, lami_ref, d_ref,
               wglu_ref, bglu_ref, o_ref, x_scr, ulast_scr, *, tm, sw):
    i = pl.program_id(0)
    ts = S5_SUBTILE
    n2 = ts // 2
    rows = SUBLANES * n2
    cw = sw // S5_SLABS
    hs = cw * S5_STATE // S5_GROUP

    @pl.when(i == 0)
    def _():
        x_scr[...] = jnp.zeros_like(x_scr)
        ulast_scr[...] = jnp.zeros_like(ulast_scr)

    sub8 = lax.broadcasted_iota(jnp.int32, (SUBLANES, sw), 0)
    odd = (lax.broadcasted_iota(jnp.int32, (rows, sw), 0) & 1) == 1

    def natural(j):
        return jnp.concatenate([u_ref[b, j * ts:(j + 1) * ts, :] for b in range(4)], axis=0)

    def last_rows(j):
        last = jnp.zeros((SUBLANES, sw), F32)
        for b in range(4):
            row = u_ref[b, (j + 1) * ts - 1:(j + 1) * ts, :].astype(BF16).astype(F32)
            last = jnp.where(sub8 == 2 * b, jnp.broadcast_to(row, (SUBLANES, sw)), last)
        return last

    def input_stage(j, before):
        a_cur = _dot(perm_ref[...], natural(j).astype(BF16))
        a_prev = jnp.where(odd, pltpu.roll(a_cur, 1, 0), pltpu.roll(a_cur, SUBLANES - 1, 0))
        first = jnp.where((sub8 & 1) == 1, a_prev[:SUBLANES], before)
        a_prev = jnp.concatenate([first, a_prev[SUBLANES:]], axis=0).astype(BF16)
        a_cur = a_cur.astype(BF16)
        return [_dot(jnp.concatenate([a_cur[:, s * cw:(s + 1) * cw],
                                      a_prev[:, s * cw:(s + 1) * cw]], axis=1), wb_ref[s])
                for s in range(S5_SLABS)]

    def scan_stage(bus, x):
        states, x_out = [], []
        for s in range(S5_SLABS):
            ar = lamr_ref[:, hs * s:hs * (s + 1)]
            ai = lami_ref[:, hs * s:hs * (s + 1)]
            xr, xi = x[s]
            st = []
            for t2 in range(n2):
                rs = slice(SUBLANES * t2, SUBLANES * (t2 + 1))
                xr, xi = (ar * xr - ai * xi + bus[s][rs, :hs],
                          ar * xi + ai * xr + bus[s][rs, hs:])
                st.append(jnp.concatenate([xr, xi], axis=1))
            x_out.append((xr, xi))
            states.append(jnp.concatenate(st, axis=0).astype(BF16))
        return states, x_out

    def output_stage(j, states):
        y_il = jnp.concatenate([_dot(states[s], cm_ref[s]) for s in range(S5_SLABS)], axis=1)
        hi = y_il.astype(BF16)
        lo = (y_il - hi.astype(F32)).astype(BF16)
        y = _dot(permt_ref[...], jnp.concatenate([hi, lo], axis=0))
        y = y + d_ref[...] * natural(j)
        y = jax.nn.gelu(y)
        z = _dot(y.astype(BF16), wglu_ref[...]) + bglu_ref[...]
        out = y * jax.nn.sigmoid(z)
        for b in range(4):
            o_ref[b, j * ts:(j + 1) * ts, :] = out[b * ts:(b + 1) * ts]

    nsub = tm // ts
    befores = [ulast_scr[...]] + [last_rows(j) for j in range(nsub - 1)]
    bus = [input_stage(j, befores[j]) for j in range(nsub)]
    ulast_scr[...] = last_rows(nsub - 1)
    x = [(x_scr[s, 0], x_scr[s, 1]) for s in range(S5_SLABS)]
    for j in range(nsub):
        states, x = scan_stage(bus[j], x)
        output_stage(j, states)
    for s in range(S5_SLABS):
        x_scr[s, 0], x_scr[s, 1] = x[s]


def _s5_perms(tm):
    n2 = tm // 2
    rows = SUBLANES * n2
    perm = np.zeros((rows, 4 * tm), np.float32)
    permt = np.zeros((4 * tm, 2 * rows), np.float32)
    for t2 in range(n2):
        for b in range(4):
            for par in range(2):
                r = SUBLANES * t2 + 2 * b + par
                t = 2 * t2 + par
                perm[r, b * tm + t] = 1.0
                permt[b * tm + t, r] = 1.0
                permt[b * tm + t, rows + r] = 1.0
    return jnp.asarray(perm, BF16), jnp.asarray(permt, BF16)


def _block_diag(blocks):
    n, a, b = blocks.shape
    eye = jnp.eye(n, dtype=blocks.dtype)
    return (blocks[:, :, None, :] * eye[:, None, :, None]).reshape(n * a, n * b)


def _s5_weights(lam_re, lam_im, log_dt, b_re, b_im, c_re, c_im):
    g = lam_re.shape[0]
    dt = jnp.exp(log_dt)[:, None]
    mag = jnp.exp(lam_re * dt)
    lbr = mag * jnp.cos(lam_im * dt)
    lbi = mag * jnp.sin(lam_im * dt)
    nr, ni = lbr - 1.0, lbi
    den = lam_re * lam_re + lam_im * lam_im
    cr = (nr * lam_re + ni * lam_im) / den
    ci = (ni * lam_re - nr * lam_im) / den
    bbr = cr[..., None] * b_re - ci[..., None] * b_im
    bbi = cr[..., None] * b_im + ci[..., None] * b_re
    lr = lbr[..., None] * bbr - lbi[..., None] * bbi
    li = lbr[..., None] * bbi + lbi[..., None] * bbr
    l2r = lbr * lbr - lbi * lbi
    l2i = 2.0 * lbr * lbi
    gh = g // S5_SLABS
    wbs, cms = [], []
    for h in range(S5_SLABS):
        s = slice(h * gh, (h + 1) * gh)
        t = lambda a: jnp.swapaxes(a[s], 1, 2)
        top = jnp.concatenate([_block_diag(t(bbr)), _block_diag(t(bbi))], axis=1)
        bot = jnp.concatenate([_block_diag(t(lr)), _block_diag(t(li))], axis=1)
        wbs.append(jnp.concatenate([top, bot], axis=0))
        cms.append(jnp.concatenate([_block_diag(jnp.swapaxes(c_re[s], 1, 2)),
                                    -_block_diag(jnp.swapaxes(c_im[s], 1, 2))], axis=0))
    wb = jnp.stack(wbs).astype(BF16)
    cm = jnp.stack(cms).astype(BF16)
    lamr = jnp.broadcast_to(l2r.reshape(1, -1), (SUBLANES, l2r.size))
    lami = jnp.broadcast_to(l2i.reshape(1, -1), (SUBLANES, l2i.size))
    return wb, cm, lamr, lami


def _s5(u, wb, cm, lamr, lami, d_skip, w_glu_b, b_glu):
    bsz, seq, sw = u.shape
    assert bsz == 4, "the scan packs 4 batch rows x 2 token parities into 8 sublanes"
    tm = S5_TILE
    ns = lamr.shape[1]
    perm, permt = _s5_perms(S5_SUBTILE)
    const = lambda a: pl.BlockSpec(a.shape, lambda i: (0,) * a.ndim)
    d_row = d_skip.reshape(1, sw)
    bg = b_glu.reshape(1, sw)
    return pl.pallas_call(
        functools.partial(_s5_kernel, tm=tm, sw=sw),
        grid=(seq // tm,),
        in_specs=[pl.BlockSpec((4, tm, sw), lambda i: (0, i, 0)),
                  const(perm), const(permt), const(wb), const(cm), const(lamr), const(lami),
                  const(d_row), const(w_glu_b), const(bg)],
        out_specs=pl.BlockSpec((4, tm, sw), lambda i: (0, i, 0)),
        out_shape=jax.ShapeDtypeStruct((bsz, seq, sw), F32),
        scratch_shapes=[pltpu.VMEM((S5_SLABS, 2, SUBLANES, ns // S5_SLABS), F32),
                        pltpu.VMEM((SUBLANES, sw), F32)],
        compiler_params=pltpu.CompilerParams(
            dimension_semantics=("arbitrary",),
            vmem_limit_bytes=VMEM_LIMIT_BYTES),
        name="s5",
    )(u, perm, permt, wb, cm, lamr, lami, d_row, w_glu_b, bg)


def _front_kernel(x_ref, mod_ref, g_ref, w_ref, tri_ref, u_ref, ga_ref, gb_ref, at_ref,
                  q_scr, k_scr, v_scr, c_scr, acc_scr, z_scr, w_scr, *, sw, aw, d):
    blk = ATTN_BLOCK
    top = ATTN_TOP_ROWS
    n_pairs = aw // LANES
    nh = 2 * n_pairs
    tm = x_ref.shape[1]
    nsub = tm // blk
    step = pl.program_id(1)
    tile0 = pl.multiple_of(step * tm, tm)
    mod = mod_ref[0]
    nt = (((1,), (1,)), ((), ()))
    even_head = (lax.broadcasted_iota(jnp.int32, (tm, aw), 1) // HEAD_DIM) % 2 == 0

    def qkv(hb):
        o = sw
        q = (_dot(hb, w_ref[:, o:o + aw]) * Q_SCALE).astype(BF16); o += aw
        q_scr[0] = jnp.where(even_head, q, jnp.zeros_like(q))
        q_scr[1] = jnp.where(even_head, jnp.zeros_like(q), q)
        k_scr[pl.ds(tile0, tm), :] = _dot(hb, w_ref[:, o:o + aw]).astype(BF16); o += aw
        v = _dot(hb, w_ref[:, o:o + aw]).astype(BF16)
        v_scr[0, pl.ds(tile0, tm), :] = jnp.where(even_head, v, jnp.zeros_like(v))
        v_scr[1, pl.ds(tile0, tm), :] = jnp.where(even_head, jnp.zeros_like(v), v)

    def region(units, fillers=()):
        starts = [pl.multiple_of(kb * blk, blk) for kb, _, _, _ in units]
        for u, (_, r0, r1, _) in enumerate(units):
            n = r1 - r0
            for p in range(n_pairs):
                ls = slice(p * LANES, (p + 1) * LANES)
                kblk = k_scr[pl.ds(starts[u], blk), ls]
                zz = lax.dot_general(
                    jnp.concatenate([q_scr[0, r0:r1, ls], q_scr[1, r0:r1, ls]], axis=0),
                    kblk, nt, preferred_element_type=F32)
                z_scr[u * nh + 2 * p, 0:n] = zz[:n]
                z_scr[u * nh + 2 * p + 1, 0:n] = zz[n:]
        fillers = list(fillers)
        every = -(-len(units) // (len(fillers) + 1))
        for u, (_, r0, r1, diag) in enumerate(units):
            n = r1 - r0
            if diag:
                row = lax.broadcasted_iota(jnp.int32, (n, blk), 0) + r0 % blk
                valid = lax.broadcasted_iota(jnp.int32, (n, blk), 1) < row
            zs, sps = [], []
            for h in range(nh):
                z = z_scr[u * nh + h, 0:n]
                sp = jnp.maximum(z, 0.0) + jnp.log2(1.0 + jnp.exp2(-jnp.abs(z)))
                if diag:
                    sp = jnp.where(valid, sp, 0.0)
                zs.append(z)
                sps.append(sp.astype(BF16))
            incl_all = _dot(jnp.concatenate(sps, axis=0), tri_ref[...])
            for h in range(nh):
                z = zs[h]
                incl = incl_all[h * n:(h + 1) * n]
                total = jnp.broadcast_to(incl[:, 0:1], (n, blk))
                if diag:
                    w = jnp.where(valid, jnp.exp2(z - incl), 0.0)
                    c_scr[h, r0:r1] = total
                else:
                    c = c_scr[h, r0:r1]
                    w = jnp.exp2(z - incl - c)
                    c_scr[h, r0:r1] = c + total
                w_scr[u * nh + h, 0:n] = w.astype(BF16)
            if fillers and (u + 1) % every == 0:
                fillers.pop(0)()
        for filler in fillers:
            filler()
        for r0, r1 in dict.fromkeys((r0, r1) for _, r0, r1, _ in units):
            us = [u for u, (_, a, b, _) in enumerate(units) if (a, b) == (r0, r1)]
            for p in range(n_pairs):
                ls = slice(p * LANES, (p + 1) * LANES)
                ww = jnp.concatenate([w_scr[u * nh + 2 * p + hh, 0:r1 - r0]
                                      for u in us for hh in range(2)], axis=1)
                vv = jnp.concatenate([v_scr[hh, pl.ds(starts[u], blk), ls]
                                      for u in us for hh in range(2)], axis=0)
                if any(units[u][3] for u in us):
                    acc_scr[p, r0:r1] = _dot(ww, vv)
                else:
                    acc_scr[p, r0:r1] = acc_scr[p, r0:r1] + _dot(ww, vv)

    def c_min(r0, r1):
        m = c_scr[0, r0:r1]
        for h in range(1, nh):
            m = jnp.minimum(m, c_scr[h, r0:r1])
        return jnp.min(m)

    def head_units(sub, qi, n_prev):
        base = sub * blk
        units = [(qi, base, base + blk, True)]
        if n_prev >= 1:
            units.append((qi - 1, base, base + blk, False))
        if n_prev >= 2:
            units.append((qi - 2, base, base + top, False))
        return units

    def tile(units):
        h = _rms(x_ref[0]) * g_ref[...]
        hb = (h * (1.0 + mod[1:2]) + mod[0:1]).astype(BF16)
        qkv(hb)
        o = sw + 3 * aw

        def chunk(ref, col, w0):
            def run():
                ref[0, :, col:col + FRONT_CHUNK] = _dot(hb, w_ref[:, w0 + col:w0 + col + FRONT_CHUNK])
            return run

        region(units, [chunk(ref, col, w0)
                       for ref, w0, width in ((ga_ref, o, d), (gb_ref, o + d, d), (u_ref, 0, sw))
                       for col in range(0, width, FRONT_CHUNK)])

    @pl.when(step >= 1)
    def _():
        tile([u for sub in range(nsub) for u in head_units(sub, nsub * step + sub, 2)])

    @pl.when(step == 0)
    def _():
        tile([u for sub in range(nsub) for u in head_units(sub, sub, min(sub, 2))])

    def sweep(first_kb, cmin, r0, r1):
        def more(carry):
            kb, cmin = carry
            return jnp.logical_and(kb >= 0, cmin < UNDERFLOW_LOG2)

        def body(carry):
            kb, _ = carry
            region([(kb, r0, r1, False)])
            return kb - 1, c_min(r0, r1)

        lax.while_loop(more, body, (first_kb, cmin))

    tails = []
    for sub in range(nsub):
        qi = nsub * step + sub
        base = sub * blk
        tails.append((jnp.where(qi >= 2, qi - 3, -1), base, base + top))
        tails.append((jnp.where(qi >= 2, qi - 2, -1), base + top, base + blk))
    cmins = [c_min(r0, r1) for _, r0, r1 in tails]
    for (first_kb, r0, r1), cmin in zip(tails, cmins):
        sweep(first_kb, cmin, r0, r1)
    for p in range(n_pairs):
        at_ref[0, :, p * LANES:(p + 1) * LANES] = acc_scr[p].astype(at_ref.dtype)


def _attn_tri():
    blk = ATTN_BLOCK
    m = np.arange(blk)[:, None]
    j = np.arange(blk)[None, :]
    return jnp.asarray((m >= j).astype(np.float32), BF16)


def _front(x, mod, norm_g, w_in_b, sw, aw):
    bsz, seq, d = x.shape
    tm = FRONT_TILE
    blk = ATTN_BLOCK
    n = w_in_b.shape[1]
    n_pairs = aw // LANES
    nz = ATTN_REGION * (tm // blk) * 2 * n_pairs
    tri = _attn_tri()
    tok = lambda w: pl.BlockSpec((1, tm, w), lambda b, i: (b, i, 0))
    const = lambda a: pl.BlockSpec(a.shape, lambda b, i: (0,) * a.ndim,
                                   pipeline_mode=pl.Buffered(1))
    return pl.pallas_call(
        functools.partial(_front_kernel, sw=sw, aw=aw, d=d),
        grid=(bsz, seq // tm),
        in_specs=[tok(d),
                  pl.BlockSpec((1, N_ADA, d), lambda b, i: (b, 0, 0)),
                  const(norm_g), const(w_in_b), const(tri)],
        out_specs=[tok(sw), tok(d), tok(d), tok(aw)],
        out_shape=[jax.ShapeDtypeStruct((bsz, seq, sw), F32),
                   jax.ShapeDtypeStruct((bsz, seq, d), F32),
                   jax.ShapeDtypeStruct((bsz, seq, d), F32),
                   jax.ShapeDtypeStruct((bsz, seq, aw), BF16)],
        scratch_shapes=[pltpu.VMEM((2, tm, aw), BF16),
                        pltpu.VMEM((seq, aw), BF16),
                        pltpu.VMEM((2, seq, aw), BF16),
                        pltpu.VMEM((2 * n_pairs, tm, blk), F32),
                        pltpu.VMEM((n_pairs, tm, LANES), F32),
                        pltpu.VMEM((nz, blk, blk), F32),
                        pltpu.VMEM((nz, blk, blk), BF16)],
        compiler_params=pltpu.CompilerParams(
            dimension_semantics=("arbitrary", "arbitrary"),
            vmem_limit_bytes=VMEM_LIMIT_BYTES),
        name="front",
    )(x, mod, norm_g, w_in_b, tri)


def _out_ffn_kernel(x_ref, s5_ref, at_ref, ga_ref, gb_ref, mod_ref, n2_ref, nf_ref,
                    wa_ref, wb_ref, wo_ref, wg_ref, wu_ref, wd_ref, o_ref, *, final_norm):
    mod = mod_ref[0]
    tm = x_ref.shape[1]
    groups = [slice(r, r + tm // OUT_GROUPS) for r in range(0, tm, tm // OUT_GROUPS)]
    ms = []
    for g in groups:
        ya = _dot(s5_ref[0, g].astype(BF16), wa_ref[...])
        yb = _dot(at_ref[0, g], wb_ref[...])
        m = jax.nn.sigmoid(ga_ref[0, g]) * ya + jax.nn.sigmoid(gb_ref[0, g]) * yb
        ms.append(m.astype(BF16))
    x1s, hs = [], []
    for g, m in zip(groups, ms):
        x1 = x_ref[0, g] + mod[2:3] * _dot(m, wo_ref[...])
        h = _rms(x1) * n2_ref[...]
        x1s.append(x1)
        hs.append((h * (1.0 + mod[4:5]) + mod[3:4]).astype(BF16))
    acts = []
    for h in hs:
        gate = _dot(h, wg_ref[...])
        up = _dot(h, wu_ref[...])
        acts.append((gate * jax.nn.sigmoid(gate) * up).astype(BF16))
    for g, x1, act in zip(groups, x1s, acts):
        x2 = x1 + mod[5:6] * _dot(act, wd_ref[...])
        o_ref[0, g] = _rms(x2) * nf_ref[...] if final_norm else x2


def _out_ffn(x, s5o, attn, ga, gb, mod, n2g, nfg, wa, wb, wo, wg, wu, wd, final_norm):
    bsz, seq, d = x.shape
    tm = OUT_TILE
    tok = lambda a: pl.BlockSpec((1, tm, a.shape[-1]), lambda b, i: (b, i, 0))
    const = lambda a: pl.BlockSpec(a.shape, lambda b, i: (0,) * a.ndim,
                                   pipeline_mode=pl.Buffered(1))
    return pl.pallas_call(
        functools.partial(_out_ffn_kernel, final_norm=final_norm),
        grid=(bsz, seq // tm),
        in_specs=[tok(x), tok(s5o), tok(attn), tok(ga), tok(gb),
                  pl.BlockSpec((1, N_ADA, d), lambda b, i: (b, 0, 0)),
                  const(n2g), const(nfg),
                  const(wa), const(wb), const(wo), const(wg), const(wu), const(wd)],
        out_specs=pl.BlockSpec((1, tm, d), lambda b, i: (b, i, 0)),
        out_shape=jax.ShapeDtypeStruct((bsz, seq, d), F32),
        compiler_params=pltpu.CompilerParams(
            dimension_semantics=("arbitrary", "arbitrary"),
            vmem_limit_bytes=VMEM_LIMIT_BYTES),
        name="out_ffn",
    )(x, s5o, attn, ga, gb, mod, n2g, nfg, wa, wb, wo, wg, wu, wd)


def kernel(x, c, w_ada, b_ada, norm1_g, w_in, lam_re, lam_im, log_dt, b_re, b_im, c_re, c_im,
           d_skip, w_glu, b_glu, w_a, w_b, w_o, norm2_g, w_ffn_gate, w_ffn_up, w_ffn_down,
           norm_f_g):
    depth = w_ada.shape[0]
    bsz, seq, d = x.shape
    sw = w_glu.shape[1]
    aw = w_b.shape[1]
    for l in range(depth):
        mod = _ada(c, w_ada[l], b_ada[l]).reshape(bsz, N_ADA, d)
        u, ga, gb, attn = _front(x, mod, norm1_g[l].reshape(1, d), w_in[l].astype(BF16), sw, aw)
        wb, cm, lamr, lami = _s5_weights(lam_re[l], lam_im[l], log_dt[l], b_re[l], b_im[l],
                                         c_re[l], c_im[l])
        s5o = _s5(u, wb, cm, lamr, lami, d_skip[l], w_glu[l].astype(BF16), b_glu[l])
        x = _out_ffn(x, s5o, attn, ga, gb, mod, norm2_g[l].reshape(1, d), norm_f_g.reshape(1, d),
                     w_a[l].astype(BF16), w_b[l].astype(BF16), w_o[l].astype(BF16),
                     w_ffn_gate[l].astype(BF16), w_ffn_up[l].astype(BF16),
                     w_ffn_down[l].astype(BF16), final_norm=(l == depth - 1))
    return x
```

```python
import functools
import math

import numpy as np
import jax
import jax.numpy as jnp
from jax import lax
from jax.experimental import pallas as pl
from jax.experimental.pallas import tpu as pltpu

F32 = jnp.float32
BF16 = jnp.bfloat16

S5_GROUP = 16
S5_STATE = 64
HEAD_DIM = 64
N_ADA = 6
RMS_EPS = 1e-6
Q_SCALE = math.log2(math.e) / math.sqrt(HEAD_DIM)
UNDERFLOW_LOG2 = 151.0

LANES = 128
SUBLANES = 8
VMEM_LIMIT_BYTES = 56 * 1024 * 1024

ATTN_BLOCK = 128
ATTN_REGION = 3
ATTN_TOP_ROWS = 32
S5_TILE = 512
S5_SUBTILE = 128
S5_SLABS = 4
FRONT_TILE = 512
FRONT_CHUNK = 256
OUT_TILE = 512
OUT_GROUPS = 2


def _dot(a, b):
    return jnp.dot(a, b, preferred_element_type=F32)


def _rms(x):
    return x * lax.rsqrt(jnp.mean(x * x, axis=-1, keepdims=True) + RMS_EPS)


def _ada_kernel(c_ref, w_ref, b_ref, o_ref):
    c = c_ref[...]
    bsz = c.shape[0]
    cond = c * jax.nn.sigmoid(c)
    pad = -bsz % SUBLANES
    if pad:
        cond = jnp.concatenate([cond, jnp.zeros((pad, c.shape[1]), F32)], axis=0)
    o_ref[...] = _dot(cond.astype(BF16), w_ref[...].astype(BF16))[:bsz] + b_ref[...]


def _ada(c, w_ada, b_ada):
    bsz, d = c.shape
    n = w_ada.shape[1]
    tn = 1536
    return pl.pallas_call(
        _ada_kernel,
        grid=(n // tn,),
        in_specs=[pl.BlockSpec((bsz, d), lambda j: (0, 0)),
                  pl.BlockSpec((d, tn), lambda j: (0, j)),
                  pl.BlockSpec((1, tn), lambda j: (0, j))],
        out_specs=pl.BlockSpec((bsz, tn), lambda j: (0, j)),
        out_shape=jax.ShapeDtypeStruct((bsz, n), F32),
        name="ada",
    )(c, w_ada, b_ada.reshape(1, n))


def _s5_kernel(u_ref, perm_ref, permt_ref, wb_ref, cm_ref, lamr_ref, lami_ref, d_ref,
               wglu_ref, bglu_ref, o_ref, x_scr, ulast_scr, *, tm, sw):
    i = pl.program_id(0)
    ts = S5_SUBTILE
    n2 = ts // 2
    rows = SUBLANES * n2
    cw = sw // S5_SLABS
    hs = cw * S5_STATE // S5_GROUP

    @pl.when(i == 0)
    def _():
        x_scr[...] = jnp.zeros_like(x_scr)
        ulast_scr[...] = jnp.zeros_like(ulast_scr)

    sub8 = lax.broadcasted_iota(jnp.int32, (SUBLANES, sw), 0)
    odd = (lax.broadcasted_iota(jnp.int32, (rows, sw), 0) & 1) == 1

    def natural(j):
        return jnp.concatenate([u_ref[b, j * ts:(j + 1) * ts, :] for b in range(4)], axis=0)

    def last_rows(j):
        last = jnp.zeros((SUBLANES, sw), F32)
        for b in range(4):
            row = u_ref[b, (j + 1) * ts - 1:(j + 1) * ts, :].astype(BF16).astype(F32)
            last = jnp.where(sub8 == 2 * b, jnp.broadcast_to(row, (SUBLANES, sw)), last)
        return last

    def input_stage(j, before):
        a_cur = _dot(perm_ref[...], natural(j).astype(BF16))
        a_prev = jnp.where(odd, pltpu.roll(a_cur, 1, 0), pltpu.roll(a_cur, SUBLANES - 1, 0))
        first = jnp.where((sub8 & 1) == 1, a_prev[:SUBLANES], before)
        a_prev = jnp.concatenate([first, a_prev[SUBLANES:]], axis=0).astype(BF16)
        a_cur = a_cur.astype(BF16)
        return [_dot(jnp.concatenate([a_cur[:, s * cw:(s + 1) * cw],
                                      a_prev[:, s * cw:(s + 1) * cw]], axis=1), wb_ref[s])
                for s in range(S5_SLABS)]

    def scan_stage(bus, x):
        states, x_out = [], []
        for s in range(S5_SLABS):
            ar = lamr_ref[:, hs * s:hs * (s + 1)]
            ai = lami_ref[:, hs * s:hs * (s + 1)]
            xr, xi = x[s]
            st = []
            for t2 in range(n2):
                rs = slice(SUBLANES * t2, SUBLANES * (t2 + 1))
                xr, xi = (ar * xr - ai * xi + bus[s][rs, :hs],
                          ar * xi + ai * xr + bus[s][rs, hs:])
                st.append(jnp.concatenate([xr, xi], axis=1))
            x_out.append((xr, xi))
            states.append(jnp.concatenate(st, axis=0).astype(BF16))
        return states, x_out

    def output_stage(j, states):
        y_il = jnp.concatenate([_dot(states[s], cm_ref[s]) for s in range(S5_SLABS)], axis=1)
        y = _dot(permt_ref[...], y_il.astype(BF16))
        y = y + d_ref[...] * natural(j)
        y = jax.nn.gelu(y)
        z = _dot(y.astype(BF16), wglu_ref[...]) + bglu_ref[...]
        out = y * jax.nn.sigmoid(z)
        for b in range(4):
            o_ref[b, j * ts:(j + 1) * ts, :] = out[b * ts:(b + 1) * ts]

    nsub = tm // ts
    befores = [ulast_scr[...]] + [last_rows(j) for j in range(nsub - 1)]
    bus = [input_stage(j, befores[j]) for j in range(nsub)]
    ulast_scr[...] = last_rows(nsub - 1)
    x = [(x_scr[s, 0], x_scr[s, 1]) for s in range(S5_SLABS)]
    for j in range(nsub):
        states, x = scan_stage(bus[j], x)
        output_stage(j, states)
    for s in range(S5_SLABS):
        x_scr[s, 0], x_scr[s, 1] = x[s]


def _s5_perms(tm):
    n2 = tm // 2
    rows = SUBLANES * n2
    perm = np.zeros((rows, 4 * tm), np.float32)
    permt = np.zeros((4 * tm, rows), np.float32)
    for t2 in range(n2):
        for b in range(4):
            for par in range(2):
                r = SUBLANES * t2 + 2 * b + par
                t = 2 * t2 + par
                perm[r, b * tm + t] = 1.0
                permt[b * tm + t, r] = 1.0
    return jnp.asarray(perm, BF16), jnp.asarray(permt, BF16)


def _block_diag(blocks):
    n, a, b = blocks.shape
    eye = jnp.eye(n, dtype=blocks.dtype)
    return (blocks[:, :, None, :] * eye[:, None, :, None]).reshape(n * a, n * b)


def _s5_weights(lam_re, lam_im, log_dt, b_re, b_im, c_re, c_im):
    g = lam_re.shape[0]
    dt = jnp.exp(log_dt)[:, None]
    mag = jnp.exp(lam_re * dt)
    lbr = mag * jnp.cos(lam_im * dt)
    lbi = mag * jnp.sin(lam_im * dt)
    nr, ni = lbr - 1.0, lbi
    den = lam_re * lam_re + lam_im * lam_im
    cr = (nr * lam_re + ni * lam_im) / den
    ci = (ni * lam_re - nr * lam_im) / den
    bbr = cr[..., None] * b_re - ci[..., None] * b_im
    bbi = cr[..., None] * b_im + ci[..., None] * b_re
    lr = lbr[..., None] * bbr - lbi[..., None] * bbi
    li = lbr[..., None] * bbi + lbi[..., None] * bbr
    l2r = lbr * lbr - lbi * lbi
    l2i = 2.0 * lbr * lbi
    gh = g // S5_SLABS
    wbs, cms = [], []
    for h in range(S5_SLABS):
        s = slice(h * gh, (h + 1) * gh)
        t = lambda a: jnp.swapaxes(a[s], 1, 2)
        top = jnp.concatenate([_block_diag(t(bbr)), _block_diag(t(bbi))], axis=1)
        bot = jnp.concatenate([_block_diag(t(lr)), _block_diag(t(li))], axis=1)
        wbs.append(jnp.concatenate([top, bot], axis=0))
        cms.append(jnp.concatenate([_block_diag(jnp.swapaxes(c_re[s], 1, 2)),
                                    -_block_diag(jnp.swapaxes(c_im[s], 1, 2))], axis=0))
    wb = jnp.stack(wbs).astype(BF16)
    cm = jnp.stack(cms).astype(BF16)
    lamr = jnp.broadcast_to(l2r.reshape(1, -1), (SUBLANES, l2r.size))
    lami = jnp.broadcast_to(l2i.reshape(1, -1), (SUBLANES, l2i.size))
    return wb, cm, lamr, lami


def _s5(u, wb, cm, lamr, lami, d_skip, w_glu_b, b_glu):
    bsz, seq, sw = u.shape
    assert bsz == 4, "the scan packs 4 batch rows x 2 token parities into 8 sublanes"
    tm = S5_TILE
    ns = lamr.shape[1]
    perm, permt = _s5_perms(S5_SUBTILE)
    const = lambda a: pl.BlockSpec(a.shape, lambda i: (0,) * a.ndim)
    d_row = d_skip.reshape(1, sw)
    bg = b_glu.reshape(1, sw)
    return pl.pallas_call(
        functools.partial(_s5_kernel, tm=tm, sw=sw),
        grid=(seq // tm,),
        in_specs=[pl.BlockSpec((4, tm, sw), lambda i: (0, i, 0)),
                  const(perm), const(permt), const(wb), const(cm), const(lamr), const(lami),
                  const(d_row), const(w_glu_b), const(bg)],
        out_specs=pl.BlockSpec((4, tm, sw), lambda i: (0, i, 0)),
        out_shape=jax.ShapeDtypeStruct((bsz, seq, sw), F32),
        scratch_shapes=[pltpu.VMEM((S5_SLABS, 2, SUBLANES, ns // S5_SLABS), F32),
                        pltpu.VMEM((SUBLANES, sw), F32)],
        compiler_params=pltpu.CompilerParams(
            dimension_semantics=("arbitrary",),
            vmem_limit_bytes=VMEM_LIMIT_BYTES),
        name="s5",
    )(u, perm, permt, wb, cm, lamr, lami, d_row, w_glu_b, bg)


def _front_kernel(x_ref, mod_ref, g_ref, w_ref, tri_ref, u_ref, ga_ref, gb_ref, at_ref,
                  q_scr, k_scr, v_scr, c_scr, acc_scr, z_scr, w_scr, *, sw, aw, d):
    blk = ATTN_BLOCK
    top = ATTN_TOP_ROWS
    n_pairs = aw // LANES
    nh = 2 * n_pairs
    tm = x_ref.shape[1]
    nsub = tm // blk
    step = pl.program_id(1)
    tile0 = pl.multiple_of(step * tm, tm)
    mod = mod_ref[0]
    nt = (((1,), (1,)), ((), ()))
    even_head = (lax.broadcasted_iota(jnp.int32, (tm, aw), 1) // HEAD_DIM) % 2 == 0

    def qkv(hb):
        o = sw
        q = (_dot(hb, w_ref[:, o:o + aw]) * Q_SCALE).astype(BF16); o += aw
        q_scr[0] = jnp.where(even_head, q, jnp.zeros_like(q))
        q_scr[1] = jnp.where(even_head, jnp.zeros_like(q), q)
        k_scr[pl.ds(tile0, tm), :] = _dot(hb, w_ref[:, o:o + aw]).astype(BF16); o += aw
        v = _dot(hb, w_ref[:, o:o + aw]).astype(BF16)
        v_scr[0, pl.ds(tile0, tm), :] = jnp.where(even_head, v, jnp.zeros_like(v))
        v_scr[1, pl.ds(tile0, tm), :] = jnp.where(even_head, jnp.zeros_like(v), v)

    def region(units, fillers=()):
        starts = [pl.multiple_of(kb * blk, blk) for kb, _, _, _ in units]
        for u, (_, r0, r1, _) in enumerate(units):
            n = r1 - r0
            for p in range(n_pairs):
                ls = slice(p * LANES, (p + 1) * LANES)
                kblk = k_scr[pl.ds(starts[u], blk), ls]
                zz = lax.dot_general(
                    jnp.concatenate([q_scr[0, r0:r1, ls], q_scr[1, r0:r1, ls]], axis=0),
                    kblk, nt, preferred_element_type=F32)
                z_scr[u * nh + 2 * p, 0:n] = zz[:n]
                z_scr[u * nh + 2 * p + 1, 0:n] = zz[n:]
        fillers = list(fillers)
        every = -(-len(units) // (len(fillers) + 1))
        for u, (_, r0, r1, diag) in enumerate(units):
            n = r1 - r0
            if diag:
                row = lax.broadcasted_iota(jnp.int32, (n, blk), 0) + r0 % blk
                valid = lax.broadcasted_iota(jnp.int32, (n, blk), 1) < row
            zs, sps = [], []
            for h in range(nh):
                z = z_scr[u * nh + h, 0:n]
                sp = jnp.maximum(z, 0.0) + jnp.log2(1.0 + jnp.exp2(-jnp.abs(z)))
                if diag:
                    sp = jnp.where(valid, sp, 0.0)
                zs.append(z)
                sps.append(sp.astype(BF16))
            incl_all = _dot(jnp.concatenate(sps, axis=0), tri_ref[...])
            for h in range(nh):
                z = zs[h]
                incl = incl_all[h * n:(h + 1) * n]
                total = jnp.broadcast_to(incl[:, 0:1], (n, blk))
                if diag:
                    w = jnp.where(valid, jnp.exp2(z - incl), 0.0)
                    c_scr[h, r0:r1] = total
                else:
                    c = c_scr[h, r0:r1]
                    w = jnp.exp2(z - incl - c)
                    c_scr[h, r0:r1] = c + total
                w_scr[u * nh + h, 0:n] = w.astype(BF16)
            if fillers and (u + 1) % every == 0:
                fillers.pop(0)()
        for filler in fillers:
            filler()
        for r0, r1 in dict.fromkeys((r0, r1) for _, r0, r1, _ in units):
            us = [u for u, (_, a, b, _) in enumerate(units) if (a, b) == (r0, r1)]
            for p in range(n_pairs):
                ls = slice(p * LANES, (p + 1) * LANES)
                ww = jnp.concatenate([w_scr[u * nh + 2 * p + hh, 0:r1 - r0]
                                      for u in us for hh in range(2)], axis=1)
                vv = jnp.concatenate([v_scr[hh, pl.ds(starts[u], blk), ls]
                                      for u in us for hh in range(2)], axis=0)
                if any(units[u][3] for u in us):
                    acc_scr[p, r0:r1] = _dot(ww, vv)
                else:
                    acc_scr[p, r0:r1] = acc_scr[p, r0:r1] + _dot(ww, vv)

    def c_min(r0, r1):
        m = c_scr[0, r0:r1]
        for h in range(1, nh):
            m = jnp.minimum(m, c_scr[h, r0:r1])
        return jnp.min(m)

    def head_units(sub, qi, n_prev):
        base = sub * blk
        units = [(qi, base, base + blk, True)]
        if n_prev >= 1:
            units.append((qi - 1, base, base + blk, False))
        if n_prev >= 2:
            units.append((qi - 2, base, base + top, False))
        return units

    def tile(units):
        h = _rms(x_ref[0]) * g_ref[...]
        hb = (h * (1.0 + mod[1:2]) + mod[0:1]).astype(BF16)
        qkv(hb)
        o = sw + 3 * aw

        def chunk(ref, col, w0):
            def run():
                ref[0, :, col:col + FRONT_CHUNK] = _dot(hb, w_ref[:, w0 + col:w0 + col + FRONT_CHUNK])
            return run

        region(units, [chunk(ref, col, w0)
                       for ref, w0, width in ((ga_ref, o, d), (gb_ref, o + d, d), (u_ref, 0, sw))
                       for col in range(0, width, FRONT_CHUNK)])

    @pl.when(step >= 1)
    def _():
        tile([u for sub in range(nsub) for u in head_units(sub, nsub * step + sub, 2)])

    @pl.when(step == 0)
    def _():
        tile([u for sub in range(nsub) for u in head_units(sub, sub, min(sub, 2))])

    def sweep(first_kb, cmin, r0, r1):
        def more(carry):
            kb, cmin = carry
            return jnp.logical_and(kb >= 0, cmin < UNDERFLOW_LOG2)

        def body(carry):
            kb, _ = carry
            region([(kb, r0, r1, False)])
            return kb - 1, c_min(r0, r1)

        lax.while_loop(more, body, (first_kb, cmin))

    tails = []
    for sub in range(nsub):
        qi = nsub * step + sub
        base = sub * blk
        tails.append((jnp.where(qi >= 2, qi - 3, -1), base, base + top))
        tails.append((jnp.where(qi >= 2, qi - 2, -1), base + top, base + blk))
    cmins = [c_min(r0, r1) for _, r0, r1 in tails]
    for (first_kb, r0, r1), cmin in zip(tails, cmins):
        sweep(first_kb, cmin, r0, r1)
    for p in range(n_pairs):
        at_ref[0, :, p * LANES:(p + 1) * LANES] = acc_scr[p].astype(at_ref.dtype)


def _attn_tri():
    blk = ATTN_BLOCK
    m = np.arange(blk)[:, None]
    j = np.arange(blk)[None, :]
    return jnp.asarray((m >= j).astype(np.float32), BF16)


def _front(x, mod, norm_g, w_in_b, sw, aw):
    bsz, seq, d = x.shape
    tm = FRONT_TILE
    blk = ATTN_BLOCK
    n = w_in_b.shape[1]
    n_pairs = aw // LANES
    nz = ATTN_REGION * (tm // blk) * 2 * n_pairs
    tri = _attn_tri()
    tok = lambda w: pl.BlockSpec((1, tm, w), lambda b, i: (b, i, 0))
    const = lambda a: pl.BlockSpec(a.shape, lambda b, i: (0,) * a.ndim,
                                   pipeline_mode=pl.Buffered(1))
    return pl.pallas_call(
        functools.partial(_front_kernel, sw=sw, aw=aw, d=d),
        grid=(bsz, seq // tm),
        in_specs=[tok(d),
                  pl.BlockSpec((1, N_ADA, d), lambda b, i: (b, 0, 0)),
                  const(norm_g), const(w_in_b), const(tri)],
        out_specs=[tok(sw), tok(d), tok(d), tok(aw)],
        out_shape=[jax.ShapeDtypeStruct((bsz, seq, sw), F32),
                   jax.ShapeDtypeStruct((bsz, seq, d), F32),
                   jax.ShapeDtypeStruct((bsz, seq, d), F32),
                   jax.ShapeDtypeStruct((bsz, seq, aw), BF16)],
        scratch_shapes=[pltpu.VMEM((2, tm, aw), BF16),
                        pltpu.VMEM((seq, aw), BF16),
                        pltpu.VMEM((2, seq, aw), BF16),
                        pltpu.VMEM((2 * n_pairs, tm, blk), F32),
                        pltpu.VMEM((n_pairs, tm, LANES), F32),
                        pltpu.VMEM((nz, blk, blk), F32),
                        pltpu.VMEM((nz, blk, blk), BF16)],
        compiler_params=pltpu.CompilerParams(
            dimension_semantics=("arbitrary", "arbitrary"),
            vmem_limit_bytes=VMEM_LIMIT_BYTES),
        name="front",
    )(x, mod, norm_g, w_in_b, tri)


def _out_ffn_kernel(x_ref, s5_ref, at_ref, ga_ref, gb_ref, mod_ref, n2_ref, nf_ref,
                    wa_ref, wb_ref, wo_ref, wg_ref, wu_ref, wd_ref, o_ref, *, final_norm):
    mod = mod_ref[0]
    tm = x_ref.shape[1]
    groups = [slice(r, r + tm // OUT_GROUPS) for r in range(0, tm, tm // OUT_GROUPS)]
    ms = []
    for g in groups:
        ya = _dot(s5_ref[0, g].astype(BF16), wa_ref[...])
        yb = _dot(at_ref[0, g], wb_ref[...])
        m = jax.nn.sigmoid(ga_ref[0, g]) * ya + jax.nn.sigmoid(gb_ref[0, g]) * yb
        ms.append(m.astype(BF16))
    x1s, hs = [], []
    for g, m in zip(groups, ms):
        x1 = x_ref[0, g] + mod[2:3] * _dot(m, wo_ref[...])
        h = _rms(x1) * n2_ref[...]
        x1s.append(x1)
        hs.append((h * (1.0 + mod[4:5]) + mod[3:4]).astype(BF16))
    acts = []
    for h in hs:
        gate = _dot(h, wg_ref[...])
        up = _dot(h, wu_ref[...])
        acts.append((gate * jax.nn.sigmoid(gate) * up).astype(BF16))
    for g, x1, act in zip(groups, x1s, acts):
        x2 = x1 + mod[5:6] * _dot(act, wd_ref[...])
        o_ref[0, g] = _rms(x2) * nf_ref[...] if final_norm else x2


def _out_ffn(x, s5o, attn, ga, gb, mod, n2g, nfg, wa, wb, wo, wg, wu, wd, final_norm):
    bsz, seq, d = x.shape
    tm = OUT_TILE
    tok = lambda a: pl.BlockSpec((1, tm, a.shape[-1]), lambda b, i: (b, i, 0))
    const = lambda a: pl.BlockSpec(a.shape, lambda b, i: (0,) * a.ndim,
                                   pipeline_mode=pl.Buffered(1))
    return pl.pallas_call(
        functools.partial(_out_ffn_kernel, final_norm=final_norm),
        grid=(bsz, seq // tm),
        in_specs=[tok(x), tok(s5o), tok(attn), tok(ga), tok(gb),
                  pl.BlockSpec((1, N_ADA, d), lambda b, i: (b, 0, 0)),
                  const(n2g), const(nfg),
                  const(wa), const(wb), const(wo), const(wg), const(wu), const(wd)],
        out_specs=pl.BlockSpec((1, tm, d), lambda b, i: (b, i, 0)),
        out_shape=jax.ShapeDtypeStruct((bsz, seq, d), F32),
        compiler_params=pltpu.CompilerParams(
            dimension_semantics=("arbitrary", "arbitrary"),
            vmem_limit_bytes=VMEM_LIMIT_BYTES),
        name="out_ffn",
    )(x, s5o, attn, ga, gb, mod, n2g, nfg, wa, wb, wo, wg, wu, wd)


def kernel(x, c, w_ada, b_ada, norm1_g, w_in, lam_re, lam_im, log_dt, b_re, b_im, c_re, c_im,
           d_skip, w_glu, b_glu, w_a, w_b, w_o, norm2_g, w_ffn_gate, w_ffn_up, w_ffn_down,
           norm_f_g):
    depth = w_ada.shape[0]
    bsz, seq, d = x.shape
    sw = w_glu.shape[1]
    aw = w_b.shape[1]
    for l in range(depth):
        mod = _ada(c, w_ada[l], b_ada[l]).reshape(bsz, N_ADA, d)
        u, ga, gb, attn = _front(x, mod, norm1_g[l].reshape(1, d), w_in[l].astype(BF16), sw, aw)
        wb, cm, lamr, lami = _s5_weights(lam_re[l], lam_im[l], log_dt[l], b_re[l], b_im[l],
                                         c_re[l], c_im[l])
        s5o = _s5(u, wb, cm, lamr, lami, d_skip[l], w_glu[l].astype(BF16), b_glu[l])
        x = _out_ffn(x, s5o, attn, ga, gb, mod, norm2_g[l].reshape(1, d), norm_f_g.reshape(1, d),
                     w_a[l].astype(BF16), w_b[l].astype(BF16), w_o[l].astype(BF16),
                     w_ffn_gate[l].astype(BF16), w_ffn_up[l].astype(BF16),
                     w_ffn_down[l].astype(BF16), final_norm=(l == depth - 1))
    return x
```

```python
import functools
import math

import numpy as np
import jax
import jax.numpy as jnp
from jax import lax
from jax.experimental import pallas as pl
from jax.experimental.pallas import tpu as pltpu

F32 = jnp.float32
BF16 = jnp.bfloat16

S5_GROUP = 16
S5_STATE = 64
HEAD_DIM = 64
N_ADA = 6
RMS_EPS = 1e-6
Q_SCALE = math.log2(math.e) / math.sqrt(HEAD_DIM)
UNDERFLOW_LOG2 = 151.0

LANES = 128
SUBLANES = 8
VMEM_LIMIT_BYTES = 56 * 1024 * 1024

ATTN_BLOCK = 128
ATTN_REGION = 3
ATTN_TOP_ROWS = 32
S5_TILE = 512
S5_SUBTILE = 128
S5_SLABS = 4
FRONT_TILE = 512
FRONT_CHUNK = 256
OUT_TILE = 512
OUT_GROUPS = 2


def _dot(a, b):
    return jnp.dot(a, b, preferred_element_type=F32)


def _rms(x):
    return x * lax.rsqrt(jnp.mean(x * x, axis=-1, keepdims=True) + RMS_EPS)


def _ada_kernel(c_ref, w_ref, b_ref, o_ref):
    c = c_ref[...]
    bsz = c.shape[0]
    cond = c * jax.nn.sigmoid(c)
    pad = -bsz % SUBLANES
    if pad:
        cond = jnp.concatenate([cond, jnp.zeros((pad, c.shape[1]), F32)], axis=0)
    o_ref[...] = _dot(cond.astype(BF16), w_ref[...].astype(BF16))[:bsz] + b_ref[...]


def _ada(c, w_ada, b_ada):
    bsz, d = c.shape
    n = w_ada.shape[1]
    tn = 1536
    return pl.pallas_call(
        _ada_kernel,
        grid=(n // tn,),
        in_specs=[pl.BlockSpec((bsz, d), lambda j: (0, 0)),
                  pl.BlockSpec((d, tn), lambda j: (0, j)),
                  pl.BlockSpec((1, tn), lambda j: (0, j))],
        out_specs=pl.BlockSpec((bsz, tn), lambda j: (0, j)),
        out_shape=jax.ShapeDtypeStruct((bsz, n), F32),
        name="ada",
    )(c, w_ada, b_ada.reshape(1, n))


def _s5_kernel(u_ref, perm_ref, permt_ref, wb_ref, cm_ref, lamr_ref, lami_ref, d_ref,
               wglu_ref, bglu_ref, o_ref, x_scr, ulast_scr, *, tm, sw):
    i = pl.program_id(0)
    ts = S5_SUBTILE
    n2 = ts // 2
    rows = SUBLANES * n2
    cw = sw // S5_SLABS
    hs = cw * S5_STATE // S5_GROUP

    @pl.when(i == 0)
    def _():
        x_scr[...] = jnp.zeros_like(x_scr)
        ulast_scr[...] = jnp.zeros_like(ulast_scr)

    sub8 = lax.broadcasted_iota(jnp.int32, (SUBLANES, sw), 0)
    odd = (lax.broadcasted_iota(jnp.int32, (rows, sw), 0) & 1) == 1

    def natural(j):
        return jnp.concatenate([u_ref[b, j * ts:(j + 1) * ts, :] for b in range(4)], axis=0)

    def last_rows(j):
        last = jnp.zeros((SUBLANES, sw), F32)
        for b in range(4):
            row = u_ref[b, (j + 1) * ts - 1:(j + 1) * ts, :].astype(BF16).astype(F32)
            last = jnp.where(sub8 == 2 * b, jnp.broadcast_to(row, (SUBLANES, sw)), last)
        return last

    def input_stage(j, before):
        a_cur = _dot(perm_ref[...], natural(j).astype(BF16))
        a_prev = jnp.where(odd, pltpu.roll(a_cur, 1, 0), pltpu.roll(a_cur, SUBLANES - 1, 0))
        first = jnp.where((sub8 & 1) == 1, a_prev[:SUBLANES], before)
        a_prev = jnp.concatenate([first, a_prev[SUBLANES:]], axis=0).astype(BF16)
        a_cur = a_cur.astype(BF16)
        return [_dot(jnp.concatenate([a_cur[:, s * cw:(s + 1) * cw],
                                      a_prev[:, s * cw:(s + 1) * cw]], axis=1), wb_ref[s])
                for s in range(S5_SLABS)]

    def scan_stage(bus, x):
        states, x_out = [], []
        for s in range(S5_SLABS):
            ar = lamr_ref[:, hs * s:hs * (s + 1)]
            ai = lami_ref[:, hs * s:hs * (s + 1)]
            xr, xi = x[s]
            st = []
            for t2 in range(n2):
                rs = slice(SUBLANES * t2, SUBLANES * (t2 + 1))
                xr, xi = (ar * xr - ai * xi + bus[s][rs, :hs],
                          ar * xi + ai * xr + bus[s][rs, hs:])
                st.append(jnp.concatenate([xr, xi], axis=1))
            x_out.append((xr, xi))
            states.append(jnp.concatenate(st, axis=0).astype(BF16))
        return states, x_out

    def output_stage(j, states):
        y_il = jnp.concatenate([_dot(states[s], cm_ref[s]) for s in range(S5_SLABS)], axis=1)
        y = _dot(permt_ref[...], y_il.astype(BF16))
        y = y + d_ref[...] * natural(j)
        y = jax.nn.gelu(y)
        z = _dot(y.astype(BF16), wglu_ref[...]) + bglu_ref[...]
        out = y * jax.nn.sigmoid(z)
        for b in range(4):
            o_ref[b, j * ts:(j + 1) * ts, :] = out[b * ts:(b + 1) * ts]

    nsub = tm // ts
    befores = [ulast_scr[...]] + [last_rows(j) for j in range(nsub - 1)]
    bus = [input_stage(j, befores[j]) for j in range(nsub)]
    ulast_scr[...] = last_rows(nsub - 1)
    x = [(x_scr[s, 0], x_scr[s, 1]) for s in range(S5_SLABS)]
    for j in range(nsub):
        states, x = scan_stage(bus[j], x)
        output_stage(j, states)
    for s in range(S5_SLABS):
        x_scr[s, 0], x_scr[s, 1] = x[s]


def _s5_perms(tm):
    n2 = tm // 2
    rows = SUBLANES * n2
    perm = np.zeros((rows, 4 * tm), np.float32)
    permt = np.zeros((4 * tm, rows), np.float32)
    for t2 in range(n2):
        for b in range(4):
            for par in range(2):
                r = SUBLANES * t2 + 2 * b + par
                t = 2 * t2 + par
                perm[r, b * tm + t] = 1.0
                permt[b * tm + t, r] = 1.0
    return jnp.asarray(perm, BF16), jnp.asarray(permt, BF16)


def _slab_block_diag(blocks):
    g, a, b = blocks.shape
    n = g // S5_SLABS
    eye = jnp.eye(n, dtype=blocks.dtype)
    placed = blocks.reshape(S5_SLABS, n, a, 1, b) * eye[None, :, None, :, None]
    return placed.reshape(S5_SLABS, n * a, n * b)


def _s5_weights(lam_re, lam_im, log_dt, b_re, b_im, c_re, c_im):
    g = lam_re.shape[0]
    dt = jnp.exp(log_dt)[:, None]
    mag = jnp.exp(lam_re * dt)
    lbr = mag * jnp.cos(lam_im * dt)
    lbi = mag * jnp.sin(lam_im * dt)
    nr, ni = lbr - 1.0, lbi
    den = lam_re * lam_re + lam_im * lam_im
    cr = (nr * lam_re + ni * lam_im) / den
    ci = (ni * lam_re - nr * lam_im) / den
    bbr = cr[..., None] * b_re - ci[..., None] * b_im
    bbi = cr[..., None] * b_im + ci[..., None] * b_re
    lr = lbr[..., None] * bbr - lbi[..., None] * bbi
    li = lbr[..., None] * bbi + lbi[..., None] * bbr
    l2r = lbr * lbr - lbi * lbi
    l2i = 2.0 * lbr * lbi
    bd = lambda a: _slab_block_diag(jnp.swapaxes(a, 1, 2))
    wb = jnp.concatenate([jnp.concatenate([bd(bbr), bd(bbi)], axis=2),
                          jnp.concatenate([bd(lr), bd(li)], axis=2)], axis=1).astype(BF16)
    cm = jnp.concatenate([bd(c_re), -bd(c_im)], axis=1).astype(BF16)
    lamr = jnp.broadcast_to(l2r.reshape(1, -1), (SUBLANES, l2r.size))
    lami = jnp.broadcast_to(l2i.reshape(1, -1), (SUBLANES, l2i.size))
    return wb, cm, lamr, lami


def _s5(u, wb, cm, lamr, lami, d_skip, w_glu_b, b_glu):
    bsz, seq, sw = u.shape
    assert bsz == 4, "the scan packs 4 batch rows x 2 token parities into 8 sublanes"
    tm = S5_TILE
    ns = lamr.shape[1]
    perm, permt = _s5_perms(S5_SUBTILE)
    const = lambda a: pl.BlockSpec(a.shape, lambda i: (0,) * a.ndim)
    d_row = d_skip.reshape(1, sw)
    bg = b_glu.reshape(1, sw)
    return pl.pallas_call(
        functools.partial(_s5_kernel, tm=tm, sw=sw),
        grid=(seq // tm,),
        in_specs=[pl.BlockSpec((4, tm, sw), lambda i: (0, i, 0)),
                  const(perm), const(permt), const(wb), const(cm), const(lamr), const(lami),
                  const(d_row), const(w_glu_b), const(bg)],
        out_specs=pl.BlockSpec((4, tm, sw), lambda i: (0, i, 0)),
        out_shape=jax.ShapeDtypeStruct((bsz, seq, sw), F32),
        scratch_shapes=[pltpu.VMEM((S5_SLABS, 2, SUBLANES, ns // S5_SLABS), F32),
                        pltpu.VMEM((SUBLANES, sw), F32)],
        compiler_params=pltpu.CompilerParams(
            dimension_semantics=("arbitrary",),
            vmem_limit_bytes=VMEM_LIMIT_BYTES),
        name="s5",
    )(u, perm, permt, wb, cm, lamr, lami, d_row, w_glu_b, bg)


def _front_kernel(x_ref, mod_ref, g_ref, w_ref, tri_ref, u_ref, ga_ref, gb_ref, at_ref,
                  q_scr, k_scr, v_scr, c_scr, acc_scr, z_scr, w_scr, *, sw, aw, d):
    blk = ATTN_BLOCK
    top = ATTN_TOP_ROWS
    n_pairs = aw // LANES
    nh = 2 * n_pairs
    tm = x_ref.shape[1]
    nsub = tm // blk
    step = pl.program_id(1)
    tile0 = pl.multiple_of(step * tm, tm)
    mod = mod_ref[0]
    nt = (((1,), (1,)), ((), ()))
    even_head = (lax.broadcasted_iota(jnp.int32, (tm, aw), 1) // HEAD_DIM) % 2 == 0

    def qkv(hb):
        o = sw
        q = (_dot(hb, w_ref[:, o:o + aw]) * Q_SCALE).astype(BF16); o += aw
        q_scr[0] = jnp.where(even_head, q, jnp.zeros_like(q))
        q_scr[1] = jnp.where(even_head, jnp.zeros_like(q), q)
        k_scr[pl.ds(tile0, tm), :] = _dot(hb, w_ref[:, o:o + aw]).astype(BF16); o += aw
        v = _dot(hb, w_ref[:, o:o + aw]).astype(BF16)
        v_scr[0, pl.ds(tile0, tm), :] = jnp.where(even_head, v, jnp.zeros_like(v))
        v_scr[1, pl.ds(tile0, tm), :] = jnp.where(even_head, jnp.zeros_like(v), v)

    def region(units, fillers=()):
        starts = [pl.multiple_of(kb * blk, blk) for kb, _, _, _ in units]
        for u, (_, r0, r1, _) in enumerate(units):
            n = r1 - r0
            for p in range(n_pairs):
                ls = slice(p * LANES, (p + 1) * LANES)
                kblk = k_scr[pl.ds(starts[u], blk), ls]
                zz = lax.dot_general(
                    jnp.concatenate([q_scr[0, r0:r1, ls], q_scr[1, r0:r1, ls]], axis=0),
                    kblk, nt, preferred_element_type=F32)
                z_scr[u * nh + 2 * p, 0:n] = zz[:n]
                z_scr[u * nh + 2 * p + 1, 0:n] = zz[n:]
        fillers = list(fillers)
        every = -(-len(units) // (len(fillers) + 1))
        for u, (_, r0, r1, diag) in enumerate(units):
            n = r1 - r0
            if diag:
                row = lax.broadcasted_iota(jnp.int32, (n, blk), 0) + r0 % blk
                valid = lax.broadcasted_iota(jnp.int32, (n, blk), 1) < row
            zs, sps = [], []
            for h in range(nh):
                z = z_scr[u * nh + h, 0:n]
                sp = jnp.maximum(z, 0.0) + jnp.log2(1.0 + jnp.exp2(-jnp.abs(z)))
                if diag:
                    sp = jnp.where(valid, sp, 0.0)
                zs.append(z)
                sps.append(sp.astype(BF16))
            incl_all = _dot(jnp.concatenate(sps, axis=0), tri_ref[...])
            for h in range(nh):
                z = zs[h]
                incl = incl_all[h * n:(h + 1) * n]
                total = jnp.broadcast_to(incl[:, 0:1], (n, blk))
                if diag:
                    w = jnp.where(valid, jnp.exp2(z - incl), 0.0)
                    c_scr[h, r0:r1] = total
                else:
                    c = c_scr[h, r0:r1]
                    w = jnp.exp2(z - incl - c)
                    c_scr[h, r0:r1] = c + total
                w_scr[u * nh + h, 0:n] = w.astype(BF16)
            if fillers and (u + 1) % every == 0:
                fillers.pop(0)()
        for filler in fillers:
            filler()
        for r0, r1 in dict.fromkeys((r0, r1) for _, r0, r1, _ in units):
            us = [u for u, (_, a, b, _) in enumerate(units) if (a, b) == (r0, r1)]
            for p in range(n_pairs):
                ls = slice(p * LANES, (p + 1) * LANES)
                ww = jnp.concatenate([w_scr[u * nh + 2 * p + hh, 0:r1 - r0]
                                      for u in us for hh in range(2)], axis=1)
                vv = jnp.concatenate([v_scr[hh, pl.ds(starts[u], blk), ls]
                                      for u in us for hh in range(2)], axis=0)
                if any(units[u][3] for u in us):
                    acc_scr[p, r0:r1] = _dot(ww, vv)
                else:
                    acc_scr[p, r0:r1] = acc_scr[p, r0:r1] + _dot(ww, vv)

    def c_min(r0, r1):
        m = c_scr[0, r0:r1]
        for h in range(1, nh):
            m = jnp.minimum(m, c_scr[h, r0:r1])
        return jnp.min(m)

    def head_units(sub, qi, n_prev):
        base = sub * blk
        units = [(qi, base, base + blk, True)]
        if n_prev >= 1:
            units.append((qi - 1, base, base + blk, False))
        if n_prev >= 2:
            units.append((qi - 2, base, base + top, False))
        return units

    def tile(units):
        h = _rms(x_ref[0]) * g_ref[...]
        hb = (h * (1.0 + mod[1:2]) + mod[0:1]).astype(BF16)
        qkv(hb)
        o = sw + 3 * aw

        def chunk(ref, col, w0):
            def run():
                ref[0, :, col:col + FRONT_CHUNK] = _dot(hb, w_ref[:, w0 + col:w0 + col + FRONT_CHUNK])
            return run

        region(units, [chunk(ref, col, w0)
                       for ref, w0, width in ((ga_ref, o, d), (gb_ref, o + d, d), (u_ref, 0, sw))
                       for col in range(0, width, FRONT_CHUNK)])

    @pl.when(step >= 1)
    def _():
        tile([u for sub in range(nsub) for u in head_units(sub, nsub * step + sub, 2)])

    @pl.when(step == 0)
    def _():
        tile([u for sub in range(nsub) for u in head_units(sub, sub, min(sub, 2))])

    def sweep(first_kb, cmin, r0, r1):
        def more(carry):
            kb, cmin = carry
            return jnp.logical_and(kb >= 0, cmin < UNDERFLOW_LOG2)

        def body(carry):
            kb, _ = carry
            region([(kb, r0, r1, False)])
            return kb - 1, c_min(r0, r1)

        lax.while_loop(more, body, (first_kb, cmin))

    tails = []
    for sub in range(nsub):
        qi = nsub * step + sub
        base = sub * blk
        tails.append((jnp.where(qi >= 2, qi - 3, -1), base, base + top))
        tails.append((jnp.where(qi >= 2, qi - 2, -1), base + top, base + blk))
    cmins = [c_min(r0, r1) for _, r0, r1 in tails]
    for (first_kb, r0, r1), cmin in zip(tails, cmins):
        sweep(first_kb, cmin, r0, r1)
    for p in range(n_pairs):
        at_ref[0, :, p * LANES:(p + 1) * LANES] = acc_scr[p].astype(at_ref.dtype)


def _attn_tri():
    blk = ATTN_BLOCK
    m = np.arange(blk)[:, None]
    j = np.arange(blk)[None, :]
    return jnp.asarray((m >= j).astype(np.float32), BF16)


def _front(x, mod, norm_g, w_in_b, sw, aw):
    bsz, seq, d = x.shape
    tm = FRONT_TILE
    blk = ATTN_BLOCK
    n = w_in_b.shape[1]
    n_pairs = aw // LANES
    nz = ATTN_REGION * (tm // blk) * 2 * n_pairs
    tri = _attn_tri()
    tok = lambda w: pl.BlockSpec((1, tm, w), lambda b, i: (b, i, 0))
    const = lambda a: pl.BlockSpec(a.shape, lambda b, i: (0,) * a.ndim,
                                   pipeline_mode=pl.Buffered(1))
    return pl.pallas_call(
        functools.partial(_front_kernel, sw=sw, aw=aw, d=d),
        grid=(bsz, seq // tm),
        in_specs=[tok(d),
                  pl.BlockSpec((1, N_ADA, d), lambda b, i: (b, 0, 0)),
                  const(norm_g), const(w_in_b), const(tri)],
        out_specs=[tok(sw), tok(d), tok(d), tok(aw)],
        out_shape=[jax.ShapeDtypeStruct((bsz, seq, sw), F32),
                   jax.ShapeDtypeStruct((bsz, seq, d), F32),
                   jax.ShapeDtypeStruct((bsz, seq, d), F32),
                   jax.ShapeDtypeStruct((bsz, seq, aw), BF16)],
        scratch_shapes=[pltpu.VMEM((2, tm, aw), BF16),
                        pltpu.VMEM((seq, aw), BF16),
                        pltpu.VMEM((2, seq, aw), BF16),
                        pltpu.VMEM((2 * n_pairs, tm, blk), F32),
                        pltpu.VMEM((n_pairs, tm, LANES), F32),
                        pltpu.VMEM((nz, blk, blk), F32),
                        pltpu.VMEM((nz, blk, blk), BF16)],
        compiler_params=pltpu.CompilerParams(
            dimension_semantics=("arbitrary", "arbitrary"),
            vmem_limit_bytes=VMEM_LIMIT_BYTES),
        name="front",
    )(x, mod, norm_g, w_in_b, tri)


def _out_ffn_kernel(x_ref, s5_ref, at_ref, ga_ref, gb_ref, mod_ref, n2_ref, nf_ref,
                    wa_ref, wb_ref, wo_ref, wg_ref, wu_ref, wd_ref, o_ref, *, final_norm):
    mod = mod_ref[0]
    tm = x_ref.shape[1]
    groups = [slice(r, r + tm // OUT_GROUPS) for r in range(0, tm, tm // OUT_GROUPS)]
    ms = []
    for g in groups:
        ya = _dot(s5_ref[0, g].astype(BF16), wa_ref[...])
        yb = _dot(at_ref[0, g], wb_ref[...])
        m = jax.nn.sigmoid(ga_ref[0, g]) * ya + jax.nn.sigmoid(gb_ref[0, g]) * yb
        ms.append(m.astype(BF16))
    x1s, hs = [], []
    for g, m in zip(groups, ms):
        x1 = x_ref[0, g] + mod[2:3] * _dot(m, wo_ref[...])
        h = _rms(x1) * n2_ref[...]
        x1s.append(x1)
        hs.append((h * (1.0 + mod[4:5]) + mod[3:4]).astype(BF16))
    acts = []
    for h in hs:
        gate = _dot(h, wg_ref[...])
        up = _dot(h, wu_ref[...])
        acts.append((gate * jax.nn.sigmoid(gate) * up).astype(BF16))
    for g, x1, act in zip(groups, x1s, acts):
        x2 = x1 + mod[5:6] * _dot(act, wd_ref[...])
        o_ref[0, g] = _rms(x2) * nf_ref[...] if final_norm else x2


def _out_ffn(x, s5o, attn, ga, gb, mod, n2g, nfg, wa, wb, wo, wg, wu, wd, final_norm):
    bsz, seq, d = x.shape
    tm = OUT_TILE
    tok = lambda a: pl.BlockSpec((1, tm, a.shape[-1]), lambda b, i: (b, i, 0))
    const = lambda a: pl.BlockSpec(a.shape, lambda b, i: (0,) * a.ndim,
                                   pipeline_mode=pl.Buffered(1))
    return pl.pallas_call(
        functools.partial(_out_ffn_kernel, final_norm=final_norm),
        grid=(bsz, seq // tm),
        in_specs=[tok(x), tok(s5o), tok(attn), tok(ga), tok(gb),
                  pl.BlockSpec((1, N_ADA, d), lambda b, i: (b, 0, 0)),
                  const(n2g), const(nfg),
                  const(wa), const(wb), const(wo), const(wg), const(wu), const(wd)],
        out_specs=pl.BlockSpec((1, tm, d), lambda b, i: (b, i, 0)),
        out_shape=jax.ShapeDtypeStruct((bsz, seq, d), F32),
        compiler_params=pltpu.CompilerParams(
            dimension_semantics=("arbitrary", "arbitrary"),
            vmem_limit_bytes=VMEM_LIMIT_BYTES),
        name="out_ffn",
    )(x, s5o, attn, ga, gb, mod, n2g, nfg, wa, wb, wo, wg, wu, wd)


def kernel(x, c, w_ada, b_ada, norm1_g, w_in, lam_re, lam_im, log_dt, b_re, b_im, c_re, c_im,
           d_skip, w_glu, b_glu, w_a, w_b, w_o, norm2_g, w_ffn_gate, w_ffn_up, w_ffn_down,
           norm_f_g):
    depth = w_ada.shape[0]
    bsz, seq, d = x.shape
    sw = w_glu.shape[1]
    aw = w_b.shape[1]
    for l in range(depth):
        mod = _ada(c, w_ada[l], b_ada[l]).reshape(bsz, N_ADA, d)
        u, ga, gb, attn = _front(x, mod, norm1_g[l].reshape(1, d), w_in[l].astype(BF16), sw, aw)
        wb, cm, lamr, lami = _s5_weights(lam_re[l], lam_im[l], log_dt[l], b_re[l], b_im[l],
                                         c_re[l], c_im[l])
        s5o = _s5(u, wb, cm, lamr, lami, d_skip[l], w_glu[l].astype(BF16), b_glu[l])
        x = _out_ffn(x, s5o, attn, ga, gb, mod, norm2_g[l].reshape(1, d), norm_f_g.reshape(1, d),
                     w_a[l].astype(BF16), w_b[l].astype(BF16), w_o[l].astype(BF16),
                     w_ffn_gate[l].astype(BF16), w_ffn_up[l].astype(BF16),
                     w_ffn_down[l].astype(BF16), final_norm=(l == depth - 1))
    return x
```

```python
import functools
import math

import numpy as np
import jax
import jax.numpy as jnp
from jax import lax
from jax.experimental import pallas as pl
from jax.experimental.pallas import tpu as pltpu

F32 = jnp.float32
BF16 = jnp.bfloat16

S5_GROUP = 16
S5_STATE = 64
HEAD_DIM = 64
N_ADA = 6
RMS_EPS = 1e-6
Q_SCALE = math.log2(math.e) / math.sqrt(HEAD_DIM)
UNDERFLOW_LOG2 = 151.0

LANES = 128
SUBLANES = 8
VMEM_LIMIT_BYTES = 56 * 1024 * 1024

ATTN_BLOCK = 128
ATTN_REGION = 3
ATTN_TOP_ROWS = 32
S5_TILE = 512
S5_SUBTILE = 128
S5_SLABS = 4
FRONT_TILE = 512
FRONT_CHUNK = 256
OUT_TILE = 512
OUT_GROUPS = 2
W_CHUNK_ROWS = 128


def _dot(a, b):
    return jnp.dot(a, b, preferred_element_type=F32)


def _rms(x):
    return x * lax.rsqrt(jnp.mean(x * x, axis=-1, keepdims=True) + RMS_EPS)


def _ada_kernel(c_ref, w_ref, b_ref, o_ref):
    c = c_ref[...]
    bsz = c.shape[0]
    cond = c * jax.nn.sigmoid(c)
    pad = -bsz % SUBLANES
    if pad:
        cond = jnp.concatenate([cond, jnp.zeros((pad, c.shape[1]), F32)], axis=0)
    o_ref[...] = _dot(cond.astype(BF16), w_ref[...].astype(BF16))[:bsz] + b_ref[...]


def _ada(c, w_ada, b_ada):
    bsz, d = c.shape
    n = w_ada.shape[1]
    tn = 1536
    return pl.pallas_call(
        _ada_kernel,
        grid=(n // tn,),
        in_specs=[pl.BlockSpec((bsz, d), lambda j: (0, 0)),
                  pl.BlockSpec((d, tn), lambda j: (0, j)),
                  pl.BlockSpec((1, tn), lambda j: (0, j))],
        out_specs=pl.BlockSpec((bsz, tn), lambda j: (0, j)),
        out_shape=jax.ShapeDtypeStruct((bsz, n), F32),
        name="ada",
    )(c, w_ada, b_ada.reshape(1, n))


def _s5_kernel(u_ref, perm_ref, permt_ref, wb_ref, cm_ref, lamr_ref, lami_ref, d_ref,
               wglu_ref, bglu_ref, o_ref, x_scr, ulast_scr, *, tm, sw):
    i = pl.program_id(0)
    ts = S5_SUBTILE
    n2 = ts // 2
    rows = SUBLANES * n2
    cw = sw // S5_SLABS
    hs = cw * S5_STATE // S5_GROUP

    @pl.when(i == 0)
    def _():
        x_scr[...] = jnp.zeros_like(x_scr)
        ulast_scr[...] = jnp.zeros_like(ulast_scr)

    sub8 = lax.broadcasted_iota(jnp.int32, (SUBLANES, sw), 0)
    odd = (lax.broadcasted_iota(jnp.int32, (rows, sw), 0) & 1) == 1

    def natural(j):
        return jnp.concatenate([u_ref[b, j * ts:(j + 1) * ts, :] for b in range(4)], axis=0)

    def last_rows(j):
        last = jnp.zeros((SUBLANES, sw), F32)
        for b in range(4):
            row = u_ref[b, (j + 1) * ts - 1:(j + 1) * ts, :].astype(BF16).astype(F32)
            last = jnp.where(sub8 == 2 * b, jnp.broadcast_to(row, (SUBLANES, sw)), last)
        return last

    def input_stage(j, before):
        a_cur = _dot(perm_ref[...], natural(j).astype(BF16))
        a_prev = jnp.where(odd, pltpu.roll(a_cur, 1, 0), pltpu.roll(a_cur, SUBLANES - 1, 0))
        first = jnp.where((sub8 & 1) == 1, a_prev[:SUBLANES], before)
        a_prev = jnp.concatenate([first, a_prev[SUBLANES:]], axis=0).astype(BF16)
        a_cur = a_cur.astype(BF16)
        return [_dot(jnp.concatenate([a_cur[:, s * cw:(s + 1) * cw],
                                      a_prev[:, s * cw:(s + 1) * cw]], axis=1), wb_ref[s])
                for s in range(S5_SLABS)]

    def scan_stage(bus, x):
        states, x_out = [], []
        for s in range(S5_SLABS):
            ar = lamr_ref[:, hs * s:hs * (s + 1)]
            ai = lami_ref[:, hs * s:hs * (s + 1)]
            xr, xi = x[s]
            st = []
            for t2 in range(n2):
                rs = slice(SUBLANES * t2, SUBLANES * (t2 + 1))
                xr, xi = (ar * xr - ai * xi + bus[s][rs, :hs],
                          ar * xi + ai * xr + bus[s][rs, hs:])
                st.append(jnp.concatenate([xr, xi], axis=1))
            x_out.append((xr, xi))
            states.append(jnp.concatenate(st, axis=0).astype(BF16))
        return states, x_out

    def output_stage(j, states):
        y_il = jnp.concatenate([_dot(states[s], cm_ref[s]) for s in range(S5_SLABS)], axis=1)
        y = _dot(permt_ref[...], y_il.astype(BF16))
        y = y + d_ref[...] * natural(j)
        y = jax.nn.gelu(y)
        z = _dot(y.astype(BF16), wglu_ref[...]) + bglu_ref[...]
        out = y * jax.nn.sigmoid(z)
        for b in range(4):
            o_ref[b, j * ts:(j + 1) * ts, :] = out[b * ts:(b + 1) * ts]

    nsub = tm // ts
    befores = [ulast_scr[...]] + [last_rows(j) for j in range(nsub - 1)]
    bus = [input_stage(j, befores[j]) for j in range(nsub)]
    ulast_scr[...] = last_rows(nsub - 1)
    x = [(x_scr[s, 0], x_scr[s, 1]) for s in range(S5_SLABS)]
    for j in range(nsub):
        states, x = scan_stage(bus[j], x)
        output_stage(j, states)
    for s in range(S5_SLABS):
        x_scr[s, 0], x_scr[s, 1] = x[s]


def _s5_perms(tm):
    n2 = tm // 2
    rows = SUBLANES * n2
    perm = np.zeros((rows, 4 * tm), np.float32)
    permt = np.zeros((4 * tm, rows), np.float32)
    for t2 in range(n2):
        for b in range(4):
            for par in range(2):
                r = SUBLANES * t2 + 2 * b + par
                t = 2 * t2 + par
                perm[r, b * tm + t] = 1.0
                permt[b * tm + t, r] = 1.0
    return jnp.asarray(perm, BF16), jnp.asarray(permt, BF16)


def _slab_block_diag(blocks):
    g, a, b = blocks.shape
    n = g // S5_SLABS
    eye = jnp.eye(n, dtype=blocks.dtype)
    placed = blocks.reshape(S5_SLABS, n, a, 1, b) * eye[None, :, None, :, None]
    return placed.reshape(S5_SLABS, n * a, n * b)


def _s5_weights(lam_re, lam_im, log_dt, b_re, b_im, c_re, c_im):
    g = lam_re.shape[0]
    dt = jnp.exp(log_dt)[:, None]
    mag = jnp.exp(lam_re * dt)
    lbr = mag * jnp.cos(lam_im * dt)
    lbi = mag * jnp.sin(lam_im * dt)
    nr, ni = lbr - 1.0, lbi
    den = lam_re * lam_re + lam_im * lam_im
    cr = (nr * lam_re + ni * lam_im) / den
    ci = (ni * lam_re - nr * lam_im) / den
    bbr = cr[..., None] * b_re - ci[..., None] * b_im
    bbi = cr[..., None] * b_im + ci[..., None] * b_re
    lr = lbr[..., None] * bbr - lbi[..., None] * bbi
    li = lbr[..., None] * bbi + lbi[..., None] * bbr
    l2r = lbr * lbr - lbi * lbi
    l2i = 2.0 * lbr * lbi
    bd = lambda a: _slab_block_diag(jnp.swapaxes(a, 1, 2))
    wb = jnp.concatenate([jnp.concatenate([bd(bbr), bd(bbi)], axis=2),
                          jnp.concatenate([bd(lr), bd(li)], axis=2)], axis=1).astype(BF16)
    cm = jnp.concatenate([bd(c_re), -bd(c_im)], axis=1).astype(BF16)
    lamr = jnp.broadcast_to(l2r.reshape(1, -1), (SUBLANES, l2r.size))
    lami = jnp.broadcast_to(l2i.reshape(1, -1), (SUBLANES, l2i.size))
    return wb, cm, lamr, lami


def _s5(u, wb, cm, lamr, lami, d_skip, w_glu_b, b_glu):
    bsz, seq, sw = u.shape
    assert bsz == 4, "the scan packs 4 batch rows x 2 token parities into 8 sublanes"
    tm = S5_TILE
    ns = lamr.shape[1]
    perm, permt = _s5_perms(S5_SUBTILE)
    const = lambda a: pl.BlockSpec(a.shape, lambda i: (0,) * a.ndim)
    d_row = d_skip.reshape(1, sw)
    bg = b_glu.reshape(1, sw)
    return pl.pallas_call(
        functools.partial(_s5_kernel, tm=tm, sw=sw),
        grid=(seq // tm,),
        in_specs=[pl.BlockSpec((4, tm, sw), lambda i: (0, i, 0)),
                  const(perm), const(permt), const(wb), const(cm), const(lamr), const(lami),
                  const(d_row), const(w_glu_b), const(bg)],
        out_specs=pl.BlockSpec((4, tm, sw), lambda i: (0, i, 0)),
        out_shape=jax.ShapeDtypeStruct((bsz, seq, sw), F32),
        scratch_shapes=[pltpu.VMEM((S5_SLABS, 2, SUBLANES, ns // S5_SLABS), F32),
                        pltpu.VMEM((SUBLANES, sw), F32)],
        compiler_params=pltpu.CompilerParams(
            dimension_semantics=("arbitrary",),
            vmem_limit_bytes=VMEM_LIMIT_BYTES),
        name="s5",
    )(u, perm, permt, wb, cm, lamr, lami, d_row, w_glu_b, bg)


def _front_kernel(x_ref, mod_ref, g_ref, w_ref, tri_ref, u_ref, ga_ref, gb_ref, at_ref,
                  q_scr, k_scr, v_scr, c_scr, acc_scr, z_scr, w_scr, *, sw, aw, d):
    blk = ATTN_BLOCK
    top = ATTN_TOP_ROWS
    n_pairs = aw // LANES
    nh = 2 * n_pairs
    tm = x_ref.shape[1]
    nsub = tm // blk
    step = pl.program_id(1)
    tile0 = pl.multiple_of(step * tm, tm)
    mod = mod_ref[0]
    nt = (((1,), (1,)), ((), ()))
    even_head = (lax.broadcasted_iota(jnp.int32, (tm, aw), 1) // HEAD_DIM) % 2 == 0

    def qkv(hb):
        o = sw
        q = (_dot(hb, w_ref[:, o:o + aw]) * Q_SCALE).astype(BF16); o += aw
        q_scr[0] = jnp.where(even_head, q, jnp.zeros_like(q))
        q_scr[1] = jnp.where(even_head, jnp.zeros_like(q), q)
        k_scr[pl.ds(tile0, tm), :] = _dot(hb, w_ref[:, o:o + aw]).astype(BF16); o += aw
        v = _dot(hb, w_ref[:, o:o + aw]).astype(BF16)
        v_scr[0, pl.ds(tile0, tm), :] = jnp.where(even_head, v, jnp.zeros_like(v))
        v_scr[1, pl.ds(tile0, tm), :] = jnp.where(even_head, jnp.zeros_like(v), v)

    def region(units, fillers=()):
        starts = [pl.multiple_of(kb * blk, blk) for kb, _, _, _ in units]
        for u, (_, r0, r1, _) in enumerate(units):
            n = r1 - r0
            for p in range(n_pairs):
                ls = slice(p * LANES, (p + 1) * LANES)
                kblk = k_scr[pl.ds(starts[u], blk), ls]
                zz = lax.dot_general(
                    jnp.concatenate([q_scr[0, r0:r1, ls], q_scr[1, r0:r1, ls]], axis=0),
                    kblk, nt, preferred_element_type=F32)
                z_scr[u * nh + 2 * p, 0:n] = zz[:n]
                z_scr[u * nh + 2 * p + 1, 0:n] = zz[n:]
        fillers = list(fillers)
        every = -(-len(units) // (len(fillers) + 1))
        for u, (_, r0, r1, diag) in enumerate(units):
            n = r1 - r0
            if diag:
                row = lax.broadcasted_iota(jnp.int32, (n, blk), 0) + r0 % blk
                valid = lax.broadcasted_iota(jnp.int32, (n, blk), 1) < row
            zs, sps = [], []
            for h in range(nh):
                z = z_scr[u * nh + h, 0:n]
                sp = jnp.maximum(z, 0.0) + jnp.log2(1.0 + jnp.exp2(-jnp.abs(z)))
                if diag:
                    sp = jnp.where(valid, sp, 0.0)
                zs.append(z)
                sps.append(sp.astype(BF16))
            incl_all = _dot(jnp.concatenate(sps, axis=0), tri_ref[...])
            for h in range(nh):
                z = zs[h]
                incl = incl_all[h * n:(h + 1) * n]
                total = jnp.broadcast_to(incl[:, 0:1], (n, blk))
                if diag:
                    w = jnp.where(valid, jnp.exp2(z - incl), 0.0)
                    c_scr[h, r0:r1] = total
                else:
                    c = c_scr[h, r0:r1]
                    w = jnp.exp2(z - incl - c)
                    c_scr[h, r0:r1] = c + total
                w_scr[u * nh + h, 0:n] = w.astype(BF16)
            if fillers and (u + 1) % every == 0:
                fillers.pop(0)()
        for filler in fillers:
            filler()
        for r0, r1 in dict.fromkeys((r0, r1) for _, r0, r1, _ in units):
            us = [u for u, (_, a, b, _) in enumerate(units) if (a, b) == (r0, r1)]
            for p in range(n_pairs):
                ls = slice(p * LANES, (p + 1) * LANES)
                ww = jnp.concatenate([w_scr[u * nh + 2 * p + hh, 0:r1 - r0]
                                      for u in us for hh in range(2)], axis=1)
                vv = jnp.concatenate([v_scr[hh, pl.ds(starts[u], blk), ls]
                                      for u in us for hh in range(2)], axis=0)
                if any(units[u][3] for u in us):
                    acc_scr[p, r0:r1] = _dot(ww, vv)
                else:
                    acc_scr[p, r0:r1] = acc_scr[p, r0:r1] + _dot(ww, vv)

    def c_min(r0, r1):
        m = c_scr[0, r0:r1]
        for h in range(1, nh):
            m = jnp.minimum(m, c_scr[h, r0:r1])
        return jnp.min(m)

    def head_units(sub, qi, n_prev):
        base = sub * blk
        units = [(qi, base, base + blk, True)]
        if n_prev >= 1:
            units.append((qi - 1, base, base + blk, False))
        if n_prev >= 2:
            units.append((qi - 2, base, base + top, False))
        return units

    def tile(units):
        h = _rms(x_ref[0]) * g_ref[...]
        hb = (h * (1.0 + mod[1:2]) + mod[0:1]).astype(BF16)
        qkv(hb)
        o = sw + 3 * aw

        def chunk(ref, col, w0):
            def run():
                ref[0, :, col:col + FRONT_CHUNK] = _dot(hb, w_ref[:, w0 + col:w0 + col + FRONT_CHUNK])
            return run

        region(units, [chunk(ref, col, w0)
                       for ref, w0, width in ((ga_ref, o, d), (gb_ref, o + d, d), (u_ref, 0, sw))
                       for col in range(0, width, FRONT_CHUNK)])

    @pl.when(step >= 1)
    def _():
        tile([u for sub in range(nsub) for u in head_units(sub, nsub * step + sub, 2)])

    @pl.when(step == 0)
    def _():
        tile([u for sub in range(nsub) for u in head_units(sub, sub, min(sub, 2))])

    def sweep(first_kb, cmin, r0, r1):
        def more(carry):
            kb, cmin = carry
            return jnp.logical_and(kb >= 0, cmin < UNDERFLOW_LOG2)

        def body(carry):
            kb, _ = carry
            region([(kb, r0, r1, False)])
            return kb - 1, c_min(r0, r1)

        lax.while_loop(more, body, (first_kb, cmin))

    tails = []
    for sub in range(nsub):
        qi = nsub * step + sub
        base = sub * blk
        tails.append((jnp.where(qi >= 2, qi - 3, -1), base, base + top))
        tails.append((jnp.where(qi >= 2, qi - 2, -1), base + top, base + blk))
    cmins = [c_min(r0, r1) for _, r0, r1 in tails]
    for (first_kb, r0, r1), cmin in zip(tails, cmins):
        sweep(first_kb, cmin, r0, r1)
    for p in range(n_pairs):
        at_ref[0, :, p * LANES:(p + 1) * LANES] = acc_scr[p].astype(at_ref.dtype)


def _attn_tri():
    blk = ATTN_BLOCK
    m = np.arange(blk)[:, None]
    j = np.arange(blk)[None, :]
    return jnp.asarray((m >= j).astype(np.float32), BF16)


def _front(x, mod, norm_g, w_in_b, sw, aw):
    bsz, seq, d = x.shape
    tm = FRONT_TILE
    blk = ATTN_BLOCK
    n = w_in_b.shape[1]
    n_pairs = aw // LANES
    nz = ATTN_REGION * (tm // blk) * 2 * n_pairs
    tri = _attn_tri()
    tok = lambda w: pl.BlockSpec((1, tm, w), lambda b, i: (b, i, 0))
    const = lambda a: pl.BlockSpec(a.shape, lambda b, i: (0,) * a.ndim,
                                   pipeline_mode=pl.Buffered(1))
    return pl.pallas_call(
        functools.partial(_front_kernel, sw=sw, aw=aw, d=d),
        grid=(bsz, seq // tm),
        in_specs=[tok(d),
                  pl.BlockSpec((1, N_ADA, d), lambda b, i: (b, 0, 0)),
                  const(norm_g), const(w_in_b), const(tri)],
        out_specs=[tok(sw), tok(d), tok(d), tok(aw)],
        out_shape=[jax.ShapeDtypeStruct((bsz, seq, sw), F32),
                   jax.ShapeDtypeStruct((bsz, seq, d), F32),
                   jax.ShapeDtypeStruct((bsz, seq, d), F32),
                   jax.ShapeDtypeStruct((bsz, seq, aw), BF16)],
        scratch_shapes=[pltpu.VMEM((2, tm, aw), BF16),
                        pltpu.VMEM((seq, aw), BF16),
                        pltpu.VMEM((2, seq, aw), BF16),
                        pltpu.VMEM((2 * n_pairs, tm, blk), F32),
                        pltpu.VMEM((n_pairs, tm, LANES), F32),
                        pltpu.VMEM((nz, blk, blk), F32),
                        pltpu.VMEM((nz, blk, blk), BF16)],
        compiler_params=pltpu.CompilerParams(
            dimension_semantics=("arbitrary", "arbitrary"),
            vmem_limit_bytes=VMEM_LIMIT_BYTES),
        name="front",
    )(x, mod, norm_g, w_in_b, tri)


def _out_ffn_kernel(x_ref, s5_ref, at_ref, ga_ref, gb_ref, mod_ref, n2_ref, nf_ref,
                    wa_hbm, wb_hbm, wo_hbm, wg_hbm, wu_hbm, wd_hbm, o_ref,
                    wa_ref, wb_ref, wo_ref, wg_ref, wu_ref, wd_ref, stage_d, stage_f, sem_d, sem_f,
                    *, final_norm):
    @pl.when(jnp.logical_and(pl.program_id(0) == 0, pl.program_id(1) == 0))
    def _():
        def load(pairs, stage, sem):
            jobs = [(hbm, res, r) for hbm, res in pairs
                    for r in range(0, hbm.shape[0], W_CHUNK_ROWS)]

            def copy(j):
                hbm, _, r = jobs[j]
                return pltpu.make_async_copy(hbm.at[pl.ds(r, W_CHUNK_ROWS), :],
                                             stage.at[j % 2], sem.at[j % 2])

            copy(0).start()
            for j, (_, res, r) in enumerate(jobs):
                if j + 1 < len(jobs):
                    copy(j + 1).start()
                copy(j).wait()
                res[r:r + W_CHUNK_ROWS, :] = stage[j % 2].astype(BF16)

        load(((wa_hbm, wa_ref), (wb_hbm, wb_ref), (wo_hbm, wo_ref), (wd_hbm, wd_ref)),
             stage_d, sem_d)
        load(((wg_hbm, wg_ref), (wu_hbm, wu_ref)), stage_f, sem_f)

    mod = mod_ref[0]
    tm = x_ref.shape[1]
    groups = [slice(r, r + tm // OUT_GROUPS) for r in range(0, tm, tm // OUT_GROUPS)]
    ms = []
    for g in groups:
        ya = _dot(s5_ref[0, g].astype(BF16), wa_ref[...])
        yb = _dot(at_ref[0, g], wb_ref[...])
        m = jax.nn.sigmoid(ga_ref[0, g]) * ya + jax.nn.sigmoid(gb_ref[0, g]) * yb
        ms.append(m.astype(BF16))
    x1s, hs = [], []
    for g, m in zip(groups, ms):
        x1 = x_ref[0, g] + mod[2:3] * _dot(m, wo_ref[...])
        h = _rms(x1) * n2_ref[...]
        x1s.append(x1)
        hs.append((h * (1.0 + mod[4:5]) + mod[3:4]).astype(BF16))
    acts = []
    for h in hs:
        gate = _dot(h, wg_ref[...])
        up = _dot(h, wu_ref[...])
        acts.append((gate * jax.nn.sigmoid(gate) * up).astype(BF16))
    for g, x1, act in zip(groups, x1s, acts):
        x2 = x1 + mod[5:6] * _dot(act, wd_ref[...])
        o_ref[0, g] = _rms(x2) * nf_ref[...] if final_norm else x2


def _out_ffn(x, s5o, attn, ga, gb, mod, n2g, nfg, wa, wb, wo, wg, wu, wd, final_norm):
    bsz, seq, d = x.shape
    tm = OUT_TILE
    weights = (wa, wb, wo, wg, wu, wd)
    assert all(w.shape[0] % W_CHUNK_ROWS == 0 for w in weights)
    fh = wg.shape[1]
    tok = lambda a: pl.BlockSpec((1, tm, a.shape[-1]), lambda b, i: (b, i, 0))
    const = lambda a: pl.BlockSpec(a.shape, lambda b, i: (0,) * a.ndim,
                                   pipeline_mode=pl.Buffered(1))
    return pl.pallas_call(
        functools.partial(_out_ffn_kernel, final_norm=final_norm),
        grid=(bsz, seq // tm),
        in_specs=[tok(x), tok(s5o), tok(attn), tok(ga), tok(gb),
                  pl.BlockSpec((1, N_ADA, d), lambda b, i: (b, 0, 0)),
                  const(n2g), const(nfg)] + [pl.BlockSpec(memory_space=pl.ANY)] * len(weights),
        out_specs=pl.BlockSpec((1, tm, d), lambda b, i: (b, i, 0)),
        out_shape=jax.ShapeDtypeStruct((bsz, seq, d), F32),
        scratch_shapes=[pltpu.VMEM(w.shape, BF16) for w in weights] + [
            pltpu.VMEM((2, W_CHUNK_ROWS, d), F32),
            pltpu.VMEM((2, W_CHUNK_ROWS, fh), F32),
            pltpu.SemaphoreType.DMA((2,)),
            pltpu.SemaphoreType.DMA((2,))],
        compiler_params=pltpu.CompilerParams(
            dimension_semantics=("arbitrary", "arbitrary"),
            vmem_limit_bytes=VMEM_LIMIT_BYTES),
        name="out_ffn",
    )(x, s5o, attn, ga, gb, mod, n2g, nfg, *weights)


def kernel(x, c, w_ada, b_ada, norm1_g, w_in, lam_re, lam_im, log_dt, b_re, b_im, c_re, c_im,
           d_skip, w_glu, b_glu, w_a, w_b, w_o, norm2_g, w_ffn_gate, w_ffn_up, w_ffn_down,
           norm_f_g):
    depth = w_ada.shape[0]
    bsz, seq, d = x.shape
    sw = w_glu.shape[1]
    aw = w_b.shape[1]
    for l in range(depth):
        mod = _ada(c, w_ada[l], b_ada[l]).reshape(bsz, N_ADA, d)
        u, ga, gb, attn = _front(x, mod, norm1_g[l].reshape(1, d), w_in[l].astype(BF16), sw, aw)
        wb, cm, lamr, lami = _s5_weights(lam_re[l], lam_im[l], log_dt[l], b_re[l], b_im[l],
                                         c_re[l], c_im[l])
        s5o = _s5(u, wb, cm, lamr, lami, d_skip[l], w_glu[l].astype(BF16), b_glu[l])
        x = _out_ffn(x, s5o, attn, ga, gb, mod, norm2_g[l].reshape(1, d), norm_f_g.reshape(1, d),
                     w_a[l], w_b[l], w_o[l], w_ffn_gate[l], w_ffn_up[l], w_ffn_down[l],
                     final_norm=(l == depth - 1))
    return x
```

```python
import functools
import math

import numpy as np
import jax
import jax.numpy as jnp
from jax import lax
from jax.experimental import pallas as pl
from jax.experimental.pallas import tpu as pltpu

F32 = jnp.float32
BF16 = jnp.bfloat16

S5_GROUP = 16
S5_STATE = 64
HEAD_DIM = 64
N_ADA = 6
RMS_EPS = 1e-6
Q_SCALE = math.log2(math.e) / math.sqrt(HEAD_DIM)
UNDERFLOW_LOG2 = 151.0

LANES = 128
SUBLANES = 8
VMEM_LIMIT_BYTES = 56 * 1024 * 1024

ATTN_BLOCK = 128
ATTN_REGION = 3
ATTN_TOP_ROWS = 32
S5_TILE = 512
S5_SUBTILE = 128
S5_SLABS = 4
FRONT_TILE = 512
FRONT_CHUNK = 256
OUT_TILE = 512
OUT_GROUPS = 2


def _dot(a, b):
    return jnp.dot(a, b, preferred_element_type=F32)


def _rms(x):
    return x * lax.rsqrt(jnp.mean(x * x, axis=-1, keepdims=True) + RMS_EPS)


def _ada_kernel(c_ref, w_ref, b_ref, o_ref):
    c = c_ref[...]
    bsz = c.shape[0]
    cond = c * jax.nn.sigmoid(c)
    pad = -bsz % SUBLANES
    if pad:
        cond = jnp.concatenate([cond, jnp.zeros((pad, c.shape[1]), F32)], axis=0)
    o_ref[...] = _dot(cond.astype(BF16), w_ref[...].astype(BF16))[:bsz] + b_ref[...]


def _ada(c, w_ada, b_ada):
    bsz, d = c.shape
    n = w_ada.shape[1]
    tn = 1536
    return pl.pallas_call(
        _ada_kernel,
        grid=(n // tn,),
        in_specs=[pl.BlockSpec((bsz, d), lambda j: (0, 0)),
                  pl.BlockSpec((d, tn), lambda j: (0, j)),
                  pl.BlockSpec((1, tn), lambda j: (0, j))],
        out_specs=pl.BlockSpec((bsz, tn), lambda j: (0, j)),
        out_shape=jax.ShapeDtypeStruct((bsz, n), F32),
        name="ada",
    )(c, w_ada, b_ada.reshape(1, n))


def _s5_kernel(u_ref, perm_ref, permt_ref, wb_ref, cm_ref, lamr_ref, lami_ref, d_ref,
               wglu_ref, bglu_ref, o_ref, x_scr, ulast_scr, *, tm, sw):
    i = pl.program_id(0)
    ts = S5_SUBTILE
    n2 = ts // 2
    rows = SUBLANES * n2
    cw = sw // S5_SLABS
    hs = cw * S5_STATE // S5_GROUP

    @pl.when(i == 0)
    def _():
        x_scr[...] = jnp.zeros_like(x_scr)
        ulast_scr[...] = jnp.zeros_like(ulast_scr)

    sub8 = lax.broadcasted_iota(jnp.int32, (SUBLANES, sw), 0)
    odd = (lax.broadcasted_iota(jnp.int32, (rows, sw), 0) & 1) == 1

    def natural(j):
        return jnp.concatenate([u_ref[b, j * ts:(j + 1) * ts, :] for b in range(4)], axis=0)

    def last_rows(j):
        last = jnp.zeros((SUBLANES, sw), F32)
        for b in range(4):
            row = u_ref[b, (j + 1) * ts - 1:(j + 1) * ts, :].astype(BF16).astype(F32)
            last = jnp.where(sub8 == 2 * b, jnp.broadcast_to(row, (SUBLANES, sw)), last)
        return last

    def input_stage(j, before):
        a_cur = _dot(perm_ref[...], natural(j).astype(BF16))
        a_prev = jnp.where(odd, pltpu.roll(a_cur, 1, 0), pltpu.roll(a_cur, SUBLANES - 1, 0))
        first = jnp.where((sub8 & 1) == 1, a_prev[:SUBLANES], before)
        a_prev = jnp.concatenate([first, a_prev[SUBLANES:]], axis=0).astype(BF16)
        a_cur = a_cur.astype(BF16)
        return [_dot(jnp.concatenate([a_cur[:, s * cw:(s + 1) * cw],
                                      a_prev[:, s * cw:(s + 1) * cw]], axis=1), wb_ref[s])
                for s in range(S5_SLABS)]

    def scan_stage(bus, x):
        states, x_out = [], []
        for s in range(S5_SLABS):
            ar = lamr_ref[:, hs * s:hs * (s + 1)]
            ai = lami_ref[:, hs * s:hs * (s + 1)]
            xr, xi = x[s]
            st = []
            for t2 in range(n2):
                rs = slice(SUBLANES * t2, SUBLANES * (t2 + 1))
                xr, xi = (ar * xr - ai * xi + bus[s][rs, :hs],
                          ar * xi + ai * xr + bus[s][rs, hs:])
                st.append(jnp.concatenate([xr, xi], axis=1))
            x_out.append((xr, xi))
            states.append(jnp.concatenate(st, axis=0).astype(BF16))
        return states, x_out

    def output_stage(j, states):
        y_il = jnp.concatenate([_dot(states[s], cm_ref[s]) for s in range(S5_SLABS)], axis=1)
        y = _dot(permt_ref[...], y_il.astype(BF16))
        y = y + d_ref[...] * natural(j)
        y = jax.nn.gelu(y)
        z = _dot(y.astype(BF16), wglu_ref[...]) + bglu_ref[...]
        out = y * jax.nn.sigmoid(z)
        for b in range(4):
            o_ref[b, j * ts:(j + 1) * ts, :] = out[b * ts:(b + 1) * ts]

    nsub = tm // ts
    befores = [ulast_scr[...]] + [last_rows(j) for j in range(nsub - 1)]
    bus = [input_stage(j, befores[j]) for j in range(nsub)]
    ulast_scr[...] = last_rows(nsub - 1)
    x = [(x_scr[s, 0], x_scr[s, 1]) for s in range(S5_SLABS)]
    for j in range(nsub):
        states, x = scan_stage(bus[j], x)
        output_stage(j, states)
    for s in range(S5_SLABS):
        x_scr[s, 0], x_scr[s, 1] = x[s]


def _s5_perms(tm):
    n2 = tm // 2
    rows = SUBLANES * n2
    perm = np.zeros((rows, 4 * tm), np.float32)
    permt = np.zeros((4 * tm, rows), np.float32)
    for t2 in range(n2):
        for b in range(4):
            for par in range(2):
                r = SUBLANES * t2 + 2 * b + par
                t = 2 * t2 + par
                perm[r, b * tm + t] = 1.0
                permt[b * tm + t, r] = 1.0
    return jnp.asarray(perm, BF16), jnp.asarray(permt, BF16)


def _slab_block_diag(blocks):
    g, a, b = blocks.shape
    n = g // S5_SLABS
    eye = jnp.eye(n, dtype=blocks.dtype)
    placed = blocks.reshape(S5_SLABS, n, a, 1, b) * eye[None, :, None, :, None]
    return placed.reshape(S5_SLABS, n * a, n * b)


def _s5_weights(lam_re, lam_im, log_dt, b_re, b_im, c_re, c_im):
    g = lam_re.shape[0]
    dt = jnp.exp(log_dt)[:, None]
    mag = jnp.exp(lam_re * dt)
    lbr = mag * jnp.cos(lam_im * dt)
    lbi = mag * jnp.sin(lam_im * dt)
    nr, ni = lbr - 1.0, lbi
    den = lam_re * lam_re + lam_im * lam_im
    cr = (nr * lam_re + ni * lam_im) / den
    ci = (ni * lam_re - nr * lam_im) / den
    bbr = cr[..., None] * b_re - ci[..., None] * b_im
    bbi = cr[..., None] * b_im + ci[..., None] * b_re
    lr = lbr[..., None] * bbr - lbi[..., None] * bbi
    li = lbr[..., None] * bbi + lbi[..., None] * bbr
    l2r = lbr * lbr - lbi * lbi
    l2i = 2.0 * lbr * lbi
    bd = lambda a: _slab_block_diag(jnp.swapaxes(a, 1, 2))
    wb = jnp.concatenate([jnp.concatenate([bd(bbr), bd(bbi)], axis=2),
                          jnp.concatenate([bd(lr), bd(li)], axis=2)], axis=1).astype(BF16)
    cm = jnp.concatenate([bd(c_re), -bd(c_im)], axis=1).astype(BF16)
    lamr = jnp.broadcast_to(l2r.reshape(1, -1), (SUBLANES, l2r.size))
    lami = jnp.broadcast_to(l2i.reshape(1, -1), (SUBLANES, l2i.size))
    return wb, cm, lamr, lami


def _s5(u, wb, cm, lamr, lami, d_skip, w_glu_b, b_glu):
    bsz, seq, sw = u.shape
    assert bsz == 4, "the scan packs 4 batch rows x 2 token parities into 8 sublanes"
    tm = S5_TILE
    ns = lamr.shape[1]
    perm, permt = _s5_perms(S5_SUBTILE)
    const = lambda a: pl.BlockSpec(a.shape, lambda i: (0,) * a.ndim)
    d_row = d_skip.reshape(1, sw)
    bg = b_glu.reshape(1, sw)
    return pl.pallas_call(
        functools.partial(_s5_kernel, tm=tm, sw=sw),
        grid=(seq // tm,),
        in_specs=[pl.BlockSpec((4, tm, sw), lambda i: (0, i, 0)),
                  const(perm), const(permt), const(wb), const(cm), const(lamr), const(lami),
                  const(d_row), const(w_glu_b), const(bg)],
        out_specs=pl.BlockSpec((4, tm, sw), lambda i: (0, i, 0)),
        out_shape=jax.ShapeDtypeStruct((bsz, seq, sw), F32),
        scratch_shapes=[pltpu.VMEM((S5_SLABS, 2, SUBLANES, ns // S5_SLABS), F32),
                        pltpu.VMEM((SUBLANES, sw), F32)],
        compiler_params=pltpu.CompilerParams(
            dimension_semantics=("arbitrary",),
            vmem_limit_bytes=VMEM_LIMIT_BYTES),
        name="s5",
    )(u, perm, permt, wb, cm, lamr, lami, d_row, w_glu_b, bg)


def _front_kernel(x_ref, mod_ref, g_ref, w_ref, tri_ref, u_ref, ga_ref, gb_ref, at_ref,
                  q_scr, k_scr, v_scr, c_scr, acc_scr, z_scr, w_scr, *, sw, aw, d):
    blk = ATTN_BLOCK
    top = ATTN_TOP_ROWS
    n_pairs = aw // LANES
    nh = 2 * n_pairs
    tm = x_ref.shape[1]
    nsub = tm // blk
    step = pl.program_id(1)
    tile0 = pl.multiple_of(step * tm, tm)
    mod = mod_ref[0]
    even_head = (lax.broadcasted_iota(jnp.int32, (tm, aw), 1) // HEAD_DIM) % 2 == 0

    def qkv(hb):
        o = sw
        q = (_dot(hb, w_ref[:, o:o + aw]) * Q_SCALE).astype(BF16); o += aw
        q_scr[0] = jnp.where(even_head, q, jnp.zeros_like(q))
        q_scr[1] = jnp.where(even_head, jnp.zeros_like(q), q)
        kt = _dot(hb, w_ref[:, o:o + aw]).T.astype(BF16); o += aw
        for j in range(nsub):
            k_scr[nsub * step + j] = kt[:, j * blk:(j + 1) * blk]
        v = _dot(hb, w_ref[:, o:o + aw]).astype(BF16)
        v_scr[0, pl.ds(tile0, tm), :] = jnp.where(even_head, v, jnp.zeros_like(v))
        v_scr[1, pl.ds(tile0, tm), :] = jnp.where(even_head, jnp.zeros_like(v), v)

    def region(units, fillers=()):
        starts = [pl.multiple_of(kb * blk, blk) for kb, _, _, _ in units]
        for u, (_, r0, r1, _) in enumerate(units):
            n = r1 - r0
            for p in range(n_pairs):
                ls = slice(p * LANES, (p + 1) * LANES)
                kblk = k_scr[units[u][0], ls, :]
                zz = _dot(jnp.concatenate([q_scr[0, r0:r1, ls], q_scr[1, r0:r1, ls]], axis=0), kblk)
                z_scr[u * nh + 2 * p, 0:n] = zz[:n]
                z_scr[u * nh + 2 * p + 1, 0:n] = zz[n:]
        fillers = list(fillers)
        every = -(-len(units) // (len(fillers) + 1))
        for u, (_, r0, r1, diag) in enumerate(units):
            n = r1 - r0
            if diag:
                row = lax.broadcasted_iota(jnp.int32, (n, blk), 0) + r0 % blk
                valid = lax.broadcasted_iota(jnp.int32, (n, blk), 1) < row
            zs, sps = [], []
            for h in range(nh):
                z = z_scr[u * nh + h, 0:n]
                sp = jnp.maximum(z, 0.0) + jnp.log2(1.0 + jnp.exp2(-jnp.abs(z)))
                if diag:
                    sp = jnp.where(valid, sp, 0.0)
                zs.append(z)
                sps.append(sp.astype(BF16))
            incl_all = _dot(jnp.concatenate(sps, axis=0), tri_ref[...])
            for h in range(nh):
                z = zs[h]
                incl = incl_all[h * n:(h + 1) * n]
                total = jnp.broadcast_to(incl[:, 0:1], (n, blk))
                if diag:
                    w = jnp.where(valid, jnp.exp2(z - incl), 0.0)
                    c_scr[h, r0:r1] = total
                else:
                    c = c_scr[h, r0:r1]
                    w = jnp.exp2(z - incl - c)
                    c_scr[h, r0:r1] = c + total
                w_scr[u * nh + h, 0:n] = w.astype(BF16)
            if fillers and (u + 1) % every == 0:
                fillers.pop(0)()
        for filler in fillers:
            filler()
        for r0, r1 in dict.fromkeys((r0, r1) for _, r0, r1, _ in units):
            us = [u for u, (_, a, b, _) in enumerate(units) if (a, b) == (r0, r1)]
            for p in range(n_pairs):
                ls = slice(p * LANES, (p + 1) * LANES)
                ww = jnp.concatenate([w_scr[u * nh + 2 * p + hh, 0:r1 - r0]
                                      for u in us for hh in range(2)], axis=1)
                vv = jnp.concatenate([v_scr[hh, pl.ds(starts[u], blk), ls]
                                      for u in us for hh in range(2)], axis=0)
                if any(units[u][3] for u in us):
                    acc_scr[p, r0:r1] = _dot(ww, vv)
                else:
                    acc_scr[p, r0:r1] = acc_scr[p, r0:r1] + _dot(ww, vv)

    def c_min(r0, r1):
        m = c_scr[0, r0:r1]
        for h in range(1, nh):
            m = jnp.minimum(m, c_scr[h, r0:r1])
        return jnp.min(m)

    def head_units(sub, qi, n_prev):
        base = sub * blk
        units = [(qi, base, base + blk, True)]
        if n_prev >= 1:
            units.append((qi - 1, base, base + blk, False))
        if n_prev >= 2:
            units.append((qi - 2, base, base + top, False))
        return units

    def tile(units):
        h = _rms(x_ref[0]) * g_ref[...]
        hb = (h * (1.0 + mod[1:2]) + mod[0:1]).astype(BF16)
        qkv(hb)
        o = sw + 3 * aw

        def chunk(ref, col, w0):
            def run():
                ref[0, :, col:col + FRONT_CHUNK] = _dot(hb, w_ref[:, w0 + col:w0 + col + FRONT_CHUNK])
            return run

        region(units, [chunk(ref, col, w0)
                       for ref, w0, width in ((ga_ref, o, d), (gb_ref, o + d, d), (u_ref, 0, sw))
                       for col in range(0, width, FRONT_CHUNK)])

    @pl.when(step >= 1)
    def _():
        tile([u for sub in range(nsub) for u in head_units(sub, nsub * step + sub, 2)])

    @pl.when(step == 0)
    def _():
        tile([u for sub in range(nsub) for u in head_units(sub, sub, min(sub, 2))])

    def sweep(first_kb, cmin, r0, r1):
        def more(carry):
            kb, cmin = carry
            return jnp.logical_and(kb >= 0, cmin < UNDERFLOW_LOG2)

        def body(carry):
            kb, _ = carry
            region([(kb, r0, r1, False)])
            return kb - 1, c_min(r0, r1)

        lax.while_loop(more, body, (first_kb, cmin))

    tails = []
    for sub in range(nsub):
        qi = nsub * step + sub
        base = sub * blk
        tails.append((jnp.where(qi >= 2, qi - 3, -1), base, base + top))
        tails.append((jnp.where(qi >= 2, qi - 2, -1), base + top, base + blk))
    cmins = [c_min(r0, r1) for _, r0, r1 in tails]
    for (first_kb, r0, r1), cmin in zip(tails, cmins):
        sweep(first_kb, cmin, r0, r1)
    for p in range(n_pairs):
        at_ref[0, :, p * LANES:(p + 1) * LANES] = acc_scr[p].astype(at_ref.dtype)


def _attn_tri():
    blk = ATTN_BLOCK
    m = np.arange(blk)[:, None]
    j = np.arange(blk)[None, :]
    return jnp.asarray((m >= j).astype(np.float32), BF16)


def _front(x, mod, norm_g, w_in_b, sw, aw):
    bsz, seq, d = x.shape
    tm = FRONT_TILE
    blk = ATTN_BLOCK
    n = w_in_b.shape[1]
    n_pairs = aw // LANES
    nz = ATTN_REGION * (tm // blk) * 2 * n_pairs
    tri = _attn_tri()
    tok = lambda w: pl.BlockSpec((1, tm, w), lambda b, i: (b, i, 0))
    const = lambda a: pl.BlockSpec(a.shape, lambda b, i: (0,) * a.ndim,
                                   pipeline_mode=pl.Buffered(1))
    return pl.pallas_call(
        functools.partial(_front_kernel, sw=sw, aw=aw, d=d),
        grid=(bsz, seq // tm),
        in_specs=[tok(d),
                  pl.BlockSpec((1, N_ADA, d), lambda b, i: (b, 0, 0)),
                  const(norm_g), const(w_in_b), const(tri)],
        out_specs=[tok(sw), tok(d), tok(d), tok(aw)],
        out_shape=[jax.ShapeDtypeStruct((bsz, seq, sw), F32),
                   jax.ShapeDtypeStruct((bsz, seq, d), F32),
                   jax.ShapeDtypeStruct((bsz, seq, d), F32),
                   jax.ShapeDtypeStruct((bsz, seq, aw), BF16)],
        scratch_shapes=[pltpu.VMEM((2, tm, aw), BF16),
                        pltpu.VMEM((seq // blk, aw, blk), BF16),
                        pltpu.VMEM((2, seq, aw), BF16),
                        pltpu.VMEM((2 * n_pairs, tm, blk), F32),
                        pltpu.VMEM((n_pairs, tm, LANES), F32),
                        pltpu.VMEM((nz, blk, blk), F32),
                        pltpu.VMEM((nz, blk, blk), BF16)],
        compiler_params=pltpu.CompilerParams(
            dimension_semantics=("arbitrary", "arbitrary"),
            vmem_limit_bytes=VMEM_LIMIT_BYTES),
        name="front",
    )(x, mod, norm_g, w_in_b, tri)


def _out_ffn_kernel(x_ref, s5_ref, at_ref, ga_ref, gb_ref, mod_ref, n2_ref, nf_ref,
                    wa_ref, wb_ref, wo_ref, wg_ref, wu_ref, wd_ref, o_ref, *, final_norm):
    mod = mod_ref[0]
    tm = x_ref.shape[1]
    groups = [slice(r, r + tm // OUT_GROUPS) for r in range(0, tm, tm // OUT_GROUPS)]
    ms = []
    for g in groups:
        ya = _dot(s5_ref[0, g].astype(BF16), wa_ref[...])
        yb = _dot(at_ref[0, g], wb_ref[...])
        m = jax.nn.sigmoid(ga_ref[0, g]) * ya + jax.nn.sigmoid(gb_ref[0, g]) * yb
        ms.append(m.astype(BF16))
    x1s, hs = [], []
    for g, m in zip(groups, ms):
        x1 = x_ref[0, g] + mod[2:3] * _dot(m, wo_ref[...])
        h = _rms(x1) * n2_ref[...]
        x1s.append(x1)
        hs.append((h * (1.0 + mod[4:5]) + mod[3:4]).astype(BF16))
    acts = []
    for h in hs:
        gate = _dot(h, wg_ref[...])
        up = _dot(h, wu_ref[...])
        acts.append((gate * jax.nn.sigmoid(gate) * up).astype(BF16))
    for g, x1, act in zip(groups, x1s, acts):
        x2 = x1 + mod[5:6] * _dot(act, wd_ref[...])
        o_ref[0, g] = _rms(x2) * nf_ref[...] if final_norm else x2


def _out_ffn(x, s5o, attn, ga, gb, mod, n2g, nfg, wa, wb, wo, wg, wu, wd, final_norm):
    bsz, seq, d = x.shape
    tm = OUT_TILE
    tok = lambda a: pl.BlockSpec((1, tm, a.shape[-1]), lambda b, i: (b, i, 0))
    const = lambda a: pl.BlockSpec(a.shape, lambda b, i: (0,) * a.ndim,
                                   pipeline_mode=pl.Buffered(1))
    return pl.pallas_call(
        functools.partial(_out_ffn_kernel, final_norm=final_norm),
        grid=(bsz, seq // tm),
        in_specs=[tok(x), tok(s5o), tok(attn), tok(ga), tok(gb),
                  pl.BlockSpec((1, N_ADA, d), lambda b, i: (b, 0, 0)),
                  const(n2g), const(nfg),
                  const(wa), const(wb), const(wo), const(wg), const(wu), const(wd)],
        out_specs=pl.BlockSpec((1, tm, d), lambda b, i: (b, i, 0)),
        out_shape=jax.ShapeDtypeStruct((bsz, seq, d), F32),
        compiler_params=pltpu.CompilerParams(
            dimension_semantics=("arbitrary", "arbitrary"),
            vmem_limit_bytes=VMEM_LIMIT_BYTES),
        name="out_ffn",
    )(x, s5o, attn, ga, gb, mod, n2g, nfg, wa, wb, wo, wg, wu, wd)


def kernel(x, c, w_ada, b_ada, norm1_g, w_in, lam_re, lam_im, log_dt, b_re, b_im, c_re, c_im,
           d_skip, w_glu, b_glu, w_a, w_b, w_o, norm2_g, w_ffn_gate, w_ffn_up, w_ffn_down,
           norm_f_g):
    depth = w_ada.shape[0]
    bsz, seq, d = x.shape
    sw = w_glu.shape[1]
    aw = w_b.shape[1]
    for l in range(depth):
        mod = _ada(c, w_ada[l], b_ada[l]).reshape(bsz, N_ADA, d)
        u, ga, gb, attn = _front(x, mod, norm1_g[l].reshape(1, d), w_in[l].astype(BF16), sw, aw)
        wb, cm, lamr, lami = _s5_weights(lam_re[l], lam_im[l], log_dt[l], b_re[l], b_im[l],
                                         c_re[l], c_im[l])
        s5o = _s5(u, wb, cm, lamr, lami, d_skip[l], w_glu[l].astype(BF16), b_glu[l])
        x = _out_ffn(x, s5o, attn, ga, gb, mod, norm2_g[l].reshape(1, d), norm_f_g.reshape(1, d),
                     w_a[l].astype(BF16), w_b[l].astype(BF16), w_o[l].astype(BF16),
                     w_ffn_gate[l].astype(BF16), w_ffn_up[l].astype(BF16),
                     w_ffn_down[l].astype(BF16), final_norm=(l == depth - 1))
    return x
```

```python
import functools
import math

import numpy as np
import jax
import jax.numpy as jnp
from jax import lax
from jax.experimental import pallas as pl
from jax.experimental.pallas import tpu as pltpu

F32 = jnp.float32
BF16 = jnp.bfloat16

S5_GROUP = 16
S5_STATE = 64
HEAD_DIM = 64
N_ADA = 6
RMS_EPS = 1e-6
Q_SCALE = math.log2(math.e) / math.sqrt(HEAD_DIM)
UNDERFLOW_LOG2 = 151.0

LANES = 128
SUBLANES = 8
VMEM_LIMIT_BYTES = 56 * 1024 * 1024

ATTN_BLOCK = 128
ATTN_REGION = 3
ATTN_TOP_ROWS = 32
S5_TILE = 512
S5_SUBTILE = 128
S5_SLABS = 4
FRONT_TILE = 512
FRONT_CHUNK = 512
OUT_TILE = 512
OUT_GROUPS = 2


def _dot(a, b):
    return jnp.dot(a, b, preferred_element_type=F32)


def _rms(x):
    return x * lax.rsqrt(jnp.mean(x * x, axis=-1, keepdims=True) + RMS_EPS)


def _ada_kernel(c_ref, w_ref, b_ref, o_ref):
    c = c_ref[...]
    bsz = c.shape[0]
    cond = c * jax.nn.sigmoid(c)
    pad = -bsz % SUBLANES
    if pad:
        cond = jnp.concatenate([cond, jnp.zeros((pad, c.shape[1]), F32)], axis=0)
    o_ref[...] = _dot(cond.astype(BF16), w_ref[...].astype(BF16))[:bsz] + b_ref[...]


def _ada(c, w_ada, b_ada):
    bsz, d = c.shape
    n = w_ada.shape[1]
    tn = 1536
    return pl.pallas_call(
        _ada_kernel,
        grid=(n // tn,),
        in_specs=[pl.BlockSpec((bsz, d), lambda j: (0, 0)),
                  pl.BlockSpec((d, tn), lambda j: (0, j)),
                  pl.BlockSpec((1, tn), lambda j: (0, j))],
        out_specs=pl.BlockSpec((bsz, tn), lambda j: (0, j)),
        out_shape=jax.ShapeDtypeStruct((bsz, n), F32),
        name="ada",
    )(c, w_ada, b_ada.reshape(1, n))


def _s5_kernel(u_ref, perm_ref, permt_ref, wb_ref, cm_ref, lamr_ref, lami_ref, d_ref,
               wglu_ref, bglu_ref, o_ref, x_scr, ulast_scr, *, tm, sw):
    i = pl.program_id(0)
    ts = S5_SUBTILE
    n2 = ts // 2
    rows = SUBLANES * n2
    cw = sw // S5_SLABS
    hs = cw * S5_STATE // S5_GROUP

    @pl.when(i == 0)
    def _():
        x_scr[...] = jnp.zeros_like(x_scr)
        ulast_scr[...] = jnp.zeros_like(ulast_scr)

    sub8 = lax.broadcasted_iota(jnp.int32, (SUBLANES, sw), 0)
    odd = (lax.broadcasted_iota(jnp.int32, (rows, sw), 0) & 1) == 1

    def natural(j):
        return jnp.concatenate([u_ref[b, j * ts:(j + 1) * ts, :] for b in range(4)], axis=0)

    def last_rows(j):
        last = jnp.zeros((SUBLANES, sw), F32)
        for b in range(4):
            row = u_ref[b, (j + 1) * ts - 1:(j + 1) * ts, :].astype(BF16).astype(F32)
            last = jnp.where(sub8 == 2 * b, jnp.broadcast_to(row, (SUBLANES, sw)), last)
        return last

    def input_stage(j, before):
        a_cur = _dot(perm_ref[...], natural(j).astype(BF16))
        a_prev = jnp.where(odd, pltpu.roll(a_cur, 1, 0), pltpu.roll(a_cur, SUBLANES - 1, 0))
        first = jnp.where((sub8 & 1) == 1, a_prev[:SUBLANES], before)
        a_prev = jnp.concatenate([first, a_prev[SUBLANES:]], axis=0).astype(BF16)
        a_cur = a_cur.astype(BF16)
        return [_dot(jnp.concatenate([a_cur[:, s * cw:(s + 1) * cw],
                                      a_prev[:, s * cw:(s + 1) * cw]], axis=1), wb_ref[s])
                for s in range(S5_SLABS)]

    def scan_stage(bus, x):
        states, x_out = [], []
        for s in range(S5_SLABS):
            ar = lamr_ref[:, hs * s:hs * (s + 1)]
            ai = lami_ref[:, hs * s:hs * (s + 1)]
            xr, xi = x[s]
            st = []
            for t2 in range(n2):
                rs = slice(SUBLANES * t2, SUBLANES * (t2 + 1))
                xr, xi = (ar * xr - ai * xi + bus[s][rs, :hs],
                          ar * xi + ai * xr + bus[s][rs, hs:])
                st.append(jnp.concatenate([xr, xi], axis=1))
            x_out.append((xr, xi))
            states.append(jnp.concatenate(st, axis=0).astype(BF16))
        return states, x_out

    def output_stage(j, states):
        y_il = jnp.concatenate([_dot(states[s], cm_ref[s]) for s in range(S5_SLABS)], axis=1)
        y = _dot(permt_ref[...], y_il.astype(BF16))
        y = y + d_ref[...] * natural(j)
        y = jax.nn.gelu(y)
        z = _dot(y.astype(BF16), wglu_ref[...]) + bglu_ref[...]
        out = y * jax.nn.sigmoid(z)
        for b in range(4):
            o_ref[b, j * ts:(j + 1) * ts, :] = out[b * ts:(b + 1) * ts]

    nsub = tm // ts
    befores = [ulast_scr[...]] + [last_rows(j) for j in range(nsub - 1)]
    bus = [input_stage(j, befores[j]) for j in range(nsub)]
    ulast_scr[...] = last_rows(nsub - 1)
    x = [(x_scr[s, 0], x_scr[s, 1]) for s in range(S5_SLABS)]
    for j in range(nsub):
        states, x = scan_stage(bus[j], x)
        output_stage(j, states)
    for s in range(S5_SLABS):
        x_scr[s, 0], x_scr[s, 1] = x[s]


def _s5_perms(tm):
    n2 = tm // 2
    rows = SUBLANES * n2
    perm = np.zeros((rows, 4 * tm), np.float32)
    permt = np.zeros((4 * tm, rows), np.float32)
    for t2 in range(n2):
        for b in range(4):
            for par in range(2):
                r = SUBLANES * t2 + 2 * b + par
                t = 2 * t2 + par
                perm[r, b * tm + t] = 1.0
                permt[b * tm + t, r] = 1.0
    return jnp.asarray(perm, BF16), jnp.asarray(permt, BF16)


def _slab_block_diag(blocks):
    g, a, b = blocks.shape
    n = g // S5_SLABS
    eye = jnp.eye(n, dtype=blocks.dtype)
    placed = blocks.reshape(S5_SLABS, n, a, 1, b) * eye[None, :, None, :, None]
    return placed.reshape(S5_SLABS, n * a, n * b)


def _s5_weights(lam_re, lam_im, log_dt, b_re, b_im, c_re, c_im):
    g = lam_re.shape[0]
    dt = jnp.exp(log_dt)[:, None]
    mag = jnp.exp(lam_re * dt)
    lbr = mag * jnp.cos(lam_im * dt)
    lbi = mag * jnp.sin(lam_im * dt)
    nr, ni = lbr - 1.0, lbi
    den = lam_re * lam_re + lam_im * lam_im
    cr = (nr * lam_re + ni * lam_im) / den
    ci = (ni * lam_re - nr * lam_im) / den
    bbr = cr[..., None] * b_re - ci[..., None] * b_im
    bbi = cr[..., None] * b_im + ci[..., None] * b_re
    lr = lbr[..., None] * bbr - lbi[..., None] * bbi
    li = lbr[..., None] * bbi + lbi[..., None] * bbr
    l2r = lbr * lbr - lbi * lbi
    l2i = 2.0 * lbr * lbi
    bd = lambda a: _slab_block_diag(jnp.swapaxes(a, 1, 2))
    wb = jnp.concatenate([jnp.concatenate([bd(bbr), bd(bbi)], axis=2),
                          jnp.concatenate([bd(lr), bd(li)], axis=2)], axis=1).astype(BF16)
    cm = jnp.concatenate([bd(c_re), -bd(c_im)], axis=1).astype(BF16)
    lamr = jnp.broadcast_to(l2r.reshape(1, -1), (SUBLANES, l2r.size))
    lami = jnp.broadcast_to(l2i.reshape(1, -1), (SUBLANES, l2i.size))
    return wb, cm, lamr, lami


def _s5(u, wb, cm, lamr, lami, d_skip, w_glu_b, b_glu):
    bsz, seq, sw = u.shape
    assert bsz == 4, "the scan packs 4 batch rows x 2 token parities into 8 sublanes"
    tm = S5_TILE
    ns = lamr.shape[1]
    perm, permt = _s5_perms(S5_SUBTILE)
    const = lambda a: pl.BlockSpec(a.shape, lambda i: (0,) * a.ndim)
    d_row = d_skip.reshape(1, sw)
    bg = b_glu.reshape(1, sw)
    return pl.pallas_call(
        functools.partial(_s5_kernel, tm=tm, sw=sw),
        grid=(seq // tm,),
        in_specs=[pl.BlockSpec((4, tm, sw), lambda i: (0, i, 0)),
                  const(perm), const(permt), const(wb), const(cm), const(lamr), const(lami),
                  const(d_row), const(w_glu_b), const(bg)],
        out_specs=pl.BlockSpec((4, tm, sw), lambda i: (0, i, 0)),
        out_shape=jax.ShapeDtypeStruct((bsz, seq, sw), F32),
        scratch_shapes=[pltpu.VMEM((S5_SLABS, 2, SUBLANES, ns // S5_SLABS), F32),
                        pltpu.VMEM((SUBLANES, sw), F32)],
        compiler_params=pltpu.CompilerParams(
            dimension_semantics=("arbitrary",),
            vmem_limit_bytes=VMEM_LIMIT_BYTES),
        name="s5",
    )(u, perm, permt, wb, cm, lamr, lami, d_row, w_glu_b, bg)


def _front_kernel(x_ref, mod_ref, g_ref, w_ref, tri_ref, u_ref, ga_ref, gb_ref, at_ref,
                  q_scr, k_scr, v_scr, c_scr, acc_scr, z_scr, w_scr, *, sw, aw, d):
    blk = ATTN_BLOCK
    top = ATTN_TOP_ROWS
    n_pairs = aw // LANES
    nh = 2 * n_pairs
    tm = x_ref.shape[1]
    nsub = tm // blk
    step = pl.program_id(1)
    tile0 = pl.multiple_of(step * tm, tm)
    mod = mod_ref[0]
    even_head = (lax.broadcasted_iota(jnp.int32, (tm, aw), 1) // HEAD_DIM) % 2 == 0

    def qkv(hb):
        o = sw
        q = (_dot(hb, w_ref[:, o:o + aw]) * Q_SCALE).astype(BF16); o += aw
        q_scr[0] = jnp.where(even_head, q, jnp.zeros_like(q))
        q_scr[1] = jnp.where(even_head, jnp.zeros_like(q), q)
        kt = _dot(hb, w_ref[:, o:o + aw]).T.astype(BF16); o += aw
        for j in range(nsub):
            k_scr[nsub * step + j] = kt[:, j * blk:(j + 1) * blk]
        v = _dot(hb, w_ref[:, o:o + aw]).astype(BF16)
        v_scr[0, pl.ds(tile0, tm), :] = jnp.where(even_head, v, jnp.zeros_like(v))
        v_scr[1, pl.ds(tile0, tm), :] = jnp.where(even_head, jnp.zeros_like(v), v)

    def region(units, fillers=()):
        starts = [pl.multiple_of(kb * blk, blk) for kb, _, _, _ in units]
        for u, (_, r0, r1, _) in enumerate(units):
            n = r1 - r0
            for p in range(n_pairs):
                ls = slice(p * LANES, (p + 1) * LANES)
                kblk = k_scr[units[u][0], ls, :]
                zz = _dot(jnp.concatenate([q_scr[0, r0:r1, ls], q_scr[1, r0:r1, ls]], axis=0), kblk)
                z_scr[u * nh + 2 * p, 0:n] = zz[:n]
                z_scr[u * nh + 2 * p + 1, 0:n] = zz[n:]
        fillers = list(fillers)
        every = -(-len(units) // (len(fillers) + 1))
        for u, (_, r0, r1, diag) in enumerate(units):
            n = r1 - r0
            if diag:
                row = lax.broadcasted_iota(jnp.int32, (n, blk), 0) + r0 % blk
                valid = lax.broadcasted_iota(jnp.int32, (n, blk), 1) < row
            zs, sps = [], []
            for h in range(nh):
                z = z_scr[u * nh + h, 0:n]
                sp = jnp.maximum(z, 0.0) + jnp.log2(1.0 + jnp.exp2(-jnp.abs(z)))
                if diag:
                    sp = jnp.where(valid, sp, 0.0)
                zs.append(z)
                sps.append(sp.astype(BF16))
            incl_all = _dot(jnp.concatenate(sps, axis=0), tri_ref[...])
            for h in range(nh):
                z = zs[h]
                incl = incl_all[h * n:(h + 1) * n]
                total = jnp.broadcast_to(incl[:, 0:1], (n, blk))
                if diag:
                    w = jnp.where(valid, jnp.exp2(z - incl), 0.0)
                    c_scr[h, r0:r1] = total
                else:
                    c = c_scr[h, r0:r1]
                    w = jnp.exp2(z - incl - c)
                    c_scr[h, r0:r1] = c + total
                w_scr[u * nh + h, 0:n] = w.astype(BF16)
            if fillers and (u + 1) % every == 0:
                fillers.pop(0)()
        for filler in fillers:
            filler()
        for r0, r1 in dict.fromkeys((r0, r1) for _, r0, r1, _ in units):
            us = [u for u, (_, a, b, _) in enumerate(units) if (a, b) == (r0, r1)]
            for p in range(n_pairs):
                ls = slice(p * LANES, (p + 1) * LANES)
                ww = jnp.concatenate([w_scr[u * nh + 2 * p + hh, 0:r1 - r0]
                                      for u in us for hh in range(2)], axis=1)
                vv = jnp.concatenate([v_scr[hh, pl.ds(starts[u], blk), ls]
                                      for u in us for hh in range(2)], axis=0)
                if any(units[u][3] for u in us):
                    acc_scr[p, r0:r1] = _dot(ww, vv)
                else:
                    acc_scr[p, r0:r1] = acc_scr[p, r0:r1] + _dot(ww, vv)

    def c_min(r0, r1):
        m = c_scr[0, r0:r1]
        for h in range(1, nh):
            m = jnp.minimum(m, c_scr[h, r0:r1])
        return jnp.min(m)

    def head_units(sub, qi, n_prev):
        base = sub * blk
        units = [(qi, base, base + blk, True)]
        if n_prev >= 1:
            units.append((qi - 1, base, base + blk, False))
        if n_prev >= 2:
            units.append((qi - 2, base, base + top, False))
        return units

    def tile(units):
        h = _rms(x_ref[0]) * g_ref[...]
        hb = (h * (1.0 + mod[1:2]) + mod[0:1]).astype(BF16)
        qkv(hb)
        o = sw + 3 * aw

        def chunk(ref, col, w0):
            def run():
                ref[0, :, col:col + FRONT_CHUNK] = _dot(hb, w_ref[:, w0 + col:w0 + col + FRONT_CHUNK])
            return run

        region(units, [chunk(ref, col, w0)
                       for ref, w0, width in ((ga_ref, o, d), (gb_ref, o + d, d), (u_ref, 0, sw))
                       for col in range(0, width, FRONT_CHUNK)])

    @pl.when(step >= 1)
    def _():
        tile([u for sub in range(nsub) for u in head_units(sub, nsub * step + sub, 2)])

    @pl.when(step == 0)
    def _():
        tile([u for sub in range(nsub) for u in head_units(sub, sub, min(sub, 2))])

    def sweep(first_kb, cmin, r0, r1):
        def more(carry):
            kb, cmin = carry
            return jnp.logical_and(kb >= 0, cmin < UNDERFLOW_LOG2)

        def body(carry):
            kb, _ = carry
            region([(kb, r0, r1, False)])
            return kb - 1, c_min(r0, r1)

        lax.while_loop(more, body, (first_kb, cmin))

    tails = []
    for sub in range(nsub):
        qi = nsub * step + sub
        base = sub * blk
        tails.append((jnp.where(qi >= 2, qi - 3, -1), base, base + top))
        tails.append((jnp.where(qi >= 2, qi - 2, -1), base + top, base + blk))
    cmins = [c_min(r0, r1) for _, r0, r1 in tails]
    for (first_kb, r0, r1), cmin in zip(tails, cmins):
        sweep(first_kb, cmin, r0, r1)
    for p in range(n_pairs):
        at_ref[0, :, p * LANES:(p + 1) * LANES] = acc_scr[p].astype(at_ref.dtype)


def _attn_tri():
    blk = ATTN_BLOCK
    m = np.arange(blk)[:, None]
    j = np.arange(blk)[None, :]
    return jnp.asarray((m >= j).astype(np.float32), BF16)


def _front(x, mod, norm_g, w_in_b, sw, aw):
    bsz, seq, d = x.shape
    tm = FRONT_TILE
    blk = ATTN_BLOCK
    n = w_in_b.shape[1]
    n_pairs = aw // LANES
    nz = ATTN_REGION * (tm // blk) * 2 * n_pairs
    tri = _attn_tri()
    tok = lambda w: pl.BlockSpec((1, tm, w), lambda b, i: (b, i, 0))
    const = lambda a: pl.BlockSpec(a.shape, lambda b, i: (0,) * a.ndim,
                                   pipeline_mode=pl.Buffered(1))
    return pl.pallas_call(
        functools.partial(_front_kernel, sw=sw, aw=aw, d=d),
        grid=(bsz, seq // tm),
        in_specs=[tok(d),
                  pl.BlockSpec((1, N_ADA, d), lambda b, i: (b, 0, 0)),
                  const(norm_g), const(w_in_b), const(tri)],
        out_specs=[tok(sw), tok(d), tok(d), tok(aw)],
        out_shape=[jax.ShapeDtypeStruct((bsz, seq, sw), F32),
                   jax.ShapeDtypeStruct((bsz, seq, d), F32),
                   jax.ShapeDtypeStruct((bsz, seq, d), F32),
                   jax.ShapeDtypeStruct((bsz, seq, aw), BF16)],
        scratch_shapes=[pltpu.VMEM((2, tm, aw), BF16),
                        pltpu.VMEM((seq // blk, aw, blk), BF16),
                        pltpu.VMEM((2, seq, aw), BF16),
                        pltpu.VMEM((2 * n_pairs, tm, blk), F32),
                        pltpu.VMEM((n_pairs, tm, LANES), F32),
                        pltpu.VMEM((nz, blk, blk), F32),
                        pltpu.VMEM((nz, blk, blk), BF16)],
        compiler_params=pltpu.CompilerParams(
            dimension_semantics=("arbitrary", "arbitrary"),
            vmem_limit_bytes=VMEM_LIMIT_BYTES),
        name="front",
    )(x, mod, norm_g, w_in_b, tri)


def _out_ffn_kernel(x_ref, s5_ref, at_ref, ga_ref, gb_ref, mod_ref, n2_ref, nf_ref,
                    wa_ref, wb_ref, wo_ref, wg_ref, wu_ref, wd_ref, o_ref, *, final_norm):
    mod = mod_ref[0]
    tm = x_ref.shape[1]
    groups = [slice(r, r + tm // OUT_GROUPS) for r in range(0, tm, tm // OUT_GROUPS)]
    ms = []
    for g in groups:
        ya = _dot(s5_ref[0, g].astype(BF16), wa_ref[...])
        yb = _dot(at_ref[0, g], wb_ref[...])
        m = jax.nn.sigmoid(ga_ref[0, g]) * ya + jax.nn.sigmoid(gb_ref[0, g]) * yb
        ms.append(m.astype(BF16))
    x1s, hs = [], []
    for g, m in zip(groups, ms):
        x1 = x_ref[0, g] + mod[2:3] * _dot(m, wo_ref[...])
        h = _rms(x1) * n2_ref[...]
        x1s.append(x1)
        hs.append((h * (1.0 + mod[4:5]) + mod[3:4]).astype(BF16))
    acts = []
    for h in hs:
        gate = _dot(h, wg_ref[...])
        up = _dot(h, wu_ref[...])
        acts.append((gate * jax.nn.sigmoid(gate) * up).astype(BF16))
    for g, x1, act in zip(groups, x1s, acts):
        x2 = x1 + mod[5:6] * _dot(act, wd_ref[...])
        o_ref[0, g] = _rms(x2) * nf_ref[...] if final_norm else x2


def _out_ffn(x, s5o, attn, ga, gb, mod, n2g, nfg, wa, wb, wo, wg, wu, wd, final_norm):
    bsz, seq, d = x.shape
    tm = OUT_TILE
    tok = lambda a: pl.BlockSpec((1, tm, a.shape[-1]), lambda b, i: (b, i, 0))
    const = lambda a: pl.BlockSpec(a.shape, lambda b, i: (0,) * a.ndim,
                                   pipeline_mode=pl.Buffered(1))
    return pl.pallas_call(
        functools.partial(_out_ffn_kernel, final_norm=final_norm),
        grid=(bsz, seq // tm),
        in_specs=[tok(x), tok(s5o), tok(attn), tok(ga), tok(gb),
                  pl.BlockSpec((1, N_ADA, d), lambda b, i: (b, 0, 0)),
                  const(n2g), const(nfg),
                  const(wa), const(wb), const(wo), const(wg), const(wu), const(wd)],
        out_specs=pl.BlockSpec((1, tm, d), lambda b, i: (b, i, 0)),
        out_shape=jax.ShapeDtypeStruct((bsz, seq, d), F32),
        compiler_params=pltpu.CompilerParams(
            dimension_semantics=("arbitrary", "arbitrary"),
            vmem_limit_bytes=VMEM_LIMIT_BYTES),
        name="out_ffn",
    )(x, s5o, attn, ga, gb, mod, n2g, nfg, wa, wb, wo, wg, wu, wd)


def kernel(x, c, w_ada, b_ada, norm1_g, w_in, lam_re, lam_im, log_dt, b_re, b_im, c_re, c_im,
           d_skip, w_glu, b_glu, w_a, w_b, w_o, norm2_g, w_ffn_gate, w_ffn_up, w_ffn_down,
           norm_f_g):
    depth = w_ada.shape[0]
    bsz, seq, d = x.shape
    sw = w_glu.shape[1]
    aw = w_b.shape[1]
    for l in range(depth):
        mod = _ada(c, w_ada[l], b_ada[l]).reshape(bsz, N_ADA, d)
        u, ga, gb, attn = _front(x, mod, norm1_g[l].reshape(1, d), w_in[l].astype(BF16), sw, aw)
        wb, cm, lamr, lami = _s5_weights(lam_re[l], lam_im[l], log_dt[l], b_re[l], b_im[l],
                                         c_re[l], c_im[l])
        s5o = _s5(u, wb, cm, lamr, lami, d_skip[l], w_glu[l].astype(BF16), b_glu[l])
        x = _out_ffn(x, s5o, attn, ga, gb, mod, norm2_g[l].reshape(1, d), norm_f_g.reshape(1, d),
                     w_a[l].astype(BF16), w_b[l].astype(BF16), w_o[l].astype(BF16),
                     w_ffn_gate[l].astype(BF16), w_ffn_up[l].astype(BF16),
                     w_ffn_down[l].astype(BF16), final_norm=(l == depth - 1))
    return x
```

```python
import functools
import math

import numpy as np
import jax
import jax.numpy as jnp
from jax import lax
from jax.experimental import pallas as pl
from jax.experimental.pallas import tpu as pltpu

F32 = jnp.float32
BF16 = jnp.bfloat16

S5_GROUP = 16
S5_STATE = 64
HEAD_DIM = 64
N_ADA = 6
RMS_EPS = 1e-6
Q_SCALE = math.log2(math.e) / math.sqrt(HEAD_DIM)
UNDERFLOW_LOG2 = 151.0

LANES = 128
SUBLANES = 8
VMEM_LIMIT_BYTES = 56 * 1024 * 1024

ATTN_BLOCK = 128
ATTN_REGION = 3
ATTN_TOP_ROWS = 32
S5_TILE = 512
S5_SUBTILE = 128
S5_SLABS = 4
FRONT_TILE = 512
FRONT_CHUNK = 512
OUT_TILE = 512
OUT_GROUPS = 2


def _dot(a, b):
    return jnp.dot(a, b, preferred_element_type=F32)


def _rms(x):
    return x * lax.rsqrt(jnp.mean(x * x, axis=-1, keepdims=True) + RMS_EPS)


def _ada_kernel(c_ref, w_ref, b_ref, o_ref):
    c = c_ref[...]
    bsz = c.shape[0]
    cond = c * jax.nn.sigmoid(c)
    pad = -bsz % SUBLANES
    if pad:
        cond = jnp.concatenate([cond, jnp.zeros((pad, c.shape[1]), F32)], axis=0)
    o_ref[...] = _dot(cond.astype(BF16), w_ref[...].astype(BF16))[:bsz] + b_ref[...]


def _ada(c, w_ada, b_ada):
    bsz, d = c.shape
    n = w_ada.shape[1]
    tn = 1536
    return pl.pallas_call(
        _ada_kernel,
        grid=(n // tn,),
        in_specs=[pl.BlockSpec((bsz, d), lambda j: (0, 0)),
                  pl.BlockSpec((d, tn), lambda j: (0, j)),
                  pl.BlockSpec((1, tn), lambda j: (0, j))],
        out_specs=pl.BlockSpec((bsz, tn), lambda j: (0, j)),
        out_shape=jax.ShapeDtypeStruct((bsz, n), F32),
        name="ada",
    )(c, w_ada, b_ada.reshape(1, n))


def _s5_kernel(u_ref, perm_ref, permt_ref, wb_ref, cm_ref, lamr_ref, lami_ref, d_ref,
               wglu_ref, bglu_ref, o_ref, x_scr, ulast_scr, *, tm, sw):
    i = pl.program_id(0)
    ts = S5_SUBTILE
    n2 = ts // 2
    rows = SUBLANES * n2
    cw = sw // S5_SLABS
    hs = cw * S5_STATE // S5_GROUP

    @pl.when(i == 0)
    def _():
        x_scr[...] = jnp.zeros_like(x_scr)
        ulast_scr[...] = jnp.zeros_like(ulast_scr)

    sub8 = lax.broadcasted_iota(jnp.int32, (SUBLANES, sw), 0)
    odd = (lax.broadcasted_iota(jnp.int32, (rows, sw), 0) & 1) == 1

    def natural(j):
        return jnp.concatenate([u_ref[b, j * ts:(j + 1) * ts, :] for b in range(4)], axis=0)

    def last_rows(j):
        last = jnp.zeros((SUBLANES, sw), F32)
        for b in range(4):
            row = u_ref[b, (j + 1) * ts - 1:(j + 1) * ts, :].astype(BF16).astype(F32)
            last = jnp.where(sub8 == 2 * b, jnp.broadcast_to(row, (SUBLANES, sw)), last)
        return last

    def input_stage(j, before):
        a_cur = _dot(perm_ref[...], natural(j).astype(BF16))
        a_prev = jnp.where(odd, pltpu.roll(a_cur, 1, 0), pltpu.roll(a_cur, SUBLANES - 1, 0))
        first = jnp.where((sub8 & 1) == 1, a_prev[:SUBLANES], before)
        a_prev = jnp.concatenate([first, a_prev[SUBLANES:]], axis=0).astype(BF16)
        a_cur = a_cur.astype(BF16)
        return [_dot(jnp.concatenate([a_cur[:, s * cw:(s + 1) * cw],
                                      a_prev[:, s * cw:(s + 1) * cw]], axis=1), wb_ref[s])
                for s in range(S5_SLABS)]

    def scan_stage(bus, x):
        states, x_out = [], []
        for s in range(S5_SLABS):
            ar = lamr_ref[:, hs * s:hs * (s + 1)]
            ai = lami_ref[:, hs * s:hs * (s + 1)]
            xr, xi = x[s]
            st = []
            for t2 in range(n2):
                rs = slice(SUBLANES * t2, SUBLANES * (t2 + 1))
                xr, xi = (ar * xr - ai * xi + bus[s][rs, :hs],
                          ar * xi + ai * xr + bus[s][rs, hs:])
                st.append(jnp.concatenate([xr, xi], axis=1))
            x_out.append((xr, xi))
            states.append(jnp.concatenate(st, axis=0).astype(BF16))
        return states, x_out

    def output_stage(j, states):
        y_il = jnp.concatenate([_dot(states[s], cm_ref[s]) for s in range(S5_SLABS)], axis=1)
        y = _dot(permt_ref[...], y_il.astype(BF16))
        y = y + d_ref[...] * natural(j)
        y = jax.nn.gelu(y)
        z = _dot(y.astype(BF16), wglu_ref[...]) + bglu_ref[...]
        out = y * jax.nn.sigmoid(z)
        for b in range(4):
            o_ref[b, j * ts:(j + 1) * ts, :] = out[b * ts:(b + 1) * ts]

    nsub = tm // ts
    befores = [ulast_scr[...]] + [last_rows(j) for j in range(nsub - 1)]
    bus = [input_stage(j, befores[j]) for j in range(nsub)]
    ulast_scr[...] = last_rows(nsub - 1)
    x = [(x_scr[s, 0], x_scr[s, 1]) for s in range(S5_SLABS)]
    for j in range(nsub):
        states, x = scan_stage(bus[j], x)
        output_stage(j, states)
    for s in range(S5_SLABS):
        x_scr[s, 0], x_scr[s, 1] = x[s]


def _s5_perms(tm):
    n2 = tm // 2
    rows = SUBLANES * n2
    perm = np.zeros((rows, 4 * tm), np.float32)
    permt = np.zeros((4 * tm, rows), np.float32)
    for t2 in range(n2):
        for b in range(4):
            for par in range(2):
                r = SUBLANES * t2 + 2 * b + par
                t = 2 * t2 + par
                perm[r, b * tm + t] = 1.0
                permt[b * tm + t, r] = 1.0
    return jnp.asarray(perm, BF16), jnp.asarray(permt, BF16)


def _s5_prep_kernel(lr_ref, li_ref, ldt_ref, br_ref, bi_ref, cr_ref, ci_ref,
                    wb_ref, cm_ref, lamr_ref, lami_ref, *, n, p, gs):
    lam_re = lr_ref[...]
    lam_im = li_ref[...]
    dt = jnp.exp(ldt_ref[...])
    mag = jnp.exp(lam_re * dt)
    lbr = mag * jnp.cos(lam_im * dt)
    lbi = mag * jnp.sin(lam_im * dt)
    nr, ni = lbr - 1.0, lbi
    den = lam_re * lam_re + lam_im * lam_im
    cr = (nr * lam_re + ni * lam_im) / den
    ci = (ni * lam_re - nr * lam_im) / den
    b_re = br_ref[...]
    b_im = bi_ref[...]
    bbr = cr * b_re - ci * b_im
    bbi = cr * b_im + ci * b_re
    pbr = lbr * bbr - lbi * bbi
    pbi = lbr * bbi + lbi * bbr
    lamr_ref[...] = (lbr * lbr - lbi * lbi).T[:SUBLANES]
    lami_ref[...] = (2.0 * lbr * lbi).T[:SUBLANES]

    rows_b, rows_c = n * p, n * gs
    keep_b = (lax.broadcasted_iota(jnp.int32, (rows_b, n * gs), 0) // p
              == lax.broadcasted_iota(jnp.int32, (rows_b, n * gs), 1) // gs)
    keep_c = (lax.broadcasted_iota(jnp.int32, (rows_c, n * p), 0) // gs
              == lax.broadcasted_iota(jnp.int32, (rows_c, n * p), 1) // p)
    c_re = cr_ref[...]
    c_im = ci_ref[...]
    for s in range(S5_SLABS):
        sb = slice(s * rows_b, (s + 1) * rows_b)
        blk = lambda x: jnp.where(keep_b, x[sb], 0.0).T
        wb_ref[s] = jnp.concatenate(
            [jnp.concatenate([blk(bbr), blk(bbi)], axis=1),
             jnp.concatenate([blk(pbr), blk(pbi)], axis=1)], axis=0).astype(BF16)
        sc = slice(s * rows_c, (s + 1) * rows_c)
        cm_ref[s] = jnp.concatenate([jnp.where(keep_c, c_re[sc], 0.0).T,
                                     jnp.where(keep_c, -c_im[sc], 0.0).T], axis=0).astype(BF16)


def _s5_weights(lam_re, lam_im, log_dt, b_re, b_im, c_re, c_im):
    g, p = lam_re.shape
    gs = b_re.shape[2]
    n = g // S5_SLABS
    assert n * gs == LANES
    rep = lambda a: jnp.broadcast_to(a.reshape(g * p, 1), (g * p, LANES))
    ldt = jnp.broadcast_to(log_dt[:, None, None], (g, p, LANES)).reshape(g * p, LANES)
    tile_b = lambda a: jnp.tile(a.reshape(g * p, gs), (1, n))
    tile_c = lambda a: jnp.tile(a.reshape(g * gs, p), (1, n))
    args = (rep(lam_re), rep(lam_im), ldt, tile_b(b_re), tile_b(b_im), tile_c(c_re), tile_c(c_im))
    return pl.pallas_call(
        functools.partial(_s5_prep_kernel, n=n, p=p, gs=gs),
        out_shape=[jax.ShapeDtypeStruct((S5_SLABS, 2 * n * gs, 2 * n * p), BF16),
                   jax.ShapeDtypeStruct((S5_SLABS, 2 * n * p, n * gs), BF16),
                   jax.ShapeDtypeStruct((SUBLANES, g * p), F32),
                   jax.ShapeDtypeStruct((SUBLANES, g * p), F32)],
        name="s5_prep",
    )(*args)


def _s5(u, wb, cm, lamr, lami, d_skip, w_glu_b, b_glu):
    bsz, seq, sw = u.shape
    assert bsz == 4, "the scan packs 4 batch rows x 2 token parities into 8 sublanes"
    tm = S5_TILE
    ns = lamr.shape[1]
    perm, permt = _s5_perms(S5_SUBTILE)
    const = lambda a: pl.BlockSpec(a.shape, lambda i: (0,) * a.ndim)
    d_row = d_skip.reshape(1, sw)
    bg = b_glu.reshape(1, sw)
    return pl.pallas_call(
        functools.partial(_s5_kernel, tm=tm, sw=sw),
        grid=(seq // tm,),
        in_specs=[pl.BlockSpec((4, tm, sw), lambda i: (0, i, 0)),
                  const(perm), const(permt), const(wb), const(cm), const(lamr), const(lami),
                  const(d_row), const(w_glu_b), const(bg)],
        out_specs=pl.BlockSpec((4, tm, sw), lambda i: (0, i, 0)),
        out_shape=jax.ShapeDtypeStruct((bsz, seq, sw), F32),
        scratch_shapes=[pltpu.VMEM((S5_SLABS, 2, SUBLANES, ns // S5_SLABS), F32),
                        pltpu.VMEM((SUBLANES, sw), F32)],
        compiler_params=pltpu.CompilerParams(
            dimension_semantics=("arbitrary",),
            vmem_limit_bytes=VMEM_LIMIT_BYTES),
        name="s5",
    )(u, perm, permt, wb, cm, lamr, lami, d_row, w_glu_b, bg)


def _front_kernel(x_ref, mod_ref, g_ref, w_ref, tri_ref, u_ref, ga_ref, gb_ref, at_ref,
                  q_scr, k_scr, v_scr, c_scr, acc_scr, z_scr, w_scr, *, sw, aw, d):
    blk = ATTN_BLOCK
    top = ATTN_TOP_ROWS
    n_pairs = aw // LANES
    nh = 2 * n_pairs
    tm = x_ref.shape[1]
    nsub = tm // blk
    step = pl.program_id(1)
    tile0 = pl.multiple_of(step * tm, tm)
    mod = mod_ref[0]
    even_head = (lax.broadcasted_iota(jnp.int32, (tm, aw), 1) // HEAD_DIM) % 2 == 0

    def qkv(hb):
        o = sw
        q = (_dot(hb, w_ref[:, o:o + aw]) * Q_SCALE).astype(BF16); o += aw
        q_scr[0] = jnp.where(even_head, q, jnp.zeros_like(q))
        q_scr[1] = jnp.where(even_head, jnp.zeros_like(q), q)
        kt = _dot(hb, w_ref[:, o:o + aw]).T.astype(BF16); o += aw
        for j in range(nsub):
            k_scr[nsub * step + j] = kt[:, j * blk:(j + 1) * blk]
        v = _dot(hb, w_ref[:, o:o + aw]).astype(BF16)
        v_scr[0, pl.ds(tile0, tm), :] = jnp.where(even_head, v, jnp.zeros_like(v))
        v_scr[1, pl.ds(tile0, tm), :] = jnp.where(even_head, jnp.zeros_like(v), v)

    def region(units, fillers=()):
        starts = [pl.multiple_of(kb * blk, blk) for kb, _, _, _ in units]
        for u, (_, r0, r1, _) in enumerate(units):
            n = r1 - r0
            for p in range(n_pairs):
                ls = slice(p * LANES, (p + 1) * LANES)
                kblk = k_scr[units[u][0], ls, :]
                zz = _dot(jnp.concatenate([q_scr[0, r0:r1, ls], q_scr[1, r0:r1, ls]], axis=0), kblk)
                z_scr[u * nh + 2 * p, 0:n] = zz[:n]
                z_scr[u * nh + 2 * p + 1, 0:n] = zz[n:]
        fillers = list(fillers)
        every = -(-len(units) // (len(fillers) + 1))
        for u, (_, r0, r1, diag) in enumerate(units):
            n = r1 - r0
            if diag:
                row = lax.broadcasted_iota(jnp.int32, (n, blk), 0) + r0 % blk
                valid = lax.broadcasted_iota(jnp.int32, (n, blk), 1) < row
            zs, sps = [], []
            for h in range(nh):
                z = z_scr[u * nh + h, 0:n]
                sp = jnp.maximum(z, 0.0) + jnp.log2(1.0 + jnp.exp2(-jnp.abs(z)))
                if diag:
                    sp = jnp.where(valid, sp, 0.0)
                zs.append(z)
                sps.append(sp.astype(BF16))
            incl_all = _dot(jnp.concatenate(sps, axis=0), tri_ref[...])
            for h in range(nh):
                z = zs[h]
                incl = incl_all[h * n:(h + 1) * n]
                total = jnp.broadcast_to(incl[:, 0:1], (n, blk))
                if diag:
                    w = jnp.where(valid, jnp.exp2(z - incl), 0.0)
                    c_scr[h, r0:r1] = total
                else:
                    c = c_scr[h, r0:r1]
                    w = jnp.exp2(z - incl - c)
                    c_scr[h, r0:r1] = c + total
                w_scr[u * nh + h, 0:n] = w.astype(BF16)
            if fillers and (u + 1) % every == 0:
                fillers.pop(0)()
        for filler in fillers:
            filler()
        for r0, r1 in dict.fromkeys((r0, r1) for _, r0, r1, _ in units):
            us = [u for u, (_, a, b, _) in enumerate(units) if (a, b) == (r0, r1)]
            for p in range(n_pairs):
                ls = slice(p * LANES, (p + 1) * LANES)
                ww = jnp.concatenate([w_scr[u * nh + 2 * p + hh, 0:r1 - r0]
                                      for u in us for hh in range(2)], axis=1)
                vv = jnp.concatenate([v_scr[hh, pl.ds(starts[u], blk), ls]
                                      for u in us for hh in range(2)], axis=0)
                if any(units[u][3] for u in us):
                    acc_scr[p, r0:r1] = _dot(ww, vv)
                else:
                    acc_scr[p, r0:r1] = acc_scr[p, r0:r1] + _dot(ww, vv)

    def c_min(r0, r1):
        m = c_scr[0, r0:r1]
        for h in range(1, nh):
            m = jnp.minimum(m, c_scr[h, r0:r1])
        return jnp.min(m)

    def head_units(sub, qi, n_prev):
        base = sub * blk
        units = [(qi, base, base + blk, True)]
        if n_prev >= 1:
            units.append((qi - 1, base, base + blk, False))
        if n_prev >= 2:
            units.append((qi - 2, base, base + top, False))
        return units

    def tile(units):
        h = _rms(x_ref[0]) * g_ref[...]
        hb = (h * (1.0 + mod[1:2]) + mod[0:1]).astype(BF16)
        qkv(hb)
        o = sw + 3 * aw

        def chunk(ref, col, w0):
            def run():
                ref[0, :, col:col + FRONT_CHUNK] = _dot(hb, w_ref[:, w0 + col:w0 + col + FRONT_CHUNK])
            return run

        region(units, [chunk(ref, col, w0)
                       for ref, w0, width in ((ga_ref, o, d), (gb_ref, o + d, d), (u_ref, 0, sw))
                       for col in range(0, width, FRONT_CHUNK)])

    @pl.when(step >= 1)
    def _():
        tile([u for sub in range(nsub) for u in head_units(sub, nsub * step + sub, 2)])

    @pl.when(step == 0)
    def _():
        tile([u for sub in range(nsub) for u in head_units(sub, sub, min(sub, 2))])

    def sweep(first_kb, cmin, r0, r1):
        def more(carry):
            kb, cmin = carry
            return jnp.logical_and(kb >= 0, cmin < UNDERFLOW_LOG2)

        def body(carry):
            kb, _ = carry
            region([(kb, r0, r1, False)])
            return kb - 1, c_min(r0, r1)

        lax.while_loop(more, body, (first_kb, cmin))

    tails = []
    for sub in range(nsub):
        qi = nsub * step + sub
        base = sub * blk
        tails.append((jnp.where(qi >= 2, qi - 3, -1), base, base + top))
        tails.append((jnp.where(qi >= 2, qi - 2, -1), base + top, base + blk))
    cmins = [c_min(r0, r1) for _, r0, r1 in tails]
    for (first_kb, r0, r1), cmin in zip(tails, cmins):
        sweep(first_kb, cmin, r0, r1)
    for p in range(n_pairs):
        at_ref[0, :, p * LANES:(p + 1) * LANES] = acc_scr[p].astype(at_ref.dtype)


def _attn_tri():
    blk = ATTN_BLOCK
    m = np.arange(blk)[:, None]
    j = np.arange(blk)[None, :]
    return jnp.asarray((m >= j).astype(np.float32), BF16)


def _front(x, mod, norm_g, w_in_b, sw, aw):
    bsz, seq, d = x.shape
    tm = FRONT_TILE
    blk = ATTN_BLOCK
    n = w_in_b.shape[1]
    n_pairs = aw // LANES
    nz = ATTN_REGION * (tm // blk) * 2 * n_pairs
    tri = _attn_tri()
    tok = lambda w: pl.BlockSpec((1, tm, w), lambda b, i: (b, i, 0))
    const = lambda a: pl.BlockSpec(a.shape, lambda b, i: (0,) * a.ndim,
                                   pipeline_mode=pl.Buffered(1))
    return pl.pallas_call(
        functools.partial(_front_kernel, sw=sw, aw=aw, d=d),
        grid=(bsz, seq // tm),
        in_specs=[tok(d),
                  pl.BlockSpec((1, N_ADA, d), lambda b, i: (b, 0, 0)),
                  const(norm_g), const(w_in_b), const(tri)],
        out_specs=[tok(sw), tok(d), tok(d), tok(aw)],
        out_shape=[jax.ShapeDtypeStruct((bsz, seq, sw), F32),
                   jax.ShapeDtypeStruct((bsz, seq, d), F32),
                   jax.ShapeDtypeStruct((bsz, seq, d), F32),
                   jax.ShapeDtypeStruct((bsz, seq, aw), BF16)],
        scratch_shapes=[pltpu.VMEM((2, tm, aw), BF16),
                        pltpu.VMEM((seq // blk, aw, blk), BF16),
                        pltpu.VMEM((2, seq, aw), BF16),
                        pltpu.VMEM((2 * n_pairs, tm, blk), F32),
                        pltpu.VMEM((n_pairs, tm, LANES), F32),
                        pltpu.VMEM((nz, blk, blk), F32),
                        pltpu.VMEM((nz, blk, blk), BF16)],
        compiler_params=pltpu.CompilerParams(
            dimension_semantics=("arbitrary", "arbitrary"),
            vmem_limit_bytes=VMEM_LIMIT_BYTES),
        name="front",
    )(x, mod, norm_g, w_in_b, tri)


def _out_ffn_kernel(x_ref, s5_ref, at_ref, ga_ref, gb_ref, mod_ref, n2_ref, nf_ref,
                    wa_ref, wb_ref, wo_ref, wg_ref, wu_ref, wd_ref, o_ref, *, final_norm):
    mod = mod_ref[0]
    tm = x_ref.shape[1]
    groups = [slice(r, r + tm // OUT_GROUPS) for r in range(0, tm, tm // OUT_GROUPS)]
    ms = []
    for g in groups:
        ya = _dot(s5_ref[0, g].astype(BF16), wa_ref[...])
        yb = _dot(at_ref[0, g], wb_ref[...])
        m = jax.nn.sigmoid(ga_ref[0, g]) * ya + jax.nn.sigmoid(gb_ref[0, g]) * yb
        ms.append(m.astype(BF16))
    x1s, hs = [], []
    for g, m in zip(groups, ms):
        x1 = x_ref[0, g] + mod[2:3] * _dot(m, wo_ref[...])
        h = _rms(x1) * n2_ref[...]
        x1s.append(x1)
        hs.append((h * (1.0 + mod[4:5]) + mod[3:4]).astype(BF16))
    acts = []
    for h in hs:
        gate = _dot(h, wg_ref[...])
        up = _dot(h, wu_ref[...])
        acts.append((gate * jax.nn.sigmoid(gate) * up).astype(BF16))
    for g, x1, act in zip(groups, x1s, acts):
        x2 = x1 + mod[5:6] * _dot(act, wd_ref[...])
        o_ref[0, g] = _rms(x2) * nf_ref[...] if final_norm else x2


def _out_ffn(x, s5o, attn, ga, gb, mod, n2g, nfg, wa, wb, wo, wg, wu, wd, final_norm):
    bsz, seq, d = x.shape
    tm = OUT_TILE
    tok = lambda a: pl.BlockSpec((1, tm, a.shape[-1]), lambda b, i: (b, i, 0))
    const = lambda a: pl.BlockSpec(a.shape, lambda b, i: (0,) * a.ndim,
                                   pipeline_mode=pl.Buffered(1))
    return pl.pallas_call(
        functools.partial(_out_ffn_kernel, final_norm=final_norm),
        grid=(bsz, seq // tm),
        in_specs=[tok(x), tok(s5o), tok(attn), tok(ga), tok(gb),
                  pl.BlockSpec((1, N_ADA, d), lambda b, i: (b, 0, 0)),
                  const(n2g), const(nfg),
                  const(wa), const(wb), const(wo), const(wg), const(wu), const(wd)],
        out_specs=pl.BlockSpec((1, tm, d), lambda b, i: (b, i, 0)),
        out_shape=jax.ShapeDtypeStruct((bsz, seq, d), F32),
        compiler_params=pltpu.CompilerParams(
            dimension_semantics=("arbitrary", "arbitrary"),
            vmem_limit_bytes=VMEM_LIMIT_BYTES),
        name="out_ffn",
    )(x, s5o, attn, ga, gb, mod, n2g, nfg, wa, wb, wo, wg, wu, wd)


def kernel(x, c, w_ada, b_ada, norm1_g, w_in, lam_re, lam_im, log_dt, b_re, b_im, c_re, c_im,
           d_skip, w_glu, b_glu, w_a, w_b, w_o, norm2_g, w_ffn_gate, w_ffn_up, w_ffn_down,
           norm_f_g):
    depth = w_ada.shape[0]
    bsz, seq, d = x.shape
    sw = w_glu.shape[1]
    aw = w_b.shape[1]
    for l in range(depth):
        mod = _ada(c, w_ada[l], b_ada[l]).reshape(bsz, N_ADA, d)
        u, ga, gb, attn = _front(x, mod, norm1_g[l].reshape(1, d), w_in[l].astype(BF16), sw, aw)
        wb, cm, lamr, lami = _s5_weights(lam_re[l], lam_im[l], log_dt[l], b_re[l], b_im[l],
                                         c_re[l], c_im[l])
        s5o = _s5(u, wb, cm, lamr, lami, d_skip[l], w_glu[l].astype(BF16), b_glu[l])
        x = _out_ffn(x, s5o, attn, ga, gb, mod, norm2_g[l].reshape(1, d), norm_f_g.reshape(1, d),
                     w_a[l].astype(BF16), w_b[l].astype(BF16), w_o[l].astype(BF16),
                     w_ffn_gate[l].astype(BF16), w_ffn_up[l].astype(BF16),
                     w_ffn_down[l].astype(BF16), final_norm=(l == depth - 1))
    return x
```

```python
import functools
import math

import numpy as np
import jax
import jax.numpy as jnp
from jax import lax
from jax.experimental import pallas as pl
from jax.experimental.pallas import tpu as pltpu

F32 = jnp.float32
BF16 = jnp.bfloat16

S5_GROUP = 16
S5_STATE = 64
HEAD_DIM = 64
N_ADA = 6
RMS_EPS = 1e-6
Q_SCALE = math.log2(math.e) / math.sqrt(HEAD_DIM)
UNDERFLOW_LOG2 = 151.0

LANES = 128
SUBLANES = 8
VMEM_LIMIT_BYTES = 56 * 1024 * 1024

ATTN_BLOCK = 128
ATTN_REGION = 3
ATTN_TOP_ROWS = 32
S5_TILE = 256
S5_SUBTILE = 128
S5_SLABS = 4
FRONT_TILE = 512
FRONT_CHUNK = 512
OUT_TILE = 512
OUT_GROUPS = 2


def _dot(a, b):
    return jnp.dot(a, b, preferred_element_type=F32)


def _rms(x):
    return x * lax.rsqrt(jnp.mean(x * x, axis=-1, keepdims=True) + RMS_EPS)


def _ada_kernel(c_ref, w_ref, b_ref, o_ref):
    c = c_ref[...]
    bsz = c.shape[0]
    cond = c * jax.nn.sigmoid(c)
    pad = -bsz % SUBLANES
    if pad:
        cond = jnp.concatenate([cond, jnp.zeros((pad, c.shape[1]), F32)], axis=0)
    o_ref[...] = _dot(cond.astype(BF16), w_ref[...].astype(BF16))[:bsz] + b_ref[...]


def _ada(c, w_ada, b_ada):
    bsz, d = c.shape
    n = w_ada.shape[1]
    tn = 1536
    return pl.pallas_call(
        _ada_kernel,
        grid=(n // tn,),
        in_specs=[pl.BlockSpec((bsz, d), lambda j: (0, 0)),
                  pl.BlockSpec((d, tn), lambda j: (0, j)),
                  pl.BlockSpec((1, tn), lambda j: (0, j))],
        out_specs=pl.BlockSpec((bsz, tn), lambda j: (0, j)),
        out_shape=jax.ShapeDtypeStruct((bsz, n), F32),
        name="ada",
    )(c, w_ada, b_ada.reshape(1, n))


def _s5_kernel(u_ref, perm_ref, permt_ref, wb_ref, cm_ref, lamr_ref, lami_ref, d_ref,
               wglu_ref, bglu_ref, *rest, tm, sw, n_cast):
    cast_in, o_ref, cast_out = rest[:n_cast], rest[n_cast], rest[n_cast + 1:2 * n_cast + 1]
    x_scr, ulast_scr = rest[2 * n_cast + 1:]
    for src, dst in zip(cast_in, cast_out):
        dst[...] = src[...].astype(BF16)
    i = pl.program_id(0)
    ts = S5_SUBTILE
    n2 = ts // 2
    rows = SUBLANES * n2
    cw = sw // S5_SLABS
    hs = cw * S5_STATE // S5_GROUP

    @pl.when(i == 0)
    def _():
        x_scr[...] = jnp.zeros_like(x_scr)
        ulast_scr[...] = jnp.zeros_like(ulast_scr)

    sub8 = lax.broadcasted_iota(jnp.int32, (SUBLANES, sw), 0)
    odd = (lax.broadcasted_iota(jnp.int32, (rows, sw), 0) & 1) == 1

    def natural(j):
        return jnp.concatenate([u_ref[b, j * ts:(j + 1) * ts, :] for b in range(4)], axis=0)

    def last_rows(j):
        last = jnp.zeros((SUBLANES, sw), F32)
        for b in range(4):
            row = u_ref[b, (j + 1) * ts - 1:(j + 1) * ts, :].astype(BF16).astype(F32)
            last = jnp.where(sub8 == 2 * b, jnp.broadcast_to(row, (SUBLANES, sw)), last)
        return last

    def input_stage(j, before):
        a_cur = _dot(perm_ref[...], natural(j).astype(BF16))
        a_prev = jnp.where(odd, pltpu.roll(a_cur, 1, 0), pltpu.roll(a_cur, SUBLANES - 1, 0))
        first = jnp.where((sub8 & 1) == 1, a_prev[:SUBLANES], before)
        a_prev = jnp.concatenate([first, a_prev[SUBLANES:]], axis=0).astype(BF16)
        a_cur = a_cur.astype(BF16)
        return [_dot(jnp.concatenate([a_cur[:, s * cw:(s + 1) * cw],
                                      a_prev[:, s * cw:(s + 1) * cw]], axis=1), wb_ref[s])
                for s in range(S5_SLABS)]

    def scan_stage(bus, x):
        states, x_out = [], []
        for s in range(S5_SLABS):
            ar = lamr_ref[:, hs * s:hs * (s + 1)]
            ai = lami_ref[:, hs * s:hs * (s + 1)]
            xr, xi = x[s]
            st = []
            for t2 in range(n2):
                rs = slice(SUBLANES * t2, SUBLANES * (t2 + 1))
                xr, xi = (ar * xr - ai * xi + bus[s][rs, :hs],
                          ar * xi + ai * xr + bus[s][rs, hs:])
                st.append(jnp.concatenate([xr, xi], axis=1))
            x_out.append((xr, xi))
            states.append(jnp.concatenate(st, axis=0).astype(BF16))
        return states, x_out

    def output_stage(j, states):
        y_il = jnp.concatenate([_dot(states[s], cm_ref[s]) for s in range(S5_SLABS)], axis=1)
        y = _dot(permt_ref[...], y_il.astype(BF16))
        y = y + d_ref[...] * natural(j)
        y = jax.nn.gelu(y)
        z = _dot(y.astype(BF16), wglu_ref[...]) + bglu_ref[...]
        out = y * jax.nn.sigmoid(z)
        for b in range(4):
            o_ref[b, j * ts:(j + 1) * ts, :] = out[b * ts:(b + 1) * ts]

    nsub = tm // ts
    befores = [ulast_scr[...]] + [last_rows(j) for j in range(nsub - 1)]
    bus = [input_stage(j, befores[j]) for j in range(nsub)]
    ulast_scr[...] = last_rows(nsub - 1)
    x = [(x_scr[s, 0], x_scr[s, 1]) for s in range(S5_SLABS)]
    for j in range(nsub):
        states, x = scan_stage(bus[j], x)
        output_stage(j, states)
    for s in range(S5_SLABS):
        x_scr[s, 0], x_scr[s, 1] = x[s]


def _s5_perms(tm):
    n2 = tm // 2
    rows = SUBLANES * n2
    perm = np.zeros((rows, 4 * tm), np.float32)
    permt = np.zeros((4 * tm, rows), np.float32)
    for t2 in range(n2):
        for b in range(4):
            for par in range(2):
                r = SUBLANES * t2 + 2 * b + par
                t = 2 * t2 + par
                perm[r, b * tm + t] = 1.0
                permt[b * tm + t, r] = 1.0
    return jnp.asarray(perm, BF16), jnp.asarray(permt, BF16)


def _slab_block_diag(blocks):
    g, a, b = blocks.shape
    n = g // S5_SLABS
    eye = jnp.eye(n, dtype=blocks.dtype)
    placed = blocks.reshape(S5_SLABS, n, a, 1, b) * eye[None, :, None, :, None]
    return placed.reshape(S5_SLABS, n * a, n * b)


def _s5_weights(lam_re, lam_im, log_dt, b_re, b_im, c_re, c_im):
    g = lam_re.shape[0]
    dt = jnp.exp(log_dt)[:, None]
    mag = jnp.exp(lam_re * dt)
    lbr = mag * jnp.cos(lam_im * dt)
    lbi = mag * jnp.sin(lam_im * dt)
    nr, ni = lbr - 1.0, lbi
    den = lam_re * lam_re + lam_im * lam_im
    cr = (nr * lam_re + ni * lam_im) / den
    ci = (ni * lam_re - nr * lam_im) / den
    bbr = cr[..., None] * b_re - ci[..., None] * b_im
    bbi = cr[..., None] * b_im + ci[..., None] * b_re
    lr = lbr[..., None] * bbr - lbi[..., None] * bbi
    li = lbr[..., None] * bbi + lbi[..., None] * bbr
    l2r = lbr * lbr - lbi * lbi
    l2i = 2.0 * lbr * lbi
    bd = lambda a: _slab_block_diag(jnp.swapaxes(a, 1, 2))
    wb = jnp.concatenate([jnp.concatenate([bd(bbr), bd(bbi)], axis=2),
                          jnp.concatenate([bd(lr), bd(li)], axis=2)], axis=1).astype(BF16)
    cm = jnp.concatenate([bd(c_re), -bd(c_im)], axis=1).astype(BF16)
    lamr = jnp.broadcast_to(l2r.reshape(1, -1), (SUBLANES, l2r.size))
    lami = jnp.broadcast_to(l2i.reshape(1, -1), (SUBLANES, l2i.size))
    return wb, cm, lamr, lami


def _s5(u, wb, cm, lamr, lami, d_skip, w_glu_b, b_glu, cast_weights):
    bsz, seq, sw = u.shape
    assert bsz == 4, "the scan packs 4 batch rows x 2 token parities into 8 sublanes"
    tm = S5_TILE
    steps = seq // tm
    ns = lamr.shape[1]
    perm, permt = _s5_perms(S5_SUBTILE)
    const = lambda a: pl.BlockSpec(a.shape, lambda i: (0,) * a.ndim)
    rows = lambda a: pl.BlockSpec((a.shape[0] // steps, a.shape[1]), lambda i: (i, 0))
    assert all(w.shape[0] % (16 * steps) == 0 for w in cast_weights)
    d_row = d_skip.reshape(1, sw)
    bg = b_glu.reshape(1, sw)
    outs = pl.pallas_call(
        functools.partial(_s5_kernel, tm=tm, sw=sw, n_cast=len(cast_weights)),
        grid=(steps,),
        in_specs=[pl.BlockSpec((4, tm, sw), lambda i: (0, i, 0)),
                  const(perm), const(permt), const(wb), const(cm), const(lamr), const(lami),
                  const(d_row), const(w_glu_b), const(bg)] + [rows(w) for w in cast_weights],
        out_specs=[pl.BlockSpec((4, tm, sw), lambda i: (0, i, 0))] + [rows(w) for w in cast_weights],
        out_shape=[jax.ShapeDtypeStruct((bsz, seq, sw), F32)]
        + [jax.ShapeDtypeStruct(w.shape, BF16) for w in cast_weights],
        scratch_shapes=[pltpu.VMEM((S5_SLABS, 2, SUBLANES, ns // S5_SLABS), F32),
                        pltpu.VMEM((SUBLANES, sw), F32)],
        compiler_params=pltpu.CompilerParams(
            dimension_semantics=("arbitrary",),
            vmem_limit_bytes=VMEM_LIMIT_BYTES),
        name="s5",
    )(u, perm, permt, wb, cm, lamr, lami, d_row, w_glu_b, bg, *cast_weights)
    return outs[0], outs[1:]


def _front_kernel(x_ref, mod_ref, g_ref, w_ref, tri_ref, u_ref, ga_ref, gb_ref, at_ref,
                  q_scr, k_scr, v_scr, c_scr, acc_scr, z_scr, w_scr, *, sw, aw, d):
    blk = ATTN_BLOCK
    top = ATTN_TOP_ROWS
    n_pairs = aw // LANES
    nh = 2 * n_pairs
    tm = x_ref.shape[1]
    nsub = tm // blk
    step = pl.program_id(1)
    tile0 = pl.multiple_of(step * tm, tm)
    mod = mod_ref[0]
    even_head = (lax.broadcasted_iota(jnp.int32, (tm, aw), 1) // HEAD_DIM) % 2 == 0

    def qkv(hb):
        o = sw
        q = (_dot(hb, w_ref[:, o:o + aw]) * Q_SCALE).astype(BF16); o += aw
        q_scr[0] = jnp.where(even_head, q, jnp.zeros_like(q))
        q_scr[1] = jnp.where(even_head, jnp.zeros_like(q), q)
        kt = _dot(hb, w_ref[:, o:o + aw]).T.astype(BF16); o += aw
        for j in range(nsub):
            k_scr[nsub * step + j] = kt[:, j * blk:(j + 1) * blk]
        v = _dot(hb, w_ref[:, o:o + aw]).astype(BF16)
        v_scr[0, pl.ds(tile0, tm), :] = jnp.where(even_head, v, jnp.zeros_like(v))
        v_scr[1, pl.ds(tile0, tm), :] = jnp.where(even_head, jnp.zeros_like(v), v)

    def region(units, fillers=()):
        starts = [pl.multiple_of(kb * blk, blk) for kb, _, _, _ in units]
        for u, (_, r0, r1, _) in enumerate(units):
            n = r1 - r0
            for p in range(n_pairs):
                ls = slice(p * LANES, (p + 1) * LANES)
                kblk = k_scr[units[u][0], ls, :]
                zz = _dot(jnp.concatenate([q_scr[0, r0:r1, ls], q_scr[1, r0:r1, ls]], axis=0), kblk)
                z_scr[u * nh + 2 * p, 0:n] = zz[:n]
                z_scr[u * nh + 2 * p + 1, 0:n] = zz[n:]
        fillers = list(fillers)
        every = -(-len(units) // (len(fillers) + 1))
        for u, (_, r0, r1, diag) in enumerate(units):
            n = r1 - r0
            if diag:
                row = lax.broadcasted_iota(jnp.int32, (n, blk), 0) + r0 % blk
                valid = lax.broadcasted_iota(jnp.int32, (n, blk), 1) < row
            zs, sps = [], []
            for h in range(nh):
                z = z_scr[u * nh + h, 0:n]
                sp = jnp.maximum(z, 0.0) + jnp.log2(1.0 + jnp.exp2(-jnp.abs(z)))
                if diag:
                    sp = jnp.where(valid, sp, 0.0)
                zs.append(z)
                sps.append(sp.astype(BF16))
            incl_all = _dot(jnp.concatenate(sps, axis=0), tri_ref[...])
            for h in range(nh):
                z = zs[h]
                incl = incl_all[h * n:(h + 1) * n]
                total = jnp.broadcast_to(incl[:, 0:1], (n, blk))
                if diag:
                    w = jnp.where(valid, jnp.exp2(z - incl), 0.0)
                    c_scr[h, r0:r1] = total
                else:
                    c = c_scr[h, r0:r1]
                    w = jnp.exp2(z - incl - c)
                    c_scr[h, r0:r1] = c + total
                w_scr[u * nh + h, 0:n] = w.astype(BF16)
            if fillers and (u + 1) % every == 0:
                fillers.pop(0)()
        for filler in fillers:
            filler()
        for r0, r1 in dict.fromkeys((r0, r1) for _, r0, r1, _ in units):
            us = [u for u, (_, a, b, _) in enumerate(units) if (a, b) == (r0, r1)]
            for p in range(n_pairs):
                ls = slice(p * LANES, (p + 1) * LANES)
                ww = jnp.concatenate([w_scr[u * nh + 2 * p + hh, 0:r1 - r0]
                                      for u in us for hh in range(2)], axis=1)
                vv = jnp.concatenate([v_scr[hh, pl.ds(starts[u], blk), ls]
                                      for u in us for hh in range(2)], axis=0)
                if any(units[u][3] for u in us):
                    acc_scr[p, r0:r1] = _dot(ww, vv)
                else:
                    acc_scr[p, r0:r1] = acc_scr[p, r0:r1] + _dot(ww, vv)

    def c_min(r0, r1):
        m = c_scr[0, r0:r1]
        for h in range(1, nh):
            m = jnp.minimum(m, c_scr[h, r0:r1])
        return jnp.min(m)

    def head_units(sub, qi, n_prev):
        base = sub * blk
        units = [(qi, base, base + blk, True)]
        if n_prev >= 1:
            units.append((qi - 1, base, base + blk, False))
        if n_prev >= 2:
            units.append((qi - 2, base, base + top, False))
        return units

    def tile(units):
        h = _rms(x_ref[0]) * g_ref[...]
        hb = (h * (1.0 + mod[1:2]) + mod[0:1]).astype(BF16)
        qkv(hb)
        o = sw + 3 * aw

        def chunk(ref, col, w0):
            def run():
                ref[0, :, col:col + FRONT_CHUNK] = _dot(hb, w_ref[:, w0 + col:w0 + col + FRONT_CHUNK])
            return run

        region(units, [chunk(ref, col, w0)
                       for ref, w0, width in ((ga_ref, o, d), (gb_ref, o + d, d), (u_ref, 0, sw))
                       for col in range(0, width, FRONT_CHUNK)])

    @pl.when(step >= 1)
    def _():
        tile([u for sub in range(nsub) for u in head_units(sub, nsub * step + sub, 2)])

    @pl.when(step == 0)
    def _():
        tile([u for sub in range(nsub) for u in head_units(sub, sub, min(sub, 2))])

    def sweep(first_kb, cmin, r0, r1):
        def more(carry):
            kb, cmin = carry
            return jnp.logical_and(kb >= 0, cmin < UNDERFLOW_LOG2)

        def body(carry):
            kb, _ = carry
            region([(kb, r0, r1, False)])
            return kb - 1, c_min(r0, r1)

        lax.while_loop(more, body, (first_kb, cmin))

    tails = []
    for sub in range(nsub):
        qi = nsub * step + sub
        base = sub * blk
        tails.append((jnp.where(qi >= 2, qi - 3, -1), base, base + top))
        tails.append((jnp.where(qi >= 2, qi - 2, -1), base + top, base + blk))
    cmins = [c_min(r0, r1) for _, r0, r1 in tails]
    for (first_kb, r0, r1), cmin in zip(tails, cmins):
        sweep(first_kb, cmin, r0, r1)
    for p in range(n_pairs):
        at_ref[0, :, p * LANES:(p + 1) * LANES] = acc_scr[p].astype(at_ref.dtype)


def _attn_tri():
    blk = ATTN_BLOCK
    m = np.arange(blk)[:, None]
    j = np.arange(blk)[None, :]
    return jnp.asarray((m >= j).astype(np.float32), BF16)


def _front(x, mod, norm_g, w_in_b, sw, aw):
    bsz, seq, d = x.shape
    tm = FRONT_TILE
    blk = ATTN_BLOCK
    n = w_in_b.shape[1]
    n_pairs = aw // LANES
    nz = ATTN_REGION * (tm // blk) * 2 * n_pairs
    tri = _attn_tri()
    tok = lambda w: pl.BlockSpec((1, tm, w), lambda b, i: (b, i, 0))
    const = lambda a: pl.BlockSpec(a.shape, lambda b, i: (0,) * a.ndim,
                                   pipeline_mode=pl.Buffered(1))
    return pl.pallas_call(
        functools.partial(_front_kernel, sw=sw, aw=aw, d=d),
        grid=(bsz, seq // tm),
        in_specs=[tok(d),
                  pl.BlockSpec((1, N_ADA, d), lambda b, i: (b, 0, 0)),
                  const(norm_g), const(w_in_b), const(tri)],
        out_specs=[tok(sw), tok(d), tok(d), tok(aw)],
        out_shape=[jax.ShapeDtypeStruct((bsz, seq, sw), F32),
                   jax.ShapeDtypeStruct((bsz, seq, d), F32),
                   jax.ShapeDtypeStruct((bsz, seq, d), F32),
                   jax.ShapeDtypeStruct((bsz, seq, aw), BF16)],
        scratch_shapes=[pltpu.VMEM((2, tm, aw), BF16),
                        pltpu.VMEM((seq // blk, aw, blk), BF16),
                        pltpu.VMEM((2, seq, aw), BF16),
                        pltpu.VMEM((2 * n_pairs, tm, blk), F32),
                        pltpu.VMEM((n_pairs, tm, LANES), F32),
                        pltpu.VMEM((nz, blk, blk), F32),
                        pltpu.VMEM((nz, blk, blk), BF16)],
        compiler_params=pltpu.CompilerParams(
            dimension_semantics=("arbitrary", "arbitrary"),
            vmem_limit_bytes=VMEM_LIMIT_BYTES),
        name="front",
    )(x, mod, norm_g, w_in_b, tri)


def _out_ffn_kernel(x_ref, s5_ref, at_ref, ga_ref, gb_ref, mod_ref, n2_ref, nf_ref,
                    wa_ref, wb_ref, wo_ref, wg_ref, wu_ref, wd_ref, o_ref, *, final_norm):
    mod = mod_ref[0]
    tm = x_ref.shape[1]
    groups = [slice(r, r + tm // OUT_GROUPS) for r in range(0, tm, tm // OUT_GROUPS)]
    ms = []
    for g in groups:
        ya = _dot(s5_ref[0, g].astype(BF16), wa_ref[...])
        yb = _dot(at_ref[0, g], wb_ref[...])
        m = jax.nn.sigmoid(ga_ref[0, g]) * ya + jax.nn.sigmoid(gb_ref[0, g]) * yb
        ms.append(m.astype(BF16))
    x1s, hs = [], []
    for g, m in zip(groups, ms):
        x1 = x_ref[0, g] + mod[2:3] * _dot(m, wo_ref[...])
        h = _rms(x1) * n2_ref[...]
        x1s.append(x1)
        hs.append((h * (1.0 + mod[4:5]) + mod[3:4]).astype(BF16))
    acts = []
    for h in hs:
        gate = _dot(h, wg_ref[...])
        up = _dot(h, wu_ref[...])
        acts.append((gate * jax.nn.sigmoid(gate) * up).astype(BF16))
    for g, x1, act in zip(groups, x1s, acts):
        x2 = x1 + mod[5:6] * _dot(act, wd_ref[...])
        o_ref[0, g] = _rms(x2) * nf_ref[...] if final_norm else x2


def _out_ffn(x, s5o, attn, ga, gb, mod, n2g, nfg, wa, wb, wo, wg, wu, wd, final_norm):
    bsz, seq, d = x.shape
    tm = OUT_TILE
    tok = lambda a: pl.BlockSpec((1, tm, a.shape[-1]), lambda b, i: (b, i, 0))
    const = lambda a: pl.BlockSpec(a.shape, lambda b, i: (0,) * a.ndim,
                                   pipeline_mode=pl.Buffered(1))
    return pl.pallas_call(
        functools.partial(_out_ffn_kernel, final_norm=final_norm),
        grid=(bsz, seq // tm),
        in_specs=[tok(x), tok(s5o), tok(attn), tok(ga), tok(gb),
                  pl.BlockSpec((1, N_ADA, d), lambda b, i: (b, 0, 0)),
                  const(n2g), const(nfg),
                  const(wa), const(wb), const(wo), const(wg), const(wu), const(wd)],
        out_specs=pl.BlockSpec((1, tm, d), lambda b, i: (b, i, 0)),
        out_shape=jax.ShapeDtypeStruct((bsz, seq, d), F32),
        compiler_params=pltpu.CompilerParams(
            dimension_semantics=("arbitrary", "arbitrary"),
            vmem_limit_bytes=VMEM_LIMIT_BYTES),
        name="out_ffn",
    )(x, s5o, attn, ga, gb, mod, n2g, nfg, wa, wb, wo, wg, wu, wd)


def kernel(x, c, w_ada, b_ada, norm1_g, w_in, lam_re, lam_im, log_dt, b_re, b_im, c_re, c_im,
           d_skip, w_glu, b_glu, w_a, w_b, w_o, norm2_g, w_ffn_gate, w_ffn_up, w_ffn_down,
           norm_f_g):
    depth = w_ada.shape[0]
    bsz, seq, d = x.shape
    sw = w_glu.shape[1]
    aw = w_b.shape[1]
    for l in range(depth):
        mod = _ada(c, w_ada[l], b_ada[l]).reshape(bsz, N_ADA, d)
        u, ga, gb, attn = _front(x, mod, norm1_g[l].reshape(1, d), w_in[l].astype(BF16), sw, aw)
        wb, cm, lamr, lami = _s5_weights(lam_re[l], lam_im[l], log_dt[l], b_re[l], b_im[l],
                                         c_re[l], c_im[l])
        s5o, out_w = _s5(u, wb, cm, lamr, lami, d_skip[l], w_glu[l].astype(BF16), b_glu[l],
                         (w_a[l], w_b[l], w_o[l], w_ffn_gate[l], w_ffn_up[l], w_ffn_down[l]))
        x = _out_ffn(x, s5o, attn, ga, gb, mod, norm2_g[l].reshape(1, d), norm_f_g.reshape(1, d),
                     *out_w, final_norm=(l == depth - 1))
    return x
```

```python
import functools
import math

import numpy as np
import jax
import jax.numpy as jnp
from jax import lax
from jax.experimental import pallas as pl
from jax.experimental.pallas import tpu as pltpu

F32 = jnp.float32
BF16 = jnp.bfloat16

S5_GROUP = 16
S5_STATE = 64
HEAD_DIM = 64
N_ADA = 6
RMS_EPS = 1e-6
Q_SCALE = math.log2(math.e) / math.sqrt(HEAD_DIM)
UNDERFLOW_LOG2 = 151.0

LANES = 128
SUBLANES = 8
VMEM_LIMIT_BYTES = 56 * 1024 * 1024

ATTN_BLOCK = 128
ATTN_REGION = 3
ATTN_TOP_ROWS = 32
S5_TILE = 256
S5_SUBTILE = 128
S5_SLABS = 4
FRONT_TILE = 512
FRONT_CHUNK = 512
OUT_TILE = 512
OUT_GROUPS = 2


def _dot(a, b):
    return jnp.dot(a, b, preferred_element_type=F32)


def _rms(x):
    return x * lax.rsqrt(jnp.mean(x * x, axis=-1, keepdims=True) + RMS_EPS)


def _ada_kernel(c_ref, w_ref, b_ref, o_ref):
    c = c_ref[...]
    bsz = c.shape[0]
    cond = c * jax.nn.sigmoid(c)
    pad = -bsz % SUBLANES
    if pad:
        cond = jnp.concatenate([cond, jnp.zeros((pad, c.shape[1]), F32)], axis=0)
    o_ref[...] = _dot(cond.astype(BF16), w_ref[...].astype(BF16))[:bsz] + b_ref[...]


def _ada(c, w_ada, b_ada):
    bsz, d = c.shape
    n = w_ada.shape[1]
    tn = 1536
    return pl.pallas_call(
        _ada_kernel,
        grid=(n // tn,),
        in_specs=[pl.BlockSpec((bsz, d), lambda j: (0, 0)),
                  pl.BlockSpec((d, tn), lambda j: (0, j)),
                  pl.BlockSpec((1, tn), lambda j: (0, j))],
        out_specs=pl.BlockSpec((bsz, tn), lambda j: (0, j)),
        out_shape=jax.ShapeDtypeStruct((bsz, n), F32),
        name="ada",
    )(c, w_ada, b_ada.reshape(1, n))


def _s5_kernel(u_ref, perm_ref, permt_ref, wb_ref, cm_ref, lamr_ref, lami_ref, d_ref,
               wglu_ref, bglu_ref, *rest, tm, sw, n_cast):
    cast_in, o_ref, cast_out = rest[:n_cast], rest[n_cast], rest[n_cast + 1:2 * n_cast + 1]
    x_scr, ulast_scr = rest[2 * n_cast + 1:]
    for src, dst in zip(cast_in, cast_out):
        dst[...] = src[...].astype(BF16)
    i = pl.program_id(0)
    ts = S5_SUBTILE
    n2 = ts // 2
    rows = SUBLANES * n2
    cw = sw // S5_SLABS
    hs = cw * S5_STATE // S5_GROUP

    @pl.when(i == 0)
    def _():
        x_scr[...] = jnp.zeros_like(x_scr)
        ulast_scr[...] = jnp.zeros_like(ulast_scr)

    sub8 = lax.broadcasted_iota(jnp.int32, (SUBLANES, sw), 0)
    odd = (lax.broadcasted_iota(jnp.int32, (rows, sw), 0) & 1) == 1

    def natural(j):
        return jnp.concatenate([u_ref[b, j * ts:(j + 1) * ts, :] for b in range(4)], axis=0)

    def last_rows(j):
        last = jnp.zeros((SUBLANES, sw), F32)
        for b in range(4):
            row = u_ref[b, (j + 1) * ts - 1:(j + 1) * ts, :].astype(BF16).astype(F32)
            last = jnp.where(sub8 == 2 * b, jnp.broadcast_to(row, (SUBLANES, sw)), last)
        return last

    def input_stage(j, before):
        a_cur = _dot(perm_ref[...], natural(j).astype(BF16))
        a_prev = jnp.where(odd, pltpu.roll(a_cur, 1, 0), pltpu.roll(a_cur, SUBLANES - 1, 0))
        first = jnp.where((sub8 & 1) == 1, a_prev[:SUBLANES], before)
        a_prev = jnp.concatenate([first, a_prev[SUBLANES:]], axis=0).astype(BF16)
        a_cur = a_cur.astype(BF16)
        return [_dot(jnp.concatenate([a_cur[:, s * cw:(s + 1) * cw],
                                      a_prev[:, s * cw:(s + 1) * cw]], axis=1), wb_ref[s])
                for s in range(S5_SLABS)]

    def scan_stage(bus, x):
        states, x_out = [], []
        for s in range(S5_SLABS):
            ar = lamr_ref[:, hs * s:hs * (s + 1)]
            ai = lami_ref[:, hs * s:hs * (s + 1)]
            xr, xi = x[s]
            st = []
            for t2 in range(n2):
                rs = slice(SUBLANES * t2, SUBLANES * (t2 + 1))
                xr, xi = (ar * xr - ai * xi + bus[s][rs, :hs],
                          ar * xi + ai * xr + bus[s][rs, hs:])
                st.append(jnp.concatenate([xr, xi], axis=1))
            x_out.append((xr, xi))
            states.append(jnp.concatenate(st, axis=0).astype(BF16))
        return states, x_out

    def output_stage(j, states):
        y_il = jnp.concatenate([_dot(states[s], cm_ref[s]) for s in range(S5_SLABS)], axis=1)
        y = _dot(permt_ref[...], y_il.astype(BF16))
        y = y + d_ref[...] * natural(j)
        y = jax.nn.gelu(y)
        z = _dot(y.astype(BF16), wglu_ref[...]) + bglu_ref[...]
        out = y * jax.nn.sigmoid(z)
        for b in range(4):
            o_ref[b, j * ts:(j + 1) * ts, :] = out[b * ts:(b + 1) * ts]

    nsub = tm // ts
    befores = [ulast_scr[...]] + [last_rows(j) for j in range(nsub - 1)]
    bus = [input_stage(j, befores[j]) for j in range(nsub)]
    ulast_scr[...] = last_rows(nsub - 1)
    x = [(x_scr[s, 0], x_scr[s, 1]) for s in range(S5_SLABS)]
    for j in range(nsub):
        states, x = scan_stage(bus[j], x)
        output_stage(j, states)
    for s in range(S5_SLABS):
        x_scr[s, 0], x_scr[s, 1] = x[s]


def _s5_perms(tm):
    n2 = tm // 2
    rows = SUBLANES * n2
    perm = np.zeros((rows, 4 * tm), np.float32)
    permt = np.zeros((4 * tm, rows), np.float32)
    for t2 in range(n2):
        for b in range(4):
            for par in range(2):
                r = SUBLANES * t2 + 2 * b + par
                t = 2 * t2 + par
                perm[r, b * tm + t] = 1.0
                permt[b * tm + t, r] = 1.0
    return jnp.asarray(perm, BF16), jnp.asarray(permt, BF16)


def _slab_block_diag(blocks):
    g, a, b = blocks.shape
    n = g // S5_SLABS
    eye = jnp.eye(n, dtype=blocks.dtype)
    placed = blocks.reshape(S5_SLABS, n, a, 1, b) * eye[None, :, None, :, None]
    return placed.reshape(S5_SLABS, n * a, n * b)


def _s5_weights(lam_re, lam_im, log_dt, b_re, b_im, c_re, c_im):
    g = lam_re.shape[0]
    dt = jnp.exp(log_dt)[:, None]
    mag = jnp.exp(lam_re * dt)
    lbr = mag * jnp.cos(lam_im * dt)
    lbi = mag * jnp.sin(lam_im * dt)
    nr, ni = lbr - 1.0, lbi
    den = lam_re * lam_re + lam_im * lam_im
    cr = (nr * lam_re + ni * lam_im) / den
    ci = (ni * lam_re - nr * lam_im) / den
    bbr = cr[..., None] * b_re - ci[..., None] * b_im
    bbi = cr[..., None] * b_im + ci[..., None] * b_re
    lr = lbr[..., None] * bbr - lbi[..., None] * bbi
    li = lbr[..., None] * bbi + lbi[..., None] * bbr
    l2r = lbr * lbr - lbi * lbi
    l2i = 2.0 * lbr * lbi
    bd = lambda a: _slab_block_diag(jnp.swapaxes(a, 1, 2))
    wb = jnp.concatenate([jnp.concatenate([bd(bbr), bd(bbi)], axis=2),
                          jnp.concatenate([bd(lr), bd(li)], axis=2)], axis=1).astype(BF16)
    cm = jnp.concatenate([bd(c_re), -bd(c_im)], axis=1).astype(BF16)
    lamr = jnp.broadcast_to(l2r.reshape(1, -1), (SUBLANES, l2r.size))
    lami = jnp.broadcast_to(l2i.reshape(1, -1), (SUBLANES, l2i.size))
    return wb, cm, lamr, lami


def _s5(u, wb, cm, lamr, lami, d_skip, w_glu_b, b_glu, cast_weights):
    bsz, seq, sw = u.shape
    assert bsz == 4, "the scan packs 4 batch rows x 2 token parities into 8 sublanes"
    tm = S5_TILE
    steps = seq // tm
    ns = lamr.shape[1]
    perm, permt = _s5_perms(S5_SUBTILE)
    const = lambda a: pl.BlockSpec(a.shape, lambda i: (0,) * a.ndim)
    rows = lambda a: pl.BlockSpec((a.shape[0] // steps, a.shape[1]), lambda i: (i, 0))
    assert all(w.shape[0] % (16 * steps) == 0 for w in cast_weights)
    d_row = d_skip.reshape(1, sw)
    bg = b_glu.reshape(1, sw)
    outs = pl.pallas_call(
        functools.partial(_s5_kernel, tm=tm, sw=sw, n_cast=len(cast_weights)),
        grid=(steps,),
        in_specs=[pl.BlockSpec((4, tm, sw), lambda i: (0, i, 0)),
                  const(perm), const(permt), const(wb), const(cm), const(lamr), const(lami),
                  const(d_row), const(w_glu_b), const(bg)] + [rows(w) for w in cast_weights],
        out_specs=[pl.BlockSpec((4, tm, sw), lambda i: (0, i, 0))] + [rows(w) for w in cast_weights],
        out_shape=[jax.ShapeDtypeStruct((bsz, seq, sw), F32)]
        + [jax.ShapeDtypeStruct(w.shape, BF16) for w in cast_weights],
        scratch_shapes=[pltpu.VMEM((S5_SLABS, 2, SUBLANES, ns // S5_SLABS), F32),
                        pltpu.VMEM((SUBLANES, sw), F32)],
        compiler_params=pltpu.CompilerParams(
            dimension_semantics=("arbitrary",),
            vmem_limit_bytes=VMEM_LIMIT_BYTES),
        name="s5",
    )(u, perm, permt, wb, cm, lamr, lami, d_row, w_glu_b, bg, *cast_weights)
    return outs[0], outs[1:]


def _front_kernel(x_ref, mod_ref, g_ref, w_ref, tri_ref, u_ref, gates_ref, at_ref,
                  q_scr, k_scr, v_scr, c_scr, acc_scr, z_scr, w_scr, *, sw, aw, d):
    blk = ATTN_BLOCK
    top = ATTN_TOP_ROWS
    n_pairs = aw // LANES
    nh = 2 * n_pairs
    tm = x_ref.shape[1]
    nsub = tm // blk
    step = pl.program_id(1)
    tile0 = pl.multiple_of(step * tm, tm)
    mod = mod_ref[0]
    even_head = (lax.broadcasted_iota(jnp.int32, (tm, aw), 1) // HEAD_DIM) % 2 == 0

    def qkv(hb):
        o = sw
        q = (_dot(hb, w_ref[:, o:o + aw]) * Q_SCALE).astype(BF16); o += aw
        q_scr[0] = jnp.where(even_head, q, jnp.zeros_like(q))
        q_scr[1] = jnp.where(even_head, jnp.zeros_like(q), q)
        kt = _dot(hb, w_ref[:, o:o + aw]).T.astype(BF16); o += aw
        for j in range(nsub):
            k_scr[nsub * step + j] = kt[:, j * blk:(j + 1) * blk]
        v = _dot(hb, w_ref[:, o:o + aw]).astype(BF16)
        v_scr[0, pl.ds(tile0, tm), :] = jnp.where(even_head, v, jnp.zeros_like(v))
        v_scr[1, pl.ds(tile0, tm), :] = jnp.where(even_head, jnp.zeros_like(v), v)

    def region(units, fillers=()):
        starts = [pl.multiple_of(kb * blk, blk) for kb, _, _, _ in units]
        for u, (_, r0, r1, _) in enumerate(units):
            n = r1 - r0
            for p in range(n_pairs):
                ls = slice(p * LANES, (p + 1) * LANES)
                kblk = k_scr[units[u][0], ls, :]
                zz = _dot(jnp.concatenate([q_scr[0, r0:r1, ls], q_scr[1, r0:r1, ls]], axis=0), kblk)
                z_scr[u * nh + 2 * p, 0:n] = zz[:n]
                z_scr[u * nh + 2 * p + 1, 0:n] = zz[n:]
        fillers = list(fillers)
        every = -(-len(units) // (len(fillers) + 1))
        for u, (_, r0, r1, diag) in enumerate(units):
            n = r1 - r0
            if diag:
                row = lax.broadcasted_iota(jnp.int32, (n, blk), 0) + r0 % blk
                valid = lax.broadcasted_iota(jnp.int32, (n, blk), 1) < row
            zs, sps = [], []
            for h in range(nh):
                z = z_scr[u * nh + h, 0:n]
                sp = jnp.maximum(z, 0.0) + jnp.log2(1.0 + jnp.exp2(-jnp.abs(z)))
                if diag:
                    sp = jnp.where(valid, sp, 0.0)
                zs.append(z)
                sps.append(sp.astype(BF16))
            incl_all = _dot(jnp.concatenate(sps, axis=0), tri_ref[...])
            for h in range(nh):
                z = zs[h]
                incl = incl_all[h * n:(h + 1) * n]
                total = jnp.broadcast_to(incl[:, 0:1], (n, blk))
                if diag:
                    w = jnp.where(valid, jnp.exp2(z - incl), 0.0)
                    c_scr[h, r0:r1] = total
                else:
                    c = c_scr[h, r0:r1]
                    w = jnp.exp2(z - incl - c)
                    c_scr[h, r0:r1] = c + total
                w_scr[u * nh + h, 0:n] = w.astype(BF16)
            if fillers and (u + 1) % every == 0:
                fillers.pop(0)()
        for filler in fillers:
            filler()
        for r0, r1 in dict.fromkeys((r0, r1) for _, r0, r1, _ in units):
            us = [u for u, (_, a, b, _) in enumerate(units) if (a, b) == (r0, r1)]
            for p in range(n_pairs):
                ls = slice(p * LANES, (p + 1) * LANES)
                ww = jnp.concatenate([w_scr[u * nh + 2 * p + hh, 0:r1 - r0]
                                      for u in us for hh in range(2)], axis=1)
                vv = jnp.concatenate([v_scr[hh, pl.ds(starts[u], blk), ls]
                                      for u in us for hh in range(2)], axis=0)
                if any(units[u][3] for u in us):
                    acc_scr[p, r0:r1] = _dot(ww, vv)
                else:
                    acc_scr[p, r0:r1] = acc_scr[p, r0:r1] + _dot(ww, vv)

    def c_min(r0, r1):
        m = c_scr[0, r0:r1]
        for h in range(1, nh):
            m = jnp.minimum(m, c_scr[h, r0:r1])
        return jnp.min(m)

    def head_units(sub, qi, n_prev):
        base = sub * blk
        units = [(qi, base, base + blk, True)]
        if n_prev >= 1:
            units.append((qi - 1, base, base + blk, False))
        if n_prev >= 2:
            units.append((qi - 2, base, base + top, False))
        return units

    def tile(units):
        h = _rms(x_ref[0]) * g_ref[...]
        hb = (h * (1.0 + mod[1:2]) + mod[0:1]).astype(BF16)
        qkv(hb)
        o = sw + 3 * aw

        def chunk(ref, col, w0):
            def run():
                ref[0, :, col:col + FRONT_CHUNK] = _dot(hb, w_ref[:, w0 + col:w0 + col + FRONT_CHUNK])
            return run

        region(units, [chunk(ref, col, w0)
                       for ref, w0, width in ((gates_ref, o, 2 * d), (u_ref, 0, sw))
                       for col in range(0, width, FRONT_CHUNK)])

    @pl.when(step >= 1)
    def _():
        tile([u for sub in range(nsub) for u in head_units(sub, nsub * step + sub, 2)])

    @pl.when(step == 0)
    def _():
        tile([u for sub in range(nsub) for u in head_units(sub, sub, min(sub, 2))])

    def sweep(first_kb, cmin, r0, r1):
        def more(carry):
            kb, cmin = carry
            return jnp.logical_and(kb >= 0, cmin < UNDERFLOW_LOG2)

        def body(carry):
            kb, _ = carry
            region([(kb, r0, r1, False)])
            return kb - 1, c_min(r0, r1)

        lax.while_loop(more, body, (first_kb, cmin))

    tails = []
    for sub in range(nsub):
        qi = nsub * step + sub
        base = sub * blk
        tails.append((jnp.where(qi >= 2, qi - 3, -1), base, base + top))
        tails.append((jnp.where(qi >= 2, qi - 2, -1), base + top, base + blk))
    cmins = [c_min(r0, r1) for _, r0, r1 in tails]
    for (first_kb, r0, r1), cmin in zip(tails, cmins):
        sweep(first_kb, cmin, r0, r1)
    for p in range(n_pairs):
        at_ref[0, :, p * LANES:(p + 1) * LANES] = acc_scr[p].astype(at_ref.dtype)


def _attn_tri():
    blk = ATTN_BLOCK
    m = np.arange(blk)[:, None]
    j = np.arange(blk)[None, :]
    return jnp.asarray((m >= j).astype(np.float32), BF16)


def _front(x, mod, norm_g, w_in_b, sw, aw):
    bsz, seq, d = x.shape
    tm = FRONT_TILE
    blk = ATTN_BLOCK
    n = w_in_b.shape[1]
    n_pairs = aw // LANES
    nz = ATTN_REGION * (tm // blk) * 2 * n_pairs
    tri = _attn_tri()
    tok = lambda w: pl.BlockSpec((1, tm, w), lambda b, i: (b, i, 0))
    const = lambda a: pl.BlockSpec(a.shape, lambda b, i: (0,) * a.ndim,
                                   pipeline_mode=pl.Buffered(1))
    return pl.pallas_call(
        functools.partial(_front_kernel, sw=sw, aw=aw, d=d),
        grid=(bsz, seq // tm),
        in_specs=[tok(d),
                  pl.BlockSpec((1, N_ADA, d), lambda b, i: (b, 0, 0)),
                  const(norm_g), const(w_in_b), const(tri)],
        out_specs=[tok(sw), tok(2 * d), tok(aw)],
        out_shape=[jax.ShapeDtypeStruct((bsz, seq, sw), F32),
                   jax.ShapeDtypeStruct((bsz, seq, 2 * d), F32),
                   jax.ShapeDtypeStruct((bsz, seq, aw), BF16)],
        scratch_shapes=[pltpu.VMEM((2, tm, aw), BF16),
                        pltpu.VMEM((seq // blk, aw, blk), BF16),
                        pltpu.VMEM((2, seq, aw), BF16),
                        pltpu.VMEM((2 * n_pairs, tm, blk), F32),
                        pltpu.VMEM((n_pairs, tm, LANES), F32),
                        pltpu.VMEM((nz, blk, blk), F32),
                        pltpu.VMEM((nz, blk, blk), BF16)],
        compiler_params=pltpu.CompilerParams(
            dimension_semantics=("arbitrary", "arbitrary"),
            vmem_limit_bytes=VMEM_LIMIT_BYTES),
        name="front",
    )(x, mod, norm_g, w_in_b, tri)


def _out_ffn_kernel(x_ref, s5_ref, at_ref, gates_ref, mod_ref, n2_ref, nf_ref,
                    wa_ref, wb_ref, wo_ref, wg_ref, wu_ref, wd_ref, o_ref, *, final_norm):
    mod = mod_ref[0]
    tm = x_ref.shape[1]
    groups = [slice(r, r + tm // OUT_GROUPS) for r in range(0, tm, tm // OUT_GROUPS)]
    ms = []
    for g in groups:
        ya = _dot(s5_ref[0, g].astype(BF16), wa_ref[...])
        yb = _dot(at_ref[0, g], wb_ref[...])
        d = ya.shape[1]
        m = (jax.nn.sigmoid(gates_ref[0, g, :d]) * ya
             + jax.nn.sigmoid(gates_ref[0, g, d:]) * yb)
        ms.append(m.astype(BF16))
    x1s, hs = [], []
    for g, m in zip(groups, ms):
        x1 = x_ref[0, g] + mod[2:3] * _dot(m, wo_ref[...])
        h = _rms(x1) * n2_ref[...]
        x1s.append(x1)
        hs.append((h * (1.0 + mod[4:5]) + mod[3:4]).astype(BF16))
    acts = []
    for h in hs:
        gate = _dot(h, wg_ref[...])
        up = _dot(h, wu_ref[...])
        acts.append((gate * jax.nn.sigmoid(gate) * up).astype(BF16))
    for g, x1, act in zip(groups, x1s, acts):
        x2 = x1 + mod[5:6] * _dot(act, wd_ref[...])
        o_ref[0, g] = _rms(x2) * nf_ref[...] if final_norm else x2


def _out_ffn(x, s5o, attn, gates, mod, n2g, nfg, wa, wb, wo, wg, wu, wd, final_norm):
    bsz, seq, d = x.shape
    tm = OUT_TILE
    tok = lambda a: pl.BlockSpec((1, tm, a.shape[-1]), lambda b, i: (b, i, 0))
    const = lambda a: pl.BlockSpec(a.shape, lambda b, i: (0,) * a.ndim,
                                   pipeline_mode=pl.Buffered(1))
    return pl.pallas_call(
        functools.partial(_out_ffn_kernel, final_norm=final_norm),
        grid=(bsz, seq // tm),
        in_specs=[tok(x), tok(s5o), tok(attn), tok(gates),
                  pl.BlockSpec((1, N_ADA, d), lambda b, i: (b, 0, 0)),
                  const(n2g), const(nfg),
                  const(wa), const(wb), const(wo), const(wg), const(wu), const(wd)],
        out_specs=pl.BlockSpec((1, tm, d), lambda b, i: (b, i, 0)),
        out_shape=jax.ShapeDtypeStruct((bsz, seq, d), F32),
        compiler_params=pltpu.CompilerParams(
            dimension_semantics=("arbitrary", "arbitrary"),
            vmem_limit_bytes=VMEM_LIMIT_BYTES),
        name="out_ffn",
    )(x, s5o, attn, gates, mod, n2g, nfg, wa, wb, wo, wg, wu, wd)


def kernel(x, c, w_ada, b_ada, norm1_g, w_in, lam_re, lam_im, log_dt, b_re, b_im, c_re, c_im,
           d_skip, w_glu, b_glu, w_a, w_b, w_o, norm2_g, w_ffn_gate, w_ffn_up, w_ffn_down,
           norm_f_g):
    depth = w_ada.shape[0]
    bsz, seq, d = x.shape
    sw = w_glu.shape[1]
    aw = w_b.shape[1]
    for l in range(depth):
        mod = _ada(c, w_ada[l], b_ada[l]).reshape(bsz, N_ADA, d)
        u, gates, attn = _front(x, mod, norm1_g[l].reshape(1, d), w_in[l].astype(BF16), sw, aw)
        wb, cm, lamr, lami = _s5_weights(lam_re[l], lam_im[l], log_dt[l], b_re[l], b_im[l],
                                         c_re[l], c_im[l])
        s5o, out_w = _s5(u, wb, cm, lamr, lami, d_skip[l], w_glu[l].astype(BF16), b_glu[l],
                         (w_a[l], w_b[l], w_o[l], w_ffn_gate[l], w_ffn_up[l], w_ffn_down[l]))
        x = _out_ffn(x, s5o, attn, gates, mod, norm2_g[l].reshape(1, d), norm_f_g.reshape(1, d),
                     *out_w, final_norm=(l == depth - 1))
    return x
```

```python
import functools
import math

import numpy as np
import jax
import jax.numpy as jnp
from jax import lax
from jax.experimental import pallas as pl
from jax.experimental.pallas import tpu as pltpu

F32 = jnp.float32
BF16 = jnp.bfloat16

S5_GROUP = 16
S5_STATE = 64
HEAD_DIM = 64
N_ADA = 6
RMS_EPS = 1e-6
Q_SCALE = math.log2(math.e) / math.sqrt(HEAD_DIM)
UNDERFLOW_LOG2 = 151.0

LANES = 128
SUBLANES = 8
VMEM_LIMIT_BYTES = 56 * 1024 * 1024

ADA_STEPS = 4
ATTN_BLOCK = 128
ATTN_REGION = 3
ATTN_TOP_ROWS = 32
S5_TILE = 256
S5_SUBTILE = 128
S5_SLABS = 4
FRONT_TILE = 512
FRONT_CHUNK = 512
OUT_TILE = 512
OUT_GROUPS = 2


def _dot(a, b):
    return jnp.dot(a, b, preferred_element_type=F32)


def _rms(x):
    return x * lax.rsqrt(jnp.mean(x * x, axis=-1, keepdims=True) + RMS_EPS)


def _ada_kernel(c_ref, w_ref, b_ref, cast_ref, o_ref, cast_out_ref):
    c = c_ref[...]
    bsz = c.shape[0]
    cond = c * jax.nn.sigmoid(c)
    pad = -bsz % SUBLANES
    if pad:
        cond = jnp.concatenate([cond, jnp.zeros((pad, c.shape[1]), F32)], axis=0)
    o_ref[...] = _dot(cond.astype(BF16), w_ref[...].astype(BF16))[:bsz] + b_ref[...]
    cast_out_ref[...] = cast_ref[...].astype(BF16)


def _ada(c, w_ada, b_ada, cast_weight):
    bsz, d = c.shape
    n = w_ada.shape[1]
    steps = ADA_STEPS
    tn = n // steps
    rows, cols = cast_weight.shape
    assert n % (steps * LANES) == 0 and rows % (16 * steps) == 0
    return pl.pallas_call(
        _ada_kernel,
        grid=(steps,),
        in_specs=[pl.BlockSpec((bsz, d), lambda j: (0, 0)),
                  pl.BlockSpec((d, tn), lambda j: (0, j)),
                  pl.BlockSpec((1, tn), lambda j: (0, j)),
                  pl.BlockSpec((rows // steps, cols), lambda j: (j, 0))],
        out_specs=[pl.BlockSpec((bsz, tn), lambda j: (0, j)),
                   pl.BlockSpec((rows // steps, cols), lambda j: (j, 0))],
        out_shape=[jax.ShapeDtypeStruct((bsz, n), F32),
                   jax.ShapeDtypeStruct((rows, cols), BF16)],
        compiler_params=pltpu.CompilerParams(vmem_limit_bytes=VMEM_LIMIT_BYTES),
        name="ada",
    )(c, w_ada, b_ada.reshape(1, n), cast_weight)


def _s5_kernel(u_ref, perm_ref, permt_ref, wb_ref, cm_ref, lamr_ref, lami_ref, d_ref,
               wglu_ref, bglu_ref, *rest, tm, sw, n_cast):
    cast_in, o_ref, cast_out = rest[:n_cast], rest[n_cast], rest[n_cast + 1:2 * n_cast + 1]
    x_scr, ulast_scr = rest[2 * n_cast + 1:]
    for src, dst in zip(cast_in, cast_out):
        dst[...] = src[...].astype(BF16)
    i = pl.program_id(0)
    ts = S5_SUBTILE
    n2 = ts // 2
    rows = SUBLANES * n2
    cw = sw // S5_SLABS
    hs = cw * S5_STATE // S5_GROUP

    @pl.when(i == 0)
    def _():
        x_scr[...] = jnp.zeros_like(x_scr)
        ulast_scr[...] = jnp.zeros_like(ulast_scr)

    sub8 = lax.broadcasted_iota(jnp.int32, (SUBLANES, sw), 0)
    odd = (lax.broadcasted_iota(jnp.int32, (rows, sw), 0) & 1) == 1

    def natural(j):
        return jnp.concatenate([u_ref[b, j * ts:(j + 1) * ts, :] for b in range(4)], axis=0)

    def last_rows(j):
        last = jnp.zeros((SUBLANES, sw), F32)
        for b in range(4):
            row = u_ref[b, (j + 1) * ts - 1:(j + 1) * ts, :].astype(BF16).astype(F32)
            last = jnp.where(sub8 == 2 * b, jnp.broadcast_to(row, (SUBLANES, sw)), last)
        return last

    def input_stage(j, before):
        a_cur = _dot(perm_ref[...], natural(j).astype(BF16))
        a_prev = jnp.where(odd, pltpu.roll(a_cur, 1, 0), pltpu.roll(a_cur, SUBLANES - 1, 0))
        first = jnp.where((sub8 & 1) == 1, a_prev[:SUBLANES], before)
        a_prev = jnp.concatenate([first, a_prev[SUBLANES:]], axis=0).astype(BF16)
        a_cur = a_cur.astype(BF16)
        return [_dot(jnp.concatenate([a_cur[:, s * cw:(s + 1) * cw],
                                      a_prev[:, s * cw:(s + 1) * cw]], axis=1), wb_ref[s])
                for s in range(S5_SLABS)]

    def scan_stage(bus, x):
        states, x_out = [], []
        for s in range(S5_SLABS):
            ar = lamr_ref[:, hs * s:hs * (s + 1)]
            ai = lami_ref[:, hs * s:hs * (s + 1)]
            xr, xi = x[s]
            st = []
            for t2 in range(n2):
                rs = slice(SUBLANES * t2, SUBLANES * (t2 + 1))
                xr, xi = (ar * xr - ai * xi + bus[s][rs, :hs],
                          ar * xi + ai * xr + bus[s][rs, hs:])
                st.append(jnp.concatenate([xr, xi], axis=1))
            x_out.append((xr, xi))
            states.append(jnp.concatenate(st, axis=0).astype(BF16))
        return states, x_out

    def output_stage(j, states):
        y_il = jnp.concatenate([_dot(states[s], cm_ref[s]) for s in range(S5_SLABS)], axis=1)
        y = _dot(permt_ref[...], y_il.astype(BF16))
        y = y + d_ref[...] * natural(j)
        y = jax.nn.gelu(y)
        z = _dot(y.astype(BF16), wglu_ref[...]) + bglu_ref[...]
        out = y * jax.nn.sigmoid(z)
        for b in range(4):
            o_ref[b, j * ts:(j + 1) * ts, :] = out[b * ts:(b + 1) * ts]

    nsub = tm // ts
    befores = [ulast_scr[...]] + [last_rows(j) for j in range(nsub - 1)]
    bus = [input_stage(j, befores[j]) for j in range(nsub)]
    ulast_scr[...] = last_rows(nsub - 1)
    x = [(x_scr[s, 0], x_scr[s, 1]) for s in range(S5_SLABS)]
    for j in range(nsub):
        states, x = scan_stage(bus[j], x)
        output_stage(j, states)
    for s in range(S5_SLABS):
        x_scr[s, 0], x_scr[s, 1] = x[s]


def _s5_perms(tm):
    n2 = tm // 2
    rows = SUBLANES * n2
    perm = np.zeros((rows, 4 * tm), np.float32)
    permt = np.zeros((4 * tm, rows), np.float32)
    for t2 in range(n2):
        for b in range(4):
            for par in range(2):
                r = SUBLANES * t2 + 2 * b + par
                t = 2 * t2 + par
                perm[r, b * tm + t] = 1.0
                permt[b * tm + t, r] = 1.0
    return jnp.asarray(perm, BF16), jnp.asarray(permt, BF16)


def _slab_block_diag(blocks):
    g, a, b = blocks.shape
    n = g // S5_SLABS
    eye = jnp.eye(n, dtype=blocks.dtype)
    placed = blocks.reshape(S5_SLABS, n, a, 1, b) * eye[None, :, None, :, None]
    return placed.reshape(S5_SLABS, n * a, n * b)


def _s5_weights(lam_re, lam_im, log_dt, b_re, b_im, c_re, c_im):
    g = lam_re.shape[0]
    dt = jnp.exp(log_dt)[:, None]
    mag = jnp.exp(lam_re * dt)
    lbr = mag * jnp.cos(lam_im * dt)
    lbi = mag * jnp.sin(lam_im * dt)
    nr, ni = lbr - 1.0, lbi
    den = lam_re * lam_re + lam_im * lam_im
    cr = (nr * lam_re + ni * lam_im) / den
    ci = (ni * lam_re - nr * lam_im) / den
    bbr = cr[..., None] * b_re - ci[..., None] * b_im
    bbi = cr[..., None] * b_im + ci[..., None] * b_re
    lr = lbr[..., None] * bbr - lbi[..., None] * bbi
    li = lbr[..., None] * bbi + lbi[..., None] * bbr
    l2r = lbr * lbr - lbi * lbi
    l2i = 2.0 * lbr * lbi
    bd = lambda a: _slab_block_diag(jnp.swapaxes(a, 1, 2))
    wb = jnp.concatenate([jnp.concatenate([bd(bbr), bd(bbi)], axis=2),
                          jnp.concatenate([bd(lr), bd(li)], axis=2)], axis=1).astype(BF16)
    cm = jnp.concatenate([bd(c_re), -bd(c_im)], axis=1).astype(BF16)
    lamr = jnp.broadcast_to(l2r.reshape(1, -1), (SUBLANES, l2r.size))
    lami = jnp.broadcast_to(l2i.reshape(1, -1), (SUBLANES, l2i.size))
    return wb, cm, lamr, lami


def _s5(u, wb, cm, lamr, lami, d_skip, w_glu_b, b_glu, cast_weights):
    bsz, seq, sw = u.shape
    assert bsz == 4, "the scan packs 4 batch rows x 2 token parities into 8 sublanes"
    tm = S5_TILE
    steps = seq // tm
    ns = lamr.shape[1]
    perm, permt = _s5_perms(S5_SUBTILE)
    const = lambda a: pl.BlockSpec(a.shape, lambda i: (0,) * a.ndim)
    rows = lambda a: pl.BlockSpec((a.shape[0] // steps, a.shape[1]), lambda i: (i, 0))
    assert all(w.shape[0] % (16 * steps) == 0 for w in cast_weights)
    d_row = d_skip.reshape(1, sw)
    bg = b_glu.reshape(1, sw)
    outs = pl.pallas_call(
        functools.partial(_s5_kernel, tm=tm, sw=sw, n_cast=len(cast_weights)),
        grid=(steps,),
        in_specs=[pl.BlockSpec((4, tm, sw), lambda i: (0, i, 0)),
                  const(perm), const(permt), const(wb), const(cm), const(lamr), const(lami),
                  const(d_row), const(w_glu_b), const(bg)] + [rows(w) for w in cast_weights],
        out_specs=[pl.BlockSpec((4, tm, sw), lambda i: (0, i, 0))] + [rows(w) for w in cast_weights],
        out_shape=[jax.ShapeDtypeStruct((bsz, seq, sw), F32)]
        + [jax.ShapeDtypeStruct(w.shape, BF16) for w in cast_weights],
        scratch_shapes=[pltpu.VMEM((S5_SLABS, 2, SUBLANES, ns // S5_SLABS), F32),
                        pltpu.VMEM((SUBLANES, sw), F32)],
        compiler_params=pltpu.CompilerParams(
            dimension_semantics=("arbitrary",),
            vmem_limit_bytes=VMEM_LIMIT_BYTES),
        name="s5",
    )(u, perm, permt, wb, cm, lamr, lami, d_row, w_glu_b, bg, *cast_weights)
    return outs[0], outs[1:]


def _front_kernel(x_ref, mod_ref, g_ref, w_ref, tri_ref, u_ref, gates_ref, at_ref,
                  q_scr, k_scr, v_scr, c_scr, acc_scr, z_scr, w_scr, *, sw, aw, d):
    blk = ATTN_BLOCK
    top = ATTN_TOP_ROWS
    n_pairs = aw // LANES
    nh = 2 * n_pairs
    tm = x_ref.shape[1]
    nsub = tm // blk
    step = pl.program_id(1)
    tile0 = pl.multiple_of(step * tm, tm)
    mod = mod_ref[0]
    even_head = (lax.broadcasted_iota(jnp.int32, (tm, aw), 1) // HEAD_DIM) % 2 == 0

    def qkv(hb):
        o = sw
        q = (_dot(hb, w_ref[:, o:o + aw]) * Q_SCALE).astype(BF16); o += aw
        q_scr[0] = jnp.where(even_head, q, jnp.zeros_like(q))
        q_scr[1] = jnp.where(even_head, jnp.zeros_like(q), q)
        kt = _dot(hb, w_ref[:, o:o + aw]).T.astype(BF16); o += aw
        for j in range(nsub):
            k_scr[nsub * step + j] = kt[:, j * blk:(j + 1) * blk]
        v = _dot(hb, w_ref[:, o:o + aw]).astype(BF16)
        v_scr[0, pl.ds(tile0, tm), :] = jnp.where(even_head, v, jnp.zeros_like(v))
        v_scr[1, pl.ds(tile0, tm), :] = jnp.where(even_head, jnp.zeros_like(v), v)

    def region(units, fillers=()):
        starts = [pl.multiple_of(kb * blk, blk) for kb, _, _, _ in units]
        for u, (_, r0, r1, _) in enumerate(units):
            n = r1 - r0
            for p in range(n_pairs):
                ls = slice(p * LANES, (p + 1) * LANES)
                kblk = k_scr[units[u][0], ls, :]
                zz = _dot(jnp.concatenate([q_scr[0, r0:r1, ls], q_scr[1, r0:r1, ls]], axis=0), kblk)
                z_scr[u * nh + 2 * p, 0:n] = zz[:n]
                z_scr[u * nh + 2 * p + 1, 0:n] = zz[n:]
        fillers = list(fillers)
        every = -(-len(units) // (len(fillers) + 1))
        for u, (_, r0, r1, diag) in enumerate(units):
            n = r1 - r0
            if diag:
                row = lax.broadcasted_iota(jnp.int32, (n, blk), 0) + r0 % blk
                valid = lax.broadcasted_iota(jnp.int32, (n, blk), 1) < row
            zs, sps = [], []
            for h in range(nh):
                z = z_scr[u * nh + h, 0:n]
                sp = jnp.maximum(z, 0.0) + jnp.log2(1.0 + jnp.exp2(-jnp.abs(z)))
                if diag:
                    sp = jnp.where(valid, sp, 0.0)
                zs.append(z)
                sps.append(sp.astype(BF16))
            incl_all = _dot(jnp.concatenate(sps, axis=0), tri_ref[...])
            for h in range(nh):
                z = zs[h]
                incl = incl_all[h * n:(h + 1) * n]
                total = jnp.broadcast_to(incl[:, 0:1], (n, blk))
                if diag:
                    w = jnp.where(valid, jnp.exp2(z - incl), 0.0)
                    c_scr[h, r0:r1] = total
                else:
                    c = c_scr[h, r0:r1]
                    w = jnp.exp2(z - incl - c)
                    c_scr[h, r0:r1] = c + total
                w_scr[u * nh + h, 0:n] = w.astype(BF16)
            if fillers and (u + 1) % every == 0:
                fillers.pop(0)()
        for filler in fillers:
            filler()
        for r0, r1 in dict.fromkeys((r0, r1) for _, r0, r1, _ in units):
            us = [u for u, (_, a, b, _) in enumerate(units) if (a, b) == (r0, r1)]
            for p in range(n_pairs):
                ls = slice(p * LANES, (p + 1) * LANES)
                ww = jnp.concatenate([w_scr[u * nh + 2 * p + hh, 0:r1 - r0]
                                      for u in us for hh in range(2)], axis=1)
                vv = jnp.concatenate([v_scr[hh, pl.ds(starts[u], blk), ls]
                                      for u in us for hh in range(2)], axis=0)
                if any(units[u][3] for u in us):
                    acc_scr[p, r0:r1] = _dot(ww, vv)
                else:
                    acc_scr[p, r0:r1] = acc_scr[p, r0:r1] + _dot(ww, vv)

    def c_min(r0, r1):
        m = c_scr[0, r0:r1]
        for h in range(1, nh):
            m = jnp.minimum(m, c_scr[h, r0:r1])
        return jnp.min(m)

    def head_units(sub, qi, n_prev):
        base = sub * blk
        units = [(qi, base, base + blk, True)]
        if n_prev >= 1:
            units.append((qi - 1, base, base + blk, False))
        if n_prev >= 2:
            units.append((qi - 2, base, base + top, False))
        return units

    def tile(units):
        h = _rms(x_ref[0]) * g_ref[...]
        hb = (h * (1.0 + mod[1:2]) + mod[0:1]).astype(BF16)
        qkv(hb)
        o = sw + 3 * aw

        def chunk(ref, col, w0):
            def run():
                ref[0, :, col:col + FRONT_CHUNK] = _dot(hb, w_ref[:, w0 + col:w0 + col + FRONT_CHUNK])
            return run

        region(units, [chunk(ref, col, w0)
                       for ref, w0, width in ((gates_ref, o, 2 * d), (u_ref, 0, sw))
                       for col in range(0, width, FRONT_CHUNK)])

    @pl.when(step >= 1)
    def _():
        tile([u for sub in range(nsub) for u in head_units(sub, nsub * step + sub, 2)])

    @pl.when(step == 0)
    def _():
        tile([u for sub in range(nsub) for u in head_units(sub, sub, min(sub, 2))])

    def sweep(first_kb, cmin, r0, r1):
        def more(carry):
            kb, cmin = carry
            return jnp.logical_and(kb >= 0, cmin < UNDERFLOW_LOG2)

        def body(carry):
            kb, _ = carry
            region([(kb, r0, r1, False)])
            return kb - 1, c_min(r0, r1)

        lax.while_loop(more, body, (first_kb, cmin))

    tails = []
    for sub in range(nsub):
        qi = nsub * step + sub
        base = sub * blk
        tails.append((jnp.where(qi >= 2, qi - 3, -1), base, base + top))
        tails.append((jnp.where(qi >= 2, qi - 2, -1), base + top, base + blk))
    cmins = [c_min(r0, r1) for _, r0, r1 in tails]
    for (first_kb, r0, r1), cmin in zip(tails, cmins):
        sweep(first_kb, cmin, r0, r1)
    for p in range(n_pairs):
        at_ref[0, :, p * LANES:(p + 1) * LANES] = acc_scr[p].astype(at_ref.dtype)


def _attn_tri():
    blk = ATTN_BLOCK
    m = np.arange(blk)[:, None]
    j = np.arange(blk)[None, :]
    return jnp.asarray((m >= j).astype(np.float32), BF16)


def _front(x, mod, norm_g, w_in_b, sw, aw):
    bsz, seq, d = x.shape
    tm = FRONT_TILE
    blk = ATTN_BLOCK
    n = w_in_b.shape[1]
    n_pairs = aw // LANES
    nz = ATTN_REGION * (tm // blk) * 2 * n_pairs
    tri = _attn_tri()
    tok = lambda w: pl.BlockSpec((1, tm, w), lambda b, i: (b, i, 0))
    const = lambda a: pl.BlockSpec(a.shape, lambda b, i: (0,) * a.ndim,
                                   pipeline_mode=pl.Buffered(1))
    return pl.pallas_call(
        functools.partial(_front_kernel, sw=sw, aw=aw, d=d),
        grid=(bsz, seq // tm),
        in_specs=[tok(d),
                  pl.BlockSpec((1, N_ADA, d), lambda b, i: (b, 0, 0)),
                  const(norm_g), const(w_in_b), const(tri)],
        out_specs=[tok(sw), tok(2 * d), tok(aw)],
        out_shape=[jax.ShapeDtypeStruct((bsz, seq, sw), F32),
                   jax.ShapeDtypeStruct((bsz, seq, 2 * d), F32),
                   jax.ShapeDtypeStruct((bsz, seq, aw), BF16)],
        scratch_shapes=[pltpu.VMEM((2, tm, aw), BF16),
                        pltpu.VMEM((seq // blk, aw, blk), BF16),
                        pltpu.VMEM((2, seq, aw), BF16),
                        pltpu.VMEM((2 * n_pairs, tm, blk), F32),
                        pltpu.VMEM((n_pairs, tm, LANES), F32),
                        pltpu.VMEM((nz, blk, blk), F32),
                        pltpu.VMEM((nz, blk, blk), BF16)],
        compiler_params=pltpu.CompilerParams(
            dimension_semantics=("arbitrary", "arbitrary"),
            vmem_limit_bytes=VMEM_LIMIT_BYTES),
        name="front",
    )(x, mod, norm_g, w_in_b, tri)


def _out_ffn_kernel(x_ref, s5_ref, at_ref, gates_ref, mod_ref, n2_ref, nf_ref,
                    wa_ref, wb_ref, wo_ref, wg_ref, wu_ref, wd_ref, o_ref, *, final_norm):
    mod = mod_ref[0]
    tm = x_ref.shape[1]
    groups = [slice(r, r + tm // OUT_GROUPS) for r in range(0, tm, tm // OUT_GROUPS)]
    ms = []
    for g in groups:
        ya = _dot(s5_ref[0, g].astype(BF16), wa_ref[...])
        yb = _dot(at_ref[0, g], wb_ref[...])
        d = ya.shape[1]
        m = (jax.nn.sigmoid(gates_ref[0, g, :d]) * ya
             + jax.nn.sigmoid(gates_ref[0, g, d:]) * yb)
        ms.append(m.astype(BF16))
    x1s, hs = [], []
    for g, m in zip(groups, ms):
        x1 = x_ref[0, g] + mod[2:3] * _dot(m, wo_ref[...])
        h = _rms(x1) * n2_ref[...]
        x1s.append(x1)
        hs.append((h * (1.0 + mod[4:5]) + mod[3:4]).astype(BF16))
    acts = []
    for h in hs:
        gate = _dot(h, wg_ref[...])
        up = _dot(h, wu_ref[...])
        acts.append((gate * jax.nn.sigmoid(gate) * up).astype(BF16))
    for g, x1, act in zip(groups, x1s, acts):
        x2 = x1 + mod[5:6] * _dot(act, wd_ref[...])
        o_ref[0, g] = _rms(x2) * nf_ref[...] if final_norm else x2


def _out_ffn(x, s5o, attn, gates, mod, n2g, nfg, wa, wb, wo, wg, wu, wd, final_norm):
    bsz, seq, d = x.shape
    tm = OUT_TILE
    tok = lambda a: pl.BlockSpec((1, tm, a.shape[-1]), lambda b, i: (b, i, 0))
    const = lambda a: pl.BlockSpec(a.shape, lambda b, i: (0,) * a.ndim,
                                   pipeline_mode=pl.Buffered(1))
    return pl.pallas_call(
        functools.partial(_out_ffn_kernel, final_norm=final_norm),
        grid=(bsz, seq // tm),
        in_specs=[tok(x), tok(s5o), tok(attn), tok(gates),
                  pl.BlockSpec((1, N_ADA, d), lambda b, i: (b, 0, 0)),
                  const(n2g), const(nfg),
                  const(wa), const(wb), const(wo), const(wg), const(wu), const(wd)],
        out_specs=pl.BlockSpec((1, tm, d), lambda b, i: (b, i, 0)),
        out_shape=jax.ShapeDtypeStruct((bsz, seq, d), F32),
        compiler_params=pltpu.CompilerParams(
            dimension_semantics=("arbitrary", "arbitrary"),
            vmem_limit_bytes=VMEM_LIMIT_BYTES),
        name="out_ffn",
    )(x, s5o, attn, gates, mod, n2g, nfg, wa, wb, wo, wg, wu, wd)


def kernel(x, c, w_ada, b_ada, norm1_g, w_in, lam_re, lam_im, log_dt, b_re, b_im, c_re, c_im,
           d_skip, w_glu, b_glu, w_a, w_b, w_o, norm2_g, w_ffn_gate, w_ffn_up, w_ffn_down,
           norm_f_g):
    depth = w_ada.shape[0]
    bsz, seq, d = x.shape
    sw = w_glu.shape[1]
    aw = w_b.shape[1]
    for l in range(depth):
        mod, w_in_b = _ada(c, w_ada[l], b_ada[l], w_in[l])
        mod = mod.reshape(bsz, N_ADA, d)
        u, gates, attn = _front(x, mod, norm1_g[l].reshape(1, d), w_in_b, sw, aw)
        wb, cm, lamr, lami = _s5_weights(lam_re[l], lam_im[l], log_dt[l], b_re[l], b_im[l],
                                         c_re[l], c_im[l])
        s5o, out_w = _s5(u, wb, cm, lamr, lami, d_skip[l], w_glu[l].astype(BF16), b_glu[l],
                         (w_a[l], w_b[l], w_o[l], w_ffn_gate[l], w_ffn_up[l], w_ffn_down[l]))
        x = _out_ffn(x, s5o, attn, gates, mod, norm2_g[l].reshape(1, d), norm_f_g.reshape(1, d),
                     *out_w, final_norm=(l == depth - 1))
    return x
```

```python
import functools
import math

import numpy as np
import jax
import jax.numpy as jnp
from jax import lax
from jax.experimental import pallas as pl
from jax.experimental.pallas import tpu as pltpu

F32 = jnp.float32
BF16 = jnp.bfloat16

S5_GROUP = 16
S5_STATE = 64
HEAD_DIM = 64
N_ADA = 6
RMS_EPS = 1e-6
Q_SCALE = math.log2(math.e) / math.sqrt(HEAD_DIM)
UNDERFLOW_LOG2 = 151.0

LANES = 128
SUBLANES = 8
VMEM_LIMIT_BYTES = 56 * 1024 * 1024

ADA_STEPS = 4
ATTN_BLOCK = 128
ATTN_REGION = 3
ATTN_TOP_ROWS = 32
S5_TILE = 256
S5_SUBTILE = 128
S5_SLABS = 4
FRONT_TILE = 512
FRONT_CHUNK = 512
OUT_TILE = 512
OUT_GROUPS = 2


def _dot(a, b):
    return jnp.dot(a, b, preferred_element_type=F32)


def _rms(x):
    return x * lax.rsqrt(jnp.mean(x * x, axis=-1, keepdims=True) + RMS_EPS)


def _ada_kernel(c_ref, w_ref, b_ref, cast_ref, o_ref, cast_out_ref):
    c = c_ref[...]
    bsz = c.shape[0]
    cond = c * jax.nn.sigmoid(c)
    pad = -bsz % SUBLANES
    if pad:
        cond = jnp.concatenate([cond, jnp.zeros((pad, c.shape[1]), F32)], axis=0)
    o_ref[...] = _dot(cond.astype(BF16), w_ref[...].astype(BF16))[:bsz] + b_ref[...]
    cast_out_ref[...] = cast_ref[...].astype(BF16)


def _ada(c, w_ada, b_ada, cast_weight):
    bsz, d = c.shape
    n = w_ada.shape[1]
    steps = ADA_STEPS
    tn = n // steps
    rows, cols = cast_weight.shape
    assert n % (steps * LANES) == 0 and rows % (16 * steps) == 0
    return pl.pallas_call(
        _ada_kernel,
        grid=(steps,),
        in_specs=[pl.BlockSpec((bsz, d), lambda j: (0, 0)),
                  pl.BlockSpec((d, tn), lambda j: (0, j)),
                  pl.BlockSpec((1, tn), lambda j: (0, j)),
                  pl.BlockSpec((rows // steps, cols), lambda j: (j, 0))],
        out_specs=[pl.BlockSpec((bsz, tn), lambda j: (0, j)),
                   pl.BlockSpec((rows // steps, cols), lambda j: (j, 0))],
        out_shape=[jax.ShapeDtypeStruct((bsz, n), F32),
                   jax.ShapeDtypeStruct((rows, cols), BF16)],
        compiler_params=pltpu.CompilerParams(vmem_limit_bytes=VMEM_LIMIT_BYTES),
        name="ada",
    )(c, w_ada, b_ada.reshape(1, n), cast_weight)


def _s5_kernel(u_ref, perm_ref, permt_ref, wb_ref, cm_ref, lamr_ref, lami_ref, d_ref,
               wglu_ref, bglu_ref, *rest, tm, sw, n_cast):
    cast_in, o_ref, cast_out = rest[:n_cast], rest[n_cast], rest[n_cast + 1:2 * n_cast + 1]
    x_scr, ulast_scr = rest[2 * n_cast + 1:]
    for src, dst in zip(cast_in, cast_out):
        dst[...] = src[...].astype(BF16)
    i = pl.program_id(0)
    ts = S5_SUBTILE
    n2 = ts // 2
    rows = SUBLANES * n2
    cw = sw // S5_SLABS
    hs = cw * S5_STATE // S5_GROUP

    @pl.when(i == 0)
    def _():
        x_scr[...] = jnp.zeros_like(x_scr)
        ulast_scr[...] = jnp.zeros_like(ulast_scr)

    sub8 = lax.broadcasted_iota(jnp.int32, (SUBLANES, sw), 0)
    odd = (lax.broadcasted_iota(jnp.int32, (rows, sw), 0) & 1) == 1

    def natural(j):
        return jnp.concatenate([u_ref[b, j * ts:(j + 1) * ts, :] for b in range(4)], axis=0)

    def last_rows(j):
        last = jnp.zeros((SUBLANES, sw), F32)
        for b in range(4):
            row = u_ref[b, (j + 1) * ts - 1:(j + 1) * ts, :].astype(BF16).astype(F32)
            last = jnp.where(sub8 == 2 * b, jnp.broadcast_to(row, (SUBLANES, sw)), last)
        return last

    def input_stage(j, before):
        a_cur = _dot(perm_ref[...], natural(j).astype(BF16))
        a_prev = jnp.where(odd, pltpu.roll(a_cur, 1, 0), pltpu.roll(a_cur, SUBLANES - 1, 0))
        first = jnp.where((sub8 & 1) == 1, a_prev[:SUBLANES], before)
        a_prev = jnp.concatenate([first, a_prev[SUBLANES:]], axis=0).astype(BF16)
        a_cur = a_cur.astype(BF16)
        return [_dot(jnp.concatenate([a_cur[:, s * cw:(s + 1) * cw],
                                      a_prev[:, s * cw:(s + 1) * cw]], axis=1), wb_ref[s])
                for s in range(S5_SLABS)]

    def scan_stage(bus, x):
        states, x_out = [], []
        for s in range(S5_SLABS):
            ar = lamr_ref[:, hs * s:hs * (s + 1)]
            ai = lami_ref[:, hs * s:hs * (s + 1)]
            xr, xi = x[s]
            st = []
            for t2 in range(n2):
                rs = slice(SUBLANES * t2, SUBLANES * (t2 + 1))
                xr, xi = (ar * xr - ai * xi + bus[s][rs, :hs],
                          ar * xi + ai * xr + bus[s][rs, hs:])
                st.append(jnp.concatenate([xr, xi], axis=1))
            x_out.append((xr, xi))
            states.append(jnp.concatenate(st, axis=0).astype(BF16))
        return states, x_out

    def output_stage(j, states):
        y_il = jnp.concatenate([_dot(states[s], cm_ref[s]) for s in range(S5_SLABS)], axis=1)
        y = _dot(permt_ref[...], y_il.astype(BF16))
        y = y + d_ref[...] * natural(j)
        y = jax.nn.gelu(y)
        z = _dot(y.astype(BF16), wglu_ref[...]) + bglu_ref[...]
        out = y * jax.nn.sigmoid(z)
        for b in range(4):
            o_ref[b, j * ts:(j + 1) * ts, :] = out[b * ts:(b + 1) * ts]

    nsub = tm // ts
    befores = [ulast_scr[...]] + [last_rows(j) for j in range(nsub - 1)]
    bus = [input_stage(j, befores[j]) for j in range(nsub)]
    ulast_scr[...] = last_rows(nsub - 1)
    x = [(x_scr[s, 0], x_scr[s, 1]) for s in range(S5_SLABS)]
    for j in range(nsub):
        states, x = scan_stage(bus[j], x)
        output_stage(j, states)
    for s in range(S5_SLABS):
        x_scr[s, 0], x_scr[s, 1] = x[s]


def _s5_perms(tm):
    n2 = tm // 2
    rows = SUBLANES * n2
    perm = np.zeros((rows, 4 * tm), np.float32)
    permt = np.zeros((4 * tm, rows), np.float32)
    for t2 in range(n2):
        for b in range(4):
            for par in range(2):
                r = SUBLANES * t2 + 2 * b + par
                t = 2 * t2 + par
                perm[r, b * tm + t] = 1.0
                permt[b * tm + t, r] = 1.0
    return jnp.asarray(perm, BF16), jnp.asarray(permt, BF16)


def _s5_weights(lam_re, lam_im, log_dt, b_re, b_im, c_re, c_im):
    g, p, gs = b_re.shape
    n = g // S5_SLABS
    dt = jnp.exp(log_dt)[:, None]
    mag = jnp.exp(lam_re * dt)
    lbr = mag * jnp.cos(lam_im * dt)
    lbi = mag * jnp.sin(lam_im * dt)
    nr, ni = lbr - 1.0, lbi
    den = lam_re * lam_re + lam_im * lam_im
    cr = ((nr * lam_re + ni * lam_im) / den)[..., None]
    ci = ((ni * lam_re - nr * lam_im) / den)[..., None]
    bbr = cr * b_re - ci * b_im
    bbi = cr * b_im + ci * b_re
    pbr = lbr[..., None] * bbr - lbi[..., None] * bbi
    pbi = lbr[..., None] * bbi + lbi[..., None] * bbr
    eye = jnp.eye(n, dtype=F32)
    x = jnp.stack([jnp.stack([bbr, bbi]), jnp.stack([pbr, pbi])]).reshape(2, 2, S5_SLABS, n, p, gs)
    x = jnp.transpose(x, (2, 0, 3, 5, 1, 4))
    wb = (x[:, :, :, :, :, None, :] * eye[None, None, :, None, None, :, None]).reshape(
        S5_SLABS, 2 * n * gs, 2 * n * p).astype(BF16)
    y = jnp.stack([c_re, -c_im]).reshape(2, S5_SLABS, n, gs, p)
    y = jnp.transpose(y, (1, 0, 2, 4, 3))
    cm = (y[:, :, :, :, None, :] * eye[None, None, :, None, :, None]).reshape(
        S5_SLABS, 2 * n * p, n * gs).astype(BF16)
    lam2 = jnp.stack([lbr * lbr - lbi * lbi, 2.0 * lbr * lbi]).reshape(2, 1, g * p)
    lam2 = jnp.broadcast_to(lam2, (2, SUBLANES, g * p))
    return wb, cm, lam2[0], lam2[1]


def _s5(u, wb, cm, lamr, lami, d_skip, w_glu_b, b_glu, cast_weights):
    bsz, seq, sw = u.shape
    assert bsz == 4, "the scan packs 4 batch rows x 2 token parities into 8 sublanes"
    tm = S5_TILE
    steps = seq // tm
    ns = lamr.shape[1]
    perm, permt = _s5_perms(S5_SUBTILE)
    const = lambda a: pl.BlockSpec(a.shape, lambda i: (0,) * a.ndim)
    rows = lambda a: pl.BlockSpec((a.shape[0] // steps, a.shape[1]), lambda i: (i, 0))
    assert all(w.shape[0] % (16 * steps) == 0 for w in cast_weights)
    d_row = d_skip.reshape(1, sw)
    bg = b_glu.reshape(1, sw)
    outs = pl.pallas_call(
        functools.partial(_s5_kernel, tm=tm, sw=sw, n_cast=len(cast_weights)),
        grid=(steps,),
        in_specs=[pl.BlockSpec((4, tm, sw), lambda i: (0, i, 0)),
                  const(perm), const(permt), const(wb), const(cm), const(lamr), const(lami),
                  const(d_row), const(w_glu_b), const(bg)] + [rows(w) for w in cast_weights],
        out_specs=[pl.BlockSpec((4, tm, sw), lambda i: (0, i, 0))] + [rows(w) for w in cast_weights],
        out_shape=[jax.ShapeDtypeStruct((bsz, seq, sw), F32)]
        + [jax.ShapeDtypeStruct(w.shape, BF16) for w in cast_weights],
        scratch_shapes=[pltpu.VMEM((S5_SLABS, 2, SUBLANES, ns // S5_SLABS), F32),
                        pltpu.VMEM((SUBLANES, sw), F32)],
        compiler_params=pltpu.CompilerParams(
            dimension_semantics=("arbitrary",),
            vmem_limit_bytes=VMEM_LIMIT_BYTES),
        name="s5",
    )(u, perm, permt, wb, cm, lamr, lami, d_row, w_glu_b, bg, *cast_weights)
    return outs[0], outs[1:]


def _front_kernel(x_ref, mod_ref, g_ref, w_ref, tri_ref, u_ref, gates_ref, at_ref,
                  q_scr, k_scr, v_scr, c_scr, acc_scr, z_scr, w_scr, *, sw, aw, d):
    blk = ATTN_BLOCK
    top = ATTN_TOP_ROWS
    n_pairs = aw // LANES
    nh = 2 * n_pairs
    tm = x_ref.shape[1]
    nsub = tm // blk
    step = pl.program_id(1)
    tile0 = pl.multiple_of(step * tm, tm)
    mod = mod_ref[0]
    even_head = (lax.broadcasted_iota(jnp.int32, (tm, aw), 1) // HEAD_DIM) % 2 == 0

    def qkv(hb):
        o = sw
        q = (_dot(hb, w_ref[:, o:o + aw]) * Q_SCALE).astype(BF16); o += aw
        q_scr[0] = jnp.where(even_head, q, jnp.zeros_like(q))
        q_scr[1] = jnp.where(even_head, jnp.zeros_like(q), q)
        kt = _dot(hb, w_ref[:, o:o + aw]).T.astype(BF16); o += aw
        for j in range(nsub):
            k_scr[nsub * step + j] = kt[:, j * blk:(j + 1) * blk]
        v = _dot(hb, w_ref[:, o:o + aw]).astype(BF16)
        v_scr[0, pl.ds(tile0, tm), :] = jnp.where(even_head, v, jnp.zeros_like(v))
        v_scr[1, pl.ds(tile0, tm), :] = jnp.where(even_head, jnp.zeros_like(v), v)

    def region(units, fillers=()):
        starts = [pl.multiple_of(kb * blk, blk) for kb, _, _, _ in units]
        for u, (_, r0, r1, _) in enumerate(units):
            n = r1 - r0
            for p in range(n_pairs):
                ls = slice(p * LANES, (p + 1) * LANES)
                kblk = k_scr[units[u][0], ls, :]
                zz = _dot(jnp.concatenate([q_scr[0, r0:r1, ls], q_scr[1, r0:r1, ls]], axis=0), kblk)
                z_scr[u * nh + 2 * p, 0:n] = zz[:n]
                z_scr[u * nh + 2 * p + 1, 0:n] = zz[n:]
        fillers = list(fillers)
        every = -(-len(units) // (len(fillers) + 1))
        for u, (_, r0, r1, diag) in enumerate(units):
            n = r1 - r0
            if diag:
                row = lax.broadcasted_iota(jnp.int32, (n, blk), 0) + r0 % blk
                valid = lax.broadcasted_iota(jnp.int32, (n, blk), 1) < row
            zs, sps = [], []
            for h in range(nh):
                z = z_scr[u * nh + h, 0:n]
                sp = jnp.maximum(z, 0.0) + jnp.log2(1.0 + jnp.exp2(-jnp.abs(z)))
                if diag:
                    sp = jnp.where(valid, sp, 0.0)
                zs.append(z)
                sps.append(sp.astype(BF16))
            incl_all = _dot(jnp.concatenate(sps, axis=0), tri_ref[...])
            for h in range(nh):
                z = zs[h]
                incl = incl_all[h * n:(h + 1) * n]
                total = jnp.broadcast_to(incl[:, 0:1], (n, blk))
                if diag:
                    w = jnp.where(valid, jnp.exp2(z - incl), 0.0)
                    c_scr[h, r0:r1] = total
                else:
                    c = c_scr[h, r0:r1]
                    w = jnp.exp2(z - incl - c)
                    c_scr[h, r0:r1] = c + total
                w_scr[u * nh + h, 0:n] = w.astype(BF16)
            if fillers and (u + 1) % every == 0:
                fillers.pop(0)()
        for filler in fillers:
            filler()
        for r0, r1 in dict.fromkeys((r0, r1) for _, r0, r1, _ in units):
            us = [u for u, (_, a, b, _) in enumerate(units) if (a, b) == (r0, r1)]
            for p in range(n_pairs):
                ls = slice(p * LANES, (p + 1) * LANES)
                ww = jnp.concatenate([w_scr[u * nh + 2 * p + hh, 0:r1 - r0]
                                      for u in us for hh in range(2)], axis=1)
                vv = jnp.concatenate([v_scr[hh, pl.ds(starts[u], blk), ls]
                                      for u in us for hh in range(2)], axis=0)
                if any(units[u][3] for u in us):
                    acc_scr[p, r0:r1] = _dot(ww, vv)
                else:
                    acc_scr[p, r0:r1] = acc_scr[p, r0:r1] + _dot(ww, vv)

    def c_min(r0, r1):
        m = c_scr[0, r0:r1]
        for h in range(1, nh):
            m = jnp.minimum(m, c_scr[h, r0:r1])
        return jnp.min(m)

    def head_units(sub, qi, n_prev):
        base = sub * blk
        units = [(qi, base, base + blk, True)]
        if n_prev >= 1:
            units.append((qi - 1, base, base + blk, False))
        if n_prev >= 2:
            units.append((qi - 2, base, base + top, False))
        return units

    def tile(units):
        h = _rms(x_ref[0]) * g_ref[...]
        hb = (h * (1.0 + mod[1:2]) + mod[0:1]).astype(BF16)
        qkv(hb)
        o = sw + 3 * aw

        def chunk(ref, col, w0):
            def run():
                ref[0, :, col:col + FRONT_CHUNK] = _dot(hb, w_ref[:, w0 + col:w0 + col + FRONT_CHUNK])
            return run

        region(units, [chunk(ref, col, w0)
                       for ref, w0, width in ((gates_ref, o, 2 * d), (u_ref, 0, sw))
                       for col in range(0, width, FRONT_CHUNK)])

    @pl.when(step >= 1)
    def _():
        tile([u for sub in range(nsub) for u in head_units(sub, nsub * step + sub, 2)])

    @pl.when(step == 0)
    def _():
        tile([u for sub in range(nsub) for u in head_units(sub, sub, min(sub, 2))])

    def sweep(first_kb, cmin, r0, r1):
        def more(carry):
            kb, cmin = carry
            return jnp.logical_and(kb >= 0, cmin < UNDERFLOW_LOG2)

        def body(carry):
            kb, _ = carry
            region([(kb, r0, r1, False)])
            return kb - 1, c_min(r0, r1)

        lax.while_loop(more, body, (first_kb, cmin))

    tails = []
    for sub in range(nsub):
        qi = nsub * step + sub
        base = sub * blk
        tails.append((jnp.where(qi >= 2, qi - 3, -1), base, base + top))
        tails.append((jnp.where(qi >= 2, qi - 2, -1), base + top, base + blk))
    cmins = [c_min(r0, r1) for _, r0, r1 in tails]
    for (first_kb, r0, r1), cmin in zip(tails, cmins):
        sweep(first_kb, cmin, r0, r1)
    for p in range(n_pairs):
        at_ref[0, :, p * LANES:(p + 1) * LANES] = acc_scr[p].astype(at_ref.dtype)


def _attn_tri():
    blk = ATTN_BLOCK
    m = np.arange(blk)[:, None]
    j = np.arange(blk)[None, :]
    return jnp.asarray((m >= j).astype(np.float32), BF16)


def _front(x, mod, norm_g, w_in_b, sw, aw):
    bsz, seq, d = x.shape
    tm = FRONT_TILE
    blk = ATTN_BLOCK
    n = w_in_b.shape[1]
    n_pairs = aw // LANES
    nz = ATTN_REGION * (tm // blk) * 2 * n_pairs
    tri = _attn_tri()
    tok = lambda w: pl.BlockSpec((1, tm, w), lambda b, i: (b, i, 0))
    const = lambda a: pl.BlockSpec(a.shape, lambda b, i: (0,) * a.ndim,
                                   pipeline_mode=pl.Buffered(1))
    return pl.pallas_call(
        functools.partial(_front_kernel, sw=sw, aw=aw, d=d),
        grid=(bsz, seq // tm),
        in_specs=[tok(d),
                  pl.BlockSpec((1, N_ADA, d), lambda b, i: (b, 0, 0)),
                  const(norm_g), const(w_in_b), const(tri)],
        out_specs=[tok(sw), tok(2 * d), tok(aw)],
        out_shape=[jax.ShapeDtypeStruct((bsz, seq, sw), F32),
                   jax.ShapeDtypeStruct((bsz, seq, 2 * d), F32),
                   jax.ShapeDtypeStruct((bsz, seq, aw), BF16)],
        scratch_shapes=[pltpu.VMEM((2, tm, aw), BF16),
                        pltpu.VMEM((seq // blk, aw, blk), BF16),
                        pltpu.VMEM((2, seq, aw), BF16),
                        pltpu.VMEM((2 * n_pairs, tm, blk), F32),
                        pltpu.VMEM((n_pairs, tm, LANES), F32),
                        pltpu.VMEM((nz, blk, blk), F32),
                        pltpu.VMEM((nz, blk, blk), BF16)],
        compiler_params=pltpu.CompilerParams(
            dimension_semantics=("arbitrary", "arbitrary"),
            vmem_limit_bytes=VMEM_LIMIT_BYTES),
        name="front",
    )(x, mod, norm_g, w_in_b, tri)


def _out_ffn_kernel(x_ref, s5_ref, at_ref, gates_ref, mod_ref, n2_ref, nf_ref,
                    wa_ref, wb_ref, wo_ref, wg_ref, wu_ref, wd_ref, o_ref, *, final_norm):
    mod = mod_ref[0]
    tm = x_ref.shape[1]
    groups = [slice(r, r + tm // OUT_GROUPS) for r in range(0, tm, tm // OUT_GROUPS)]
    ms = []
    for g in groups:
        ya = _dot(s5_ref[0, g].astype(BF16), wa_ref[...])
        yb = _dot(at_ref[0, g], wb_ref[...])
        d = ya.shape[1]
        m = (jax.nn.sigmoid(gates_ref[0, g, :d]) * ya
             + jax.nn.sigmoid(gates_ref[0, g, d:]) * yb)
        ms.append(m.astype(BF16))
    x1s, hs = [], []
    for g, m in zip(groups, ms):
        x1 = x_ref[0, g] + mod[2:3] * _dot(m, wo_ref[...])
        h = _rms(x1) * n2_ref[...]
        x1s.append(x1)
        hs.append((h * (1.0 + mod[4:5]) + mod[3:4]).astype(BF16))
    acts = []
    for h in hs:
        gate = _dot(h, wg_ref[...])
        up = _dot(h, wu_ref[...])
        acts.append((gate * jax.nn.sigmoid(gate) * up).astype(BF16))
    for g, x1, act in zip(groups, x1s, acts):
        x2 = x1 + mod[5:6] * _dot(act, wd_ref[...])
        o_ref[0, g] = _rms(x2) * nf_ref[...] if final_norm else x2


def _out_ffn(x, s5o, attn, gates, mod, n2g, nfg, wa, wb, wo, wg, wu, wd, final_norm):
    bsz, seq, d = x.shape
    tm = OUT_TILE
    tok = lambda a: pl.BlockSpec((1, tm, a.shape[-1]), lambda b, i: (b, i, 0))
    const = lambda a: pl.BlockSpec(a.shape, lambda b, i: (0,) * a.ndim,
                                   pipeline_mode=pl.Buffered(1))
    return pl.pallas_call(
        functools.partial(_out_ffn_kernel, final_norm=final_norm),
        grid=(bsz, seq // tm),
        in_specs=[tok(x), tok(s5o), tok(attn), tok(gates),
                  pl.BlockSpec((1, N_ADA, d), lambda b, i: (b, 0, 0)),
                  const(n2g), const(nfg),
                  const(wa), const(wb), const(wo), const(wg), const(wu), const(wd)],
        out_specs=pl.BlockSpec((1, tm, d), lambda b, i: (b, i, 0)),
        out_shape=jax.ShapeDtypeStruct((bsz, seq, d), F32),
        compiler_params=pltpu.CompilerParams(
            dimension_semantics=("arbitrary", "arbitrary"),
            vmem_limit_bytes=VMEM_LIMIT_BYTES),
        name="out_ffn",
    )(x, s5o, attn, gates, mod, n2g, nfg, wa, wb, wo, wg, wu, wd)


def kernel(x, c, w_ada, b_ada, norm1_g, w_in, lam_re, lam_im, log_dt, b_re, b_im, c_re, c_im,
           d_skip, w_glu, b_glu, w_a, w_b, w_o, norm2_g, w_ffn_gate, w_ffn_up, w_ffn_down,
           norm_f_g):
    depth = w_ada.shape[0]
    bsz, seq, d = x.shape
    sw = w_glu.shape[1]
    aw = w_b.shape[1]
    for l in range(depth):
        mod, w_in_b = _ada(c, w_ada[l], b_ada[l], w_in[l])
        mod = mod.reshape(bsz, N_ADA, d)
        u, gates, attn = _front(x, mod, norm1_g[l].reshape(1, d), w_in_b, sw, aw)
        wb, cm, lamr, lami = _s5_weights(lam_re[l], lam_im[l], log_dt[l], b_re[l], b_im[l],
                                         c_re[l], c_im[l])
        s5o, out_w = _s5(u, wb, cm, lamr, lami, d_skip[l], w_glu[l].astype(BF16), b_glu[l],
                         (w_a[l], w_b[l], w_o[l], w_ffn_gate[l], w_ffn_up[l], w_ffn_down[l]))
        x = _out_ffn(x, s5o, attn, gates, mod, norm2_g[l].reshape(1, d), norm_f_g.reshape(1, d),
                     *out_w, final_norm=(l == depth - 1))
    return x
```

```python
import functools
import math

import numpy as np
import jax
import jax.numpy as jnp
from jax import lax
from jax.experimental import pallas as pl
from jax.experimental.pallas import tpu as pltpu

F32 = jnp.float32
BF16 = jnp.bfloat16

S5_GROUP = 16
S5_STATE = 64
HEAD_DIM = 64
N_ADA = 6
RMS_EPS = 1e-6
Q_SCALE = math.log2(math.e) / math.sqrt(HEAD_DIM)
UNDERFLOW_LOG2 = 151.0

LANES = 128
SUBLANES = 8
VMEM_LIMIT_BYTES = 56 * 1024 * 1024

ADA_STEPS = 4
ATTN_BLOCK = 128
ATTN_REGION = 3
ATTN_TOP_ROWS = 32
S5_TILE = 256
S5_SUBTILE = 128
S5_SLABS = 4
FRONT_TILE = 512
FRONT_CHUNK = 512
OUT_TILE = 512
OUT_GROUPS = 2


def _dot(a, b):
    return jnp.dot(a, b, preferred_element_type=F32)


def _rms(x):
    return x * lax.rsqrt(jnp.mean(x * x, axis=-1, keepdims=True) + RMS_EPS)


def _mod_rows(mod_ref, d):
    row = mod_ref[pl.ds(pl.program_id(0), 1), :]
    return [row[:, k * d:(k + 1) * d] for k in range(N_ADA)]


def _ada_kernel(c_ref, w_ref, b_ref, cast_ref, o_ref, cast_out_ref):
    c = c_ref[...]
    bsz = c.shape[0]
    cond = c * jax.nn.sigmoid(c)
    pad = -bsz % SUBLANES
    if pad:
        cond = jnp.concatenate([cond, jnp.zeros((pad, c.shape[1]), F32)], axis=0)
    o_ref[...] = _dot(cond.astype(BF16), w_ref[...].astype(BF16))[:bsz] + b_ref[...]
    cast_out_ref[...] = cast_ref[...].astype(BF16)


def _ada(c, w_ada, b_ada, cast_weight):
    bsz, d = c.shape
    n = w_ada.shape[1]
    steps = ADA_STEPS
    tn = n // steps
    rows, cols = cast_weight.shape
    assert n % (steps * LANES) == 0 and rows % (16 * steps) == 0
    return pl.pallas_call(
        _ada_kernel,
        grid=(steps,),
        in_specs=[pl.BlockSpec((bsz, d), lambda j: (0, 0)),
                  pl.BlockSpec((d, tn), lambda j: (0, j)),
                  pl.BlockSpec((1, tn), lambda j: (0, j)),
                  pl.BlockSpec((rows // steps, cols), lambda j: (j, 0))],
        out_specs=[pl.BlockSpec((bsz, tn), lambda j: (0, j)),
                   pl.BlockSpec((rows // steps, cols), lambda j: (j, 0))],
        out_shape=[jax.ShapeDtypeStruct((bsz, n), F32),
                   jax.ShapeDtypeStruct((rows, cols), BF16)],
        compiler_params=pltpu.CompilerParams(vmem_limit_bytes=VMEM_LIMIT_BYTES),
        name="ada",
    )(c, w_ada, b_ada.reshape(1, n), cast_weight)


def _s5_kernel(u_ref, perm_ref, permt_ref, wb_ref, cm_ref, lamr_ref, lami_ref, d_ref,
               wglu_ref, bglu_ref, *rest, tm, sw, n_cast):
    cast_in, o_ref, cast_out = rest[:n_cast], rest[n_cast], rest[n_cast + 1:2 * n_cast + 1]
    x_scr, ulast_scr = rest[2 * n_cast + 1:]
    for src, dst in zip(cast_in, cast_out):
        dst[...] = src[...].astype(BF16)
    i = pl.program_id(0)
    ts = S5_SUBTILE
    n2 = ts // 2
    rows = SUBLANES * n2
    cw = sw // S5_SLABS
    hs = cw * S5_STATE // S5_GROUP

    @pl.when(i == 0)
    def _():
        x_scr[...] = jnp.zeros_like(x_scr)
        ulast_scr[...] = jnp.zeros_like(ulast_scr)

    sub8 = lax.broadcasted_iota(jnp.int32, (SUBLANES, sw), 0)
    odd = (lax.broadcasted_iota(jnp.int32, (rows, sw), 0) & 1) == 1

    def natural(j):
        return jnp.concatenate([u_ref[b, j * ts:(j + 1) * ts, :] for b in range(4)], axis=0)

    def last_rows(j):
        last = jnp.zeros((SUBLANES, sw), F32)
        for b in range(4):
            row = u_ref[b, (j + 1) * ts - 1:(j + 1) * ts, :].astype(BF16).astype(F32)
            last = jnp.where(sub8 == 2 * b, jnp.broadcast_to(row, (SUBLANES, sw)), last)
        return last

    def input_stage(j, before):
        a_cur = _dot(perm_ref[...], natural(j).astype(BF16))
        a_prev = jnp.where(odd, pltpu.roll(a_cur, 1, 0), pltpu.roll(a_cur, SUBLANES - 1, 0))
        first = jnp.where((sub8 & 1) == 1, a_prev[:SUBLANES], before)
        a_prev = jnp.concatenate([first, a_prev[SUBLANES:]], axis=0).astype(BF16)
        a_cur = a_cur.astype(BF16)
        return [_dot(jnp.concatenate([a_cur[:, s * cw:(s + 1) * cw],
                                      a_prev[:, s * cw:(s + 1) * cw]], axis=1), wb_ref[s])
                for s in range(S5_SLABS)]

    def scan_stage(bus, x):
        states, x_out = [], []
        for s in range(S5_SLABS):
            ar = lamr_ref[:, hs * s:hs * (s + 1)]
            ai = lami_ref[:, hs * s:hs * (s + 1)]
            xr, xi = x[s]
            st = []
            for t2 in range(n2):
                rs = slice(SUBLANES * t2, SUBLANES * (t2 + 1))
                xr, xi = (ar * xr - ai * xi + bus[s][rs, :hs],
                          ar * xi + ai * xr + bus[s][rs, hs:])
                st.append(jnp.concatenate([xr, xi], axis=1))
            x_out.append((xr, xi))
            states.append(jnp.concatenate(st, axis=0).astype(BF16))
        return states, x_out

    def output_stage(j, states):
        y_il = jnp.concatenate([_dot(states[s], cm_ref[s]) for s in range(S5_SLABS)], axis=1)
        y = _dot(permt_ref[...], y_il.astype(BF16))
        y = y + d_ref[...] * natural(j)
        y = jax.nn.gelu(y)
        z = _dot(y.astype(BF16), wglu_ref[...]) + bglu_ref[...]
        out = y * jax.nn.sigmoid(z)
        for b in range(4):
            o_ref[b, j * ts:(j + 1) * ts, :] = out[b * ts:(b + 1) * ts]

    nsub = tm // ts
    befores = [ulast_scr[...]] + [last_rows(j) for j in range(nsub - 1)]
    bus = [input_stage(j, befores[j]) for j in range(nsub)]
    ulast_scr[...] = last_rows(nsub - 1)
    x = [(x_scr[s, 0], x_scr[s, 1]) for s in range(S5_SLABS)]
    for j in range(nsub):
        states, x = scan_stage(bus[j], x)
        output_stage(j, states)
    for s in range(S5_SLABS):
        x_scr[s, 0], x_scr[s, 1] = x[s]


def _s5_perms(tm):
    n2 = tm // 2
    rows = SUBLANES * n2
    perm = np.zeros((rows, 4 * tm), np.float32)
    permt = np.zeros((4 * tm, rows), np.float32)
    for t2 in range(n2):
        for b in range(4):
            for par in range(2):
                r = SUBLANES * t2 + 2 * b + par
                t = 2 * t2 + par
                perm[r, b * tm + t] = 1.0
                permt[b * tm + t, r] = 1.0
    return jnp.asarray(perm, BF16), jnp.asarray(permt, BF16)


def _slab_block_diag(blocks):
    g, a, b = blocks.shape
    n = g // S5_SLABS
    eye = jnp.eye(n, dtype=blocks.dtype)
    placed = blocks.reshape(S5_SLABS, n, a, 1, b) * eye[None, :, None, :, None]
    return placed.reshape(S5_SLABS, n * a, n * b)


def _s5_weights(lam_re, lam_im, log_dt, b_re, b_im, c_re, c_im):
    g = lam_re.shape[0]
    dt = jnp.exp(log_dt)[:, None]
    mag = jnp.exp(lam_re * dt)
    lbr = mag * jnp.cos(lam_im * dt)
    lbi = mag * jnp.sin(lam_im * dt)
    nr, ni = lbr - 1.0, lbi
    den = lam_re * lam_re + lam_im * lam_im
    cr = (nr * lam_re + ni * lam_im) / den
    ci = (ni * lam_re - nr * lam_im) / den
    bbr = cr[..., None] * b_re - ci[..., None] * b_im
    bbi = cr[..., None] * b_im + ci[..., None] * b_re
    lr = lbr[..., None] * bbr - lbi[..., None] * bbi
    li = lbr[..., None] * bbi + lbi[..., None] * bbr
    l2r = lbr * lbr - lbi * lbi
    l2i = 2.0 * lbr * lbi
    bd = lambda a: _slab_block_diag(jnp.swapaxes(a, 1, 2))
    wb = jnp.concatenate([jnp.concatenate([bd(bbr), bd(bbi)], axis=2),
                          jnp.concatenate([bd(lr), bd(li)], axis=2)], axis=1).astype(BF16)
    cm = jnp.concatenate([bd(c_re), -bd(c_im)], axis=1).astype(BF16)
    lamr = jnp.broadcast_to(l2r.reshape(1, -1), (SUBLANES, l2r.size))
    lami = jnp.broadcast_to(l2i.reshape(1, -1), (SUBLANES, l2i.size))
    return wb, cm, lamr, lami


def _s5(u, wb, cm, lamr, lami, d_skip, w_glu_b, b_glu, cast_weights):
    bsz, seq, sw = u.shape
    assert bsz == 4, "the scan packs 4 batch rows x 2 token parities into 8 sublanes"
    tm = S5_TILE
    steps = seq // tm
    ns = lamr.shape[1]
    perm, permt = _s5_perms(S5_SUBTILE)
    const = lambda a: pl.BlockSpec(a.shape, lambda i: (0,) * a.ndim)
    rows = lambda a: pl.BlockSpec((a.shape[0] // steps, a.shape[1]), lambda i: (i, 0))
    assert all(w.shape[0] % (16 * steps) == 0 for w in cast_weights)
    d_row = d_skip.reshape(1, sw)
    bg = b_glu.reshape(1, sw)
    outs = pl.pallas_call(
        functools.partial(_s5_kernel, tm=tm, sw=sw, n_cast=len(cast_weights)),
        grid=(steps,),
        in_specs=[pl.BlockSpec((4, tm, sw), lambda i: (0, i, 0)),
                  const(perm), const(permt), const(wb), const(cm), const(lamr), const(lami),
                  const(d_row), const(w_glu_b), const(bg)] + [rows(w) for w in cast_weights],
        out_specs=[pl.BlockSpec((4, tm, sw), lambda i: (0, i, 0))] + [rows(w) for w in cast_weights],
        out_shape=[jax.ShapeDtypeStruct((bsz, seq, sw), F32)]
        + [jax.ShapeDtypeStruct(w.shape, BF16) for w in cast_weights],
        scratch_shapes=[pltpu.VMEM((S5_SLABS, 2, SUBLANES, ns // S5_SLABS), F32),
                        pltpu.VMEM((SUBLANES, sw), F32)],
        compiler_params=pltpu.CompilerParams(
            dimension_semantics=("arbitrary",),
            vmem_limit_bytes=VMEM_LIMIT_BYTES),
        name="s5",
    )(u, perm, permt, wb, cm, lamr, lami, d_row, w_glu_b, bg, *cast_weights)
    return outs[0], outs[1:]


def _front_kernel(x_ref, mod_ref, g_ref, w_ref, tri_ref, u_ref, gates_ref, at_ref,
                  q_scr, k_scr, v_scr, c_scr, acc_scr, z_scr, w_scr, *, sw, aw, d):
    blk = ATTN_BLOCK
    top = ATTN_TOP_ROWS
    n_pairs = aw // LANES
    nh = 2 * n_pairs
    tm = x_ref.shape[1]
    nsub = tm // blk
    step = pl.program_id(1)
    tile0 = pl.multiple_of(step * tm, tm)
    mod = _mod_rows(mod_ref, d)
    even_head = (lax.broadcasted_iota(jnp.int32, (tm, aw), 1) // HEAD_DIM) % 2 == 0

    def qkv(hb):
        o = sw
        q = (_dot(hb, w_ref[:, o:o + aw]) * Q_SCALE).astype(BF16); o += aw
        q_scr[0] = jnp.where(even_head, q, jnp.zeros_like(q))
        q_scr[1] = jnp.where(even_head, jnp.zeros_like(q), q)
        kt = _dot(hb, w_ref[:, o:o + aw]).T.astype(BF16); o += aw
        for j in range(nsub):
            k_scr[nsub * step + j] = kt[:, j * blk:(j + 1) * blk]
        v = _dot(hb, w_ref[:, o:o + aw]).astype(BF16)
        v_scr[0, pl.ds(tile0, tm), :] = jnp.where(even_head, v, jnp.zeros_like(v))
        v_scr[1, pl.ds(tile0, tm), :] = jnp.where(even_head, jnp.zeros_like(v), v)

    def region(units, fillers=()):
        starts = [pl.multiple_of(kb * blk, blk) for kb, _, _, _ in units]
        for u, (_, r0, r1, _) in enumerate(units):
            n = r1 - r0
            for p in range(n_pairs):
                ls = slice(p * LANES, (p + 1) * LANES)
                kblk = k_scr[units[u][0], ls, :]
                zz = _dot(jnp.concatenate([q_scr[0, r0:r1, ls], q_scr[1, r0:r1, ls]], axis=0), kblk)
                z_scr[u * nh + 2 * p, 0:n] = zz[:n]
                z_scr[u * nh + 2 * p + 1, 0:n] = zz[n:]
        fillers = list(fillers)
        every = -(-len(units) // (len(fillers) + 1))
        for u, (_, r0, r1, diag) in enumerate(units):
            n = r1 - r0
            if diag:
                row = lax.broadcasted_iota(jnp.int32, (n, blk), 0) + r0 % blk
                valid = lax.broadcasted_iota(jnp.int32, (n, blk), 1) < row
            zs, sps = [], []
            for h in range(nh):
                z = z_scr[u * nh + h, 0:n]
                sp = jnp.maximum(z, 0.0) + jnp.log2(1.0 + jnp.exp2(-jnp.abs(z)))
                if diag:
                    sp = jnp.where(valid, sp, 0.0)
                zs.append(z)
                sps.append(sp.astype(BF16))
            incl_all = _dot(jnp.concatenate(sps, axis=0), tri_ref[...])
            for h in range(nh):
                z = zs[h]
                incl = incl_all[h * n:(h + 1) * n]
                total = jnp.broadcast_to(incl[:, 0:1], (n, blk))
                if diag:
                    w = jnp.where(valid, jnp.exp2(z - incl), 0.0)
                    c_scr[h, r0:r1] = total
                else:
                    c = c_scr[h, r0:r1]
                    w = jnp.exp2(z - incl - c)
                    c_scr[h, r0:r1] = c + total
                w_scr[u * nh + h, 0:n] = w.astype(BF16)
            if fillers and (u + 1) % every == 0:
                fillers.pop(0)()
        for filler in fillers:
            filler()
        for r0, r1 in dict.fromkeys((r0, r1) for _, r0, r1, _ in units):
            us = [u for u, (_, a, b, _) in enumerate(units) if (a, b) == (r0, r1)]
            for p in range(n_pairs):
                ls = slice(p * LANES, (p + 1) * LANES)
                ww = jnp.concatenate([w_scr[u * nh + 2 * p + hh, 0:r1 - r0]
                                      for u in us for hh in range(2)], axis=1)
                vv = jnp.concatenate([v_scr[hh, pl.ds(starts[u], blk), ls]
                                      for u in us for hh in range(2)], axis=0)
                if any(units[u][3] for u in us):
                    acc_scr[p, r0:r1] = _dot(ww, vv)
                else:
                    acc_scr[p, r0:r1] = acc_scr[p, r0:r1] + _dot(ww, vv)

    def c_min(r0, r1):
        m = c_scr[0, r0:r1]
        for h in range(1, nh):
            m = jnp.minimum(m, c_scr[h, r0:r1])
        return jnp.min(m, axis=0, keepdims=True)[0, 0]

    def head_units(sub, qi, n_prev):
        base = sub * blk
        units = [(qi, base, base + blk, True)]
        if n_prev >= 1:
            units.append((qi - 1, base, base + blk, False))
        if n_prev >= 2:
            units.append((qi - 2, base, base + top, False))
        return units

    def tile(units):
        h = _rms(x_ref[0]) * g_ref[...]
        hb = (h * (1.0 + mod[1]) + mod[0]).astype(BF16)
        qkv(hb)
        o = sw + 3 * aw

        def chunk(ref, col, w0):
            def run():
                ref[0, :, col:col + FRONT_CHUNK] = _dot(hb, w_ref[:, w0 + col:w0 + col + FRONT_CHUNK])
            return run

        region(units, [chunk(ref, col, w0)
                       for ref, w0, width in ((gates_ref, o, 2 * d), (u_ref, 0, sw))
                       for col in range(0, width, FRONT_CHUNK)])

    @pl.when(step >= 1)
    def _():
        tile([u for sub in range(nsub) for u in head_units(sub, nsub * step + sub, 2)])

    @pl.when(step == 0)
    def _():
        tile([u for sub in range(nsub) for u in head_units(sub, sub, min(sub, 2))])

    def sweep(first_kb, cmin, r0, r1):
        def more(carry):
            kb, cmin = carry
            return jnp.logical_and(kb >= 0, cmin < UNDERFLOW_LOG2)

        def body(carry):
            kb, _ = carry
            region([(kb, r0, r1, False)])
            return kb - 1, c_min(r0, r1)

        lax.while_loop(more, body, (first_kb, cmin))

    tails = []
    for sub in range(nsub):
        qi = nsub * step + sub
        base = sub * blk
        tails.append((jnp.where(qi >= 2, qi - 3, -1), base, base + top))
        tails.append((jnp.where(qi >= 2, qi - 2, -1), base + top, base + blk))
    cmins = [c_min(r0, r1) for _, r0, r1 in tails]
    for (first_kb, r0, r1), cmin in zip(tails, cmins):
        sweep(first_kb, cmin, r0, r1)
    for p in range(n_pairs):
        at_ref[0, :, p * LANES:(p + 1) * LANES] = acc_scr[p].astype(at_ref.dtype)


def _attn_tri():
    blk = ATTN_BLOCK
    m = np.arange(blk)[:, None]
    j = np.arange(blk)[None, :]
    return jnp.asarray((m >= j).astype(np.float32), BF16)


def _front(x, mod, norm_g, w_in_b, sw, aw):
    bsz, seq, d = x.shape
    tm = FRONT_TILE
    blk = ATTN_BLOCK
    n = w_in_b.shape[1]
    n_pairs = aw // LANES
    nz = ATTN_REGION * (tm // blk) * 2 * n_pairs
    tri = _attn_tri()
    tok = lambda w: pl.BlockSpec((1, tm, w), lambda b, i: (b, i, 0))
    const = lambda a: pl.BlockSpec(a.shape, lambda b, i: (0,) * a.ndim,
                                   pipeline_mode=pl.Buffered(1))
    return pl.pallas_call(
        functools.partial(_front_kernel, sw=sw, aw=aw, d=d),
        grid=(bsz, seq // tm),
        in_specs=[tok(d),
                  pl.BlockSpec(mod.shape, lambda b, i: (0, 0)),
                  const(norm_g), const(w_in_b), const(tri)],
        out_specs=[tok(sw), tok(2 * d), tok(aw)],
        out_shape=[jax.ShapeDtypeStruct((bsz, seq, sw), F32),
                   jax.ShapeDtypeStruct((bsz, seq, 2 * d), F32),
                   jax.ShapeDtypeStruct((bsz, seq, aw), BF16)],
        scratch_shapes=[pltpu.VMEM((2, tm, aw), BF16),
                        pltpu.VMEM((seq // blk, aw, blk), BF16),
                        pltpu.VMEM((2, seq, aw), BF16),
                        pltpu.VMEM((2 * n_pairs, tm, blk), F32),
                        pltpu.VMEM((n_pairs, tm, LANES), F32),
                        pltpu.VMEM((nz, blk, blk), F32),
                        pltpu.VMEM((nz, blk, blk), BF16)],
        compiler_params=pltpu.CompilerParams(
            dimension_semantics=("arbitrary", "arbitrary"),
            vmem_limit_bytes=VMEM_LIMIT_BYTES),
        name="front",
    )(x, mod, norm_g, w_in_b, tri)


def _out_ffn_kernel(x_ref, s5_ref, at_ref, gates_ref, mod_ref, n2_ref, nf_ref,
                    wa_ref, wb_ref, wo_ref, wg_ref, wu_ref, wd_ref, o_ref, *, final_norm):
    mod = _mod_rows(mod_ref, x_ref.shape[2])
    tm = x_ref.shape[1]
    groups = [slice(r, r + tm // OUT_GROUPS) for r in range(0, tm, tm // OUT_GROUPS)]
    ms = []
    for g in groups:
        ya = _dot(s5_ref[0, g].astype(BF16), wa_ref[...])
        yb = _dot(at_ref[0, g], wb_ref[...])
        d = ya.shape[1]
        m = (jax.nn.sigmoid(gates_ref[0, g, :d]) * ya
             + jax.nn.sigmoid(gates_ref[0, g, d:]) * yb)
        ms.append(m.astype(BF16))
    x1s, hs = [], []
    for g, m in zip(groups, ms):
        x1 = x_ref[0, g] + mod[2] * _dot(m, wo_ref[...])
        h = _rms(x1) * n2_ref[...]
        x1s.append(x1)
        hs.append((h * (1.0 + mod[4]) + mod[3]).astype(BF16))
    acts = []
    for h in hs:
        gate = _dot(h, wg_ref[...])
        up = _dot(h, wu_ref[...])
        acts.append((gate * jax.nn.sigmoid(gate) * up).astype(BF16))
    for g, x1, act in zip(groups, x1s, acts):
        x2 = x1 + mod[5] * _dot(act, wd_ref[...])
        o_ref[0, g] = _rms(x2) * nf_ref[...] if final_norm else x2


def _out_ffn(x, s5o, attn, gates, mod, n2g, nfg, wa, wb, wo, wg, wu, wd, final_norm):
    bsz, seq, d = x.shape
    tm = OUT_TILE
    tok = lambda a: pl.BlockSpec((1, tm, a.shape[-1]), lambda b, i: (b, i, 0))
    const = lambda a: pl.BlockSpec(a.shape, lambda b, i: (0,) * a.ndim,
                                   pipeline_mode=pl.Buffered(1))
    return pl.pallas_call(
        functools.partial(_out_ffn_kernel, final_norm=final_norm),
        grid=(bsz, seq // tm),
        in_specs=[tok(x), tok(s5o), tok(attn), tok(gates),
                  pl.BlockSpec(mod.shape, lambda b, i: (0, 0)),
                  const(n2g), const(nfg),
                  const(wa), const(wb), const(wo), const(wg), const(wu), const(wd)],
        out_specs=pl.BlockSpec((1, tm, d), lambda b, i: (b, i, 0)),
        out_shape=jax.ShapeDtypeStruct((bsz, seq, d), F32),
        compiler_params=pltpu.CompilerParams(
            dimension_semantics=("arbitrary", "arbitrary"),
            vmem_limit_bytes=VMEM_LIMIT_BYTES),
        name="out_ffn",
    )(x, s5o, attn, gates, mod, n2g, nfg, wa, wb, wo, wg, wu, wd)


def kernel(x, c, w_ada, b_ada, norm1_g, w_in, lam_re, lam_im, log_dt, b_re, b_im, c_re, c_im,
           d_skip, w_glu, b_glu, w_a, w_b, w_o, norm2_g, w_ffn_gate, w_ffn_up, w_ffn_down,
           norm_f_g):
    depth = w_ada.shape[0]
    bsz, seq, d = x.shape
    sw = w_glu.shape[1]
    aw = w_b.shape[1]
    for l in range(depth):
        mod, w_in_b = _ada(c, w_ada[l], b_ada[l], w_in[l])
        u, gates, attn = _front(x, mod, norm1_g[l].reshape(1, d), w_in_b, sw, aw)
        wb, cm, lamr, lami = _s5_weights(lam_re[l], lam_im[l], log_dt[l], b_re[l], b_im[l],
                                         c_re[l], c_im[l])
        s5o, out_w = _s5(u, wb, cm, lamr, lami, d_skip[l], w_glu[l].astype(BF16), b_glu[l],
                         (w_a[l], w_b[l], w_o[l], w_ffn_gate[l], w_ffn_up[l], w_ffn_down[l]))
        x = _out_ffn(x, s5o, attn, gates, mod, norm2_g[l].reshape(1, d), norm_f_g.reshape(1, d),
                     *out_w, final_norm=(l == depth - 1))
    return x
```

```python
import functools
import math

import numpy as np
import jax
import jax.numpy as jnp
from jax import lax
from jax.experimental import pallas as pl
from jax.experimental.pallas import tpu as pltpu

F32 = jnp.float32
BF16 = jnp.bfloat16

S5_GROUP = 16
S5_STATE = 64
HEAD_DIM = 64
N_ADA = 6
RMS_EPS = 1e-6
Q_SCALE = math.log2(math.e) / math.sqrt(HEAD_DIM)
UNDERFLOW_LOG2 = 151.0

LANES = 128
SUBLANES = 8
VMEM_LIMIT_BYTES = 56 * 1024 * 1024

ADA_STEPS = 4
ATTN_BLOCK = 128
ATTN_REGION = 3
ATTN_TOP_ROWS = 32
S5_TILE = 256
S5_SUBTILE = 128
S5_SLABS = 4
FRONT_TILE = 512
FRONT_CHUNK = 512
OUT_TILE = 512
OUT_GROUPS = 2


def _dot(a, b):
    return jnp.dot(a, b, preferred_element_type=F32)


def _rms(x):
    return x * lax.rsqrt(jnp.mean(x * x, axis=-1, keepdims=True) + RMS_EPS)


def _mod_rows(mod_ref, d):
    row = mod_ref[pl.ds(pl.program_id(0), 1), :]
    return [row[:, k * d:(k + 1) * d] for k in range(N_ADA)]


def _ada_kernel(c_ref, w_ref, b_ref, cast_ref, o_ref, cast_out_ref):
    c = c_ref[...]
    bsz = c.shape[0]
    cond = c * jax.nn.sigmoid(c)
    pad = -bsz % SUBLANES
    if pad:
        cond = jnp.concatenate([cond, jnp.zeros((pad, c.shape[1]), F32)], axis=0)
    o_ref[...] = _dot(cond.astype(BF16), w_ref[...].astype(BF16))[:bsz] + b_ref[...]
    cast_out_ref[...] = cast_ref[...].astype(BF16)


def _ada(c, w_ada, b_ada, cast_weight):
    bsz, d = c.shape
    n = w_ada.shape[1]
    steps = ADA_STEPS
    tn = n // steps
    rows, cols = cast_weight.shape
    assert n % (steps * LANES) == 0 and rows % (16 * steps) == 0
    return pl.pallas_call(
        _ada_kernel,
        grid=(steps,),
        in_specs=[pl.BlockSpec((bsz, d), lambda j: (0, 0)),
                  pl.BlockSpec((d, tn), lambda j: (0, j)),
                  pl.BlockSpec((1, tn), lambda j: (0, j)),
                  pl.BlockSpec((rows // steps, cols), lambda j: (j, 0))],
        out_specs=[pl.BlockSpec((bsz, tn), lambda j: (0, j)),
                   pl.BlockSpec((rows // steps, cols), lambda j: (j, 0))],
        out_shape=[jax.ShapeDtypeStruct((bsz, n), F32),
                   jax.ShapeDtypeStruct((rows, cols), BF16)],
        compiler_params=pltpu.CompilerParams(vmem_limit_bytes=VMEM_LIMIT_BYTES),
        name="ada",
    )(c, w_ada, b_ada.reshape(1, n), cast_weight)


def _s5_kernel(u_ref, perm_ref, permt_ref, wb_ref, cm_ref, lamr_ref, lami_ref, d_ref,
               wglu_ref, bglu_ref, *rest, tm, sw, n_cast):
    cast_in, o_ref, cast_out = rest[:n_cast], rest[n_cast], rest[n_cast + 1:2 * n_cast + 1]
    x_scr, ulast_scr = rest[2 * n_cast + 1:]
    for src, dst in zip(cast_in, cast_out):
        dst[...] = src[...].astype(BF16)
    i = pl.program_id(0)
    ts = S5_SUBTILE
    n2 = ts // 2
    rows = SUBLANES * n2
    cw = sw // S5_SLABS
    hs = cw * S5_STATE // S5_GROUP

    @pl.when(i == 0)
    def _():
        x_scr[...] = jnp.zeros_like(x_scr)
        ulast_scr[...] = jnp.zeros_like(ulast_scr)

    sub8 = lax.broadcasted_iota(jnp.int32, (SUBLANES, sw), 0)
    odd = (lax.broadcasted_iota(jnp.int32, (rows, sw), 0) & 1) == 1

    def natural(j):
        return jnp.concatenate([u_ref[b, j * ts:(j + 1) * ts, :] for b in range(4)], axis=0)

    def last_rows(j):
        last = jnp.zeros((SUBLANES, sw), F32)
        for b in range(4):
            row = u_ref[b, (j + 1) * ts - 1:(j + 1) * ts, :].astype(BF16).astype(F32)
            last = jnp.where(sub8 == 2 * b, jnp.broadcast_to(row, (SUBLANES, sw)), last)
        return last

    def input_stage(j, before):
        a_cur = _dot(perm_ref[...], natural(j).astype(BF16))
        a_prev = jnp.where(odd, pltpu.roll(a_cur, 1, 0), pltpu.roll(a_cur, SUBLANES - 1, 0))
        first = jnp.where((sub8 & 1) == 1, a_prev[:SUBLANES], before)
        a_prev = jnp.concatenate([first, a_prev[SUBLANES:]], axis=0).astype(BF16)
        a_cur = a_cur.astype(BF16)
        return [_dot(jnp.concatenate([a_cur[:, s * cw:(s + 1) * cw],
                                      a_prev[:, s * cw:(s + 1) * cw]], axis=1), wb_ref[s])
                for s in range(S5_SLABS)]

    def scan_stage(bus, x):
        states, x_out = [], []
        for s in range(S5_SLABS):
            ar = lamr_ref[:, hs * s:hs * (s + 1)]
            ai = lami_ref[:, hs * s:hs * (s + 1)]
            xr, xi = x[s]
            st = []
            for t2 in range(n2):
                rs = slice(SUBLANES * t2, SUBLANES * (t2 + 1))
                xr, xi = (ar * xr - ai * xi + bus[s][rs, :hs],
                          ar * xi + ai * xr + bus[s][rs, hs:])
                st.append(jnp.concatenate([xr, xi], axis=1))
            x_out.append((xr, xi))
            states.append(jnp.concatenate(st, axis=0).astype(BF16))
        return states, x_out

    def output_stage(j, states):
        y_il = jnp.concatenate([_dot(states[s], cm_ref[s]) for s in range(S5_SLABS)], axis=1)
        y = _dot(permt_ref[...], y_il.astype(BF16))
        y = y + d_ref[...] * natural(j)
        y = jax.nn.gelu(y)
        z = _dot(y.astype(BF16), wglu_ref[...]) + bglu_ref[...]
        out = y * jax.nn.sigmoid(z)
        for b in range(4):
            o_ref[b, j * ts:(j + 1) * ts, :] = out[b * ts:(b + 1) * ts]

    nsub = tm // ts
    befores = [ulast_scr[...]] + [last_rows(j) for j in range(nsub - 1)]
    bus = [input_stage(j, befores[j]) for j in range(nsub)]
    ulast_scr[...] = last_rows(nsub - 1)
    x = [(x_scr[s, 0], x_scr[s, 1]) for s in range(S5_SLABS)]
    for j in range(nsub):
        states, x = scan_stage(bus[j], x)
        output_stage(j, states)
    for s in range(S5_SLABS):
        x_scr[s, 0], x_scr[s, 1] = x[s]


def _s5_perms(tm):
    n2 = tm // 2
    rows = SUBLANES * n2
    perm = np.zeros((rows, 4 * tm), np.float32)
    permt = np.zeros((4 * tm, rows), np.float32)
    for t2 in range(n2):
        for b in range(4):
            for par in range(2):
                r = SUBLANES * t2 + 2 * b + par
                t = 2 * t2 + par
                perm[r, b * tm + t] = 1.0
                permt[b * tm + t, r] = 1.0
    return jnp.asarray(perm, BF16), jnp.asarray(permt, BF16)


def _slab_block_diag(blocks):
    g, a, b = blocks.shape
    n = g // S5_SLABS
    eye = jnp.eye(n, dtype=blocks.dtype)
    placed = blocks.reshape(S5_SLABS, n, a, 1, b) * eye[None, :, None, :, None]
    return placed.reshape(S5_SLABS, n * a, n * b)


def _s5_weights(lam_re, lam_im, log_dt, b_re, b_im, c_re, c_im):
    g = lam_re.shape[0]
    dt = jnp.exp(log_dt)[:, None]
    mag = jnp.exp(lam_re * dt)
    lbr = mag * jnp.cos(lam_im * dt)
    lbi = mag * jnp.sin(lam_im * dt)
    nr, ni = lbr - 1.0, lbi
    den = lam_re * lam_re + lam_im * lam_im
    cr = (nr * lam_re + ni * lam_im) / den
    ci = (ni * lam_re - nr * lam_im) / den
    bbr = cr[..., None] * b_re - ci[..., None] * b_im
    bbi = cr[..., None] * b_im + ci[..., None] * b_re
    lr = lbr[..., None] * bbr - lbi[..., None] * bbi
    li = lbr[..., None] * bbi + lbi[..., None] * bbr
    l2r = lbr * lbr - lbi * lbi
    l2i = 2.0 * lbr * lbi
    bd = lambda a: _slab_block_diag(jnp.swapaxes(a, 1, 2))
    wb = jnp.concatenate([jnp.concatenate([bd(bbr), bd(bbi)], axis=2),
                          jnp.concatenate([bd(lr), bd(li)], axis=2)], axis=1).astype(BF16)
    cm = jnp.concatenate([bd(c_re), -bd(c_im)], axis=1).astype(BF16)
    lamr = jnp.broadcast_to(l2r.reshape(1, -1), (SUBLANES, l2r.size))
    lami = jnp.broadcast_to(l2i.reshape(1, -1), (SUBLANES, l2i.size))
    return wb, cm, lamr, lami


def _s5(u, wb, cm, lamr, lami, d_skip, w_glu_b, b_glu, cast_weights):
    bsz, seq, sw = u.shape
    assert bsz == 4, "the scan packs 4 batch rows x 2 token parities into 8 sublanes"
    tm = S5_TILE
    steps = seq // tm
    ns = lamr.shape[1]
    perm, permt = _s5_perms(S5_SUBTILE)
    const = lambda a: pl.BlockSpec(a.shape, lambda i: (0,) * a.ndim)
    rows = lambda a: pl.BlockSpec((a.shape[0] // steps, a.shape[1]), lambda i: (i, 0))
    assert all(w.shape[0] % (16 * steps) == 0 for w in cast_weights)
    d_row = d_skip.reshape(1, sw)
    bg = b_glu.reshape(1, sw)
    outs = pl.pallas_call(
        functools.partial(_s5_kernel, tm=tm, sw=sw, n_cast=len(cast_weights)),
        grid=(steps,),
        in_specs=[pl.BlockSpec((4, tm, sw), lambda i: (0, i, 0)),
                  const(perm), const(permt), const(wb), const(cm), const(lamr), const(lami),
                  const(d_row), const(w_glu_b), const(bg)] + [rows(w) for w in cast_weights],
        out_specs=[pl.BlockSpec((4, tm, sw), lambda i: (0, i, 0))] + [rows(w) for w in cast_weights],
        out_shape=[jax.ShapeDtypeStruct((bsz, seq, sw), F32)]
        + [jax.ShapeDtypeStruct(w.shape, BF16) for w in cast_weights],
        scratch_shapes=[pltpu.VMEM((S5_SLABS, 2, SUBLANES, ns // S5_SLABS), F32),
                        pltpu.VMEM((SUBLANES, sw), F32)],
        compiler_params=pltpu.CompilerParams(
            dimension_semantics=("arbitrary",),
            vmem_limit_bytes=VMEM_LIMIT_BYTES),
        name="s5",
    )(u, perm, permt, wb, cm, lamr, lami, d_row, w_glu_b, bg, *cast_weights)
    return outs[0], outs[1:]


def _front_kernel(x_ref, mod_ref, g_ref, w_ref, tri_ref, u_ref, gates_ref, at_ref,
                  q_scr, k_scr, v_scr, c_scr, cmin_scr, acc_scr, z_scr, w_scr, *, sw, aw, d):
    blk = ATTN_BLOCK
    top = ATTN_TOP_ROWS
    n_pairs = aw // LANES
    nh = 2 * n_pairs
    tm = x_ref.shape[1]
    nsub = tm // blk
    step = pl.program_id(1)
    tile0 = pl.multiple_of(step * tm, tm)
    mod = _mod_rows(mod_ref, d)
    even_head = (lax.broadcasted_iota(jnp.int32, (tm, aw), 1) // HEAD_DIM) % 2 == 0

    def qkv(hb):
        o = sw
        q = (_dot(hb, w_ref[:, o:o + aw]) * Q_SCALE).astype(BF16); o += aw
        q_scr[0] = jnp.where(even_head, q, jnp.zeros_like(q))
        q_scr[1] = jnp.where(even_head, jnp.zeros_like(q), q)
        kt = _dot(hb, w_ref[:, o:o + aw]).T.astype(BF16); o += aw
        for j in range(nsub):
            k_scr[nsub * step + j] = kt[:, j * blk:(j + 1) * blk]
        v = _dot(hb, w_ref[:, o:o + aw]).astype(BF16)
        v_scr[0, pl.ds(tile0, tm), :] = jnp.where(even_head, v, jnp.zeros_like(v))
        v_scr[1, pl.ds(tile0, tm), :] = jnp.where(even_head, jnp.zeros_like(v), v)

    def region(units, fillers=()):
        starts = [pl.multiple_of(kb * blk, blk) for kb, _, _, _ in units]
        for u, (_, r0, r1, _) in enumerate(units):
            n = r1 - r0
            for p in range(n_pairs):
                ls = slice(p * LANES, (p + 1) * LANES)
                kblk = k_scr[units[u][0], ls, :]
                zz = _dot(jnp.concatenate([q_scr[0, r0:r1, ls], q_scr[1, r0:r1, ls]], axis=0), kblk)
                z_scr[u * nh + 2 * p, 0:n] = zz[:n]
                z_scr[u * nh + 2 * p + 1, 0:n] = zz[n:]
        fillers = list(fillers)
        every = -(-len(units) // (len(fillers) + 1))
        for u, (_, r0, r1, diag) in enumerate(units):
            n = r1 - r0
            if diag:
                row = lax.broadcasted_iota(jnp.int32, (n, blk), 0) + r0 % blk
                valid = lax.broadcasted_iota(jnp.int32, (n, blk), 1) < row
            zs, sps = [], []
            for h in range(nh):
                z = z_scr[u * nh + h, 0:n]
                sp = jnp.maximum(z, 0.0) + jnp.log2(1.0 + jnp.exp2(-jnp.abs(z)))
                if diag:
                    sp = jnp.where(valid, sp, 0.0)
                zs.append(z)
                sps.append(sp.astype(BF16))
            incl_all = _dot(jnp.concatenate(sps, axis=0), tri_ref[...])
            for h in range(nh):
                z = zs[h]
                incl = incl_all[h * n:(h + 1) * n]
                total = jnp.broadcast_to(incl[:, 0:1], (n, blk))
                if diag:
                    w = jnp.where(valid, jnp.exp2(z - incl), 0.0)
                    c_new = total
                else:
                    c = c_scr[h, r0:r1]
                    w = jnp.exp2(z - incl - c)
                    c_new = c + total
                c_scr[h, r0:r1] = c_new
                c_low = c_new if h == 0 else jnp.minimum(c_low, c_new)
                w_scr[u * nh + h, 0:n] = w.astype(BF16)
            cmin_scr[r0:r1] = c_low
            if fillers and (u + 1) % every == 0:
                fillers.pop(0)()
        for filler in fillers:
            filler()
        for r0, r1 in dict.fromkeys((r0, r1) for _, r0, r1, _ in units):
            us = [u for u, (_, a, b, _) in enumerate(units) if (a, b) == (r0, r1)]
            for p in range(n_pairs):
                ls = slice(p * LANES, (p + 1) * LANES)
                ww = jnp.concatenate([w_scr[u * nh + 2 * p + hh, 0:r1 - r0]
                                      for u in us for hh in range(2)], axis=1)
                vv = jnp.concatenate([v_scr[hh, pl.ds(starts[u], blk), ls]
                                      for u in us for hh in range(2)], axis=0)
                if any(units[u][3] for u in us):
                    acc_scr[p, r0:r1] = _dot(ww, vv)
                else:
                    acc_scr[p, r0:r1] = acc_scr[p, r0:r1] + _dot(ww, vv)

    def c_min(r0, r1):
        return jnp.min(cmin_scr[r0:r1], axis=0, keepdims=True)[0, 0]

    def head_units(sub, qi, n_prev):
        base = sub * blk
        units = [(qi, base, base + blk, True)]
        if n_prev >= 1:
            units.append((qi - 1, base, base + blk, False))
        if n_prev >= 2:
            units.append((qi - 2, base, base + top, False))
        return units

    def tile(units):
        h = _rms(x_ref[0]) * g_ref[...]
        hb = (h * (1.0 + mod[1]) + mod[0]).astype(BF16)
        qkv(hb)
        o = sw + 3 * aw

        def chunk(ref, col, w0):
            def run():
                ref[0, :, col:col + FRONT_CHUNK] = _dot(hb, w_ref[:, w0 + col:w0 + col + FRONT_CHUNK])
            return run

        region(units, [chunk(ref, col, w0)
                       for ref, w0, width in ((gates_ref, o, 2 * d), (u_ref, 0, sw))
                       for col in range(0, width, FRONT_CHUNK)])

    @pl.when(step >= 1)
    def _():
        tile([u for sub in range(nsub) for u in head_units(sub, nsub * step + sub, 2)])

    @pl.when(step == 0)
    def _():
        tile([u for sub in range(nsub) for u in head_units(sub, sub, min(sub, 2))])

    def sweep(first_kb, cmin, r0, r1):
        def more(carry):
            kb, cmin = carry
            return jnp.logical_and(kb >= 0, cmin < UNDERFLOW_LOG2)

        def body(carry):
            kb, _ = carry
            region([(kb, r0, r1, False)])
            return kb - 1, c_min(r0, r1)

        lax.while_loop(more, body, (first_kb, cmin))

    tails = []
    for sub in range(nsub):
        qi = nsub * step + sub
        base = sub * blk
        tails.append((jnp.where(qi >= 2, qi - 3, -1), base, base + top))
        tails.append((jnp.where(qi >= 2, qi - 2, -1), base + top, base + blk))
    cmins = [c_min(r0, r1) for _, r0, r1 in tails]
    for (first_kb, r0, r1), cmin in zip(tails, cmins):
        sweep(first_kb, cmin, r0, r1)
    for p in range(n_pairs):
        at_ref[0, :, p * LANES:(p + 1) * LANES] = acc_scr[p].astype(at_ref.dtype)


def _attn_tri():
    blk = ATTN_BLOCK
    m = np.arange(blk)[:, None]
    j = np.arange(blk)[None, :]
    return jnp.asarray((m >= j).astype(np.float32), BF16)


def _front(x, mod, norm_g, w_in_b, sw, aw):
    bsz, seq, d = x.shape
    tm = FRONT_TILE
    blk = ATTN_BLOCK
    n = w_in_b.shape[1]
    n_pairs = aw // LANES
    nz = ATTN_REGION * (tm // blk) * 2 * n_pairs
    tri = _attn_tri()
    tok = lambda w: pl.BlockSpec((1, tm, w), lambda b, i: (b, i, 0))
    const = lambda a: pl.BlockSpec(a.shape, lambda b, i: (0,) * a.ndim,
                                   pipeline_mode=pl.Buffered(1))
    return pl.pallas_call(
        functools.partial(_front_kernel, sw=sw, aw=aw, d=d),
        grid=(bsz, seq // tm),
        in_specs=[tok(d),
                  pl.BlockSpec(mod.shape, lambda b, i: (0, 0)),
                  const(norm_g), const(w_in_b), const(tri)],
        out_specs=[tok(sw), tok(2 * d), tok(aw)],
        out_shape=[jax.ShapeDtypeStruct((bsz, seq, sw), F32),
                   jax.ShapeDtypeStruct((bsz, seq, 2 * d), F32),
                   jax.ShapeDtypeStruct((bsz, seq, aw), BF16)],
        scratch_shapes=[pltpu.VMEM((2, tm, aw), BF16),
                        pltpu.VMEM((seq // blk, aw, blk), BF16),
                        pltpu.VMEM((2, seq, aw), BF16),
                        pltpu.VMEM((2 * n_pairs, tm, blk), F32),
                        pltpu.VMEM((tm, blk), F32),
                        pltpu.VMEM((n_pairs, tm, LANES), F32),
                        pltpu.VMEM((nz, blk, blk), F32),
                        pltpu.VMEM((nz, blk, blk), BF16)],
        compiler_params=pltpu.CompilerParams(
            dimension_semantics=("arbitrary", "arbitrary"),
            vmem_limit_bytes=VMEM_LIMIT_BYTES),
        name="front",
    )(x, mod, norm_g, w_in_b, tri)


def _out_ffn_kernel(x_ref, s5_ref, at_ref, gates_ref, mod_ref, n2_ref, nf_ref,
                    wa_ref, wb_ref, wo_ref, wg_ref, wu_ref, wd_ref, o_ref, *, final_norm):
    mod = _mod_rows(mod_ref, x_ref.shape[2])
    tm = x_ref.shape[1]
    groups = [slice(r, r + tm // OUT_GROUPS) for r in range(0, tm, tm // OUT_GROUPS)]
    ms = []
    for g in groups:
        ya = _dot(s5_ref[0, g].astype(BF16), wa_ref[...])
        yb = _dot(at_ref[0, g], wb_ref[...])
        d = ya.shape[1]
        m = (jax.nn.sigmoid(gates_ref[0, g, :d]) * ya
             + jax.nn.sigmoid(gates_ref[0, g, d:]) * yb)
        ms.append(m.astype(BF16))
    x1s, hs = [], []
    for g, m in zip(groups, ms):
        x1 = x_ref[0, g] + mod[2] * _dot(m, wo_ref[...])
        h = _rms(x1) * n2_ref[...]
        x1s.append(x1)
        hs.append((h * (1.0 + mod[4]) + mod[3]).astype(BF16))
    acts = []
    for h in hs:
        gate = _dot(h, wg_ref[...])
        up = _dot(h, wu_ref[...])
        acts.append((gate * jax.nn.sigmoid(gate) * up).astype(BF16))
    for g, x1, act in zip(groups, x1s, acts):
        x2 = x1 + mod[5] * _dot(act, wd_ref[...])
        o_ref[0, g] = _rms(x2) * nf_ref[...] if final_norm else x2


def _out_ffn(x, s5o, attn, gates, mod, n2g, nfg, wa, wb, wo, wg, wu, wd, final_norm):
    bsz, seq, d = x.shape
    tm = OUT_TILE
    tok = lambda a: pl.BlockSpec((1, tm, a.shape[-1]), lambda b, i: (b, i, 0))
    const = lambda a: pl.BlockSpec(a.shape, lambda b, i: (0,) * a.ndim,
                                   pipeline_mode=pl.Buffered(1))
    return pl.pallas_call(
        functools.partial(_out_ffn_kernel, final_norm=final_norm),
        grid=(bsz, seq // tm),
        in_specs=[tok(x), tok(s5o), tok(attn), tok(gates),
                  pl.BlockSpec(mod.shape, lambda b, i: (0, 0)),
                  const(n2g), const(nfg),
                  const(wa), const(wb), const(wo), const(wg), const(wu), const(wd)],
        out_specs=pl.BlockSpec((1, tm, d), lambda b, i: (b, i, 0)),
        out_shape=jax.ShapeDtypeStruct((bsz, seq, d), F32),
        compiler_params=pltpu.CompilerParams(
            dimension_semantics=("arbitrary", "arbitrary"),
            vmem_limit_bytes=VMEM_LIMIT_BYTES),
        name="out_ffn",
    )(x, s5o, attn, gates, mod, n2g, nfg, wa, wb, wo, wg, wu, wd)


def kernel(x, c, w_ada, b_ada, norm1_g, w_in, lam_re, lam_im, log_dt, b_re, b_im, c_re, c_im,
           d_skip, w_glu, b_glu, w_a, w_b, w_o, norm2_g, w_ffn_gate, w_ffn_up, w_ffn_down,
           norm_f_g):
    depth = w_ada.shape[0]
    bsz, seq, d = x.shape
    sw = w_glu.shape[1]
    aw = w_b.shape[1]
    for l in range(depth):
        mod, w_in_b = _ada(c, w_ada[l], b_ada[l], w_in[l])
        u, gates, attn = _front(x, mod, norm1_g[l].reshape(1, d), w_in_b, sw, aw)
        wb, cm, lamr, lami = _s5_weights(lam_re[l], lam_im[l], log_dt[l], b_re[l], b_im[l],
                                         c_re[l], c_im[l])
        s5o, out_w = _s5(u, wb, cm, lamr, lami, d_skip[l], w_glu[l].astype(BF16), b_glu[l],
                         (w_a[l], w_b[l], w_o[l], w_ffn_gate[l], w_ffn_up[l], w_ffn_down[l]))
        x = _out_ffn(x, s5o, attn, gates, mod, norm2_g[l].reshape(1, d), norm_f_g.reshape(1, d),
                     *out_w, final_norm=(l == depth - 1))
    return x
```

```python
import functools
import math

import numpy as np
import jax
import jax.numpy as jnp
from jax import lax
from jax.experimental import pallas as pl
from jax.experimental.pallas import tpu as pltpu

F32 = jnp.float32
BF16 = jnp.bfloat16

S5_GROUP = 16
S5_STATE = 64
HEAD_DIM = 64
N_ADA = 6
RMS_EPS = 1e-6
Q_SCALE = math.log2(math.e) / math.sqrt(HEAD_DIM)
UNDERFLOW_LOG2 = 151.0

LANES = 128
SUBLANES = 8
VMEM_LIMIT_BYTES = 56 * 1024 * 1024

ADA_STEPS = 4
ATTN_BLOCK = 128
ATTN_REGION = 3
ATTN_TOP_ROWS = 32
S5_TILE = 256
S5_SUBTILE = 128
S5_SLABS = 4
FRONT_TILE = 512
FRONT_CHUNK = 512
OUT_TILE = 512
OUT_GROUPS = 2


def _dot(a, b):
    return jnp.dot(a, b, preferred_element_type=F32)


def _rms(x):
    return x * lax.rsqrt(jnp.mean(x * x, axis=-1, keepdims=True) + RMS_EPS)


def _mod_rows(mod_ref, d):
    row = mod_ref[pl.ds(pl.program_id(0), 1), :]
    return [row[:, k * d:(k + 1) * d] for k in range(N_ADA)]


def _ada_kernel(c_ref, w_ref, b_ref, cast_ref, o_ref, cast_out_ref):
    c = c_ref[...]
    bsz = c.shape[0]
    cond = c * jax.nn.sigmoid(c)
    pad = -bsz % SUBLANES
    if pad:
        cond = jnp.concatenate([cond, jnp.zeros((pad, c.shape[1]), F32)], axis=0)
    o_ref[...] = _dot(cond.astype(BF16), w_ref[...].astype(BF16))[:bsz] + b_ref[...]
    cast_out_ref[...] = cast_ref[...].astype(BF16)


def _ada(c, w_ada, b_ada, cast_weight):
    bsz, d = c.shape
    n = w_ada.shape[1]
    steps = ADA_STEPS
    tn = n // steps
    rows, cols = cast_weight.shape
    assert n % (steps * LANES) == 0 and rows % (16 * steps) == 0
    return pl.pallas_call(
        _ada_kernel,
        grid=(steps,),
        in_specs=[pl.BlockSpec((bsz, d), lambda j: (0, 0)),
                  pl.BlockSpec((d, tn), lambda j: (0, j)),
                  pl.BlockSpec((1, tn), lambda j: (0, j)),
                  pl.BlockSpec((rows // steps, cols), lambda j: (j, 0))],
        out_specs=[pl.BlockSpec((bsz, tn), lambda j: (0, j)),
                   pl.BlockSpec((rows // steps, cols), lambda j: (j, 0))],
        out_shape=[jax.ShapeDtypeStruct((bsz, n), F32),
                   jax.ShapeDtypeStruct((rows, cols), BF16)],
        compiler_params=pltpu.CompilerParams(vmem_limit_bytes=VMEM_LIMIT_BYTES),
        name="ada",
    )(c, w_ada, b_ada.reshape(1, n), cast_weight)


def _s5_kernel(u_ref, perm_ref, permt_ref, wb_ref, cm_ref, lamr_ref, lami_ref, d_ref,
               wglu_ref, bglu_ref, *rest, tm, sw, n_cast):
    cast_in, o_ref, cast_out = rest[:n_cast], rest[n_cast], rest[n_cast + 1:2 * n_cast + 1]
    x_scr, ulast_scr = rest[2 * n_cast + 1:]
    for src, dst in zip(cast_in, cast_out):
        dst[...] = src[...].astype(BF16)
    i = pl.program_id(0)
    ts = S5_SUBTILE
    n2 = ts // 2
    rows = SUBLANES * n2
    cw = sw // S5_SLABS
    hs = cw * S5_STATE // S5_GROUP

    @pl.when(i == 0)
    def _():
        x_scr[...] = jnp.zeros_like(x_scr)
        ulast_scr[...] = jnp.zeros_like(ulast_scr)

    sub8 = lax.broadcasted_iota(jnp.int32, (SUBLANES, sw), 0)
    odd = (lax.broadcasted_iota(jnp.int32, (rows, sw), 0) & 1) == 1

    def natural(j):
        return jnp.concatenate([u_ref[b, j * ts:(j + 1) * ts, :] for b in range(4)], axis=0)

    def last_rows(j):
        last = jnp.zeros((SUBLANES, sw), F32)
        for b in range(4):
            row = u_ref[b, (j + 1) * ts - 1:(j + 1) * ts, :].astype(BF16).astype(F32)
            last = jnp.where(sub8 == 2 * b, jnp.broadcast_to(row, (SUBLANES, sw)), last)
        return last

    def input_stage(j, before):
        a_cur = _dot(perm_ref[...], natural(j).astype(BF16))
        a_prev = jnp.where(odd, pltpu.roll(a_cur, 1, 0), pltpu.roll(a_cur, SUBLANES - 1, 0))
        first = jnp.where((sub8 & 1) == 1, a_prev[:SUBLANES], before)
        a_prev = jnp.concatenate([first, a_prev[SUBLANES:]], axis=0).astype(BF16)
        a_cur = a_cur.astype(BF16)
        return [_dot(jnp.concatenate([a_cur[:, s * cw:(s + 1) * cw],
                                      a_prev[:, s * cw:(s + 1) * cw]], axis=1), wb_ref[s])
                for s in range(S5_SLABS)]

    def scan_stage(bus, x):
        states, x_out = [], []
        for s in range(S5_SLABS):
            ar = lamr_ref[:, hs * s:hs * (s + 1)]
            ai = lami_ref[:, hs * s:hs * (s + 1)]
            xr, xi = x[s]
            st = []
            for t2 in range(n2):
                rs = slice(SUBLANES * t2, SUBLANES * (t2 + 1))
                xr, xi = (ar * xr - ai * xi + bus[s][rs, :hs],
                          ar * xi + ai * xr + bus[s][rs, hs:])
                st.append(jnp.concatenate([xr, xi], axis=1))
            x_out.append((xr, xi))
            states.append(jnp.concatenate(st, axis=0).astype(BF16))
        return states, x_out

    def output_stage(j, states):
        y_il = jnp.concatenate([_dot(states[s], cm_ref[s]) for s in range(S5_SLABS)], axis=1)
        y = _dot(permt_ref[...], y_il.astype(BF16))
        y = y + d_ref[...] * natural(j)
        y = jax.nn.gelu(y)
        z = _dot(y.astype(BF16), wglu_ref[...]) + bglu_ref[...]
        out = y * jax.nn.sigmoid(z)
        for b in range(4):
            o_ref[b, j * ts:(j + 1) * ts, :] = out[b * ts:(b + 1) * ts]

    nsub = tm // ts
    befores = [ulast_scr[...]] + [last_rows(j) for j in range(nsub - 1)]
    bus = [input_stage(j, befores[j]) for j in range(nsub)]
    ulast_scr[...] = last_rows(nsub - 1)
    x = [(x_scr[s, 0], x_scr[s, 1]) for s in range(S5_SLABS)]
    for j in range(nsub):
        states, x = scan_stage(bus[j], x)
        output_stage(j, states)
    for s in range(S5_SLABS):
        x_scr[s, 0], x_scr[s, 1] = x[s]


def _s5_perms(tm):
    n2 = tm // 2
    rows = SUBLANES * n2
    perm = np.zeros((rows, 4 * tm), np.float32)
    permt = np.zeros((4 * tm, rows), np.float32)
    for t2 in range(n2):
        for b in range(4):
            for par in range(2):
                r = SUBLANES * t2 + 2 * b + par
                t = 2 * t2 + par
                perm[r, b * tm + t] = 1.0
                permt[b * tm + t, r] = 1.0
    return jnp.asarray(perm, BF16), jnp.asarray(permt, BF16)


def _slab_block_diag(blocks):
    g, a, b = blocks.shape
    n = g // S5_SLABS
    eye = jnp.eye(n, dtype=blocks.dtype)
    placed = blocks.reshape(S5_SLABS, n, a, 1, b) * eye[None, :, None, :, None]
    return placed.reshape(S5_SLABS, n * a, n * b)


def _s5_weights(lam_re, lam_im, log_dt, b_re, b_im, c_re, c_im):
    g = lam_re.shape[0]
    dt = jnp.exp(log_dt)[:, None]
    mag = jnp.exp(lam_re * dt)
    lbr = mag * jnp.cos(lam_im * dt)
    lbi = mag * jnp.sin(lam_im * dt)
    nr, ni = lbr - 1.0, lbi
    den = lam_re * lam_re + lam_im * lam_im
    cr = (nr * lam_re + ni * lam_im) / den
    ci = (ni * lam_re - nr * lam_im) / den
    bbr = cr[..., None] * b_re - ci[..., None] * b_im
    bbi = cr[..., None] * b_im + ci[..., None] * b_re
    lr = lbr[..., None] * bbr - lbi[..., None] * bbi
    li = lbr[..., None] * bbi + lbi[..., None] * bbr
    l2r = lbr * lbr - lbi * lbi
    l2i = 2.0 * lbr * lbi
    bd = lambda a: _slab_block_diag(jnp.swapaxes(a, 1, 2))
    wb = jnp.concatenate([jnp.concatenate([bd(bbr), bd(bbi)], axis=2),
                          jnp.concatenate([bd(lr), bd(li)], axis=2)], axis=1).astype(BF16)
    cm = jnp.concatenate([bd(c_re), -bd(c_im)], axis=1).astype(BF16)
    lamr = jnp.broadcast_to(l2r.reshape(1, -1), (SUBLANES, l2r.size))
    lami = jnp.broadcast_to(l2i.reshape(1, -1), (SUBLANES, l2i.size))
    return wb, cm, lamr, lami


def _s5(u, wb, cm, lamr, lami, d_skip, w_glu_b, b_glu, cast_weights):
    bsz, seq, sw = u.shape
    assert bsz == 4, "the scan packs 4 batch rows x 2 token parities into 8 sublanes"
    tm = S5_TILE
    steps = seq // tm
    ns = lamr.shape[1]
    perm, permt = _s5_perms(S5_SUBTILE)
    const = lambda a: pl.BlockSpec(a.shape, lambda i: (0,) * a.ndim)
    rows = lambda a: pl.BlockSpec((a.shape[0] // steps, a.shape[1]), lambda i: (i, 0))
    assert all(w.shape[0] % (16 * steps) == 0 for w in cast_weights)
    d_row = d_skip.reshape(1, sw)
    bg = b_glu.reshape(1, sw)
    outs = pl.pallas_call(
        functools.partial(_s5_kernel, tm=tm, sw=sw, n_cast=len(cast_weights)),
        grid=(steps,),
        in_specs=[pl.BlockSpec((4, tm, sw), lambda i: (0, i, 0)),
                  const(perm), const(permt), const(wb), const(cm), const(lamr), const(lami),
                  const(d_row), const(w_glu_b), const(bg)] + [rows(w) for w in cast_weights],
        out_specs=[pl.BlockSpec((4, tm, sw), lambda i: (0, i, 0))] + [rows(w) for w in cast_weights],
        out_shape=[jax.ShapeDtypeStruct((bsz, seq, sw), F32)]
        + [jax.ShapeDtypeStruct(w.shape, BF16) for w in cast_weights],
        scratch_shapes=[pltpu.VMEM((S5_SLABS, 2, SUBLANES, ns // S5_SLABS), F32),
                        pltpu.VMEM((SUBLANES, sw), F32)],
        compiler_params=pltpu.CompilerParams(
            dimension_semantics=("arbitrary",),
            vmem_limit_bytes=VMEM_LIMIT_BYTES),
        name="s5",
    )(u, perm, permt, wb, cm, lamr, lami, d_row, w_glu_b, bg, *cast_weights)
    return outs[0], outs[1:]


def _front_kernel(x_ref, mod_ref, g_ref, w_ref, tri_ref, u_ref, gates_ref, at_ref,
                  q_scr, k_scr, v_scr, c_scr, cmin_scr, acc_scr, z_scr, w_scr, *, sw, aw, d):
    blk = ATTN_BLOCK
    top = ATTN_TOP_ROWS
    n_pairs = aw // LANES
    nh = 2 * n_pairs
    tm = x_ref.shape[1]
    nsub = tm // blk
    step = pl.program_id(1)
    tile0 = pl.multiple_of(step * tm, tm)
    mod = _mod_rows(mod_ref, d)
    even_head = (lax.broadcasted_iota(jnp.int32, (tm, aw), 1) // HEAD_DIM) % 2 == 0

    def qkv(hb):
        o = sw
        q = (_dot(hb, w_ref[:, o:o + aw]) * Q_SCALE).astype(BF16); o += aw
        q_scr[0] = jnp.where(even_head, q, jnp.zeros_like(q))
        q_scr[1] = jnp.where(even_head, jnp.zeros_like(q), q)
        kt = _dot(hb, w_ref[:, o:o + aw]).T.astype(BF16); o += aw
        for j in range(nsub):
            k_scr[nsub * step + j] = kt[:, j * blk:(j + 1) * blk]
        v = _dot(hb, w_ref[:, o:o + aw]).astype(BF16)
        v_scr[0, pl.ds(tile0, tm), :] = jnp.where(even_head, v, jnp.zeros_like(v))
        v_scr[1, pl.ds(tile0, tm), :] = jnp.where(even_head, jnp.zeros_like(v), v)

    def region(units, fillers=()):
        starts = [pl.multiple_of(kb * blk, blk) for kb, _, _, _ in units]
        for u, (_, r0, r1, _) in enumerate(units):
            n = r1 - r0
            for p in range(n_pairs):
                ls = slice(p * LANES, (p + 1) * LANES)
                kblk = k_scr[units[u][0], ls, :]
                zz = _dot(jnp.concatenate([q_scr[0, r0:r1, ls], q_scr[1, r0:r1, ls]], axis=0), kblk)
                z_scr[u * nh + 2 * p, 0:n] = zz[:n]
                z_scr[u * nh + 2 * p + 1, 0:n] = zz[n:]
        fillers = list(fillers)
        every = -(-len(units) // (len(fillers) + 1))
        for u, (_, r0, r1, diag) in enumerate(units):
            n = r1 - r0
            if diag:
                row = lax.broadcasted_iota(jnp.int32, (n, blk), 0) + r0 % blk
                valid = lax.broadcasted_iota(jnp.int32, (n, blk), 1) < row
            zs, sps = [], []
            for h in range(nh):
                z = z_scr[u * nh + h, 0:n]
                sp = jnp.maximum(z, 0.0) + jnp.log2(1.0 + jnp.exp2(-jnp.abs(z)))
                if diag:
                    sp = jnp.where(valid, sp, 0.0)
                zs.append(z)
                sps.append(sp.astype(BF16))
            incl_all = _dot(jnp.concatenate(sps, axis=0), tri_ref[...])
            for h in range(nh):
                z = zs[h]
                incl = incl_all[h * n:(h + 1) * n]
                total = jnp.broadcast_to(incl[:, 0:1], (n, blk))
                if diag:
                    w = jnp.where(valid, jnp.exp2(z - incl), 0.0)
                    c_new = total
                else:
                    c = c_scr[h, r0:r1]
                    w = jnp.exp2(z - incl - c)
                    c_new = c + total
                c_scr[h, r0:r1] = c_new
                c_low = c_new if h == 0 else jnp.minimum(c_low, c_new)
                w_scr[u * nh + h, 0:n] = w.astype(BF16)
            cmin_scr[r0:r1] = c_low
            if fillers and (u + 1) % every == 0:
                fillers.pop(0)()
        for filler in fillers:
            filler()
        for r0, r1 in dict.fromkeys((r0, r1) for _, r0, r1, _ in units):
            us = [u for u, (_, a, b, _) in enumerate(units) if (a, b) == (r0, r1)]
            for p in range(n_pairs):
                ls = slice(p * LANES, (p + 1) * LANES)
                ww = jnp.concatenate([w_scr[u * nh + 2 * p + hh, 0:r1 - r0]
                                      for u in us for hh in range(2)], axis=1)
                vv = jnp.concatenate([v_scr[hh, pl.ds(starts[u], blk), ls]
                                      for u in us for hh in range(2)], axis=0)
                acc = _dot(ww, vv)
                if not any(units[u][3] for u in us):
                    acc = acc_scr[p, r0:r1] + acc
                acc_scr[p, r0:r1] = acc
                at_ref[0, r0:r1, ls] = acc.astype(at_ref.dtype)

    def c_min(r0, r1):
        return jnp.min(cmin_scr[r0:r1], axis=0, keepdims=True)[0, 0]

    def head_units(sub, qi, n_prev):
        base = sub * blk
        units = [(qi, base, base + blk, True)]
        if n_prev >= 1:
            units.append((qi - 1, base, base + blk, False))
        if n_prev >= 2:
            units.append((qi - 2, base, base + top, False))
        return units

    def tile(units):
        h = _rms(x_ref[0]) * g_ref[...]
        hb = (h * (1.0 + mod[1]) + mod[0]).astype(BF16)
        qkv(hb)
        o = sw + 3 * aw

        def chunk(ref, col, w0):
            def run():
                ref[0, :, col:col + FRONT_CHUNK] = _dot(hb, w_ref[:, w0 + col:w0 + col + FRONT_CHUNK])
            return run

        region(units, [chunk(ref, col, w0)
                       for ref, w0, width in ((gates_ref, o, 2 * d), (u_ref, 0, sw))
                       for col in range(0, width, FRONT_CHUNK)])

    @pl.when(step >= 1)
    def _():
        tile([u for sub in range(nsub) for u in head_units(sub, nsub * step + sub, 2)])

    @pl.when(step == 0)
    def _():
        tile([u for sub in range(nsub) for u in head_units(sub, sub, min(sub, 2))])

    def sweep(first_kb, cmin, r0, r1):
        def more(carry):
            kb, cmin = carry
            return jnp.logical_and(kb >= 0, cmin < UNDERFLOW_LOG2)

        def body(carry):
            kb, _ = carry
            region([(kb, r0, r1, False)])
            return kb - 1, c_min(r0, r1)

        lax.while_loop(more, body, (first_kb, cmin))

    tails = []
    for sub in range(nsub):
        qi = nsub * step + sub
        base = sub * blk
        tails.append((jnp.where(qi >= 2, qi - 3, -1), base, base + top))
        tails.append((jnp.where(qi >= 2, qi - 2, -1), base + top, base + blk))
    cmins = [c_min(r0, r1) for _, r0, r1 in tails]
    for (first_kb, r0, r1), cmin in zip(tails, cmins):
        sweep(first_kb, cmin, r0, r1)


def _attn_tri():
    blk = ATTN_BLOCK
    m = np.arange(blk)[:, None]
    j = np.arange(blk)[None, :]
    return jnp.asarray((m >= j).astype(np.float32), BF16)


def _front(x, mod, norm_g, w_in_b, sw, aw):
    bsz, seq, d = x.shape
    tm = FRONT_TILE
    blk = ATTN_BLOCK
    n = w_in_b.shape[1]
    n_pairs = aw // LANES
    nz = ATTN_REGION * (tm // blk) * 2 * n_pairs
    tri = _attn_tri()
    tok = lambda w: pl.BlockSpec((1, tm, w), lambda b, i: (b, i, 0))
    const = lambda a: pl.BlockSpec(a.shape, lambda b, i: (0,) * a.ndim,
                                   pipeline_mode=pl.Buffered(1))
    return pl.pallas_call(
        functools.partial(_front_kernel, sw=sw, aw=aw, d=d),
        grid=(bsz, seq // tm),
        in_specs=[tok(d),
                  pl.BlockSpec(mod.shape, lambda b, i: (0, 0)),
                  const(norm_g), const(w_in_b), const(tri)],
        out_specs=[tok(sw), tok(2 * d), tok(aw)],
        out_shape=[jax.ShapeDtypeStruct((bsz, seq, sw), F32),
                   jax.ShapeDtypeStruct((bsz, seq, 2 * d), F32),
                   jax.ShapeDtypeStruct((bsz, seq, aw), BF16)],
        scratch_shapes=[pltpu.VMEM((2, tm, aw), BF16),
                        pltpu.VMEM((seq // blk, aw, blk), BF16),
                        pltpu.VMEM((2, seq, aw), BF16),
                        pltpu.VMEM((2 * n_pairs, tm, blk), F32),
                        pltpu.VMEM((tm, blk), F32),
                        pltpu.VMEM((n_pairs, tm, LANES), F32),
                        pltpu.VMEM((nz, blk, blk), F32),
                        pltpu.VMEM((nz, blk, blk), BF16)],
        compiler_params=pltpu.CompilerParams(
            dimension_semantics=("arbitrary", "arbitrary"),
            vmem_limit_bytes=VMEM_LIMIT_BYTES),
        name="front",
    )(x, mod, norm_g, w_in_b, tri)


def _out_ffn_kernel(x_ref, s5_ref, at_ref, gates_ref, mod_ref, n2_ref, nf_ref,
                    wa_ref, wb_ref, wo_ref, wg_ref, wu_ref, wd_ref, o_ref, *, final_norm):
    mod = _mod_rows(mod_ref, x_ref.shape[2])
    tm = x_ref.shape[1]
    groups = [slice(r, r + tm // OUT_GROUPS) for r in range(0, tm, tm // OUT_GROUPS)]
    ms = []
    for g in groups:
        ya = _dot(s5_ref[0, g].astype(BF16), wa_ref[...])
        yb = _dot(at_ref[0, g], wb_ref[...])
        d = ya.shape[1]
        m = (jax.nn.sigmoid(gates_ref[0, g, :d]) * ya
             + jax.nn.sigmoid(gates_ref[0, g, d:]) * yb)
        ms.append(m.astype(BF16))
    x1s, hs = [], []
    for g, m in zip(groups, ms):
        x1 = x_ref[0, g] + mod[2] * _dot(m, wo_ref[...])
        h = _rms(x1) * n2_ref[...]
        x1s.append(x1)
        hs.append((h * (1.0 + mod[4]) + mod[3]).astype(BF16))
    acts = []
    for h in hs:
        gate = _dot(h, wg_ref[...])
        up = _dot(h, wu_ref[...])
        acts.append((gate * jax.nn.sigmoid(gate) * up).astype(BF16))
    for g, x1, act in zip(groups, x1s, acts):
        x2 = x1 + mod[5] * _dot(act, wd_ref[...])
        o_ref[0, g] = _rms(x2) * nf_ref[...] if final_norm else x2


def _out_ffn(x, s5o, attn, gates, mod, n2g, nfg, wa, wb, wo, wg, wu, wd, final_norm):
    bsz, seq, d = x.shape
    tm = OUT_TILE
    tok = lambda a: pl.BlockSpec((1, tm, a.shape[-1]), lambda b, i: (b, i, 0))
    const = lambda a: pl.BlockSpec(a.shape, lambda b, i: (0,) * a.ndim,
                                   pipeline_mode=pl.Buffered(1))
    return pl.pallas_call(
        functools.partial(_out_ffn_kernel, final_norm=final_norm),
        grid=(bsz, seq // tm),
        in_specs=[tok(x), tok(s5o), tok(attn), tok(gates),
                  pl.BlockSpec(mod.shape, lambda b, i: (0, 0)),
                  const(n2g), const(nfg),
                  const(wa), const(wb), const(wo), const(wg), const(wu), const(wd)],
        out_specs=pl.BlockSpec((1, tm, d), lambda b, i: (b, i, 0)),
        out_shape=jax.ShapeDtypeStruct((bsz, seq, d), F32),
        compiler_params=pltpu.CompilerParams(
            dimension_semantics=("arbitrary", "arbitrary"),
            vmem_limit_bytes=VMEM_LIMIT_BYTES),
        name="out_ffn",
    )(x, s5o, attn, gates, mod, n2g, nfg, wa, wb, wo, wg, wu, wd)


def kernel(x, c, w_ada, b_ada, norm1_g, w_in, lam_re, lam_im, log_dt, b_re, b_im, c_re, c_im,
           d_skip, w_glu, b_glu, w_a, w_b, w_o, norm2_g, w_ffn_gate, w_ffn_up, w_ffn_down,
           norm_f_g):
    depth = w_ada.shape[0]
    bsz, seq, d = x.shape
    sw = w_glu.shape[1]
    aw = w_b.shape[1]
    for l in range(depth):
        mod, w_in_b = _ada(c, w_ada[l], b_ada[l], w_in[l])
        u, gates, attn = _front(x, mod, norm1_g[l].reshape(1, d), w_in_b, sw, aw)
        wb, cm, lamr, lami = _s5_weights(lam_re[l], lam_im[l], log_dt[l], b_re[l], b_im[l],
                                         c_re[l], c_im[l])
        s5o, out_w = _s5(u, wb, cm, lamr, lami, d_skip[l], w_glu[l].astype(BF16), b_glu[l],
                         (w_a[l], w_b[l], w_o[l], w_ffn_gate[l], w_ffn_up[l], w_ffn_down[l]))
        x = _out_ffn(x, s5o, attn, gates, mod, norm2_g[l].reshape(1, d), norm_f_g.reshape(1, d),
                     *out_w, final_norm=(l == depth - 1))
    return x
```

```python
import functools
import math

import numpy as np
import jax
import jax.numpy as jnp
from jax import lax
from jax.experimental import pallas as pl
from jax.experimental.pallas import tpu as pltpu

F32 = jnp.float32
BF16 = jnp.bfloat16

S5_GROUP = 16
S5_STATE = 64
HEAD_DIM = 64
N_ADA = 6
RMS_EPS = 1e-6
Q_SCALE = math.log2(math.e) / math.sqrt(HEAD_DIM)
UNDERFLOW_LOG2 = 151.0

LANES = 128
SUBLANES = 8
VMEM_LIMIT_BYTES = 56 * 1024 * 1024

ADA_STEPS = 4
ATTN_BLOCK = 128
ATTN_REGION = 3
ATTN_TOP_ROWS = 64
S5_TILE = 256
S5_SUBTILE = 128
S5_SLABS = 4
FRONT_TILE = 512
FRONT_CHUNK = 512
OUT_TILE = 512
OUT_GROUPS = 2


def _dot(a, b):
    return jnp.dot(a, b, preferred_element_type=F32)


def _rms(x):
    return x * lax.rsqrt(jnp.mean(x * x, axis=-1, keepdims=True) + RMS_EPS)


def _mod_rows(mod_ref, d):
    row = mod_ref[pl.ds(pl.program_id(0), 1), :]
    return [row[:, k * d:(k + 1) * d] for k in range(N_ADA)]


def _ada_kernel(c_ref, w_ref, b_ref, cast_ref, o_ref, cast_out_ref):
    c = c_ref[...]
    bsz = c.shape[0]
    cond = c * jax.nn.sigmoid(c)
    pad = -bsz % SUBLANES
    if pad:
        cond = jnp.concatenate([cond, jnp.zeros((pad, c.shape[1]), F32)], axis=0)
    o_ref[...] = _dot(cond.astype(BF16), w_ref[...].astype(BF16))[:bsz] + b_ref[...]
    cast_out_ref[...] = cast_ref[...].astype(BF16)


def _ada(c, w_ada, b_ada, cast_weight):
    bsz, d = c.shape
    n = w_ada.shape[1]
    steps = ADA_STEPS
    tn = n // steps
    rows, cols = cast_weight.shape
    assert n % (steps * LANES) == 0 and rows % (16 * steps) == 0
    return pl.pallas_call(
        _ada_kernel,
        grid=(steps,),
        in_specs=[pl.BlockSpec((bsz, d), lambda j: (0, 0)),
                  pl.BlockSpec((d, tn), lambda j: (0, j)),
                  pl.BlockSpec((1, tn), lambda j: (0, j)),
                  pl.BlockSpec((rows // steps, cols), lambda j: (j, 0))],
        out_specs=[pl.BlockSpec((bsz, tn), lambda j: (0, j)),
                   pl.BlockSpec((rows // steps, cols), lambda j: (j, 0))],
        out_shape=[jax.ShapeDtypeStruct((bsz, n), F32),
                   jax.ShapeDtypeStruct((rows, cols), BF16)],
        compiler_params=pltpu.CompilerParams(vmem_limit_bytes=VMEM_LIMIT_BYTES),
        name="ada",
    )(c, w_ada, b_ada.reshape(1, n), cast_weight)


def _s5_kernel(u_ref, perm_ref, permt_ref, wb_ref, cm_ref, lamr_ref, lami_ref, d_ref,
               wglu_ref, bglu_ref, *rest, tm, sw, n_cast):
    cast_in, o_ref, cast_out = rest[:n_cast], rest[n_cast], rest[n_cast + 1:2 * n_cast + 1]
    x_scr, ulast_scr = rest[2 * n_cast + 1:]
    for src, dst in zip(cast_in, cast_out):
        dst[...] = src[...].astype(BF16)
    i = pl.program_id(0)
    ts = S5_SUBTILE
    n2 = ts // 2
    rows = SUBLANES * n2
    cw = sw // S5_SLABS
    hs = cw * S5_STATE // S5_GROUP

    @pl.when(i == 0)
    def _():
        x_scr[...] = jnp.zeros_like(x_scr)
        ulast_scr[...] = jnp.zeros_like(ulast_scr)

    sub8 = lax.broadcasted_iota(jnp.int32, (SUBLANES, sw), 0)
    odd = (lax.broadcasted_iota(jnp.int32, (rows, sw), 0) & 1) == 1

    def natural(j):
        return jnp.concatenate([u_ref[b, j * ts:(j + 1) * ts, :] for b in range(4)], axis=0)

    def last_rows(j):
        last = jnp.zeros((SUBLANES, sw), F32)
        for b in range(4):
            row = u_ref[b, (j + 1) * ts - 1:(j + 1) * ts, :].astype(BF16).astype(F32)
            last = jnp.where(sub8 == 2 * b, jnp.broadcast_to(row, (SUBLANES, sw)), last)
        return last

    def input_stage(j, before):
        a_cur = _dot(perm_ref[...], natural(j).astype(BF16))
        a_prev = jnp.where(odd, pltpu.roll(a_cur, 1, 0), pltpu.roll(a_cur, SUBLANES - 1, 0))
        first = jnp.where((sub8 & 1) == 1, a_prev[:SUBLANES], before)
        a_prev = jnp.concatenate([first, a_prev[SUBLANES:]], axis=0).astype(BF16)
        a_cur = a_cur.astype(BF16)
        return [_dot(jnp.concatenate([a_cur[:, s * cw:(s + 1) * cw],
                                      a_prev[:, s * cw:(s + 1) * cw]], axis=1), wb_ref[s])
                for s in range(S5_SLABS)]

    def scan_stage(bus, x):
        states, x_out = [], []
        for s in range(S5_SLABS):
            ar = lamr_ref[:, hs * s:hs * (s + 1)]
            ai = lami_ref[:, hs * s:hs * (s + 1)]
            xr, xi = x[s]
            st = []
            for t2 in range(n2):
                rs = slice(SUBLANES * t2, SUBLANES * (t2 + 1))
                xr, xi = (ar * xr - ai * xi + bus[s][rs, :hs],
                          ar * xi + ai * xr + bus[s][rs, hs:])
                st.append(jnp.concatenate([xr, xi], axis=1))
            x_out.append((xr, xi))
            states.append(jnp.concatenate(st, axis=0).astype(BF16))
        return states, x_out

    def output_stage(j, states):
        y_il = jnp.concatenate([_dot(states[s], cm_ref[s]) for s in range(S5_SLABS)], axis=1)
        y = _dot(permt_ref[...], y_il.astype(BF16))
        y = y + d_ref[...] * natural(j)
        y = jax.nn.gelu(y)
        z = _dot(y.astype(BF16), wglu_ref[...]) + bglu_ref[...]
        out = y * jax.nn.sigmoid(z)
        for b in range(4):
            o_ref[b, j * ts:(j + 1) * ts, :] = out[b * ts:(b + 1) * ts]

    nsub = tm // ts
    befores = [ulast_scr[...]] + [last_rows(j) for j in range(nsub - 1)]
    bus = [input_stage(j, befores[j]) for j in range(nsub)]
    ulast_scr[...] = last_rows(nsub - 1)
    x = [(x_scr[s, 0], x_scr[s, 1]) for s in range(S5_SLABS)]
    for j in range(nsub):
        states, x = scan_stage(bus[j], x)
        output_stage(j, states)
    for s in range(S5_SLABS):
        x_scr[s, 0], x_scr[s, 1] = x[s]


def _s5_perms(tm):
    n2 = tm // 2
    rows = SUBLANES * n2
    perm = np.zeros((rows, 4 * tm), np.float32)
    permt = np.zeros((4 * tm, rows), np.float32)
    for t2 in range(n2):
        for b in range(4):
            for par in range(2):
                r = SUBLANES * t2 + 2 * b + par
                t = 2 * t2 + par
                perm[r, b * tm + t] = 1.0
                permt[b * tm + t, r] = 1.0
    return jnp.asarray(perm, BF16), jnp.asarray(permt, BF16)


def _slab_block_diag(blocks):
    g, a, b = blocks.shape
    n = g // S5_SLABS
    eye = jnp.eye(n, dtype=blocks.dtype)
    placed = blocks.reshape(S5_SLABS, n, a, 1, b) * eye[None, :, None, :, None]
    return placed.reshape(S5_SLABS, n * a, n * b)


def _s5_weights(lam_re, lam_im, log_dt, b_re, b_im, c_re, c_im):
    g = lam_re.shape[0]
    dt = jnp.exp(log_dt)[:, None]
    mag = jnp.exp(lam_re * dt)
    lbr = mag * jnp.cos(lam_im * dt)
    lbi = mag * jnp.sin(lam_im * dt)
    nr, ni = lbr - 1.0, lbi
    den = lam_re * lam_re + lam_im * lam_im
    cr = (nr * lam_re + ni * lam_im) / den
    ci = (ni * lam_re - nr * lam_im) / den
    bbr = cr[..., None] * b_re - ci[..., None] * b_im
    bbi = cr[..., None] * b_im + ci[..., None] * b_re
    lr = lbr[..., None] * bbr - lbi[..., None] * bbi
    li = lbr[..., None] * bbi + lbi[..., None] * bbr
    l2r = lbr * lbr - lbi * lbi
    l2i = 2.0 * lbr * lbi
    bd = lambda a: _slab_block_diag(jnp.swapaxes(a, 1, 2))
    wb = jnp.concatenate([jnp.concatenate([bd(bbr), bd(bbi)], axis=2),
                          jnp.concatenate([bd(lr), bd(li)], axis=2)], axis=1).astype(BF16)
    cm = jnp.concatenate([bd(c_re), -bd(c_im)], axis=1).astype(BF16)
    lamr = jnp.broadcast_to(l2r.reshape(1, -1), (SUBLANES, l2r.size))
    lami = jnp.broadcast_to(l2i.reshape(1, -1), (SUBLANES, l2i.size))
    return wb, cm, lamr, lami


def _s5(u, wb, cm, lamr, lami, d_skip, w_glu_b, b_glu, cast_weights):
    bsz, seq, sw = u.shape
    assert bsz == 4, "the scan packs 4 batch rows x 2 token parities into 8 sublanes"
    tm = S5_TILE
    steps = seq // tm
    ns = lamr.shape[1]
    perm, permt = _s5_perms(S5_SUBTILE)
    const = lambda a: pl.BlockSpec(a.shape, lambda i: (0,) * a.ndim)
    rows = lambda a: pl.BlockSpec((a.shape[0] // steps, a.shape[1]), lambda i: (i, 0))
    assert all(w.shape[0] % (16 * steps) == 0 for w in cast_weights)
    d_row = d_skip.reshape(1, sw)
    bg = b_glu.reshape(1, sw)
    outs = pl.pallas_call(
        functools.partial(_s5_kernel, tm=tm, sw=sw, n_cast=len(cast_weights)),
        grid=(steps,),
        in_specs=[pl.BlockSpec((4, tm, sw), lambda i: (0, i, 0)),
                  const(perm), const(permt), const(wb), const(cm), const(lamr), const(lami),
                  const(d_row), const(w_glu_b), const(bg)] + [rows(w) for w in cast_weights],
        out_specs=[pl.BlockSpec((4, tm, sw), lambda i: (0, i, 0))] + [rows(w) for w in cast_weights],
        out_shape=[jax.ShapeDtypeStruct((bsz, seq, sw), F32)]
        + [jax.ShapeDtypeStruct(w.shape, BF16) for w in cast_weights],
        scratch_shapes=[pltpu.VMEM((S5_SLABS, 2, SUBLANES, ns // S5_SLABS), F32),
                        pltpu.VMEM((SUBLANES, sw), F32)],
        compiler_params=pltpu.CompilerParams(
            dimension_semantics=("arbitrary",),
            vmem_limit_bytes=VMEM_LIMIT_BYTES),
        name="s5",
    )(u, perm, permt, wb, cm, lamr, lami, d_row, w_glu_b, bg, *cast_weights)
    return outs[0], outs[1:]


def _front_kernel(x_ref, mod_ref, g_ref, w_ref, tri_ref, u_ref, gates_ref, at_ref,
                  q_scr, k_scr, v_scr, c_scr, cmin_scr, acc_scr, z_scr, w_scr, *, sw, aw, d):
    blk = ATTN_BLOCK
    top = ATTN_TOP_ROWS
    n_pairs = aw // LANES
    nh = 2 * n_pairs
    tm = x_ref.shape[1]
    nsub = tm // blk
    step = pl.program_id(1)
    tile0 = pl.multiple_of(step * tm, tm)
    mod = _mod_rows(mod_ref, d)
    even_head = (lax.broadcasted_iota(jnp.int32, (tm, aw), 1) // HEAD_DIM) % 2 == 0

    def qkv(hb):
        o = sw
        q = (_dot(hb, w_ref[:, o:o + aw]) * Q_SCALE).astype(BF16); o += aw
        q_scr[0] = jnp.where(even_head, q, jnp.zeros_like(q))
        q_scr[1] = jnp.where(even_head, jnp.zeros_like(q), q)
        kt = _dot(hb, w_ref[:, o:o + aw]).T.astype(BF16); o += aw
        for j in range(nsub):
            k_scr[nsub * step + j] = kt[:, j * blk:(j + 1) * blk]
        v = _dot(hb, w_ref[:, o:o + aw]).astype(BF16)
        v_scr[0, pl.ds(tile0, tm), :] = jnp.where(even_head, v, jnp.zeros_like(v))
        v_scr[1, pl.ds(tile0, tm), :] = jnp.where(even_head, jnp.zeros_like(v), v)

    def region(units, fillers=()):
        starts = [pl.multiple_of(kb * blk, blk) for kb, _, _, _ in units]
        for u, (_, r0, r1, _) in enumerate(units):
            n = r1 - r0
            for p in range(n_pairs):
                ls = slice(p * LANES, (p + 1) * LANES)
                kblk = k_scr[units[u][0], ls, :]
                zz = _dot(jnp.concatenate([q_scr[0, r0:r1, ls], q_scr[1, r0:r1, ls]], axis=0), kblk)
                z_scr[u * nh + 2 * p, 0:n] = zz[:n]
                z_scr[u * nh + 2 * p + 1, 0:n] = zz[n:]
        fillers = list(fillers)
        every = -(-len(units) // (len(fillers) + 1))
        for u, (_, r0, r1, diag) in enumerate(units):
            n = r1 - r0
            if diag:
                row = lax.broadcasted_iota(jnp.int32, (n, blk), 0) + r0 % blk
                valid = lax.broadcasted_iota(jnp.int32, (n, blk), 1) < row
            zs, sps = [], []
            for h in range(nh):
                z = z_scr[u * nh + h, 0:n]
                sp = jnp.maximum(z, 0.0) + jnp.log2(1.0 + jnp.exp2(-jnp.abs(z)))
                if diag:
                    sp = jnp.where(valid, sp, 0.0)
                zs.append(z)
                sps.append(sp.astype(BF16))
            incl_all = _dot(jnp.concatenate(sps, axis=0), tri_ref[...])
            for h in range(nh):
                z = zs[h]
                incl = incl_all[h * n:(h + 1) * n]
                total = jnp.broadcast_to(incl[:, 0:1], (n, blk))
                if diag:
                    w = jnp.where(valid, jnp.exp2(z - incl), 0.0)
                    c_new = total
                else:
                    c = c_scr[h, r0:r1]
                    w = jnp.exp2(z - incl - c)
                    c_new = c + total
                c_scr[h, r0:r1] = c_new
                c_low = c_new if h == 0 else jnp.minimum(c_low, c_new)
                w_scr[u * nh + h, 0:n] = w.astype(BF16)
            cmin_scr[r0:r1] = c_low
            if fillers and (u + 1) % every == 0:
                fillers.pop(0)()
        for filler in fillers:
            filler()
        for r0, r1 in dict.fromkeys((r0, r1) for _, r0, r1, _ in units):
            us = [u for u, (_, a, b, _) in enumerate(units) if (a, b) == (r0, r1)]
            for p in range(n_pairs):
                ls = slice(p * LANES, (p + 1) * LANES)
                ww = jnp.concatenate([w_scr[u * nh + 2 * p + hh, 0:r1 - r0]
                                      for u in us for hh in range(2)], axis=1)
                vv = jnp.concatenate([v_scr[hh, pl.ds(starts[u], blk), ls]
                                      for u in us for hh in range(2)], axis=0)
                if any(units[u][3] for u in us):
                    acc_scr[p, r0:r1] = _dot(ww, vv)
                else:
                    acc_scr[p, r0:r1] = acc_scr[p, r0:r1] + _dot(ww, vv)

    def c_min(r0, r1):
        return jnp.min(cmin_scr[r0:r1], axis=0, keepdims=True)[0, 0]

    def head_units(sub, qi, n_prev):
        base = sub * blk
        units = [(qi, base, base + blk, True)]
        if n_prev >= 1:
            units.append((qi - 1, base, base + blk, False))
        if n_prev >= 2:
            units.append((qi - 2, base, base + top, False))
        return units

    def tile(units):
        h = _rms(x_ref[0]) * g_ref[...]
        hb = (h * (1.0 + mod[1]) + mod[0]).astype(BF16)
        qkv(hb)
        o = sw + 3 * aw

        def chunk(ref, col, w0):
            def run():
                ref[0, :, col:col + FRONT_CHUNK] = _dot(hb, w_ref[:, w0 + col:w0 + col + FRONT_CHUNK])
            return run

        region(units, [chunk(ref, col, w0)
                       for ref, w0, width in ((gates_ref, o, 2 * d), (u_ref, 0, sw))
                       for col in range(0, width, FRONT_CHUNK)])

    @pl.when(step >= 1)
    def _():
        tile([u for sub in range(nsub) for u in head_units(sub, nsub * step + sub, 2)])

    @pl.when(step == 0)
    def _():
        tile([u for sub in range(nsub) for u in head_units(sub, sub, min(sub, 2))])

    def sweep(first_kb, cmin, r0, r1):
        def more(carry):
            kb, cmin = carry
            return jnp.logical_and(kb >= 0, cmin < UNDERFLOW_LOG2)

        def body(carry):
            kb, _ = carry
            region([(kb, r0, r1, False)])
            return kb - 1, c_min(r0, r1)

        lax.while_loop(more, body, (first_kb, cmin))

    tails = []
    for sub in range(nsub):
        qi = nsub * step + sub
        base = sub * blk
        tails.append((jnp.where(qi >= 2, qi - 3, -1), base, base + top))
        tails.append((jnp.where(qi >= 2, qi - 2, -1), base + top, base + blk))
    cmins = [c_min(r0, r1) for _, r0, r1 in tails]
    for (first_kb, r0, r1), cmin in zip(tails, cmins):
        sweep(first_kb, cmin, r0, r1)
    for p in range(n_pairs):
        at_ref[0, :, p * LANES:(p + 1) * LANES] = acc_scr[p].astype(at_ref.dtype)


def _attn_tri():
    blk = ATTN_BLOCK
    m = np.arange(blk)[:, None]
    j = np.arange(blk)[None, :]
    return jnp.asarray((m >= j).astype(np.float32), BF16)


def _front(x, mod, norm_g, w_in_b, sw, aw):
    bsz, seq, d = x.shape
    tm = FRONT_TILE
    blk = ATTN_BLOCK
    n = w_in_b.shape[1]
    n_pairs = aw // LANES
    nz = ATTN_REGION * (tm // blk) * 2 * n_pairs
    tri = _attn_tri()
    tok = lambda w: pl.BlockSpec((1, tm, w), lambda b, i: (b, i, 0))
    const = lambda a: pl.BlockSpec(a.shape, lambda b, i: (0,) * a.ndim,
                                   pipeline_mode=pl.Buffered(1))
    return pl.pallas_call(
        functools.partial(_front_kernel, sw=sw, aw=aw, d=d),
        grid=(bsz, seq // tm),
        in_specs=[tok(d),
                  pl.BlockSpec(mod.shape, lambda b, i: (0, 0)),
                  const(norm_g), const(w_in_b), const(tri)],
        out_specs=[tok(sw), tok(2 * d), tok(aw)],
        out_shape=[jax.ShapeDtypeStruct((bsz, seq, sw), F32),
                   jax.ShapeDtypeStruct((bsz, seq, 2 * d), F32),
                   jax.ShapeDtypeStruct((bsz, seq, aw), BF16)],
        scratch_shapes=[pltpu.VMEM((2, tm, aw), BF16),
                        pltpu.VMEM((seq // blk, aw, blk), BF16),
                        pltpu.VMEM((2, seq, aw), BF16),
                        pltpu.VMEM((2 * n_pairs, tm, blk), F32),
                        pltpu.VMEM((tm, blk), F32),
                        pltpu.VMEM((n_pairs, tm, LANES), F32),
                        pltpu.VMEM((nz, blk, blk), F32),
                        pltpu.VMEM((nz, blk, blk), BF16)],
        compiler_params=pltpu.CompilerParams(
            dimension_semantics=("arbitrary", "arbitrary"),
            vmem_limit_bytes=VMEM_LIMIT_BYTES),
        name="front",
    )(x, mod, norm_g, w_in_b, tri)


def _out_ffn_kernel(x_ref, s5_ref, at_ref, gates_ref, mod_ref, n2_ref, nf_ref,
                    wa_ref, wb_ref, wo_ref, wg_ref, wu_ref, wd_ref, o_ref, *, final_norm):
    mod = _mod_rows(mod_ref, x_ref.shape[2])
    tm = x_ref.shape[1]
    groups = [slice(r, r + tm // OUT_GROUPS) for r in range(0, tm, tm // OUT_GROUPS)]
    ms = []
    for g in groups:
        ya = _dot(s5_ref[0, g].astype(BF16), wa_ref[...])
        yb = _dot(at_ref[0, g], wb_ref[...])
        d = ya.shape[1]
        m = (jax.nn.sigmoid(gates_ref[0, g, :d]) * ya
             + jax.nn.sigmoid(gates_ref[0, g, d:]) * yb)
        ms.append(m.astype(BF16))
    x1s, hs = [], []
    for g, m in zip(groups, ms):
        x1 = x_ref[0, g] + mod[2] * _dot(m, wo_ref[...])
        h = _rms(x1) * n2_ref[...]
        x1s.append(x1)
        hs.append((h * (1.0 + mod[4]) + mod[3]).astype(BF16))
    acts = []
    for h in hs:
        gate = _dot(h, wg_ref[...])
        up = _dot(h, wu_ref[...])
        acts.append((gate * jax.nn.sigmoid(gate) * up).astype(BF16))
    for g, x1, act in zip(groups, x1s, acts):
        x2 = x1 + mod[5] * _dot(act, wd_ref[...])
        o_ref[0, g] = _rms(x2) * nf_ref[...] if final_norm else x2


def _out_ffn(x, s5o, attn, gates, mod, n2g, nfg, wa, wb, wo, wg, wu, wd, final_norm):
    bsz, seq, d = x.shape
    tm = OUT_TILE
    tok = lambda a: pl.BlockSpec((1, tm, a.shape[-1]), lambda b, i: (b, i, 0))
    const = lambda a: pl.BlockSpec(a.shape, lambda b, i: (0,) * a.ndim,
                                   pipeline_mode=pl.Buffered(1))
    return pl.pallas_call(
        functools.partial(_out_ffn_kernel, final_norm=final_norm),
        grid=(bsz, seq // tm),
        in_specs=[tok(x), tok(s5o), tok(attn), tok(gates),
                  pl.BlockSpec(mod.shape, lambda b, i: (0, 0)),
                  const(n2g), const(nfg),
                  const(wa), const(wb), const(wo), const(wg), const(wu), const(wd)],
        out_specs=pl.BlockSpec((1, tm, d), lambda b, i: (b, i, 0)),
        out_shape=jax.ShapeDtypeStruct((bsz, seq, d), F32),
        compiler_params=pltpu.CompilerParams(
            dimension_semantics=("arbitrary", "arbitrary"),
            vmem_limit_bytes=VMEM_LIMIT_BYTES),
        name="out_ffn",
    )(x, s5o, attn, gates, mod, n2g, nfg, wa, wb, wo, wg, wu, wd)


def kernel(x, c, w_ada, b_ada, norm1_g, w_in, lam_re, lam_im, log_dt, b_re, b_im, c_re, c_im,
           d_skip, w_glu, b_glu, w_a, w_b, w_o, norm2_g, w_ffn_gate, w_ffn_up, w_ffn_down,
           norm_f_g):
    depth = w_ada.shape[0]
    bsz, seq, d = x.shape
    sw = w_glu.shape[1]
    aw = w_b.shape[1]
    for l in range(depth):
        mod, w_in_b = _ada(c, w_ada[l], b_ada[l], w_in[l])
        u, gates, attn = _front(x, mod, norm1_g[l].reshape(1, d), w_in_b, sw, aw)
        wb, cm, lamr, lami = _s5_weights(lam_re[l], lam_im[l], log_dt[l], b_re[l], b_im[l],
                                         c_re[l], c_im[l])
        s5o, out_w = _s5(u, wb, cm, lamr, lami, d_skip[l], w_glu[l].astype(BF16), b_glu[l],
                         (w_a[l], w_b[l], w_o[l], w_ffn_gate[l], w_ffn_up[l], w_ffn_down[l]))
        x = _out_ffn(x, s5o, attn, gates, mod, norm2_g[l].reshape(1, d), norm_f_g.reshape(1, d),
                     *out_w, final_norm=(l == depth - 1))
    return x
```

```python
import functools
import math

import numpy as np
import jax
import jax.numpy as jnp
from jax import lax
from jax.experimental import pallas as pl
from jax.experimental.pallas import tpu as pltpu

F32 = jnp.float32
BF16 = jnp.bfloat16

S5_GROUP = 16
S5_STATE = 64
HEAD_DIM = 64
N_ADA = 6
RMS_EPS = 1e-6
Q_SCALE = math.log2(math.e) / math.sqrt(HEAD_DIM)
UNDERFLOW_LOG2 = 151.0

LANES = 128
SUBLANES = 8
VMEM_LIMIT_BYTES = 56 * 1024 * 1024

ADA_STEPS = 4
ATTN_BLOCK = 128
ATTN_REGION = 3
ATTN_TOP_ROWS = 48
S5_TILE = 256
S5_SUBTILE = 128
S5_SLABS = 4
FRONT_TILE = 512
FRONT_CHUNK = 512
OUT_TILE = 512
OUT_GROUPS = 2


def _dot(a, b):
    return jnp.dot(a, b, preferred_element_type=F32)


def _rms(x):
    return x * lax.rsqrt(jnp.mean(x * x, axis=-1, keepdims=True) + RMS_EPS)


def _mod_rows(mod_ref, d):
    row = mod_ref[pl.ds(pl.program_id(0), 1), :]
    return [row[:, k * d:(k + 1) * d] for k in range(N_ADA)]


def _ada_kernel(c_ref, w_ref, b_ref, cast_ref, o_ref, cast_out_ref):
    c = c_ref[...]
    bsz = c.shape[0]
    cond = c * jax.nn.sigmoid(c)
    pad = -bsz % SUBLANES
    if pad:
        cond = jnp.concatenate([cond, jnp.zeros((pad, c.shape[1]), F32)], axis=0)
    o_ref[...] = _dot(cond.astype(BF16), w_ref[...].astype(BF16))[:bsz] + b_ref[...]
    cast_out_ref[...] = cast_ref[...].astype(BF16)


def _ada(c, w_ada, b_ada, cast_weight):
    bsz, d = c.shape
    n = w_ada.shape[1]
    steps = ADA_STEPS
    tn = n // steps
    rows, cols = cast_weight.shape
    assert n % (steps * LANES) == 0 and rows % (16 * steps) == 0
    return pl.pallas_call(
        _ada_kernel,
        grid=(steps,),
        in_specs=[pl.BlockSpec((bsz, d), lambda j: (0, 0)),
                  pl.BlockSpec((d, tn), lambda j: (0, j)),
                  pl.BlockSpec((1, tn), lambda j: (0, j)),
                  pl.BlockSpec((rows // steps, cols), lambda j: (j, 0))],
        out_specs=[pl.BlockSpec((bsz, tn), lambda j: (0, j)),
                   pl.BlockSpec((rows // steps, cols), lambda j: (j, 0))],
        out_shape=[jax.ShapeDtypeStruct((bsz, n), F32),
                   jax.ShapeDtypeStruct((rows, cols), BF16)],
        compiler_params=pltpu.CompilerParams(vmem_limit_bytes=VMEM_LIMIT_BYTES),
        name="ada",
    )(c, w_ada, b_ada.reshape(1, n), cast_weight)


def _s5_kernel(u_ref, perm_ref, permt_ref, wb_ref, cm_ref, lamr_ref, lami_ref, d_ref,
               wglu_ref, bglu_ref, *rest, tm, sw, n_cast):
    cast_in, o_ref, cast_out = rest[:n_cast], rest[n_cast], rest[n_cast + 1:2 * n_cast + 1]
    x_scr, ulast_scr = rest[2 * n_cast + 1:]
    for src, dst in zip(cast_in, cast_out):
        dst[...] = src[...].astype(BF16)
    i = pl.program_id(0)
    ts = S5_SUBTILE
    n2 = ts // 2
    rows = SUBLANES * n2
    cw = sw // S5_SLABS
    hs = cw * S5_STATE // S5_GROUP

    @pl.when(i == 0)
    def _():
        x_scr[...] = jnp.zeros_like(x_scr)
        ulast_scr[...] = jnp.zeros_like(ulast_scr)

    sub8 = lax.broadcasted_iota(jnp.int32, (SUBLANES, sw), 0)
    odd = (lax.broadcasted_iota(jnp.int32, (rows, sw), 0) & 1) == 1

    def natural(j):
        return jnp.concatenate([u_ref[b, j * ts:(j + 1) * ts, :] for b in range(4)], axis=0)

    def last_rows(j):
        last = jnp.zeros((SUBLANES, sw), F32)
        for b in range(4):
            row = u_ref[b, (j + 1) * ts - 1:(j + 1) * ts, :].astype(BF16).astype(F32)
            last = jnp.where(sub8 == 2 * b, jnp.broadcast_to(row, (SUBLANES, sw)), last)
        return last

    def input_stage(j, before):
        a_cur = _dot(perm_ref[...], natural(j).astype(BF16))
        a_prev = jnp.where(odd, pltpu.roll(a_cur, 1, 0), pltpu.roll(a_cur, SUBLANES - 1, 0))
        first = jnp.where((sub8 & 1) == 1, a_prev[:SUBLANES], before)
        a_prev = jnp.concatenate([first, a_prev[SUBLANES:]], axis=0).astype(BF16)
        a_cur = a_cur.astype(BF16)
        return [_dot(jnp.concatenate([a_cur[:, s * cw:(s + 1) * cw],
                                      a_prev[:, s * cw:(s + 1) * cw]], axis=1), wb_ref[s])
                for s in range(S5_SLABS)]

    def scan_stage(bus, x):
        states, x_out = [], []
        for s in range(S5_SLABS):
            ar = lamr_ref[:, hs * s:hs * (s + 1)]
            ai = lami_ref[:, hs * s:hs * (s + 1)]
            xr, xi = x[s]
            st = []
            for t2 in range(n2):
                rs = slice(SUBLANES * t2, SUBLANES * (t2 + 1))
                xr, xi = (ar * xr - ai * xi + bus[s][rs, :hs],
                          ar * xi + ai * xr + bus[s][rs, hs:])
                st.append(jnp.concatenate([xr, xi], axis=1))
            x_out.append((xr, xi))
            states.append(jnp.concatenate(st, axis=0).astype(BF16))
        return states, x_out

    def output_stage(j, states):
        y_il = jnp.concatenate([_dot(states[s], cm_ref[s]) for s in range(S5_SLABS)], axis=1)
        y = _dot(permt_ref[...], y_il.astype(BF16))
        y = y + d_ref[...] * natural(j)
        y = jax.nn.gelu(y)
        z = _dot(y.astype(BF16), wglu_ref[...]) + bglu_ref[...]
        out = y * jax.nn.sigmoid(z)
        for b in range(4):
            o_ref[b, j * ts:(j + 1) * ts, :] = out[b * ts:(b + 1) * ts]

    nsub = tm // ts
    befores = [ulast_scr[...]] + [last_rows(j) for j in range(nsub - 1)]
    bus = [input_stage(j, befores[j]) for j in range(nsub)]
    ulast_scr[...] = last_rows(nsub - 1)
    x = [(x_scr[s, 0], x_scr[s, 1]) for s in range(S5_SLABS)]
    for j in range(nsub):
        states, x = scan_stage(bus[j], x)
        output_stage(j, states)
    for s in range(S5_SLABS):
        x_scr[s, 0], x_scr[s, 1] = x[s]


def _s5_perms(tm):
    n2 = tm // 2
    rows = SUBLANES * n2
    perm = np.zeros((rows, 4 * tm), np.float32)
    permt = np.zeros((4 * tm, rows), np.float32)
    for t2 in range(n2):
        for b in range(4):
            for par in range(2):
                r = SUBLANES * t2 + 2 * b + par
                t = 2 * t2 + par
                perm[r, b * tm + t] = 1.0
                permt[b * tm + t, r] = 1.0
    return jnp.asarray(perm, BF16), jnp.asarray(permt, BF16)


def _slab_block_diag(blocks):
    g, a, b = blocks.shape
    n = g // S5_SLABS
    eye = jnp.eye(n, dtype=blocks.dtype)
    placed = blocks.reshape(S5_SLABS, n, a, 1, b) * eye[None, :, None, :, None]
    return placed.reshape(S5_SLABS, n * a, n * b)


def _s5_weights(lam_re, lam_im, log_dt, b_re, b_im, c_re, c_im):
    g = lam_re.shape[0]
    dt = jnp.exp(log_dt)[:, None]
    mag = jnp.exp(lam_re * dt)
    lbr = mag * jnp.cos(lam_im * dt)
    lbi = mag * jnp.sin(lam_im * dt)
    nr, ni = lbr - 1.0, lbi
    den = lam_re * lam_re + lam_im * lam_im
    cr = (nr * lam_re + ni * lam_im) / den
    ci = (ni * lam_re - nr * lam_im) / den
    bbr = cr[..., None] * b_re - ci[..., None] * b_im
    bbi = cr[..., None] * b_im + ci[..., None] * b_re
    lr = lbr[..., None] * bbr - lbi[..., None] * bbi
    li = lbr[..., None] * bbi + lbi[..., None] * bbr
    l2r = lbr * lbr - lbi * lbi
    l2i = 2.0 * lbr * lbi
    bd = lambda a: _slab_block_diag(jnp.swapaxes(a, 1, 2))
    wb = jnp.concatenate([jnp.concatenate([bd(bbr), bd(bbi)], axis=2),
                          jnp.concatenate([bd(lr), bd(li)], axis=2)], axis=1).astype(BF16)
    cm = jnp.concatenate([bd(c_re), -bd(c_im)], axis=1).astype(BF16)
    lamr = jnp.broadcast_to(l2r.reshape(1, -1), (SUBLANES, l2r.size))
    lami = jnp.broadcast_to(l2i.reshape(1, -1), (SUBLANES, l2i.size))
    return wb, cm, lamr, lami


def _s5(u, wb, cm, lamr, lami, d_skip, w_glu_b, b_glu, cast_weights):
    bsz, seq, sw = u.shape
    assert bsz == 4, "the scan packs 4 batch rows x 2 token parities into 8 sublanes"
    tm = S5_TILE
    steps = seq // tm
    ns = lamr.shape[1]
    perm, permt = _s5_perms(S5_SUBTILE)
    const = lambda a: pl.BlockSpec(a.shape, lambda i: (0,) * a.ndim)
    rows = lambda a: pl.BlockSpec((a.shape[0] // steps, a.shape[1]), lambda i: (i, 0))
    assert all(w.shape[0] % (16 * steps) == 0 for w in cast_weights)
    d_row = d_skip.reshape(1, sw)
    bg = b_glu.reshape(1, sw)
    outs = pl.pallas_call(
        functools.partial(_s5_kernel, tm=tm, sw=sw, n_cast=len(cast_weights)),
        grid=(steps,),
        in_specs=[pl.BlockSpec((4, tm, sw), lambda i: (0, i, 0)),
                  const(perm), const(permt), const(wb), const(cm), const(lamr), const(lami),
                  const(d_row), const(w_glu_b), const(bg)] + [rows(w) for w in cast_weights],
        out_specs=[pl.BlockSpec((4, tm, sw), lambda i: (0, i, 0))] + [rows(w) for w in cast_weights],
        out_shape=[jax.ShapeDtypeStruct((bsz, seq, sw), F32)]
        + [jax.ShapeDtypeStruct(w.shape, BF16) for w in cast_weights],
        scratch_shapes=[pltpu.VMEM((S5_SLABS, 2, SUBLANES, ns // S5_SLABS), F32),
                        pltpu.VMEM((SUBLANES, sw), F32)],
        compiler_params=pltpu.CompilerParams(
            dimension_semantics=("arbitrary",),
            vmem_limit_bytes=VMEM_LIMIT_BYTES),
        name="s5",
    )(u, perm, permt, wb, cm, lamr, lami, d_row, w_glu_b, bg, *cast_weights)
    return outs[0], outs[1:]


def _front_kernel(x_ref, mod_ref, g_ref, w_ref, tri_ref, u_ref, gates_ref, at_ref,
                  q_scr, k_scr, v_scr, c_scr, cmin_scr, acc_scr, z_scr, w_scr, *, sw, aw, d):
    blk = ATTN_BLOCK
    top = ATTN_TOP_ROWS
    n_pairs = aw // LANES
    nh = 2 * n_pairs
    tm = x_ref.shape[1]
    nsub = tm // blk
    step = pl.program_id(1)
    tile0 = pl.multiple_of(step * tm, tm)
    mod = _mod_rows(mod_ref, d)
    even_head = (lax.broadcasted_iota(jnp.int32, (tm, aw), 1) // HEAD_DIM) % 2 == 0

    def qkv(hb):
        o = sw
        q = (_dot(hb, w_ref[:, o:o + aw]) * Q_SCALE).astype(BF16); o += aw
        q_scr[0] = jnp.where(even_head, q, jnp.zeros_like(q))
        q_scr[1] = jnp.where(even_head, jnp.zeros_like(q), q)
        kt = _dot(hb, w_ref[:, o:o + aw]).T.astype(BF16); o += aw
        for j in range(nsub):
            k_scr[nsub * step + j] = kt[:, j * blk:(j + 1) * blk]
        v = _dot(hb, w_ref[:, o:o + aw]).astype(BF16)
        v_scr[0, pl.ds(tile0, tm), :] = jnp.where(even_head, v, jnp.zeros_like(v))
        v_scr[1, pl.ds(tile0, tm), :] = jnp.where(even_head, jnp.zeros_like(v), v)

    def region(units, fillers=()):
        starts = [pl.multiple_of(kb * blk, blk) for kb, _, _, _ in units]
        for u, (_, r0, r1, _) in enumerate(units):
            n = r1 - r0
            for p in range(n_pairs):
                ls = slice(p * LANES, (p + 1) * LANES)
                kblk = k_scr[units[u][0], ls, :]
                zz = _dot(jnp.concatenate([q_scr[0, r0:r1, ls], q_scr[1, r0:r1, ls]], axis=0), kblk)
                z_scr[u * nh + 2 * p, 0:n] = zz[:n]
                z_scr[u * nh + 2 * p + 1, 0:n] = zz[n:]
        fillers = list(fillers)
        every = -(-len(units) // (len(fillers) + 1))
        for u, (_, r0, r1, diag) in enumerate(units):
            n = r1 - r0
            if diag:
                row = lax.broadcasted_iota(jnp.int32, (n, blk), 0) + r0 % blk
                valid = lax.broadcasted_iota(jnp.int32, (n, blk), 1) < row
            zs, sps = [], []
            for h in range(nh):
                z = z_scr[u * nh + h, 0:n]
                sp = jnp.maximum(z, 0.0) + jnp.log2(1.0 + jnp.exp2(-jnp.abs(z)))
                if diag:
                    sp = jnp.where(valid, sp, 0.0)
                zs.append(z)
                sps.append(sp.astype(BF16))
            incl_all = _dot(jnp.concatenate(sps, axis=0), tri_ref[...])
            for h in range(nh):
                z = zs[h]
                incl = incl_all[h * n:(h + 1) * n]
                total = jnp.broadcast_to(incl[:, 0:1], (n, blk))
                if diag:
                    w = jnp.where(valid, jnp.exp2(z - incl), 0.0)
                    c_new = total
                else:
                    c = c_scr[h, r0:r1]
                    w = jnp.exp2(z - incl - c)
                    c_new = c + total
                c_scr[h, r0:r1] = c_new
                c_low = c_new if h == 0 else jnp.minimum(c_low, c_new)
                w_scr[u * nh + h, 0:n] = w.astype(BF16)
            cmin_scr[r0:r1] = c_low
            if fillers and (u + 1) % every == 0:
                fillers.pop(0)()
        for filler in fillers:
            filler()
        for r0, r1 in dict.fromkeys((r0, r1) for _, r0, r1, _ in units):
            us = [u for u, (_, a, b, _) in enumerate(units) if (a, b) == (r0, r1)]
            for p in range(n_pairs):
                ls = slice(p * LANES, (p + 1) * LANES)
                ww = jnp.concatenate([w_scr[u * nh + 2 * p + hh, 0:r1 - r0]
                                      for u in us for hh in range(2)], axis=1)
                vv = jnp.concatenate([v_scr[hh, pl.ds(starts[u], blk), ls]
                                      for u in us for hh in range(2)], axis=0)
                if any(units[u][3] for u in us):
                    acc_scr[p, r0:r1] = _dot(ww, vv)
                else:
                    acc_scr[p, r0:r1] = acc_scr[p, r0:r1] + _dot(ww, vv)

    def c_min(r0, r1):
        return jnp.min(cmin_scr[r0:r1], axis=0, keepdims=True)[0, 0]

    def head_units(sub, qi, n_prev):
        base = sub * blk
        units = [(qi, base, base + blk, True)]
        if n_prev >= 1:
            units.append((qi - 1, base, base + blk, False))
        if n_prev >= 2:
            units.append((qi - 2, base, base + top, False))
        return units

    def tile(units):
        h = _rms(x_ref[0]) * g_ref[...]
        hb = (h * (1.0 + mod[1]) + mod[0]).astype(BF16)
        qkv(hb)
        o = sw + 3 * aw

        def chunk(ref, col, w0):
            def run():
                ref[0, :, col:col + FRONT_CHUNK] = _dot(hb, w_ref[:, w0 + col:w0 + col + FRONT_CHUNK])
            return run

        region(units, [chunk(ref, col, w0)
                       for ref, w0, width in ((gates_ref, o, 2 * d), (u_ref, 0, sw))
                       for col in range(0, width, FRONT_CHUNK)])

    @pl.when(step >= 1)
    def _():
        tile([u for sub in range(nsub) for u in head_units(sub, nsub * step + sub, 2)])

    @pl.when(step == 0)
    def _():
        tile([u for sub in range(nsub) for u in head_units(sub, sub, min(sub, 2))])

    def sweep(first_kb, cmin, r0, r1):
        def more(carry):
            kb, cmin = carry
            return jnp.logical_and(kb >= 0, cmin < UNDERFLOW_LOG2)

        def body(carry):
            kb, _ = carry
            region([(kb, r0, r1, False)])
            return kb - 1, c_min(r0, r1)

        lax.while_loop(more, body, (first_kb, cmin))

    tails = []
    for sub in range(nsub):
        qi = nsub * step + sub
        base = sub * blk
        tails.append((jnp.where(qi >= 2, qi - 3, -1), base, base + top))
        tails.append((jnp.where(qi >= 2, qi - 2, -1), base + top, base + blk))
    cmins = [c_min(r0, r1) for _, r0, r1 in tails]
    for (first_kb, r0, r1), cmin in zip(tails, cmins):
        sweep(first_kb, cmin, r0, r1)
    for p in range(n_pairs):
        at_ref[0, :, p * LANES:(p + 1) * LANES] = acc_scr[p].astype(at_ref.dtype)


def _attn_tri():
    blk = ATTN_BLOCK
    m = np.arange(blk)[:, None]
    j = np.arange(blk)[None, :]
    return jnp.asarray((m >= j).astype(np.float32), BF16)


def _front(x, mod, norm_g, w_in_b, sw, aw):
    bsz, seq, d = x.shape
    tm = FRONT_TILE
    blk = ATTN_BLOCK
    n = w_in_b.shape[1]
    n_pairs = aw // LANES
    nz = ATTN_REGION * (tm // blk) * 2 * n_pairs
    tri = _attn_tri()
    tok = lambda w: pl.BlockSpec((1, tm, w), lambda b, i: (b, i, 0))
    const = lambda a: pl.BlockSpec(a.shape, lambda b, i: (0,) * a.ndim,
                                   pipeline_mode=pl.Buffered(1))
    return pl.pallas_call(
        functools.partial(_front_kernel, sw=sw, aw=aw, d=d),
        grid=(bsz, seq // tm),
        in_specs=[tok(d),
                  pl.BlockSpec(mod.shape, lambda b, i: (0, 0)),
                  const(norm_g), const(w_in_b), const(tri)],
        out_specs=[tok(sw), tok(2 * d), tok(aw)],
        out_shape=[jax.ShapeDtypeStruct((bsz, seq, sw), F32),
                   jax.ShapeDtypeStruct((bsz, seq, 2 * d), F32),
                   jax.ShapeDtypeStruct((bsz, seq, aw), BF16)],
        scratch_shapes=[pltpu.VMEM((2, tm, aw), BF16),
                        pltpu.VMEM((seq // blk, aw, blk), BF16),
                        pltpu.VMEM((2, seq, aw), BF16),
                        pltpu.VMEM((2 * n_pairs, tm, blk), F32),
                        pltpu.VMEM((tm, blk), F32),
                        pltpu.VMEM((n_pairs, tm, LANES), F32),
                        pltpu.VMEM((nz, blk, blk), F32),
                        pltpu.VMEM((nz, blk, blk), BF16)],
        compiler_params=pltpu.CompilerParams(
            dimension_semantics=("arbitrary", "arbitrary"),
            vmem_limit_bytes=VMEM_LIMIT_BYTES),
        name="front",
    )(x, mod, norm_g, w_in_b, tri)


def _out_ffn_kernel(x_ref, s5_ref, at_ref, gates_ref, mod_ref, n2_ref, nf_ref,
                    wa_ref, wb_ref, wo_ref, wg_ref, wu_ref, wd_ref, o_ref, *, final_norm):
    mod = _mod_rows(mod_ref, x_ref.shape[2])
    tm = x_ref.shape[1]
    groups = [slice(r, r + tm // OUT_GROUPS) for r in range(0, tm, tm // OUT_GROUPS)]
    ms = []
    for g in groups:
        ya = _dot(s5_ref[0, g].astype(BF16), wa_ref[...])
        yb = _dot(at_ref[0, g], wb_ref[...])
        d = ya.shape[1]
        m = (jax.nn.sigmoid(gates_ref[0, g, :d]) * ya
             + jax.nn.sigmoid(gates_ref[0, g, d:]) * yb)
        ms.append(m.astype(BF16))
    x1s, hs = [], []
    for g, m in zip(groups, ms):
        x1 = x_ref[0, g] + mod[2] * _dot(m, wo_ref[...])
        h = _rms(x1) * n2_ref[...]
        x1s.append(x1)
        hs.append((h * (1.0 + mod[4]) + mod[3]).astype(BF16))
    acts = []
    for h in hs:
        gate = _dot(h, wg_ref[...])
        up = _dot(h, wu_ref[...])
        acts.append((gate * jax.nn.sigmoid(gate) * up).astype(BF16))
    for g, x1, act in zip(groups, x1s, acts):
        x2 = x1 + mod[5] * _dot(act, wd_ref[...])
        o_ref[0, g] = _rms(x2) * nf_ref[...] if final_norm else x2


def _out_ffn(x, s5o, attn, gates, mod, n2g, nfg, wa, wb, wo, wg, wu, wd, final_norm):
    bsz, seq, d = x.shape
    tm = OUT_TILE
    tok = lambda a: pl.BlockSpec((1, tm, a.shape[-1]), lambda b, i: (b, i, 0))
    const = lambda a: pl.BlockSpec(a.shape, lambda b, i: (0,) * a.ndim,
                                   pipeline_mode=pl.Buffered(1))
    return pl.pallas_call(
        functools.partial(_out_ffn_kernel, final_norm=final_norm),
        grid=(bsz, seq // tm),
        in_specs=[tok(x), tok(s5o), tok(attn), tok(gates),
                  pl.BlockSpec(mod.shape, lambda b, i: (0, 0)),
                  const(n2g), const(nfg),
                  const(wa), const(wb), const(wo), const(wg), const(wu), const(wd)],
        out_specs=pl.BlockSpec((1, tm, d), lambda b, i: (b, i, 0)),
        out_shape=jax.ShapeDtypeStruct((bsz, seq, d), F32),
        compiler_params=pltpu.CompilerParams(
            dimension_semantics=("arbitrary", "arbitrary"),
            vmem_limit_bytes=VMEM_LIMIT_BYTES),
        name="out_ffn",
    )(x, s5o, attn, gates, mod, n2g, nfg, wa, wb, wo, wg, wu, wd)


def kernel(x, c, w_ada, b_ada, norm1_g, w_in, lam_re, lam_im, log_dt, b_re, b_im, c_re, c_im,
           d_skip, w_glu, b_glu, w_a, w_b, w_o, norm2_g, w_ffn_gate, w_ffn_up, w_ffn_down,
           norm_f_g):
    depth = w_ada.shape[0]
    bsz, seq, d = x.shape
    sw = w_glu.shape[1]
    aw = w_b.shape[1]
    for l in range(depth):
        mod, w_in_b = _ada(c, w_ada[l], b_ada[l], w_in[l])
        u, gates, attn = _front(x, mod, norm1_g[l].reshape(1, d), w_in_b, sw, aw)
        wb, cm, lamr, lami = _s5_weights(lam_re[l], lam_im[l], log_dt[l], b_re[l], b_im[l],
                                         c_re[l], c_im[l])
        s5o, out_w = _s5(u, wb, cm, lamr, lami, d_skip[l], w_glu[l].astype(BF16), b_glu[l],
                         (w_a[l], w_b[l], w_o[l], w_ffn_gate[l], w_ffn_up[l], w_ffn_down[l]))
        x = _out_ffn(x, s5o, attn, gates, mod, norm2_g[l].reshape(1, d), norm_f_g.reshape(1, d),
                     *out_w, final_norm=(l == depth - 1))
    return x
```

```python
import functools
import math

import numpy as np
import jax
import jax.numpy as jnp
from jax import lax
from jax.experimental import pallas as pl
from jax.experimental.pallas import tpu as pltpu

F32 = jnp.float32
BF16 = jnp.bfloat16

S5_GROUP = 16
S5_STATE = 64
HEAD_DIM = 64
N_ADA = 6
RMS_EPS = 1e-6
Q_SCALE = math.log2(math.e) / math.sqrt(HEAD_DIM)
UNDERFLOW_LOG2 = 151.0

LANES = 128
SUBLANES = 8
VMEM_LIMIT_BYTES = 56 * 1024 * 1024

ADA_STEPS = 4
ATTN_BLOCK = 128
ATTN_REGION = 3
ATTN_TOP_ROWS = 64
S5_TILE = 256
S5_SUBTILE = 128
S5_SLABS = 4
FRONT_TILE = 512
FRONT_CHUNK = 512
OUT_TILE = 512
OUT_GROUPS = 2


def _dot(a, b):
    return jnp.dot(a, b, preferred_element_type=F32)


def _rms(x):
    return x * lax.rsqrt(jnp.mean(x * x, axis=-1, keepdims=True) + RMS_EPS)


def _mod_rows(mod_ref, d):
    row = mod_ref[pl.ds(pl.program_id(0), 1), :]
    return [row[:, k * d:(k + 1) * d] for k in range(N_ADA)]


def _ada_kernel(c_ref, w_ref, b_ref, cast_ref, o_ref, cast_out_ref):
    c = c_ref[...]
    bsz = c.shape[0]
    cond = c * jax.nn.sigmoid(c)
    pad = -bsz % SUBLANES
    if pad:
        cond = jnp.concatenate([cond, jnp.zeros((pad, c.shape[1]), F32)], axis=0)
    o_ref[...] = _dot(cond.astype(BF16), w_ref[...].astype(BF16))[:bsz] + b_ref[...]
    cast_out_ref[...] = cast_ref[...].astype(BF16)


def _ada(c, w_ada, b_ada, cast_weight):
    bsz, d = c.shape
    n = w_ada.shape[1]
    steps = ADA_STEPS
    tn = n // steps
    rows, cols = cast_weight.shape
    assert n % (steps * LANES) == 0 and rows % (16 * steps) == 0
    return pl.pallas_call(
        _ada_kernel,
        grid=(steps,),
        in_specs=[pl.BlockSpec((bsz, d), lambda j: (0, 0)),
                  pl.BlockSpec((d, tn), lambda j: (0, j)),
                  pl.BlockSpec((1, tn), lambda j: (0, j)),
                  pl.BlockSpec((rows // steps, cols), lambda j: (j, 0))],
        out_specs=[pl.BlockSpec((bsz, tn), lambda j: (0, j)),
                   pl.BlockSpec((rows // steps, cols), lambda j: (j, 0))],
        out_shape=[jax.ShapeDtypeStruct((bsz, n), F32),
                   jax.ShapeDtypeStruct((rows, cols), BF16)],
        compiler_params=pltpu.CompilerParams(vmem_limit_bytes=VMEM_LIMIT_BYTES),
        name="ada",
    )(c, w_ada, b_ada.reshape(1, n), cast_weight)


def _s5_kernel(u_ref, perm_ref, permt_ref, wb_ref, cm_ref, lamr_ref, lami_ref, d_ref,
               wglu_ref, bglu_ref, wa_ref, *rest, tm, sw, n_cast):
    cast_in, o_ref, cast_out = rest[:n_cast], rest[n_cast], rest[n_cast + 1:2 * n_cast + 1]
    x_scr, ulast_scr = rest[2 * n_cast + 1:]
    for src, dst in zip(cast_in, cast_out):
        dst[...] = src[...].astype(BF16)
    i = pl.program_id(0)
    ts = S5_SUBTILE
    n2 = ts // 2
    rows = SUBLANES * n2
    cw = sw // S5_SLABS
    hs = cw * S5_STATE // S5_GROUP

    @pl.when(i == 0)
    def _():
        x_scr[...] = jnp.zeros_like(x_scr)
        ulast_scr[...] = jnp.zeros_like(ulast_scr)

    sub8 = lax.broadcasted_iota(jnp.int32, (SUBLANES, sw), 0)
    odd = (lax.broadcasted_iota(jnp.int32, (rows, sw), 0) & 1) == 1

    def natural(j):
        return jnp.concatenate([u_ref[b, j * ts:(j + 1) * ts, :] for b in range(4)], axis=0)

    def last_rows(j):
        last = jnp.zeros((SUBLANES, sw), F32)
        for b in range(4):
            row = u_ref[b, (j + 1) * ts - 1:(j + 1) * ts, :].astype(BF16).astype(F32)
            last = jnp.where(sub8 == 2 * b, jnp.broadcast_to(row, (SUBLANES, sw)), last)
        return last

    def input_stage(j, before):
        a_cur = _dot(perm_ref[...], natural(j).astype(BF16))
        a_prev = jnp.where(odd, pltpu.roll(a_cur, 1, 0), pltpu.roll(a_cur, SUBLANES - 1, 0))
        first = jnp.where((sub8 & 1) == 1, a_prev[:SUBLANES], before)
        a_prev = jnp.concatenate([first, a_prev[SUBLANES:]], axis=0).astype(BF16)
        a_cur = a_cur.astype(BF16)
        return [_dot(jnp.concatenate([a_cur[:, s * cw:(s + 1) * cw],
                                      a_prev[:, s * cw:(s + 1) * cw]], axis=1), wb_ref[s])
                for s in range(S5_SLABS)]

    def scan_stage(bus, x):
        states, x_out = [], []
        for s in range(S5_SLABS):
            ar = lamr_ref[:, hs * s:hs * (s + 1)]
            ai = lami_ref[:, hs * s:hs * (s + 1)]
            xr, xi = x[s]
            st = []
            for t2 in range(n2):
                rs = slice(SUBLANES * t2, SUBLANES * (t2 + 1))
                xr, xi = (ar * xr - ai * xi + bus[s][rs, :hs],
                          ar * xi + ai * xr + bus[s][rs, hs:])
                st.append(jnp.concatenate([xr, xi], axis=1))
            x_out.append((xr, xi))
            states.append(jnp.concatenate(st, axis=0).astype(BF16))
        return states, x_out

    def output_stage(j, states):
        y_il = jnp.concatenate([_dot(states[s], cm_ref[s]) for s in range(S5_SLABS)], axis=1)
        y = _dot(permt_ref[...], y_il.astype(BF16))
        y = y + d_ref[...] * natural(j)
        y = jax.nn.gelu(y)
        z = _dot(y.astype(BF16), wglu_ref[...]) + bglu_ref[...]
        out = _dot((y * jax.nn.sigmoid(z)).astype(BF16), wa_ref[...])
        for b in range(4):
            o_ref[b, j * ts:(j + 1) * ts, :] = out[b * ts:(b + 1) * ts]

    nsub = tm // ts
    befores = [ulast_scr[...]] + [last_rows(j) for j in range(nsub - 1)]
    bus = [input_stage(j, befores[j]) for j in range(nsub)]
    ulast_scr[...] = last_rows(nsub - 1)
    x = [(x_scr[s, 0], x_scr[s, 1]) for s in range(S5_SLABS)]
    for j in range(nsub):
        states, x = scan_stage(bus[j], x)
        output_stage(j, states)
    for s in range(S5_SLABS):
        x_scr[s, 0], x_scr[s, 1] = x[s]


def _s5_perms(tm):
    n2 = tm // 2
    rows = SUBLANES * n2
    perm = np.zeros((rows, 4 * tm), np.float32)
    permt = np.zeros((4 * tm, rows), np.float32)
    for t2 in range(n2):
        for b in range(4):
            for par in range(2):
                r = SUBLANES * t2 + 2 * b + par
                t = 2 * t2 + par
                perm[r, b * tm + t] = 1.0
                permt[b * tm + t, r] = 1.0
    return jnp.asarray(perm, BF16), jnp.asarray(permt, BF16)


def _slab_block_diag(blocks):
    g, a, b = blocks.shape
    n = g // S5_SLABS
    eye = jnp.eye(n, dtype=blocks.dtype)
    placed = blocks.reshape(S5_SLABS, n, a, 1, b) * eye[None, :, None, :, None]
    return placed.reshape(S5_SLABS, n * a, n * b)


def _s5_weights(lam_re, lam_im, log_dt, b_re, b_im, c_re, c_im):
    g = lam_re.shape[0]
    dt = jnp.exp(log_dt)[:, None]
    mag = jnp.exp(lam_re * dt)
    lbr = mag * jnp.cos(lam_im * dt)
    lbi = mag * jnp.sin(lam_im * dt)
    nr, ni = lbr - 1.0, lbi
    den = lam_re * lam_re + lam_im * lam_im
    cr = (nr * lam_re + ni * lam_im) / den
    ci = (ni * lam_re - nr * lam_im) / den
    bbr = cr[..., None] * b_re - ci[..., None] * b_im
    bbi = cr[..., None] * b_im + ci[..., None] * b_re
    lr = lbr[..., None] * bbr - lbi[..., None] * bbi
    li = lbr[..., None] * bbi + lbi[..., None] * bbr
    l2r = lbr * lbr - lbi * lbi
    l2i = 2.0 * lbr * lbi
    bd = lambda a: _slab_block_diag(jnp.swapaxes(a, 1, 2))
    wb = jnp.concatenate([jnp.concatenate([bd(bbr), bd(bbi)], axis=2),
                          jnp.concatenate([bd(lr), bd(li)], axis=2)], axis=1).astype(BF16)
    cm = jnp.concatenate([bd(c_re), -bd(c_im)], axis=1).astype(BF16)
    lamr = jnp.broadcast_to(l2r.reshape(1, -1), (SUBLANES, l2r.size))
    lami = jnp.broadcast_to(l2i.reshape(1, -1), (SUBLANES, l2i.size))
    return wb, cm, lamr, lami


def _s5(u, wb, cm, lamr, lami, d_skip, w_glu_b, b_glu, w_a_b, cast_weights):
    bsz, seq, sw = u.shape
    d = w_a_b.shape[1]
    assert bsz == 4, "the scan packs 4 batch rows x 2 token parities into 8 sublanes"
    tm = S5_TILE
    steps = seq // tm
    ns = lamr.shape[1]
    perm, permt = _s5_perms(S5_SUBTILE)
    const = lambda a: pl.BlockSpec(a.shape, lambda i: (0,) * a.ndim)
    rows = lambda a: pl.BlockSpec((a.shape[0] // steps, a.shape[1]), lambda i: (i, 0))
    assert all(w.shape[0] % (16 * steps) == 0 for w in cast_weights)
    d_row = d_skip.reshape(1, sw)
    bg = b_glu.reshape(1, sw)
    outs = pl.pallas_call(
        functools.partial(_s5_kernel, tm=tm, sw=sw, n_cast=len(cast_weights)),
        grid=(steps,),
        in_specs=[pl.BlockSpec((4, tm, sw), lambda i: (0, i, 0)),
                  const(perm), const(permt), const(wb), const(cm), const(lamr), const(lami),
                  const(d_row), const(w_glu_b), const(bg), const(w_a_b)]
        + [rows(w) for w in cast_weights],
        out_specs=[pl.BlockSpec((4, tm, d), lambda i: (0, i, 0))] + [rows(w) for w in cast_weights],
        out_shape=[jax.ShapeDtypeStruct((bsz, seq, d), F32)]
        + [jax.ShapeDtypeStruct(w.shape, BF16) for w in cast_weights],
        scratch_shapes=[pltpu.VMEM((S5_SLABS, 2, SUBLANES, ns // S5_SLABS), F32),
                        pltpu.VMEM((SUBLANES, sw), F32)],
        compiler_params=pltpu.CompilerParams(
            dimension_semantics=("arbitrary",),
            vmem_limit_bytes=VMEM_LIMIT_BYTES),
        name="s5",
    )(u, perm, permt, wb, cm, lamr, lami, d_row, w_glu_b, bg, w_a_b, *cast_weights)
    return outs[0], outs[1:]


def _front_kernel(x_ref, mod_ref, g_ref, w_ref, tri_ref, u_ref, gates_ref, at_ref,
                  q_scr, k_scr, v_scr, c_scr, cmin_scr, acc_scr, z_scr, w_scr, *, sw, aw, d):
    blk = ATTN_BLOCK
    top = ATTN_TOP_ROWS
    n_pairs = aw // LANES
    nh = 2 * n_pairs
    tm = x_ref.shape[1]
    nsub = tm // blk
    step = pl.program_id(1)
    tile0 = pl.multiple_of(step * tm, tm)
    mod = _mod_rows(mod_ref, d)
    even_head = (lax.broadcasted_iota(jnp.int32, (tm, aw), 1) // HEAD_DIM) % 2 == 0

    def qkv(hb):
        o = sw
        q = (_dot(hb, w_ref[:, o:o + aw]) * Q_SCALE).astype(BF16); o += aw
        q_scr[0] = jnp.where(even_head, q, jnp.zeros_like(q))
        q_scr[1] = jnp.where(even_head, jnp.zeros_like(q), q)
        kt = _dot(hb, w_ref[:, o:o + aw]).T.astype(BF16); o += aw
        for j in range(nsub):
            k_scr[nsub * step + j] = kt[:, j * blk:(j + 1) * blk]
        v = _dot(hb, w_ref[:, o:o + aw]).astype(BF16)
        v_scr[0, pl.ds(tile0, tm), :] = jnp.where(even_head, v, jnp.zeros_like(v))
        v_scr[1, pl.ds(tile0, tm), :] = jnp.where(even_head, jnp.zeros_like(v), v)

    def region(units, fillers=()):
        starts = [pl.multiple_of(kb * blk, blk) for kb, _, _, _ in units]
        for u, (_, r0, r1, _) in enumerate(units):
            n = r1 - r0
            for p in range(n_pairs):
                ls = slice(p * LANES, (p + 1) * LANES)
                kblk = k_scr[units[u][0], ls, :]
                zz = _dot(jnp.concatenate([q_scr[0, r0:r1, ls], q_scr[1, r0:r1, ls]], axis=0), kblk)
                z_scr[u * nh + 2 * p, 0:n] = zz[:n]
                z_scr[u * nh + 2 * p + 1, 0:n] = zz[n:]
        fillers = list(fillers)
        every = -(-len(units) // (len(fillers) + 1))
        for u, (_, r0, r1, diag) in enumerate(units):
            n = r1 - r0
            if diag:
                row = lax.broadcasted_iota(jnp.int32, (n, blk), 0) + r0 % blk
                valid = lax.broadcasted_iota(jnp.int32, (n, blk), 1) < row
            zs, sps = [], []
            for h in range(nh):
                z = z_scr[u * nh + h, 0:n]
                sp = jnp.maximum(z, 0.0) + jnp.log2(1.0 + jnp.exp2(-jnp.abs(z)))
                if diag:
                    sp = jnp.where(valid, sp, 0.0)
                zs.append(z)
                sps.append(sp.astype(BF16))
            incl_all = _dot(jnp.concatenate(sps, axis=0), tri_ref[...])
            for h in range(nh):
                z = zs[h]
                incl = incl_all[h * n:(h + 1) * n]
                total = jnp.broadcast_to(incl[:, 0:1], (n, blk))
                if diag:
                    w = jnp.where(valid, jnp.exp2(z - incl), 0.0)
                    c_new = total
                else:
                    c = c_scr[h, r0:r1]
                    w = jnp.exp2(z - incl - c)
                    c_new = c + total
                c_scr[h, r0:r1] = c_new
                c_low = c_new if h == 0 else jnp.minimum(c_low, c_new)
                w_scr[u * nh + h, 0:n] = w.astype(BF16)
            cmin_scr[r0:r1] = c_low
            if fillers and (u + 1) % every == 0:
                fillers.pop(0)()
        for filler in fillers:
            filler()
        for r0, r1 in dict.fromkeys((r0, r1) for _, r0, r1, _ in units):
            us = [u for u, (_, a, b, _) in enumerate(units) if (a, b) == (r0, r1)]
            for p in range(n_pairs):
                ls = slice(p * LANES, (p + 1) * LANES)
                ww = jnp.concatenate([w_scr[u * nh + 2 * p + hh, 0:r1 - r0]
                                      for u in us for hh in range(2)], axis=1)
                vv = jnp.concatenate([v_scr[hh, pl.ds(starts[u], blk), ls]
                                      for u in us for hh in range(2)], axis=0)
                if any(units[u][3] for u in us):
                    acc_scr[p, r0:r1] = _dot(ww, vv)
                else:
                    acc_scr[p, r0:r1] = acc_scr[p, r0:r1] + _dot(ww, vv)

    def c_min(r0, r1):
        return jnp.min(cmin_scr[r0:r1], axis=0, keepdims=True)[0, 0]

    def head_units(sub, qi, n_prev):
        base = sub * blk
        units = [(qi, base, base + blk, True)]
        if n_prev >= 1:
            units.append((qi - 1, base, base + blk, False))
        if n_prev >= 2:
            units.append((qi - 2, base, base + top, False))
        return units

    def tile(units):
        h = _rms(x_ref[0]) * g_ref[...]
        hb = (h * (1.0 + mod[1]) + mod[0]).astype(BF16)
        qkv(hb)
        o = sw + 3 * aw

        def chunk(ref, col, w0):
            def run():
                ref[0, :, col:col + FRONT_CHUNK] = _dot(hb, w_ref[:, w0 + col:w0 + col + FRONT_CHUNK])
            return run

        region(units, [chunk(ref, col, w0)
                       for ref, w0, width in ((gates_ref, o, 2 * d), (u_ref, 0, sw))
                       for col in range(0, width, FRONT_CHUNK)])

    @pl.when(step >= 1)
    def _():
        tile([u for sub in range(nsub) for u in head_units(sub, nsub * step + sub, 2)])

    @pl.when(step == 0)
    def _():
        tile([u for sub in range(nsub) for u in head_units(sub, sub, min(sub, 2))])

    def sweep(first_kb, cmin, r0, r1):
        def more(carry):
            kb, cmin = carry
            return jnp.logical_and(kb >= 0, cmin < UNDERFLOW_LOG2)

        def body(carry):
            kb, _ = carry
            region([(kb, r0, r1, False)])
            return kb - 1, c_min(r0, r1)

        lax.while_loop(more, body, (first_kb, cmin))

    tails = []
    for sub in range(nsub):
        qi = nsub * step + sub
        base = sub * blk
        tails.append((jnp.where(qi >= 2, qi - 3, -1), base, base + top))
        tails.append((jnp.where(qi >= 2, qi - 2, -1), base + top, base + blk))
    cmins = [c_min(r0, r1) for _, r0, r1 in tails]
    for (first_kb, r0, r1), cmin in zip(tails, cmins):
        sweep(first_kb, cmin, r0, r1)
    for p in range(n_pairs):
        at_ref[0, :, p * LANES:(p + 1) * LANES] = acc_scr[p].astype(at_ref.dtype)


def _attn_tri():
    blk = ATTN_BLOCK
    m = np.arange(blk)[:, None]
    j = np.arange(blk)[None, :]
    return jnp.asarray((m >= j).astype(np.float32), BF16)


def _front(x, mod, norm_g, w_in_b, sw, aw):
    bsz, seq, d = x.shape
    tm = FRONT_TILE
    blk = ATTN_BLOCK
    n = w_in_b.shape[1]
    n_pairs = aw // LANES
    nz = ATTN_REGION * (tm // blk) * 2 * n_pairs
    tri = _attn_tri()
    tok = lambda w: pl.BlockSpec((1, tm, w), lambda b, i: (b, i, 0))
    const = lambda a: pl.BlockSpec(a.shape, lambda b, i: (0,) * a.ndim,
                                   pipeline_mode=pl.Buffered(1))
    return pl.pallas_call(
        functools.partial(_front_kernel, sw=sw, aw=aw, d=d),
        grid=(bsz, seq // tm),
        in_specs=[tok(d),
                  pl.BlockSpec(mod.shape, lambda b, i: (0, 0)),
                  const(norm_g), const(w_in_b), const(tri)],
        out_specs=[tok(sw), tok(2 * d), tok(aw)],
        out_shape=[jax.ShapeDtypeStruct((bsz, seq, sw), F32),
                   jax.ShapeDtypeStruct((bsz, seq, 2 * d), F32),
                   jax.ShapeDtypeStruct((bsz, seq, aw), BF16)],
        scratch_shapes=[pltpu.VMEM((2, tm, aw), BF16),
                        pltpu.VMEM((seq // blk, aw, blk), BF16),
                        pltpu.VMEM((2, seq, aw), BF16),
                        pltpu.VMEM((2 * n_pairs, tm, blk), F32),
                        pltpu.VMEM((tm, blk), F32),
                        pltpu.VMEM((n_pairs, tm, LANES), F32),
                        pltpu.VMEM((nz, blk, blk), F32),
                        pltpu.VMEM((nz, blk, blk), BF16)],
        compiler_params=pltpu.CompilerParams(
            dimension_semantics=("arbitrary", "arbitrary"),
            vmem_limit_bytes=VMEM_LIMIT_BYTES),
        name="front",
    )(x, mod, norm_g, w_in_b, tri)


def _out_ffn_kernel(x_ref, s5_ref, at_ref, gates_ref, mod_ref, n2_ref, nf_ref,
                    wb_ref, wo_ref, wg_ref, wu_ref, wd_ref, o_ref, *, final_norm):
    mod = _mod_rows(mod_ref, x_ref.shape[2])
    tm = x_ref.shape[1]
    groups = [slice(r, r + tm // OUT_GROUPS) for r in range(0, tm, tm // OUT_GROUPS)]
    ms = []
    for g in groups:
        ya = s5_ref[0, g]
        yb = _dot(at_ref[0, g], wb_ref[...])
        d = ya.shape[1]
        m = (jax.nn.sigmoid(gates_ref[0, g, :d]) * ya
             + jax.nn.sigmoid(gates_ref[0, g, d:]) * yb)
        ms.append(m.astype(BF16))
    x1s, hs = [], []
    for g, m in zip(groups, ms):
        x1 = x_ref[0, g] + mod[2] * _dot(m, wo_ref[...])
        h = _rms(x1) * n2_ref[...]
        x1s.append(x1)
        hs.append((h * (1.0 + mod[4]) + mod[3]).astype(BF16))
    acts = []
    for h in hs:
        gate = _dot(h, wg_ref[...])
        up = _dot(h, wu_ref[...])
        acts.append((gate * jax.nn.sigmoid(gate) * up).astype(BF16))
    for g, x1, act in zip(groups, x1s, acts):
        x2 = x1 + mod[5] * _dot(act, wd_ref[...])
        o_ref[0, g] = _rms(x2) * nf_ref[...] if final_norm else x2


def _out_ffn(x, s5o, attn, gates, mod, n2g, nfg, wb, wo, wg, wu, wd, final_norm):
    bsz, seq, d = x.shape
    tm = OUT_TILE
    tok = lambda a: pl.BlockSpec((1, tm, a.shape[-1]), lambda b, i: (b, i, 0))
    const = lambda a: pl.BlockSpec(a.shape, lambda b, i: (0,) * a.ndim,
                                   pipeline_mode=pl.Buffered(1))
    return pl.pallas_call(
        functools.partial(_out_ffn_kernel, final_norm=final_norm),
        grid=(bsz, seq // tm),
        in_specs=[tok(x), tok(s5o), tok(attn), tok(gates),
                  pl.BlockSpec(mod.shape, lambda b, i: (0, 0)),
                  const(n2g), const(nfg),
                  const(wb), const(wo), const(wg), const(wu), const(wd)],
        out_specs=pl.BlockSpec((1, tm, d), lambda b, i: (b, i, 0)),
        out_shape=jax.ShapeDtypeStruct((bsz, seq, d), F32),
        compiler_params=pltpu.CompilerParams(
            dimension_semantics=("arbitrary", "arbitrary"),
            vmem_limit_bytes=VMEM_LIMIT_BYTES),
        name="out_ffn",
    )(x, s5o, attn, gates, mod, n2g, nfg, wb, wo, wg, wu, wd)


def kernel(x, c, w_ada, b_ada, norm1_g, w_in, lam_re, lam_im, log_dt, b_re, b_im, c_re, c_im,
           d_skip, w_glu, b_glu, w_a, w_b, w_o, norm2_g, w_ffn_gate, w_ffn_up, w_ffn_down,
           norm_f_g):
    depth = w_ada.shape[0]
    bsz, seq, d = x.shape
    sw = w_glu.shape[1]
    aw = w_b.shape[1]
    for l in range(depth):
        mod, w_in_b = _ada(c, w_ada[l], b_ada[l], w_in[l])
        u, gates, attn = _front(x, mod, norm1_g[l].reshape(1, d), w_in_b, sw, aw)
        wb, cm, lamr, lami = _s5_weights(lam_re[l], lam_im[l], log_dt[l], b_re[l], b_im[l],
                                         c_re[l], c_im[l])
        s5o, out_w = _s5(u, wb, cm, lamr, lami, d_skip[l], w_glu[l].astype(BF16), b_glu[l],
                         w_a[l].astype(BF16),
                         (w_b[l], w_o[l], w_ffn_gate[l], w_ffn_up[l], w_ffn_down[l]))
        x = _out_ffn(x, s5o, attn, gates, mod, norm2_g[l].reshape(1, d), norm_f_g.reshape(1, d),
                     *out_w, final_norm=(l == depth - 1))
    return x
```

```python
import functools
import math

import numpy as np
import jax
import jax.numpy as jnp
from jax import lax
from jax.experimental import pallas as pl
from jax.experimental.pallas import tpu as pltpu

F32 = jnp.float32
BF16 = jnp.bfloat16

S5_GROUP = 16
S5_STATE = 64
HEAD_DIM = 64
N_ADA = 6
RMS_EPS = 1e-6
Q_SCALE = math.log2(math.e) / math.sqrt(HEAD_DIM)
UNDERFLOW_LOG2 = 151.0

LANES = 128
SUBLANES = 8
VMEM_LIMIT_BYTES = 56 * 1024 * 1024

ADA_STEPS = 4
ATTN_BLOCK = 128
ATTN_REGION = 3
ATTN_TOP_ROWS = 64
S5_TILE = 256
S5_SUBTILE = 128
S5_SLABS = 4
FRONT_TILE = 512
FRONT_CHUNK = 512
OUT_TILE = 512
OUT_GROUPS = 2


def _dot(a, b):
    return jnp.dot(a, b, preferred_element_type=F32)


def _rms(x):
    return x * lax.rsqrt(jnp.mean(x * x, axis=-1, keepdims=True) + RMS_EPS)


def _mod_rows(mod_ref, d):
    row = mod_ref[pl.ds(pl.program_id(0), 1), :]
    return [row[:, k * d:(k + 1) * d] for k in range(N_ADA)]


def _ada_kernel(c_ref, w_ref, b_ref, cast_ref, o_ref, cast_out_ref):
    c = c_ref[...]
    bsz = c.shape[0]
    cond = c * jax.nn.sigmoid(c)
    pad = -bsz % SUBLANES
    if pad:
        cond = jnp.concatenate([cond, jnp.zeros((pad, c.shape[1]), F32)], axis=0)
    o_ref[...] = _dot(cond.astype(BF16), w_ref[...].astype(BF16))[:bsz] + b_ref[...]
    cast_out_ref[...] = cast_ref[...].astype(BF16)


def _ada(c, w_ada, b_ada, cast_weight):
    bsz, d = c.shape
    n = w_ada.shape[1]
    steps = ADA_STEPS
    tn = n // steps
    rows, cols = cast_weight.shape
    assert n % (steps * LANES) == 0 and rows % (16 * steps) == 0
    return pl.pallas_call(
        _ada_kernel,
        grid=(steps,),
        in_specs=[pl.BlockSpec((bsz, d), lambda j: (0, 0)),
                  pl.BlockSpec((d, tn), lambda j: (0, j)),
                  pl.BlockSpec((1, tn), lambda j: (0, j)),
                  pl.BlockSpec((rows // steps, cols), lambda j: (j, 0))],
        out_specs=[pl.BlockSpec((bsz, tn), lambda j: (0, j)),
                   pl.BlockSpec((rows // steps, cols), lambda j: (j, 0))],
        out_shape=[jax.ShapeDtypeStruct((bsz, n), F32),
                   jax.ShapeDtypeStruct((rows, cols), BF16)],
        compiler_params=pltpu.CompilerParams(vmem_limit_bytes=VMEM_LIMIT_BYTES),
        name="ada",
    )(c, w_ada, b_ada.reshape(1, n), cast_weight)


def _s5_kernel(u_ref, perm_ref, permt_ref, wb_ref, cm_ref, lamr_ref, lami_ref, d_ref,
               wglu_ref, bglu_ref, wa_ref, *rest, tm, sw, n_cast):
    cast_in, o_ref, cast_out = rest[:n_cast], rest[n_cast], rest[n_cast + 1:2 * n_cast + 1]
    x_scr, ulast_scr = rest[2 * n_cast + 1:]
    for src, dst in zip(cast_in, cast_out):
        dst[...] = src[...].astype(BF16)
    i = pl.program_id(0)
    ts = S5_SUBTILE
    n2 = ts // 2
    rows = SUBLANES * n2
    cw = sw // S5_SLABS
    hs = cw * S5_STATE // S5_GROUP

    @pl.when(i == 0)
    def _():
        x_scr[...] = jnp.zeros_like(x_scr)
        ulast_scr[...] = jnp.zeros_like(ulast_scr)

    sub8 = lax.broadcasted_iota(jnp.int32, (SUBLANES, sw), 0)
    odd = (lax.broadcasted_iota(jnp.int32, (rows, sw), 0) & 1) == 1

    def natural(j):
        return jnp.concatenate([u_ref[b, j * ts:(j + 1) * ts, :] for b in range(4)], axis=0)

    def last_rows(j):
        last = jnp.zeros((SUBLANES, sw), F32)
        for b in range(4):
            row = u_ref[b, (j + 1) * ts - 1:(j + 1) * ts, :].astype(BF16).astype(F32)
            last = jnp.where(sub8 == 2 * b, jnp.broadcast_to(row, (SUBLANES, sw)), last)
        return last

    def input_stage(j, before):
        a_cur = _dot(perm_ref[...], natural(j).astype(BF16))
        a_prev = jnp.where(odd, pltpu.roll(a_cur, 1, 0), pltpu.roll(a_cur, SUBLANES - 1, 0))
        first = jnp.where((sub8 & 1) == 1, a_prev[:SUBLANES], before)
        a_prev = jnp.concatenate([first, a_prev[SUBLANES:]], axis=0).astype(BF16)
        a_cur = a_cur.astype(BF16)
        return [_dot(jnp.concatenate([a_cur[:, s * cw:(s + 1) * cw],
                                      a_prev[:, s * cw:(s + 1) * cw]], axis=1), wb_ref[s])
                for s in range(S5_SLABS)]

    def scan_stage(bus, x):
        states, x_out = [], []
        for s in range(S5_SLABS):
            ar = lamr_ref[:, hs * s:hs * (s + 1)]
            ai = lami_ref[:, hs * s:hs * (s + 1)]
            xr, xi = x[s]
            st = []
            for t2 in range(n2):
                rs = slice(SUBLANES * t2, SUBLANES * (t2 + 1))
                xr, xi = (ar * xr - ai * xi + bus[s][rs, :hs],
                          ar * xi + ai * xr + bus[s][rs, hs:])
                st.append(jnp.concatenate([xr, xi], axis=1))
            x_out.append((xr, xi))
            states.append(jnp.concatenate(st, axis=0).astype(BF16))
        return states, x_out

    def output_stage(j, states):
        y_il = jnp.concatenate([_dot(states[s], cm_ref[s]) for s in range(S5_SLABS)], axis=1)
        y = _dot(permt_ref[...], y_il.astype(BF16))
        y = y + d_ref[...] * natural(j)
        y = jax.nn.gelu(y)
        z = _dot(y.astype(BF16), wglu_ref[...]) + bglu_ref[...]
        out = _dot((y * jax.nn.sigmoid(z)).astype(BF16), wa_ref[...])
        for b in range(4):
            o_ref[b, j * ts:(j + 1) * ts, :] = out[b * ts:(b + 1) * ts]

    nsub = tm // ts
    befores = [ulast_scr[...]] + [last_rows(j) for j in range(nsub - 1)]
    bus = [input_stage(j, befores[j]) for j in range(nsub)]
    ulast_scr[...] = last_rows(nsub - 1)
    x = [(x_scr[s, 0], x_scr[s, 1]) for s in range(S5_SLABS)]
    states = []
    for j in range(nsub):
        st, x = scan_stage(bus[j], x)
        states.append(st)
        if j >= 1:
            output_stage(j - 1, states[j - 1])
    output_stage(nsub - 1, states[nsub - 1])
    for s in range(S5_SLABS):
        x_scr[s, 0], x_scr[s, 1] = x[s]


def _s5_perms(tm):
    n2 = tm // 2
    rows = SUBLANES * n2
    perm = np.zeros((rows, 4 * tm), np.float32)
    permt = np.zeros((4 * tm, rows), np.float32)
    for t2 in range(n2):
        for b in range(4):
            for par in range(2):
                r = SUBLANES * t2 + 2 * b + par
                t = 2 * t2 + par
                perm[r, b * tm + t] = 1.0
                permt[b * tm + t, r] = 1.0
    return jnp.asarray(perm, BF16), jnp.asarray(permt, BF16)


def _slab_block_diag(blocks):
    g, a, b = blocks.shape
    n = g // S5_SLABS
    eye = jnp.eye(n, dtype=blocks.dtype)
    placed = blocks.reshape(S5_SLABS, n, a, 1, b) * eye[None, :, None, :, None]
    return placed.reshape(S5_SLABS, n * a, n * b)


def _s5_weights(lam_re, lam_im, log_dt, b_re, b_im, c_re, c_im):
    g = lam_re.shape[0]
    dt = jnp.exp(log_dt)[:, None]
    mag = jnp.exp(lam_re * dt)
    lbr = mag * jnp.cos(lam_im * dt)
    lbi = mag * jnp.sin(lam_im * dt)
    nr, ni = lbr - 1.0, lbi
    den = lam_re * lam_re + lam_im * lam_im
    cr = (nr * lam_re + ni * lam_im) / den
    ci = (ni * lam_re - nr * lam_im) / den
    bbr = cr[..., None] * b_re - ci[..., None] * b_im
    bbi = cr[..., None] * b_im + ci[..., None] * b_re
    lr = lbr[..., None] * bbr - lbi[..., None] * bbi
    li = lbr[..., None] * bbi + lbi[..., None] * bbr
    l2r = lbr * lbr - lbi * lbi
    l2i = 2.0 * lbr * lbi
    bd = lambda a: _slab_block_diag(jnp.swapaxes(a, 1, 2))
    wb = jnp.concatenate([jnp.concatenate([bd(bbr), bd(bbi)], axis=2),
                          jnp.concatenate([bd(lr), bd(li)], axis=2)], axis=1).astype(BF16)
    cm = jnp.concatenate([bd(c_re), -bd(c_im)], axis=1).astype(BF16)
    lamr = jnp.broadcast_to(l2r.reshape(1, -1), (SUBLANES, l2r.size))
    lami = jnp.broadcast_to(l2i.reshape(1, -1), (SUBLANES, l2i.size))
    return wb, cm, lamr, lami


def _s5(u, wb, cm, lamr, lami, d_skip, w_glu_b, b_glu, w_a_b, cast_weights):
    bsz, seq, sw = u.shape
    d = w_a_b.shape[1]
    assert bsz == 4, "the scan packs 4 batch rows x 2 token parities into 8 sublanes"
    tm = S5_TILE
    steps = seq // tm
    ns = lamr.shape[1]
    perm, permt = _s5_perms(S5_SUBTILE)
    const = lambda a: pl.BlockSpec(a.shape, lambda i: (0,) * a.ndim)
    rows = lambda a: pl.BlockSpec((a.shape[0] // steps, a.shape[1]), lambda i: (i, 0))
    assert all(w.shape[0] % (16 * steps) == 0 for w in cast_weights)
    d_row = d_skip.reshape(1, sw)
    bg = b_glu.reshape(1, sw)
    outs = pl.pallas_call(
        functools.partial(_s5_kernel, tm=tm, sw=sw, n_cast=len(cast_weights)),
        grid=(steps,),
        in_specs=[pl.BlockSpec((4, tm, sw), lambda i: (0, i, 0)),
                  const(perm), const(permt), const(wb), const(cm), const(lamr), const(lami),
                  const(d_row), const(w_glu_b), const(bg), const(w_a_b)]
        + [rows(w) for w in cast_weights],
        out_specs=[pl.BlockSpec((4, tm, d), lambda i: (0, i, 0))] + [rows(w) for w in cast_weights],
        out_shape=[jax.ShapeDtypeStruct((bsz, seq, d), F32)]
        + [jax.ShapeDtypeStruct(w.shape, BF16) for w in cast_weights],
        scratch_shapes=[pltpu.VMEM((S5_SLABS, 2, SUBLANES, ns // S5_SLABS), F32),
                        pltpu.VMEM((SUBLANES, sw), F32)],
        compiler_params=pltpu.CompilerParams(
            dimension_semantics=("arbitrary",),
            vmem_limit_bytes=VMEM_LIMIT_BYTES),
        name="s5",
    )(u, perm, permt, wb, cm, lamr, lami, d_row, w_glu_b, bg, w_a_b, *cast_weights)
    return outs[0], outs[1:]


def _front_kernel(x_ref, mod_ref, g_ref, w_ref, tri_ref, u_ref, gates_ref, at_ref,
                  q_scr, k_scr, v_scr, c_scr, cmin_scr, acc_scr, z_scr, w_scr, *, sw, aw, d):
    blk = ATTN_BLOCK
    top = ATTN_TOP_ROWS
    n_pairs = aw // LANES
    nh = 2 * n_pairs
    tm = x_ref.shape[1]
    nsub = tm // blk
    step = pl.program_id(1)
    tile0 = pl.multiple_of(step * tm, tm)
    mod = _mod_rows(mod_ref, d)
    even_head = (lax.broadcasted_iota(jnp.int32, (tm, aw), 1) // HEAD_DIM) % 2 == 0

    def qkv(hb):
        o = sw
        q = (_dot(hb, w_ref[:, o:o + aw]) * Q_SCALE).astype(BF16); o += aw
        q_scr[0] = jnp.where(even_head, q, jnp.zeros_like(q))
        q_scr[1] = jnp.where(even_head, jnp.zeros_like(q), q)
        kt = _dot(hb, w_ref[:, o:o + aw]).T.astype(BF16); o += aw
        for j in range(nsub):
            k_scr[nsub * step + j] = kt[:, j * blk:(j + 1) * blk]
        v = _dot(hb, w_ref[:, o:o + aw]).astype(BF16)
        v_scr[0, pl.ds(tile0, tm), :] = jnp.where(even_head, v, jnp.zeros_like(v))
        v_scr[1, pl.ds(tile0, tm), :] = jnp.where(even_head, jnp.zeros_like(v), v)

    def region(units, fillers=()):
        starts = [pl.multiple_of(kb * blk, blk) for kb, _, _, _ in units]
        for u, (_, r0, r1, _) in enumerate(units):
            n = r1 - r0
            for p in range(n_pairs):
                ls = slice(p * LANES, (p + 1) * LANES)
                kblk = k_scr[units[u][0], ls, :]
                zz = _dot(jnp.concatenate([q_scr[0, r0:r1, ls], q_scr[1, r0:r1, ls]], axis=0), kblk)
                z_scr[u * nh + 2 * p, 0:n] = zz[:n]
                z_scr[u * nh + 2 * p + 1, 0:n] = zz[n:]
        fillers = list(fillers)
        every = -(-len(units) // (len(fillers) + 1))
        for u, (_, r0, r1, diag) in enumerate(units):
            n = r1 - r0
            if diag:
                row = lax.broadcasted_iota(jnp.int32, (n, blk), 0) + r0 % blk
                valid = lax.broadcasted_iota(jnp.int32, (n, blk), 1) < row
            zs, sps = [], []
            for h in range(nh):
                z = z_scr[u * nh + h, 0:n]
                sp = jnp.maximum(z, 0.0) + jnp.log2(1.0 + jnp.exp2(-jnp.abs(z)))
                if diag:
                    sp = jnp.where(valid, sp, 0.0)
                zs.append(z)
                sps.append(sp.astype(BF16))
            incl_all = _dot(jnp.concatenate(sps, axis=0), tri_ref[...])
            for h in range(nh):
                z = zs[h]
                incl = incl_all[h * n:(h + 1) * n]
                total = jnp.broadcast_to(incl[:, 0:1], (n, blk))
                if diag:
                    w = jnp.where(valid, jnp.exp2(z - incl), 0.0)
                    c_new = total
                else:
                    c = c_scr[h, r0:r1]
                    w = jnp.exp2(z - incl - c)
                    c_new = c + total
                c_scr[h, r0:r1] = c_new
                c_low = c_new if h == 0 else jnp.minimum(c_low, c_new)
                w_scr[u * nh + h, 0:n] = w.astype(BF16)
            cmin_scr[r0:r1] = c_low
            if fillers and (u + 1) % every == 0:
                fillers.pop(0)()
        for filler in fillers:
            filler()
        for r0, r1 in dict.fromkeys((r0, r1) for _, r0, r1, _ in units):
            us = [u for u, (_, a, b, _) in enumerate(units) if (a, b) == (r0, r1)]
            for p in range(n_pairs):
                ls = slice(p * LANES, (p + 1) * LANES)
                ww = jnp.concatenate([w_scr[u * nh + 2 * p + hh, 0:r1 - r0]
                                      for u in us for hh in range(2)], axis=1)
                vv = jnp.concatenate([v_scr[hh, pl.ds(starts[u], blk), ls]
                                      for u in us for hh in range(2)], axis=0)
                if any(units[u][3] for u in us):
                    acc_scr[p, r0:r1] = _dot(ww, vv)
                else:
                    acc_scr[p, r0:r1] = acc_scr[p, r0:r1] + _dot(ww, vv)

    def c_min(r0, r1):
        return jnp.min(cmin_scr[r0:r1], axis=0, keepdims=True)[0, 0]

    def head_units(sub, qi, n_prev):
        base = sub * blk
        units = [(qi, base, base + blk, True)]
        if n_prev >= 1:
            units.append((qi - 1, base, base + blk, False))
        if n_prev >= 2:
            units.append((qi - 2, base, base + top, False))
        return units

    def tile(units):
        h = _rms(x_ref[0]) * g_ref[...]
        hb = (h * (1.0 + mod[1]) + mod[0]).astype(BF16)
        qkv(hb)
        o = sw + 3 * aw

        def chunk(ref, col, w0):
            def run():
                ref[0, :, col:col + FRONT_CHUNK] = _dot(hb, w_ref[:, w0 + col:w0 + col + FRONT_CHUNK])
            return run

        region(units, [chunk(ref, col, w0)
                       for ref, w0, width in ((gates_ref, o, 2 * d), (u_ref, 0, sw))
                       for col in range(0, width, FRONT_CHUNK)])

    @pl.when(step >= 1)
    def _():
        tile([u for sub in range(nsub) for u in head_units(sub, nsub * step + sub, 2)])

    @pl.when(step == 0)
    def _():
        tile([u for sub in range(nsub) for u in head_units(sub, sub, min(sub, 2))])

    def sweep(first_kb, cmin, r0, r1):
        def more(carry):
            kb, cmin = carry
            return jnp.logical_and(kb >= 0, cmin < UNDERFLOW_LOG2)

        def body(carry):
            kb, _ = carry
            region([(kb, r0, r1, False)])
            return kb - 1, c_min(r0, r1)

        lax.while_loop(more, body, (first_kb, cmin))

    tails = []
    for sub in range(nsub):
        qi = nsub * step + sub
        base = sub * blk
        tails.append((jnp.where(qi >= 2, qi - 3, -1), base, base + top))
        tails.append((jnp.where(qi >= 2, qi - 2, -1), base + top, base + blk))
    cmins = [c_min(r0, r1) for _, r0, r1 in tails]
    for (first_kb, r0, r1), cmin in zip(tails, cmins):
        sweep(first_kb, cmin, r0, r1)
    for p in range(n_pairs):
        at_ref[0, :, p * LANES:(p + 1) * LANES] = acc_scr[p].astype(at_ref.dtype)


def _attn_tri():
    blk = ATTN_BLOCK
    m = np.arange(blk)[:, None]
    j = np.arange(blk)[None, :]
    return jnp.asarray((m >= j).astype(np.float32), BF16)


def _front(x, mod, norm_g, w_in_b, sw, aw):
    bsz, seq, d = x.shape
    tm = FRONT_TILE
    blk = ATTN_BLOCK
    n = w_in_b.shape[1]
    n_pairs = aw // LANES
    nz = ATTN_REGION * (tm // blk) * 2 * n_pairs
    tri = _attn_tri()
    tok = lambda w: pl.BlockSpec((1, tm, w), lambda b, i: (b, i, 0))
    const = lambda a: pl.BlockSpec(a.shape, lambda b, i: (0,) * a.ndim,
                                   pipeline_mode=pl.Buffered(1))
    return pl.pallas_call(
        functools.partial(_front_kernel, sw=sw, aw=aw, d=d),
        grid=(bsz, seq // tm),
        in_specs=[tok(d),
                  pl.BlockSpec(mod.shape, lambda b, i: (0, 0)),
                  const(norm_g), const(w_in_b), const(tri)],
        out_specs=[tok(sw), tok(2 * d), tok(aw)],
        out_shape=[jax.ShapeDtypeStruct((bsz, seq, sw), F32),
                   jax.ShapeDtypeStruct((bsz, seq, 2 * d), F32),
                   jax.ShapeDtypeStruct((bsz, seq, aw), BF16)],
        scratch_shapes=[pltpu.VMEM((2, tm, aw), BF16),
                        pltpu.VMEM((seq // blk, aw, blk), BF16),
                        pltpu.VMEM((2, seq, aw), BF16),
                        pltpu.VMEM((2 * n_pairs, tm, blk), F32),
                        pltpu.VMEM((tm, blk), F32),
                        pltpu.VMEM((n_pairs, tm, LANES), F32),
                        pltpu.VMEM((nz, blk, blk), F32),
                        pltpu.VMEM((nz, blk, blk), BF16)],
        compiler_params=pltpu.CompilerParams(
            dimension_semantics=("arbitrary", "arbitrary"),
            vmem_limit_bytes=VMEM_LIMIT_BYTES),
        name="front",
    )(x, mod, norm_g, w_in_b, tri)


def _out_ffn_kernel(x_ref, s5_ref, at_ref, gates_ref, mod_ref, n2_ref, nf_ref,
                    wb_ref, wo_ref, wg_ref, wu_ref, wd_ref, o_ref, *, final_norm):
    mod = _mod_rows(mod_ref, x_ref.shape[2])
    tm = x_ref.shape[1]
    groups = [slice(r, r + tm // OUT_GROUPS) for r in range(0, tm, tm // OUT_GROUPS)]
    ms = []
    for g in groups:
        ya = s5_ref[0, g]
        yb = _dot(at_ref[0, g], wb_ref[...])
        d = ya.shape[1]
        m = (jax.nn.sigmoid(gates_ref[0, g, :d]) * ya
             + jax.nn.sigmoid(gates_ref[0, g, d:]) * yb)
        ms.append(m.astype(BF16))
    x1s, hs = [], []
    for g, m in zip(groups, ms):
        x1 = x_ref[0, g] + mod[2] * _dot(m, wo_ref[...])
        h = _rms(x1) * n2_ref[...]
        x1s.append(x1)
        hs.append((h * (1.0 + mod[4]) + mod[3]).astype(BF16))
    acts = []
    for h in hs:
        gate = _dot(h, wg_ref[...])
        up = _dot(h, wu_ref[...])
        acts.append((gate * jax.nn.sigmoid(gate) * up).astype(BF16))
    for g, x1, act in zip(groups, x1s, acts):
        x2 = x1 + mod[5] * _dot(act, wd_ref[...])
        o_ref[0, g] = _rms(x2) * nf_ref[...] if final_norm else x2


def _out_ffn(x, s5o, attn, gates, mod, n2g, nfg, wb, wo, wg, wu, wd, final_norm):
    bsz, seq, d = x.shape
    tm = OUT_TILE
    tok = lambda a: pl.BlockSpec((1, tm, a.shape[-1]), lambda b, i: (b, i, 0))
    const = lambda a: pl.BlockSpec(a.shape, lambda b, i: (0,) * a.ndim,
                                   pipeline_mode=pl.Buffered(1))
    return pl.pallas_call(
        functools.partial(_out_ffn_kernel, final_norm=final_norm),
        grid=(bsz, seq // tm),
        in_specs=[tok(x), tok(s5o), tok(attn), tok(gates),
                  pl.BlockSpec(mod.shape, lambda b, i: (0, 0)),
                  const(n2g), const(nfg),
                  const(wb), const(wo), const(wg), const(wu), const(wd)],
        out_specs=pl.BlockSpec((1, tm, d), lambda b, i: (b, i, 0)),
        out_shape=jax.ShapeDtypeStruct((bsz, seq, d), F32),
        compiler_params=pltpu.CompilerParams(
            dimension_semantics=("arbitrary", "arbitrary"),
            vmem_limit_bytes=VMEM_LIMIT_BYTES),
        name="out_ffn",
    )(x, s5o, attn, gates, mod, n2g, nfg, wb, wo, wg, wu, wd)


def kernel(x, c, w_ada, b_ada, norm1_g, w_in, lam_re, lam_im, log_dt, b_re, b_im, c_re, c_im,
           d_skip, w_glu, b_glu, w_a, w_b, w_o, norm2_g, w_ffn_gate, w_ffn_up, w_ffn_down,
           norm_f_g):
    depth = w_ada.shape[0]
    bsz, seq, d = x.shape
    sw = w_glu.shape[1]
    aw = w_b.shape[1]
    for l in range(depth):
        mod, w_in_b = _ada(c, w_ada[l], b_ada[l], w_in[l])
        u, gates, attn = _front(x, mod, norm1_g[l].reshape(1, d), w_in_b, sw, aw)
        wb, cm, lamr, lami = _s5_weights(lam_re[l], lam_im[l], log_dt[l], b_re[l], b_im[l],
                                         c_re[l], c_im[l])
        s5o, out_w = _s5(u, wb, cm, lamr, lami, d_skip[l], w_glu[l].astype(BF16), b_glu[l],
                         w_a[l].astype(BF16),
                         (w_b[l], w_o[l], w_ffn_gate[l], w_ffn_up[l], w_ffn_down[l]))
        x = _out_ffn(x, s5o, attn, gates, mod, norm2_g[l].reshape(1, d), norm_f_g.reshape(1, d),
                     *out_w, final_norm=(l == depth - 1))
    return x
```

```python
import functools
import math

import numpy as np
import jax
import jax.numpy as jnp
from jax import lax
from jax.experimental import pallas as pl
from jax.experimental.pallas import tpu as pltpu

F32 = jnp.float32
BF16 = jnp.bfloat16

S5_GROUP = 16
S5_STATE = 64
HEAD_DIM = 64
N_ADA = 6
RMS_EPS = 1e-6
Q_SCALE = math.log2(math.e) / math.sqrt(HEAD_DIM)
UNDERFLOW_LOG2 = 151.0

LANES = 128
SUBLANES = 8
VMEM_LIMIT_BYTES = 56 * 1024 * 1024

ADA_STEPS = 4
ATTN_BLOCK = 128
ATTN_REGION = 3
ATTN_TOP_ROWS = 64
S5_TILE = 256
S5_SUBTILE = 128
S5_SLABS = 4
FRONT_TILE = 512
FRONT_CHUNK = 512
OUT_TILE = 512
OUT_GROUPS = 2


def _dot(a, b):
    return jnp.dot(a, b, preferred_element_type=F32)


def _rms(x):
    return x * lax.rsqrt(jnp.mean(x * x, axis=-1, keepdims=True) + RMS_EPS)


def _mod_rows(mod_ref, d):
    row = mod_ref[pl.ds(pl.program_id(0), 1), :]
    return [row[:, k * d:(k + 1) * d] for k in range(N_ADA)]


def _ada_kernel(c_ref, w_ref, b_ref, cast_ref, o_ref, cast_out_ref):
    c = c_ref[...]
    bsz = c.shape[0]
    cond = c * jax.nn.sigmoid(c)
    pad = -bsz % SUBLANES
    if pad:
        cond = jnp.concatenate([cond, jnp.zeros((pad, c.shape[1]), F32)], axis=0)
    o_ref[...] = _dot(cond.astype(BF16), w_ref[...].astype(BF16))[:bsz] + b_ref[...]
    cast_out_ref[...] = cast_ref[...].astype(BF16)


def _ada(c, w_ada, b_ada, cast_weight):
    bsz, d = c.shape
    n = w_ada.shape[1]
    steps = ADA_STEPS
    tn = n // steps
    rows, cols = cast_weight.shape
    assert n % (steps * LANES) == 0 and rows % (16 * steps) == 0
    return pl.pallas_call(
        _ada_kernel,
        grid=(steps,),
        in_specs=[pl.BlockSpec((bsz, d), lambda j: (0, 0)),
                  pl.BlockSpec((d, tn), lambda j: (0, j)),
                  pl.BlockSpec((1, tn), lambda j: (0, j)),
                  pl.BlockSpec((rows // steps, cols), lambda j: (j, 0))],
        out_specs=[pl.BlockSpec((bsz, tn), lambda j: (0, j)),
                   pl.BlockSpec((rows // steps, cols), lambda j: (j, 0))],
        out_shape=[jax.ShapeDtypeStruct((bsz, n), F32),
                   jax.ShapeDtypeStruct((rows, cols), BF16)],
        compiler_params=pltpu.CompilerParams(vmem_limit_bytes=VMEM_LIMIT_BYTES),
        name="ada",
    )(c, w_ada, b_ada.reshape(1, n), cast_weight)


def _s5_kernel(u_ref, perm_ref, permt_ref, wb_ref, cm_ref, lamr_ref, lami_ref, d_ref,
               wglu_ref, bglu_ref, wa_ref, *rest, tm, sw, n_cast):
    cast_in, o_ref, cast_out = rest[:n_cast], rest[n_cast], rest[n_cast + 1:2 * n_cast + 1]
    x_scr, ulast_scr = rest[2 * n_cast + 1:]
    for src, dst in zip(cast_in, cast_out):
        dst[...] = src[...].astype(BF16)
    i = pl.program_id(0)
    ts = S5_SUBTILE
    n2 = ts // 2
    rows = SUBLANES * n2
    cw = sw // S5_SLABS
    hs = cw * S5_STATE // S5_GROUP

    @pl.when(i == 0)
    def _():
        x_scr[...] = jnp.zeros_like(x_scr)
        ulast_scr[...] = jnp.zeros_like(ulast_scr)

    sub8 = lax.broadcasted_iota(jnp.int32, (SUBLANES, sw), 0)
    odd = (lax.broadcasted_iota(jnp.int32, (rows, sw), 0) & 1) == 1

    def natural(j):
        return jnp.concatenate([u_ref[b, j * ts:(j + 1) * ts, :] for b in range(4)], axis=0)

    def last_rows(j):
        last = jnp.zeros((SUBLANES, sw), F32)
        for b in range(4):
            row = u_ref[b, (j + 1) * ts - 1:(j + 1) * ts, :].astype(BF16).astype(F32)
            last = jnp.where(sub8 == 2 * b, jnp.broadcast_to(row, (SUBLANES, sw)), last)
        return last

    def input_stage(j, before):
        a_cur = _dot(perm_ref[...], natural(j).astype(BF16))
        a_prev = jnp.where(odd, pltpu.roll(a_cur, 1, 0), pltpu.roll(a_cur, SUBLANES - 1, 0))
        first = jnp.where((sub8 & 1) == 1, a_prev[:SUBLANES], before)
        a_prev = jnp.concatenate([first, a_prev[SUBLANES:]], axis=0).astype(BF16)
        a_cur = a_cur.astype(BF16)
        return [_dot(jnp.concatenate([a_cur[:, s * cw:(s + 1) * cw],
                                      a_prev[:, s * cw:(s + 1) * cw]], axis=1), wb_ref[s])
                for s in range(S5_SLABS)]

    def scan_stage(bus, x, fillers):
        fillers = list(fillers)
        every = -(-S5_SLABS * n2 // (len(fillers) + 1))
        done = 0
        states, x_out = [], []
        for s in range(S5_SLABS):
            ar = lamr_ref[:, hs * s:hs * (s + 1)]
            ai = lami_ref[:, hs * s:hs * (s + 1)]
            xr, xi = x[s]
            st = []
            for t2 in range(n2):
                rs = slice(SUBLANES * t2, SUBLANES * (t2 + 1))
                xr, xi = (ar * xr - ai * xi + bus[s][rs, :hs],
                          ar * xi + ai * xr + bus[s][rs, hs:])
                st.append(jnp.concatenate([xr, xi], axis=1))
                done += 1
                if fillers and done % every == 0:
                    fillers.pop(0)()
            x_out.append((xr, xi))
            states.append(jnp.concatenate(st, axis=0).astype(BF16))
        for filler in fillers:
            filler()
        return states, x_out

    def output_stage(j, states):
        y_il = {}

        def slab_piece(s):
            def run():
                y_il[s] = _dot(states[s], cm_ref[s]).astype(BF16)
            return run

        def batch_piece(b):
            def run():
                y = _dot(permt_ref[b * ts:(b + 1) * ts, :],
                         jnp.concatenate([y_il[s] for s in range(S5_SLABS)], axis=1))
                y = y + d_ref[...] * u_ref[b, j * ts:(j + 1) * ts, :]
                y = jax.nn.gelu(y)
                z = _dot(y.astype(BF16), wglu_ref[...]) + bglu_ref[...]
                o_ref[b, j * ts:(j + 1) * ts, :] = _dot((y * jax.nn.sigmoid(z)).astype(BF16), wa_ref[...])
            return run

        return [slab_piece(s) for s in range(S5_SLABS)] + [batch_piece(b) for b in range(4)]

    nsub = tm // ts
    befores = [ulast_scr[...]] + [last_rows(j) for j in range(nsub - 1)]
    bus = [input_stage(j, befores[j]) for j in range(nsub)]
    ulast_scr[...] = last_rows(nsub - 1)
    x = [(x_scr[s, 0], x_scr[s, 1]) for s in range(S5_SLABS)]
    pending = []
    for j in range(nsub):
        states, x = scan_stage(bus[j], x, pending)
        pending = output_stage(j, states)
    for piece in pending:
        piece()
    for s in range(S5_SLABS):
        x_scr[s, 0], x_scr[s, 1] = x[s]


def _s5_perms(tm):
    n2 = tm // 2
    rows = SUBLANES * n2
    perm = np.zeros((rows, 4 * tm), np.float32)
    permt = np.zeros((4 * tm, rows), np.float32)
    for t2 in range(n2):
        for b in range(4):
            for par in range(2):
                r = SUBLANES * t2 + 2 * b + par
                t = 2 * t2 + par
                perm[r, b * tm + t] = 1.0
                permt[b * tm + t, r] = 1.0
    return jnp.asarray(perm, BF16), jnp.asarray(permt, BF16)


def _slab_block_diag(blocks):
    g, a, b = blocks.shape
    n = g // S5_SLABS
    eye = jnp.eye(n, dtype=blocks.dtype)
    placed = blocks.reshape(S5_SLABS, n, a, 1, b) * eye[None, :, None, :, None]
    return placed.reshape(S5_SLABS, n * a, n * b)


def _s5_weights(lam_re, lam_im, log_dt, b_re, b_im, c_re, c_im):
    g = lam_re.shape[0]
    dt = jnp.exp(log_dt)[:, None]
    mag = jnp.exp(lam_re * dt)
    lbr = mag * jnp.cos(lam_im * dt)
    lbi = mag * jnp.sin(lam_im * dt)
    nr, ni = lbr - 1.0, lbi
    den = lam_re * lam_re + lam_im * lam_im
    cr = (nr * lam_re + ni * lam_im) / den
    ci = (ni * lam_re - nr * lam_im) / den
    bbr = cr[..., None] * b_re - ci[..., None] * b_im
    bbi = cr[..., None] * b_im + ci[..., None] * b_re
    lr = lbr[..., None] * bbr - lbi[..., None] * bbi
    li = lbr[..., None] * bbi + lbi[..., None] * bbr
    l2r = lbr * lbr - lbi * lbi
    l2i = 2.0 * lbr * lbi
    bd = lambda a: _slab_block_diag(jnp.swapaxes(a, 1, 2))
    wb = jnp.concatenate([jnp.concatenate([bd(bbr), bd(bbi)], axis=2),
                          jnp.concatenate([bd(lr), bd(li)], axis=2)], axis=1).astype(BF16)
    cm = jnp.concatenate([bd(c_re), -bd(c_im)], axis=1).astype(BF16)
    lamr = jnp.broadcast_to(l2r.reshape(1, -1), (SUBLANES, l2r.size))
    lami = jnp.broadcast_to(l2i.reshape(1, -1), (SUBLANES, l2i.size))
    return wb, cm, lamr, lami


def _s5(u, wb, cm, lamr, lami, d_skip, w_glu_b, b_glu, w_a_b, cast_weights):
    bsz, seq, sw = u.shape
    d = w_a_b.shape[1]
    assert bsz == 4, "the scan packs 4 batch rows x 2 token parities into 8 sublanes"
    tm = S5_TILE
    steps = seq // tm
    ns = lamr.shape[1]
    perm, permt = _s5_perms(S5_SUBTILE)
    const = lambda a: pl.BlockSpec(a.shape, lambda i: (0,) * a.ndim)
    rows = lambda a: pl.BlockSpec((a.shape[0] // steps, a.shape[1]), lambda i: (i, 0))
    assert all(w.shape[0] % (16 * steps) == 0 for w in cast_weights)
    d_row = d_skip.reshape(1, sw)
    bg = b_glu.reshape(1, sw)
    outs = pl.pallas_call(
        functools.partial(_s5_kernel, tm=tm, sw=sw, n_cast=len(cast_weights)),
        grid=(steps,),
        in_specs=[pl.BlockSpec((4, tm, sw), lambda i: (0, i, 0)),
                  const(perm), const(permt), const(wb), const(cm), const(lamr), const(lami),
                  const(d_row), const(w_glu_b), const(bg), const(w_a_b)]
        + [rows(w) for w in cast_weights],
        out_specs=[pl.BlockSpec((4, tm, d), lambda i: (0, i, 0))] + [rows(w) for w in cast_weights],
        out_shape=[jax.ShapeDtypeStruct((bsz, seq, d), F32)]
        + [jax.ShapeDtypeStruct(w.shape, BF16) for w in cast_weights],
        scratch_shapes=[pltpu.VMEM((S5_SLABS, 2, SUBLANES, ns // S5_SLABS), F32),
                        pltpu.VMEM((SUBLANES, sw), F32)],
        compiler_params=pltpu.CompilerParams(
            dimension_semantics=("arbitrary",),
            vmem_limit_bytes=VMEM_LIMIT_BYTES),
        name="s5",
    )(u, perm, permt, wb, cm, lamr, lami, d_row, w_glu_b, bg, w_a_b, *cast_weights)
    return outs[0], outs[1:]


def _front_kernel(x_ref, mod_ref, g_ref, w_ref, tri_ref, u_ref, gates_ref, at_ref,
                  q_scr, k_scr, v_scr, c_scr, cmin_scr, acc_scr, z_scr, w_scr, *, sw, aw, d):
    blk = ATTN_BLOCK
    top = ATTN_TOP_ROWS
    n_pairs = aw // LANES
    nh = 2 * n_pairs
    tm = x_ref.shape[1]
    nsub = tm // blk
    step = pl.program_id(1)
    tile0 = pl.multiple_of(step * tm, tm)
    mod = _mod_rows(mod_ref, d)
    even_head = (lax.broadcasted_iota(jnp.int32, (tm, aw), 1) // HEAD_DIM) % 2 == 0

    def qkv(hb):
        o = sw
        q = (_dot(hb, w_ref[:, o:o + aw]) * Q_SCALE).astype(BF16); o += aw
        q_scr[0] = jnp.where(even_head, q, jnp.zeros_like(q))
        q_scr[1] = jnp.where(even_head, jnp.zeros_like(q), q)
        kt = _dot(hb, w_ref[:, o:o + aw]).T.astype(BF16); o += aw
        for j in range(nsub):
            k_scr[nsub * step + j] = kt[:, j * blk:(j + 1) * blk]
        v = _dot(hb, w_ref[:, o:o + aw]).astype(BF16)
        v_scr[0, pl.ds(tile0, tm), :] = jnp.where(even_head, v, jnp.zeros_like(v))
        v_scr[1, pl.ds(tile0, tm), :] = jnp.where(even_head, jnp.zeros_like(v), v)

    def region(units, fillers=()):
        starts = [pl.multiple_of(kb * blk, blk) for kb, _, _, _ in units]
        for u, (_, r0, r1, _) in enumerate(units):
            n = r1 - r0
            for p in range(n_pairs):
                ls = slice(p * LANES, (p + 1) * LANES)
                kblk = k_scr[units[u][0], ls, :]
                zz = _dot(jnp.concatenate([q_scr[0, r0:r1, ls], q_scr[1, r0:r1, ls]], axis=0), kblk)
                z_scr[u * nh + 2 * p, 0:n] = zz[:n]
                z_scr[u * nh + 2 * p + 1, 0:n] = zz[n:]
        fillers = list(fillers)
        every = -(-len(units) // (len(fillers) + 1))
        for u, (_, r0, r1, diag) in enumerate(units):
            n = r1 - r0
            if diag:
                row = lax.broadcasted_iota(jnp.int32, (n, blk), 0) + r0 % blk
                valid = lax.broadcasted_iota(jnp.int32, (n, blk), 1) < row
            zs, sps = [], []
            for h in range(nh):
                z = z_scr[u * nh + h, 0:n]
                sp = jnp.maximum(z, 0.0) + jnp.log2(1.0 + jnp.exp2(-jnp.abs(z)))
                if diag:
                    sp = jnp.where(valid, sp, 0.0)
                zs.append(z)
                sps.append(sp.astype(BF16))
            incl_all = _dot(jnp.concatenate(sps, axis=0), tri_ref[...])
            for h in range(nh):
                z = zs[h]
                incl = incl_all[h * n:(h + 1) * n]
                total = jnp.broadcast_to(incl[:, 0:1], (n, blk))
                if diag:
                    w = jnp.where(valid, jnp.exp2(z - incl), 0.0)
                    c_new = total
                else:
                    c = c_scr[h, r0:r1]
                    w = jnp.exp2(z - incl - c)
                    c_new = c + total
                c_scr[h, r0:r1] = c_new
                c_low = c_new if h == 0 else jnp.minimum(c_low, c_new)
                w_scr[u * nh + h, 0:n] = w.astype(BF16)
            cmin_scr[r0:r1] = c_low
            if fillers and (u + 1) % every == 0:
                fillers.pop(0)()
        for filler in fillers:
            filler()
        for r0, r1 in dict.fromkeys((r0, r1) for _, r0, r1, _ in units):
            us = [u for u, (_, a, b, _) in enumerate(units) if (a, b) == (r0, r1)]
            for p in range(n_pairs):
                ls = slice(p * LANES, (p + 1) * LANES)
                ww = jnp.concatenate([w_scr[u * nh + 2 * p + hh, 0:r1 - r0]
                                      for u in us for hh in range(2)], axis=1)
                vv = jnp.concatenate([v_scr[hh, pl.ds(starts[u], blk), ls]
                                      for u in us for hh in range(2)], axis=0)
                if any(units[u][3] for u in us):
                    acc_scr[p, r0:r1] = _dot(ww, vv)
                else:
                    acc_scr[p, r0:r1] = acc_scr[p, r0:r1] + _dot(ww, vv)

    def c_min(r0, r1):
        return jnp.min(cmin_scr[r0:r1], axis=0, keepdims=True)[0, 0]

    def head_units(sub, qi, n_prev):
        base = sub * blk
        units = [(qi, base, base + blk, True)]
        if n_prev >= 1:
            units.append((qi - 1, base, base + blk, False))
        if n_prev >= 2:
            units.append((qi - 2, base, base + top, False))
        return units

    def tile(units):
        h = _rms(x_ref[0]) * g_ref[...]
        hb = (h * (1.0 + mod[1]) + mod[0]).astype(BF16)
        qkv(hb)
        o = sw + 3 * aw

        def chunk(ref, col, w0):
            def run():
                ref[0, :, col:col + FRONT_CHUNK] = _dot(hb, w_ref[:, w0 + col:w0 + col + FRONT_CHUNK])
            return run

        region(units, [chunk(ref, col, w0)
                       for ref, w0, width in ((gates_ref, o, 2 * d), (u_ref, 0, sw))
                       for col in range(0, width, FRONT_CHUNK)])

    @pl.when(step >= 1)
    def _():
        tile([u for sub in range(nsub) for u in head_units(sub, nsub * step + sub, 2)])

    @pl.when(step == 0)
    def _():
        tile([u for sub in range(nsub) for u in head_units(sub, sub, min(sub, 2))])

    def sweep(first_kb, cmin, r0, r1):
        def more(carry):
            kb, cmin = carry
            return jnp.logical_and(kb >= 0, cmin < UNDERFLOW_LOG2)

        def body(carry):
            kb, _ = carry
            region([(kb, r0, r1, False)])
            return kb - 1, c_min(r0, r1)

        lax.while_loop(more, body, (first_kb, cmin))

    tails = []
    for sub in range(nsub):
        qi = nsub * step + sub
        base = sub * blk
        tails.append((jnp.where(qi >= 2, qi - 3, -1), base, base + top))
        tails.append((jnp.where(qi >= 2, qi - 2, -1), base + top, base + blk))
    cmins = [c_min(r0, r1) for _, r0, r1 in tails]
    for (first_kb, r0, r1), cmin in zip(tails, cmins):
        sweep(first_kb, cmin, r0, r1)
    for p in range(n_pairs):
        at_ref[0, :, p * LANES:(p + 1) * LANES] = acc_scr[p].astype(at_ref.dtype)


def _attn_tri():
    blk = ATTN_BLOCK
    m = np.arange(blk)[:, None]
    j = np.arange(blk)[None, :]
    return jnp.asarray((m >= j).astype(np.float32), BF16)


def _front(x, mod, norm_g, w_in_b, sw, aw):
    bsz, seq, d = x.shape
    tm = FRONT_TILE
    blk = ATTN_BLOCK
    n = w_in_b.shape[1]
    n_pairs = aw // LANES
    nz = ATTN_REGION * (tm // blk) * 2 * n_pairs
    tri = _attn_tri()
    tok = lambda w: pl.BlockSpec((1, tm, w), lambda b, i: (b, i, 0))
    const = lambda a: pl.BlockSpec(a.shape, lambda b, i: (0,) * a.ndim,
                                   pipeline_mode=pl.Buffered(1))
    return pl.pallas_call(
        functools.partial(_front_kernel, sw=sw, aw=aw, d=d),
        grid=(bsz, seq // tm),
        in_specs=[tok(d),
                  pl.BlockSpec(mod.shape, lambda b, i: (0, 0)),
                  const(norm_g), const(w_in_b), const(tri)],
        out_specs=[tok(sw), tok(2 * d), tok(aw)],
        out_shape=[jax.ShapeDtypeStruct((bsz, seq, sw), F32),
                   jax.ShapeDtypeStruct((bsz, seq, 2 * d), F32),
                   jax.ShapeDtypeStruct((bsz, seq, aw), BF16)],
        scratch_shapes=[pltpu.VMEM((2, tm, aw), BF16),
                        pltpu.VMEM((seq // blk, aw, blk), BF16),
                        pltpu.VMEM((2, seq, aw), BF16),
                        pltpu.VMEM((2 * n_pairs, tm, blk), F32),
                        pltpu.VMEM((tm, blk), F32),
                        pltpu.VMEM((n_pairs, tm, LANES), F32),
                        pltpu.VMEM((nz, blk, blk), F32),
                        pltpu.VMEM((nz, blk, blk), BF16)],
        compiler_params=pltpu.CompilerParams(
            dimension_semantics=("arbitrary", "arbitrary"),
            vmem_limit_bytes=VMEM_LIMIT_BYTES),
        name="front",
    )(x, mod, norm_g, w_in_b, tri)


def _out_ffn_kernel(x_ref, s5_ref, at_ref, gates_ref, mod_ref, n2_ref, nf_ref,
                    wb_ref, wo_ref, wg_ref, wu_ref, wd_ref, o_ref, *, final_norm):
    mod = _mod_rows(mod_ref, x_ref.shape[2])
    tm = x_ref.shape[1]
    groups = [slice(r, r + tm // OUT_GROUPS) for r in range(0, tm, tm // OUT_GROUPS)]
    ms = []
    for g in groups:
        ya = s5_ref[0, g]
        yb = _dot(at_ref[0, g], wb_ref[...])
        d = ya.shape[1]
        m = (jax.nn.sigmoid(gates_ref[0, g, :d]) * ya
             + jax.nn.sigmoid(gates_ref[0, g, d:]) * yb)
        ms.append(m.astype(BF16))
    x1s, hs = [], []
    for g, m in zip(groups, ms):
        x1 = x_ref[0, g] + mod[2] * _dot(m, wo_ref[...])
        h = _rms(x1) * n2_ref[...]
        x1s.append(x1)
        hs.append((h * (1.0 + mod[4]) + mod[3]).astype(BF16))
    acts = []
    for h in hs:
        gate = _dot(h, wg_ref[...])
        up = _dot(h, wu_ref[...])
        acts.append((gate * jax.nn.sigmoid(gate) * up).astype(BF16))
    for g, x1, act in zip(groups, x1s, acts):
        x2 = x1 + mod[5] * _dot(act, wd_ref[...])
        o_ref[0, g] = _rms(x2) * nf_ref[...] if final_norm else x2


def _out_ffn(x, s5o, attn, gates, mod, n2g, nfg, wb, wo, wg, wu, wd, final_norm):
    bsz, seq, d = x.shape
    tm = OUT_TILE
    tok = lambda a: pl.BlockSpec((1, tm, a.shape[-1]), lambda b, i: (b, i, 0))
    const = lambda a: pl.BlockSpec(a.shape, lambda b, i: (0,) * a.ndim,
                                   pipeline_mode=pl.Buffered(1))
    return pl.pallas_call(
        functools.partial(_out_ffn_kernel, final_norm=final_norm),
        grid=(bsz, seq // tm),
        in_specs=[tok(x), tok(s5o), tok(attn), tok(gates),
                  pl.BlockSpec(mod.shape, lambda b, i: (0, 0)),
                  const(n2g), const(nfg),
                  const(wb), const(wo), const(wg), const(wu), const(wd)],
        out_specs=pl.BlockSpec((1, tm, d), lambda b, i: (b, i, 0)),
        out_shape=jax.ShapeDtypeStruct((bsz, seq, d), F32),
        compiler_params=pltpu.CompilerParams(
            dimension_semantics=("arbitrary", "arbitrary"),
            vmem_limit_bytes=VMEM_LIMIT_BYTES),
        name="out_ffn",
    )(x, s5o, attn, gates, mod, n2g, nfg, wb, wo, wg, wu, wd)


def kernel(x, c, w_ada, b_ada, norm1_g, w_in, lam_re, lam_im, log_dt, b_re, b_im, c_re, c_im,
           d_skip, w_glu, b_glu, w_a, w_b, w_o, norm2_g, w_ffn_gate, w_ffn_up, w_ffn_down,
           norm_f_g):
    depth = w_ada.shape[0]
    bsz, seq, d = x.shape
    sw = w_glu.shape[1]
    aw = w_b.shape[1]
    for l in range(depth):
        mod, w_in_b = _ada(c, w_ada[l], b_ada[l], w_in[l])
        u, gates, attn = _front(x, mod, norm1_g[l].reshape(1, d), w_in_b, sw, aw)
        wb, cm, lamr, lami = _s5_weights(lam_re[l], lam_im[l], log_dt[l], b_re[l], b_im[l],
                                         c_re[l], c_im[l])
        s5o, out_w = _s5(u, wb, cm, lamr, lami, d_skip[l], w_glu[l].astype(BF16), b_glu[l],
                         w_a[l].astype(BF16),
                         (w_b[l], w_o[l], w_ffn_gate[l], w_ffn_up[l], w_ffn_down[l]))
        x = _out_ffn(x, s5o, attn, gates, mod, norm2_g[l].reshape(1, d), norm_f_g.reshape(1, d),
                     *out_w, final_norm=(l == depth - 1))
    return x
```
